```python
import math
import jax
import jax.numpy as jnp
from jax import lax
import numpy as np


D_MODEL = 1024
BATCH = 8
SEQ = 8192
DEPTH = 4

CTX_LEN = 256
GRID_W = 64
HEAD_DIM = 64
ROPE_PAIRS = HEAD_DIM // 4
ROPE_BASE = 10000.0
Q_BLOCK = 128
NORM_EPS = 1e-6

DIFF_HEADS = 4
DIFF_V_DIM = 2 * HEAD_DIM
GQA_HEADS = 8
GQA_KV_HEADS = 2
GQA_GROUP = GQA_HEADS // GQA_KV_HEADS

DIFF_Q_COLS = DIFF_HEADS * 2 * HEAD_DIM
DIFF_K_COLS = DIFF_HEADS * 2 * HEAD_DIM
DIFF_V_COLS = DIFF_HEADS * DIFF_V_DIM
GQA_Q_COLS = GQA_HEADS * HEAD_DIM
GQA_KV_COLS = GQA_KV_HEADS * HEAD_DIM
ATTN_IN_COLS = DIFF_Q_COLS + DIFF_K_COLS + DIFF_V_COLS + GQA_Q_COLS + 2 * GQA_KV_COLS
ATTN_OUT_COLS = DIFF_V_COLS + GQA_Q_COLS
ATTN_SPLITS = (DIFF_Q_COLS,
               DIFF_Q_COLS + DIFF_K_COLS,
               DIFF_Q_COLS + DIFF_K_COLS + DIFF_V_COLS,
               DIFF_Q_COLS + DIFF_K_COLS + DIFF_V_COLS + GQA_Q_COLS,
               DIFF_Q_COLS + DIFF_K_COLS + DIFF_V_COLS + GQA_Q_COLS + GQA_KV_COLS)

SSM_D_INNER = 2 * D_MODEL
SSM_HEAD_DIM = 64
SSM_HEADS = SSM_D_INNER // SSM_HEAD_DIM
SSM_GROUPS = 4
SSM_HEADS_PER_GROUP = SSM_HEADS // SSM_GROUPS
SSM_STATE = 128
SSM_CONV = 3
SSM_CHUNK = 128
SSM_XBC = SSM_D_INNER + 2 * SSM_GROUPS * SSM_STATE
SSM_IN_COLS = SSM_D_INNER + SSM_XBC + 2 * SSM_HEADS

D_FF = 2816
FFN_CONV = 3

N_ATTN_LAYERS = (DEPTH + 1) // 2
N_SSM_LAYERS = DEPTH // 2

kernel_name = 'hybrid_diffattn_gqa_ssd_convglu_trunk'


def rms_norm(x, g):
    x32 = x.astype(jnp.float32)
    y = x32 * lax.rsqrt(jnp.mean(x32 * x32, axis=-1, keepdims=True) + NORM_EPS)
    return (y * g.astype(jnp.float32)).astype(x.dtype)


def modulate(x, g, shift, scale):
    return rms_norm(x, g) * (1.0 + scale) + shift


def dw_conv_centred(x, w, b):
    k = w.shape[0]
    pad = k // 2
    t = x.shape[1]
    xp = jnp.pad(x, ((0, 0), (pad, pad), (0, 0)))
    out = b
    for i in range(k):
        out = out + xp[:, i:i + t] * w[i]
    return out


def axial_rope_tables(n):
    rows = n // GRID_W
    row = jnp.broadcast_to(jnp.arange(rows)[:, None], (rows, GRID_W)).reshape(n)
    col = jnp.broadcast_to(jnp.arange(GRID_W)[None, :], (rows, GRID_W)).reshape(n)
    inv_freq = 1.0 / (ROPE_BASE ** (jnp.arange(ROPE_PAIRS, dtype=jnp.float32) / ROPE_PAIRS))
    ang = jnp.concatenate([row.astype(jnp.float32)[:, None] * inv_freq,
                           col.astype(jnp.float32)[:, None] * inv_freq], axis=-1)
    return jnp.cos(ang)[:, None, :], jnp.sin(ang)[:, None, :]


def apply_axial_rope(t, cos, sin):
    cos = cos.astype(t.dtype)
    sin = sin.astype(t.dtype)

    def rot(v, cs, sn):
        v1, v2 = v[..., :ROPE_PAIRS], v[..., ROPE_PAIRS:]
        return jnp.concatenate([v1 * cs - v2 * sn, v2 * cs + v1 * sn], axis=-1)

    half = 2 * ROPE_PAIRS
    return jnp.concatenate([rot(t[..., :half], cos[..., :ROPE_PAIRS], sin[..., :ROPE_PAIRS]),
                            rot(t[..., half:], cos[..., ROPE_PAIRS:], sin[..., ROPE_PAIRS:])], axis=-1)


def _to_blocks(t):
    b, n = t.shape[:2]
    return jnp.moveaxis(t.reshape((b, n // Q_BLOCK, Q_BLOCK) + t.shape[2:]), 1, 0)


def _from_blocks(t):
    nb, b, q = t.shape[:3]
    return jnp.moveaxis(t, 0, 1).reshape((b, nb * q) + t.shape[3:])


def attention_sweep(qa1, qa2, qb, ka1, ka2, va, kb, vb, lam):
    scale = HEAD_DIM ** -0.5

    def block(qs):
        q1, q2, qg = qs
        s1 = jnp.einsum('bqhd,bkhd->bhqk', q1, ka1).astype(jnp.float32) * scale
        s2 = jnp.einsum('bqhd,bkhd->bhqk', q2, ka2).astype(jnp.float32) * scale
        p = jax.nn.softmax(s1, axis=-1) - lam * jax.nn.softmax(s2, axis=-1)
        oa = jnp.einsum('bhqk,bkhe->bqhe', p.astype(va.dtype), va)
        b, nq = qg.shape[:2]
        qg = qg.reshape(b, nq, GQA_KV_HEADS, GQA_GROUP, HEAD_DIM)
        sg = jnp.einsum('bqgrd,bkgd->bgrqk', qg, kb).astype(jnp.float32) * scale
        pg = jax.nn.softmax(sg, axis=-1)
        ob = jnp.einsum('bgrqk,bkgd->bqgrd', pg.astype(vb.dtype), vb).reshape(b, nq, GQA_HEADS, HEAD_DIM)
        return oa, ob

    oa, ob = lax.map(block, (_to_blocks(qa1), _to_blocks(qa2), _to_blocks(qb)))
    return _from_blocks(oa), _from_blocks(ob)


def attn_qkv(h, w_in, q_norm_g, k_norm_g, rope):
    b, n = h.shape[:2]
    qa, ka, va, qb, kb, vb = jnp.split(h @ w_in, ATTN_SPLITS, axis=-1)
    qa = qa.reshape(b, n, DIFF_HEADS, 2, HEAD_DIM)
    ka = ka.reshape(b, n, DIFF_HEADS, 2, HEAD_DIM)
    va = va.reshape(b, n, DIFF_HEADS, DIFF_V_DIM)
    qb = rms_norm(qb.reshape(b, n, GQA_HEADS, HEAD_DIM), q_norm_g)
    kb = rms_norm(kb.reshape(b, n, GQA_KV_HEADS, HEAD_DIM), k_norm_g)
    vb = vb.reshape(b, n, GQA_KV_HEADS, HEAD_DIM)
    qa1, qa2, ka1, ka2 = qa[..., 0, :], qa[..., 1, :], ka[..., 0, :], ka[..., 1, :]
    if rope is not None:
        cos, sin = rope
        qa1, qa2, ka1, ka2, qb, kb = [apply_axial_rope(t, cos, sin) for t in (qa1, qa2, ka1, ka2, qb, kb)]
    return qa1, qa2, ka1, ka2, va, qb, kb, vb


def hybrid_attention_mixer(h, hc, w_in, w_out, lq1, lk1, lq2, lk2, subln_g, q_norm_g, k_norm_g,
                           rope, lambda_init, with_ctx):
    f32 = jnp.float32
    lam = (jnp.exp(jnp.sum(lq1.astype(f32) * lk1.astype(f32)))
           - jnp.exp(jnp.sum(lq2.astype(f32) * lk2.astype(f32))) + lambda_init)
    qa1, qa2, ka1, ka2, va, qb, kb, vb = attn_qkv(h, w_in, q_norm_g, k_norm_g, rope)
    cqa1, cqa2, cka1, cka2, cva, cqb, ckb, cvb = attn_qkv(hc, w_in, q_norm_g, k_norm_g, None)

    def cat(ctx_t, lat_t):
        return jnp.concatenate([ctx_t, lat_t], axis=1)

    def project(oa, ob):
        b, n = oa.shape[:2]
        oa = rms_norm(oa, subln_g) * (1.0 - lambda_init)
        o = jnp.concatenate([oa.reshape(b, n, DIFF_V_COLS), ob.reshape(b, n, GQA_Q_COLS)], axis=-1)
        return o @ w_out

    y = project(*attention_sweep(qa1, qa2, qb, cat(cka1, ka1), cat(cka2, ka2), cat(cva, va),
                                 cat(ckb, kb), cat(cvb, vb), lam))
    yc = project(*attention_sweep(cqa1, cqa2, cqb, cka1, cka2, cva, ckb, cvb, lam)) if with_ctx else None
    return y, yc


def ssd_scan(x, dt, bm, cm, a, state):
    b, n = x.shape[:2]
    nc = n // SSM_CHUNK
    f32 = jnp.float32

    def chunks(t):
        return jnp.moveaxis(t.astype(f32).reshape((b, nc, SSM_CHUNK) + t.shape[2:]), 1, 0)

    xs = chunks(x.reshape(b, n, SSM_GROUPS, SSM_HEADS_PER_GROUP, SSM_HEAD_DIM))
    dts = chunks(dt.reshape(b, n, SSM_GROUPS, SSM_HEADS_PER_GROUP))
    bs = chunks(bm)
    cs = chunks(cm)
    a = a.astype(f32).reshape(SSM_GROUPS, SSM_HEADS_PER_GROUP)
    lower = jnp.tril(jnp.ones((SSM_CHUNK, SSM_CHUNK), dtype=bool))[None, :, :, None, None]

    def step(s, inp):
        xc, dtc, bc, cc = inp
        cum = jnp.cumsum(dtc * a, axis=1)
        seg = cum[:, :, None] - cum[:, None, :]
        decay = jnp.exp(jnp.where(lower, seg, -jnp.inf))
        cb = jnp.einsum('bign,bjgn->bijg', cc, bc)
        w = cb[..., None] * decay * dtc[:, None]
        y = jnp.einsum('bijgr,bjgrp->bigrp', w, xc)
        y = y + jnp.einsum('bign,bgrpn->bigrp', cc, s) * jnp.exp(cum)[..., None]
        to_end = jnp.exp(cum[:, -1:] - cum) * dtc
        s = s * jnp.exp(cum[:, -1])[..., None, None] + jnp.einsum('bjgr,bjgn,bjgrp->bgrpn', to_end, bc, xc)
        return s, y

    state, ys = lax.scan(step, state.astype(f32), (xs, dts, bs, cs))
    y = jnp.moveaxis(ys, 0, 1).reshape(b, n, SSM_HEADS, SSM_HEAD_DIM)
    return y.astype(x.dtype), state


def ssm_pre(h, w_in, conv_w, conv_b):
    b, n = h.shape[:2]
    z, xbc, dt = jnp.split(h @ w_in, (SSM_D_INNER, SSM_D_INNER + SSM_XBC), axis=-1)
    xbc = jax.nn.silu(dw_conv_centred(xbc, conv_w, conv_b))
    xs, bm, cm = jnp.split(xbc, (SSM_D_INNER, SSM_D_INNER + SSM_GROUPS * SSM_STATE), axis=-1)
    return (z,
            xs.reshape(b, n, SSM_HEADS, SSM_HEAD_DIM),
            bm.reshape(b, n, SSM_GROUPS, SSM_STATE),
            cm.reshape(b, n, SSM_GROUPS, SSM_STATE),
            dt.reshape(b, n, 2, SSM_HEADS))


def bidir_ssd_mixer(h, hc, w_in, conv_w, conv_b, dt_bias, a_log, d_skip, norm_g, w_out, with_ctx):
    f32 = jnp.float32
    lat = ssm_pre(h, w_in, conv_w, conv_b)
    ctx_p = ssm_pre(hc, w_in, conv_w, conv_b)
    a = -jnp.exp(a_log.astype(f32))
    zero = jnp.zeros((h.shape[0], SSM_GROUPS, SSM_HEADS_PER_GROUP, SSM_HEAD_DIM, SSM_STATE), f32)

    def flip(t):
        return t[:, ::-1]

    def direction(d):
        rev = d == 1

        def prep(parts):
            _, xs, bm, cm, dt = parts
            dt_d = jax.nn.softplus((dt[:, :, d] + dt_bias[d]).astype(f32))
            seqs = (xs, dt_d, bm, cm)
            return tuple(flip(t) for t in seqs) if rev else seqs

        yc, s_ctx = ssd_scan(*prep(ctx_p), a[d], zero)
        yl, _ = ssd_scan(*prep(lat), a[d], s_ctx)
        if rev:
            yc, yl = flip(yc), flip(yl)
        return yl, yc

    yl_f, yc_f = direction(0)
    yl_b, yc_b = direction(1)

    def finish(parts, y):
        z, xs = parts[0], parts[1]
        b, n = xs.shape[:2]
        y = (y + d_skip[:, None] * xs).reshape(b, n, SSM_D_INNER)
        return rms_norm(y * jax.nn.silu(z), norm_g) @ w_out

    y = finish(lat, yl_f + yl_b)
    yc = finish(ctx_p, yc_f + yc_b) if with_ctx else None
    return y, yc


def conv_glu_ffn(h, w_in, conv_w, conv_b, w_out):
    val, gate = jnp.split(h @ w_in, 2, axis=-1)
    gate = dw_conv_centred(gate, conv_w, conv_b)
    return (jax.nn.gelu(gate, approximate=False) * val) @ w_out


def setup_inputs(seed: int = 0) -> dict:
    key = jax.random.key(seed)
    ks = iter(jax.random.split(key, 40))
    f32 = jnp.float32

    def nrm(shape, scale):
        return jax.random.normal(next(ks), shape, f32) * scale

    def gain(shape):
        return 1.0 + nrm(shape, 0.02)

    na, ns = N_ATTN_LAYERS, N_SSM_LAYERS
    x = nrm((BATCH, SEQ, D_MODEL), 1.0)
    c = nrm((BATCH, D_MODEL), 1.0)
    ctx = nrm((BATCH, CTX_LEN, D_MODEL), 1.0)
    c_ctx = nrm((D_MODEL,), 1.0)
    mod_w = nrm((DEPTH, D_MODEL, 6 * D_MODEL), 0.5 * D_MODEL ** -0.5)
    mod_b = nrm((DEPTH, 6 * D_MODEL), 0.02)
    norm_mix_g = gain((DEPTH, D_MODEL))
    norm_ffn_g = gain((DEPTH, D_MODEL))
    attn_w_in = nrm((na, D_MODEL, ATTN_IN_COLS), D_MODEL ** -0.5)
    attn_w_out = nrm((na, ATTN_OUT_COLS, D_MODEL), ATTN_OUT_COLS ** -0.5)
    diff_lq1 = nrm((na, HEAD_DIM), 0.1)
    diff_lk1 = nrm((na, HEAD_DIM), 0.1)
    diff_lq2 = nrm((na, HEAD_DIM), 0.1)
    diff_lk2 = nrm((na, HEAD_DIM), 0.1)
    diff_subln_g = gain((na, DIFF_V_DIM))
    gqa_q_norm_g = gain((na, HEAD_DIM))
    gqa_k_norm_g = gain((na, HEAD_DIM))
    ssm_w_in = nrm((ns, D_MODEL, SSM_IN_COLS), D_MODEL ** -0.5)
    ssm_conv_w = nrm((ns, SSM_CONV, SSM_XBC), SSM_CONV ** -0.5)
    ssm_conv_b = nrm((ns, SSM_XBC), 0.02)
    u = jax.random.uniform(next(ks), (ns, 2, SSM_HEADS), f32)
    dt0 = jnp.exp(u * (math.log(0.1) - math.log(0.001)) + math.log(0.001))
    ssm_dt_bias = dt0 + jnp.log(-jnp.expm1(-dt0))
    ssm_a_log = jnp.log(jax.random.uniform(next(ks), (ns, 2, SSM_HEADS), f32, minval=1.0, maxval=16.0))
    ssm_d = 1.0 + nrm((ns, SSM_HEADS), 0.1)
    ssm_norm_g = gain((ns, SSM_D_INNER))
    ssm_w_out = nrm((ns, SSM_D_INNER, D_MODEL), SSM_D_INNER ** -0.5)
    ffn_w_in = nrm((DEPTH, D_MODEL, 2 * D_FF), D_MODEL ** -0.5)
    ffn_conv_w = nrm((DEPTH, FFN_CONV, D_FF), FFN_CONV ** -0.5)
    ffn_conv_b = nrm((DEPTH, D_FF), 0.02)
    ffn_w_out = nrm((DEPTH, D_FF, D_MODEL), D_FF ** -0.5)
    final_norm_g = gain((D_MODEL,))
    return {'x': x, 'c': c, 'ctx': ctx, 'c_ctx': c_ctx,
            'mod_w': mod_w, 'mod_b': mod_b, 'norm_mix_g': norm_mix_g, 'norm_ffn_g': norm_ffn_g,
            'attn_w_in': attn_w_in, 'attn_w_out': attn_w_out,
            'diff_lq1': diff_lq1, 'diff_lk1': diff_lk1, 'diff_lq2': diff_lq2, 'diff_lk2': diff_lk2,
            'diff_subln_g': diff_subln_g, 'gqa_q_norm_g': gqa_q_norm_g, 'gqa_k_norm_g': gqa_k_norm_g,
            'ssm_w_in': ssm_w_in, 'ssm_conv_w': ssm_conv_w, 'ssm_conv_b': ssm_conv_b,
            'ssm_dt_bias': ssm_dt_bias, 'ssm_a_log': ssm_a_log, 'ssm_d': ssm_d,
            'ssm_norm_g': ssm_norm_g, 'ssm_w_out': ssm_w_out,
            'ffn_w_in': ffn_w_in, 'ffn_conv_w': ffn_conv_w, 'ffn_conv_b': ffn_conv_b, 'ffn_w_out': ffn_w_out,
            'final_norm_g': final_norm_g}


def reference(x, c, ctx, c_ctx, mod_w, mod_b, norm_mix_g, norm_ffn_g, attn_w_in, attn_w_out,
              diff_lq1, diff_lk1, diff_lq2, diff_lk2, diff_subln_g, gqa_q_norm_g, gqa_k_norm_g,
              ssm_w_in, ssm_conv_w, ssm_conv_b, ssm_dt_bias, ssm_a_log, ssm_d, ssm_norm_g, ssm_w_out,
              ffn_w_in, ffn_conv_w, ffn_conv_b, ffn_w_out, final_norm_g):
    rope = axial_rope_tables(x.shape[1])
    for layer in range(DEPTH):
        with_ctx = layer < DEPTH - 1
        mod = jax.nn.silu(c) @ mod_w[layer] + mod_b[layer]
        mod_c = jax.nn.silu(c_ctx) @ mod_w[layer] + mod_b[layer]
        sh1, sc1, g1, sh2, sc2, g2 = jnp.split(mod[:, None, :], 6, axis=-1)
        sh1c, sc1c, g1c, sh2c, sc2c, g2c = jnp.split(mod_c[None, None, :], 6, axis=-1)
        h = modulate(x, norm_mix_g[layer], sh1, sc1)
        hc = modulate(ctx, norm_mix_g[layer], sh1c, sc1c)
        i = layer // 2
        if layer % 2 == 0:
            lambda_init = 0.8 - 0.6 * math.exp(-0.3 * layer)
            y, yc = hybrid_attention_mixer(h, hc, attn_w_in[i], attn_w_out[i], diff_lq1[i], diff_lk1[i],
                                           diff_lq2[i], diff_lk2[i], diff_subln_g[i], gqa_q_norm_g[i],
                                           gqa_k_norm_g[i], rope, lambda_init, with_ctx)
        else:
            y, yc = bidir_ssd_mixer(h, hc, ssm_w_in[i], ssm_conv_w[i], ssm_conv_b[i], ssm_dt_bias[i],
                                    ssm_a_log[i], ssm_d[i], ssm_norm_g[i], ssm_w_out[i], with_ctx)
        x = x + g1 * y
        h = modulate(x, norm_ffn_g[layer], sh2, sc2)
        x = x + g2 * conv_glu_ffn(h, ffn_w_in[layer], ffn_conv_w[layer], ffn_conv_b[layer], ffn_w_out[layer])
        if with_ctx:
            ctx = ctx + g1c * yc
            hc = modulate(ctx, norm_ffn_g[layer], sh2c, sc2c)
            ctx = ctx + g2c * conv_glu_ffn(hc, ffn_w_in[layer], ffn_conv_w[layer], ffn_conv_b[layer],
                                           ffn_w_out[layer])
    return rms_norm(x, final_norm_g)
```

```python
import functools
import math

import jax
import jax.numpy as jnp
from jax import lax
from jax.experimental import pallas as pl
from jax.experimental.pallas import tpu as pltpu

F32 = jnp.float32
BF16 = jnp.bfloat16

D_MODEL = 1024
DEPTH = 4
GRID_W = 64
HEAD_DIM = 64
ROPE_PAIRS = HEAD_DIM // 4
ROPE_BASE = 10000.0
NORM_EPS = 1e-6
DIFF_HEADS = 4
GQA_HEADS = 8
GQA_KV_HEADS = 2
GQA_GROUP = GQA_HEADS // GQA_KV_HEADS
ATTN_IN_COLS = 2304
SSM_D_INNER = 2048
SSM_HEADS = 32
SSM_GROUPS = 4
SSM_HEADS_PER_GROUP = SSM_HEADS // SSM_GROUPS
SSM_STATE = 128
SSM_XBC = 3072
SSM_IN_COLS = 5184
D_FF = 2816
LOG2E = math.log2(math.e)

V7X_VMEM_BYTES = 64 * 1024 * 1024
VMEM_LIMIT_BYTES = V7X_VMEM_BYTES - 8 * 1024 * 1024
V7X_LANES = 128
BF16_SUBLANES = 16

V_ROWS = 2 * HEAD_DIM
V_ROWS_PADDED = V_ROWS + BF16_SUBLANES
ATTN_UNITS = DIFF_HEADS + 1
K_COLS = ATTN_UNITS * 2 * HEAD_DIM

CONV_HALO = 16
SSD_CHUNK = 256

NT_DIMS = (((1,), (1,)), ((), ()))


def _cparams(*sem):
    return pltpu.CompilerParams(dimension_semantics=sem, vmem_limit_bytes=VMEM_LIMIT_BYTES)


def _resident(shape):
    nd = len(shape)
    return pl.BlockSpec(shape, lambda *_: (0,) * nd, pipeline_mode=pl.Buffered(1))


def _silu(v):
    return v * jax.nn.sigmoid(v)


def _modulate(x, g, shift, scale):
    ms = jnp.mean(x * x, axis=-1, keepdims=True)
    return (x * lax.rsqrt(ms + NORM_EPS) * g) * (1.0 + scale) + shift


def _mod_body(c_ref, w_ref, b_ref, o_ref):
    s = _silu(c_ref[...])
    o_ref[0] = jnp.dot(s, w_ref[0], preferred_element_type=F32,
                       precision=lax.Precision.HIGHEST) + b_ref[0]


def _mod_all(c_rows, mod_w, mod_b):
    rows = c_rows.shape[0]
    depth, d, cols = mod_w.shape
    tn = 2048
    return pl.pallas_call(
        _mod_body,
        grid=(depth, cols // tn),
        in_specs=[pl.BlockSpec((rows, d), lambda l, n: (0, 0)),
                  pl.BlockSpec((1, d, tn), lambda l, n: (l, 0, n)),
                  pl.BlockSpec((1, 1, tn), lambda l, n: (l, 0, n))],
        out_specs=pl.BlockSpec((1, rows, tn), lambda l, n: (l, 0, n)),
        out_shape=jax.ShapeDtypeStruct((depth, rows, cols), F32),
        compiler_params=_cparams("parallel", "parallel"),
        name="mod_vectors",
    )(c_rows, mod_w, mod_b.reshape(depth, 1, cols))


def _attn_in_body(x_ref, mod_ref, g_ref, wt_ref, cos_ref, sin_ref, qg_ref, kg_ref,
                  qt_ref, k_ref, vt_ref, *, rope):
    m = mod_ref[0]
    h = _modulate(x_ref[0], g_ref[...], m[0:1], m[1:2]).astype(BF16)
    t = lax.dot_general(wt_ref[...], h, NT_DIMS, preferred_element_type=F32)
    tm = t.shape[1]

    def rot(u):
        if not rope:
            return u
        sw = jnp.concatenate([u[:, 16:32], u[:, 0:16], u[:, 48:64], u[:, 32:48]], axis=1)
        return u * cos_ref[...][None] + sw * sin_ref[...][None]

    def qk_norm(u, g):
        ms = jnp.mean(u * u, axis=1, keepdims=True)
        return u * lax.rsqrt(ms + NORM_EPS) * g[None]

    nq = 2 * DIFF_HEADS
    qa = rot(t[0:512].reshape(nq, HEAD_DIM, tm))
    ka = rot(t[512:1024].reshape(nq, HEAD_DIM, tm))
    va = t[1024:1536]
    qb = rot(qk_norm(t[1536:2048].reshape(GQA_HEADS, HEAD_DIM, tm), qg_ref[...]))
    kb = rot(qk_norm(t[2048:2176].reshape(GQA_KV_HEADS, HEAD_DIM, tm), kg_ref[...]))
    vb = t[2176:2304]

    qs = (HEAD_DIM ** -0.5) * LOG2E
    qt_ref[0, 0:512] = (qa * qs).reshape(512, tm).astype(BF16)
    qt_ref[0, 512:1024] = (qb * qs).reshape(512, tm).astype(BF16)
    kt = jnp.concatenate([ka.reshape(512, tm), kb.reshape(128, tm)], axis=0)
    k_ref[0] = kt.T.astype(BF16)
    ones = jnp.ones((V_ROWS_PADDED - V_ROWS, tm), BF16)
    for u in range(DIFF_HEADS):
        vt_ref[0, u, 0:V_ROWS] = va[u * V_ROWS:(u + 1) * V_ROWS].astype(BF16)
        vt_ref[0, u, V_ROWS:V_ROWS_PADDED] = ones
    vt_ref[0, DIFF_HEADS, 0:V_ROWS] = vb.astype(BF16)
    vt_ref[0, DIFF_HEADS, V_ROWS:V_ROWS_PADDED] = ones


def _attn_in(x, mod_l, mod_row, g, wt, cos_t, sin_t, qg, kg, *, rope, tm):
    b, s, d = x.shape
    tm = min(tm, s)
    body = functools.partial(_attn_in_body, rope=rope)
    return pl.pallas_call(
        body,
        grid=(b, s // tm),
        in_specs=[pl.BlockSpec((1, tm, d), lambda i, t: (i, t, 0)),
                  pl.BlockSpec((1, 6, d), lambda i, t: (mod_row(i), 0, 0)),
                  _resident((1, d)),
                  _resident((ATTN_IN_COLS, d)),
                  pl.BlockSpec((HEAD_DIM, tm), lambda i, t: (0, t)),
                  pl.BlockSpec((HEAD_DIM, tm), lambda i, t: (0, t)),
                  _resident((HEAD_DIM, 1)),
                  _resident((HEAD_DIM, 1))],
        out_specs=[pl.BlockSpec((1, 1024, tm), lambda i, t: (i, 0, t)),
                   pl.BlockSpec((1, tm, K_COLS), lambda i, t: (i, t, 0)),
                   pl.BlockSpec((1, ATTN_UNITS, V_ROWS_PADDED, tm), lambda i, t: (i, 0, 0, t))],
        out_shape=[jax.ShapeDtypeStruct((b, 1024, s), BF16),
                   jax.ShapeDtypeStruct((b, s, K_COLS), BF16),
                   jax.ShapeDtypeStruct((b, ATTN_UNITS, V_ROWS_PADDED, s), BF16)],
        compiler_params=_cparams("parallel", "parallel"),
        name="attn_in_rope" if rope else "attn_in_ctx",
    )(x, mod_l, g, wt, cos_t, sin_t, qg, kg)


def _attn_unit_cols(tq):
    units = []
    for h in range(DIFF_HEADS):
        units.append((h * 128, h, h * 2 * tq, 2 * tq))
    base = DIFF_HEADS * 2 * tq
    for g in range(GQA_KV_HEADS):
        units.append((DIFF_HEADS * 128, DIFF_HEADS, base + g * GQA_GROUP * tq, GQA_GROUP * tq))
    return units


def _flash_body(*refs, tq, lambda_init, has_lat):
    if has_lat:
        (lq1, lk1, lq2, lk2, sg_ref, qt_ref, kc_ref, vtc_ref, kl_ref, vtl_ref,
         o_ref, rhs, acc, mrow) = refs
    else:
        (lq1, lk1, lq2, lk2, sg_ref, qt_ref, kc_ref, vtc_ref,
         o_ref, rhs, acc, mrow) = refs
    j = pl.program_id(2)
    nj = pl.num_programs(2)
    units = _attn_unit_cols(tq)

    def process(k_ref, vt_ref):
        for (kc0, vu, c0, w) in units:
            k = k_ref[0, :, kc0:kc0 + 128]
            s = jnp.dot(k, rhs[:, c0:c0 + w], preferred_element_type=F32)
            mp = mrow[:, c0:c0 + w]
            mn = jnp.maximum(mp, jnp.max(s, axis=0, keepdims=True))
            alpha = jnp.exp2(mp - mn)
            p = jnp.exp2(s - mn).astype(BF16)
            pv = jnp.dot(vt_ref[0, vu], p, preferred_element_type=F32)
            acc[:, c0:c0 + w] = acc[:, c0:c0 + w] * alpha + pv
            mrow[:, c0:c0 + w] = mn

    @pl.when(j == 0)
    def _init():
        zeros = jnp.zeros((HEAD_DIM, tq), BF16)
        for h in range(DIFF_HEADS):
            c0 = h * 2 * tq
            rhs[0:64, c0:c0 + tq] = qt_ref[0, h * 128:h * 128 + 64, :]
            rhs[64:128, c0:c0 + tq] = zeros
            rhs[0:64, c0 + tq:c0 + 2 * tq] = zeros
            rhs[64:128, c0 + tq:c0 + 2 * tq] = qt_ref[0, h * 128 + 64:h * 128 + 128, :]
        base = DIFF_HEADS * 2 * tq
        for g in range(GQA_KV_HEADS):
            for r in range(GQA_GROUP):
                c0 = base + (g * GQA_GROUP + r) * tq
                hd = 512 + (g * GQA_GROUP + r) * HEAD_DIM
                rhs[g * 64:(g + 1) * 64, c0:c0 + tq] = qt_ref[0, hd:hd + HEAD_DIM, :]
                rhs[(1 - g) * 64:(2 - g) * 64, c0:c0 + tq] = zeros
        acc[...] = jnp.zeros(acc.shape, F32)
        mrow[...] = jnp.full(mrow.shape, -jnp.inf, F32)
        process(kc_ref, vtc_ref)

    if has_lat:
        process(kl_ref, vtl_ref)

    @pl.when(j == nj - 1)
    def _finish():
        lam = (jnp.exp(jnp.sum(lq1[...] * lk1[...], keepdims=True))
               - jnp.exp(jnp.sum(lq2[...] * lk2[...], keepdims=True)) + lambda_init)
        pieces = []
        for h in range(DIFF_HEADS):
            c0 = h * 2 * tq
            o1 = acc[0:V_ROWS, c0:c0 + tq] / acc[V_ROWS:V_ROWS + 1, c0:c0 + tq]
            o2 = acc[0:V_ROWS, c0 + tq:c0 + 2 * tq] / acc[V_ROWS:V_ROWS + 1, c0 + tq:c0 + 2 * tq]
            oh = o1 - lam * o2
            ms = jnp.mean(oh * oh, axis=0, keepdims=True)
            pieces.append(oh * lax.rsqrt(ms + NORM_EPS) * sg_ref[...] * (1.0 - lambda_init))
        base = DIFF_HEADS * 2 * tq
        for g in range(GQA_KV_HEADS):
            for r in range(GQA_GROUP):
                c0 = base + (g * GQA_GROUP + r) * tq
                pieces.append(acc[g * 64:(g + 1) * 64, c0:c0 + tq]
                              / acc[V_ROWS:V_ROWS + 1, c0:c0 + tq])
        ot = jnp.concatenate(pieces, axis=0)
        o_ref[0] = ot.T.astype(BF16)


def _flash(lam_vecs, sg, qt, kc, vtc, kl, vtl, *, lambda_init, tq, tk):
    b, _, sq = qt.shape
    c = kc.shape[1]
    has_lat = kl is not None
    tq = min(tq, sq)
    ncols = (DIFF_HEADS * 2 + GQA_HEADS) * tq
    in_specs = [_resident((1, HEAD_DIM))] * 4 + [
        _resident((V_ROWS, 1)),
        pl.BlockSpec((1, 1024, tq), lambda i, q, j: (i, 0, q)),
        pl.BlockSpec((1, c, K_COLS), lambda i, q, j: (i, 0, 0)),
        pl.BlockSpec((1, ATTN_UNITS, V_ROWS_PADDED, c), lambda i, q, j: (i, 0, 0, 0)),
    ]
    args = list(lam_vecs) + [sg, qt, kc, vtc]
    nkv = 1
    if has_lat:
        n = kl.shape[1]
        tk = min(tk, n)
        nkv = n // tk
        in_specs += [pl.BlockSpec((1, tk, K_COLS), lambda i, q, j: (i, j, 0)),
                     pl.BlockSpec((1, ATTN_UNITS, V_ROWS_PADDED, tk), lambda i, q, j: (i, 0, 0, j))]
        args += [kl, vtl]
    body = functools.partial(_flash_body, tq=tq, lambda_init=lambda_init, has_lat=has_lat)
    return pl.pallas_call(
        body,
        grid=(b, sq // tq, nkv),
        in_specs=in_specs,
        out_specs=pl.BlockSpec((1, tq, 1024), lambda i, q, j: (i, q, 0)),
        out_shape=jax.ShapeDtypeStruct((b, sq, 1024), BF16),
        scratch_shapes=[pltpu.VMEM((2 * HEAD_DIM, ncols), BF16),
                        pltpu.VMEM((V_ROWS_PADDED, ncols), F32),
                        pltpu.VMEM((1, ncols), F32)],
        compiler_params=_cparams("parallel", "parallel", "arbitrary"),
        name="attn_sweep_lat" if has_lat else "attn_sweep_ctx",
    )(*args)


def _proj_residual_body(x_ref, a_ref, w_ref, mod_ref, o_ref, *, gate_row):
    y = jnp.dot(a_ref[0], w_ref[...], preferred_element_type=F32)
    o_ref[0] = x_ref[0] + mod_ref[0][gate_row:gate_row + 1] * y


def _proj_residual(x, a, w, mod_l, mod_row, *, gate_row, tm):
    b, s, d = x.shape
    tm = min(tm, s)
    k = a.shape[2]
    body = functools.partial(_proj_residual_body, gate_row=gate_row)
    return pl.pallas_call(
        body,
        grid=(b, s // tm),
        in_specs=[pl.BlockSpec((1, tm, d), lambda i, t: (i, t, 0)),
                  pl.BlockSpec((1, tm, k), lambda i, t: (i, t, 0)),
                  _resident((k, d)),
                  pl.BlockSpec((1, 6, d), lambda i, t: (mod_row(i), 0, 0))],
        out_specs=pl.BlockSpec((1, tm, d), lambda i, t: (i, t, 0)),
        out_shape=jax.ShapeDtypeStruct((b, s, d), F32),
        compiler_params=_cparams("parallel", "parallel"),
        name="attn_out_residual",
    )(x, a, w, mod_l)


def _ffn_body(x_ref, xp_ref, xn_ref, mod_ref, g_ref, wv_ref, wg_ref, cw_ref, cb_ref, wo_ref,
              o_ref, *, tm):
    t = pl.program_id(1)
    nt = pl.num_programs(1)
    m = mod_ref[0]
    x = x_ref[0]
    xe = jnp.concatenate([xp_ref[0], x, xn_ref[0]], axis=0)
    he = _modulate(xe, g_ref[...], m[3:4], m[4:5]).astype(BF16)
    ge = jnp.dot(he, wg_ref[...], preferred_element_type=F32)
    rows = lax.broadcasted_iota(jnp.int32, (tm + 2 * CONV_HALO, 1), 0)
    lo = jnp.where(t > 0, 0, CONV_HALO)
    hi = jnp.where(t < nt - 1, tm + 2 * CONV_HALO, tm + CONV_HALO)
    ge = jnp.where((rows >= lo) & (rows < hi), ge, 0.0)
    val = jnp.dot(he[CONV_HALO:CONV_HALO + tm], wv_ref[...], preferred_element_type=F32)
    cw = cw_ref[...]
    h0 = CONV_HALO
    conv = (cb_ref[...] + cw[0:1] * ge[h0 - 1:h0 - 1 + tm] + cw[1:2] * ge[h0:h0 + tm]
            + cw[2:3] * ge[h0 + 1:h0 + 1 + tm])
    gelu = 0.5 * conv * (1.0 + lax.erf(conv * math.sqrt(0.5)))
    act = (gelu * val).astype(BF16)
    y = jnp.dot(act, wo_ref[...], preferred_element_type=F32)
    o_ref[0] = x + m[5:6] * y


def _ffn(x, mod_l, mod_row, g, wv, wg, cw, cb, wo, *, tm):
    b, s, d = x.shape
    tm = min(tm, s)
    hb = tm // CONV_HALO
    last = s // CONV_HALO - 1
    body = functools.partial(_ffn_body, tm=tm)
    return pl.pallas_call(
        body,
        grid=(b, s // tm),
        in_specs=[pl.BlockSpec((1, tm, d), lambda i, t: (i, t, 0)),
                  pl.BlockSpec((1, CONV_HALO, d), lambda i, t: (i, jnp.maximum(t * hb - 1, 0), 0)),
                  pl.BlockSpec((1, CONV_HALO, d), lambda i, t: (i, jnp.minimum((t + 1) * hb, last), 0)),
                  pl.BlockSpec((1, 6, d), lambda i, t: (mod_row(i), 0, 0)),
                  _resident((1, d)),
                  _resident((d, D_FF)),
                  _resident((d, D_FF)),
                  _resident((3, D_FF)),
                  _resident((1, D_FF)),
                  _resident((D_FF, d))],
        out_specs=pl.BlockSpec((1, tm, d), lambda i, t: (i, t, 0)),
        out_shape=jax.ShapeDtypeStruct((b, s, d), F32),
        compiler_params=_cparams("parallel", "parallel"),
        name="conv_glu_ffn",
    )(x, x, x, mod_l, g, wv, wg, cw, cb, wo)


def _ssm_proj_body(x_ref, mod_ref, g_ref, wt_ref, dtb_ref, zt_ref, xbct_ref, dtt_ref):
    m = mod_ref[0]
    h = _modulate(x_ref[0], g_ref[...], m[0:1], m[1:2]).astype(BF16)
    t = lax.dot_general(wt_ref[...], h, NT_DIMS, preferred_element_type=F32)
    zt_ref[0] = t[0:SSM_D_INNER].astype(BF16)
    xbct_ref[0] = t[SSM_D_INNER:SSM_D_INNER + SSM_XBC].astype(BF16)
    dtt_ref[0] = jax.nn.softplus(t[SSM_D_INNER + SSM_XBC:SSM_IN_COLS] + dtb_ref[...])


def _ssm_proj(x, mod_l, mod_row, g, wt, dtb, *, tm):
    b, s, d = x.shape
    tm = min(tm, s)
    return pl.pallas_call(
        _ssm_proj_body,
        grid=(b, s // tm),
        in_specs=[pl.BlockSpec((1, tm, d), lambda i, t: (i, t, 0)),
                  pl.BlockSpec((1, 6, d), lambda i, t: (mod_row(i), 0, 0)),
                  _resident((1, d)),
                  _resident((SSM_IN_COLS, d)),
                  _resident((2 * SSM_HEADS, 1))],
        out_specs=[pl.BlockSpec((1, SSM_D_INNER, tm), lambda i, t: (i, 0, t)),
                   pl.BlockSpec((1, SSM_XBC, tm), lambda i, t: (i, 0, t)),
                   pl.BlockSpec((1, 2 * SSM_HEADS, tm), lambda i, t: (i, 0, t))],
        out_shape=[jax.ShapeDtypeStruct((b, SSM_D_INNER, s), BF16),
                   jax.ShapeDtypeStruct((b, SSM_XBC, s), BF16),
                   jax.ShapeDtypeStruct((b, 2 * SSM_HEADS, s), F32)],
        compiler_params=_cparams("parallel", "parallel"),
        name="ssm_in_proj",
    )(x, mod_l, g, wt, dtb)


def _ssm_conv_body(u_ref, up_ref, un_ref, w_ref, b_ref, xst_ref, ct_ref, bm_ref, *, tc):
    t = pl.program_id(1)
    nt = pl.num_programs(1)
    u = u_ref[0].astype(F32)
    prev = jnp.where(t > 0, up_ref[0][:, V7X_LANES - 1:V7X_LANES].astype(F32), 0.0)
    nxt = jnp.where(t < nt - 1, un_ref[0][:, 0:1].astype(F32), 0.0)
    lane = lax.broadcasted_iota(jnp.int32, (1, tc), 1)
    left = jnp.where(lane == 0, prev, pltpu.roll(u, 1, 1))
    right = jnp.where(lane == tc - 1, nxt, pltpu.roll(u, tc - 1, 1))
    w = w_ref[...]
    v = _silu(b_ref[...] + w[:, 0:1] * left + w[:, 1:2] * u + w[:, 2:3] * right)
    bn = SSM_GROUPS * SSM_STATE
    xst_ref[0] = v[0:SSM_D_INNER].astype(BF16)
    bm_ref[0] = v[SSM_D_INNER:SSM_D_INNER + bn].T.astype(BF16)
    ct_ref[0] = v[SSM_D_INNER + bn:SSM_XBC].astype(BF16)


def _ssm_conv(xbct, w, bias, *, tc):
    b, ch, s = xbct.shape
    tc = min(tc, s)
    hb = tc // V7X_LANES
    last = s // V7X_LANES - 1
    bn = SSM_GROUPS * SSM_STATE
    body = functools.partial(_ssm_conv_body, tc=tc)
    return pl.pallas_call(
        body,
        grid=(b, s // tc),
        in_specs=[pl.BlockSpec((1, ch, tc), lambda i, t: (i, 0, t)),
                  pl.BlockSpec((1, ch, V7X_LANES), lambda i, t: (i, 0, jnp.maximum(t * hb - 1, 0))),
                  pl.BlockSpec((1, ch, V7X_LANES), lambda i, t: (i, 0, jnp.minimum((t + 1) * hb, last))),
                  _resident((ch, 3)),
                  _resident((ch, 1))],
        out_specs=[pl.BlockSpec((1, SSM_D_INNER, tc), lambda i, t: (i, 0, t)),
                   pl.BlockSpec((1, bn, tc), lambda i, t: (i, 0, t)),
                   pl.BlockSpec((1, tc, bn), lambda i, t: (i, t, 0))],
        out_shape=[jax.ShapeDtypeStruct((b, SSM_D_INNER, s), BF16),
                   jax.ShapeDtypeStruct((b, bn, s), BF16),
                   jax.ShapeDtypeStruct((b, s, bn), BF16)],
        compiler_params=_cparams("parallel", "parallel"),
        name="ssm_conv_silu",
    )(xbct, xbct, xbct, w, bias)


def _dot_f32_by_01(a, m01):
    hi = a.astype(BF16)
    r1 = a - hi.astype(F32)
    mid = r1.astype(BF16)
    lo = (r1 - mid.astype(F32)).astype(BF16)
    return (jnp.dot(hi, m01, preferred_element_type=F32)
            + jnp.dot(mid, m01, preferred_element_type=F32)
            + jnp.dot(lo, m01, preferred_element_type=F32))


def _scan_body(a_ref, dt_ref, xs_ref, ct_ref, b_ref, s0_ref, y_ref, sout_ref, state, *, chunk):
    d = pl.program_id(1)
    c = pl.program_id(2)
    nc = pl.num_programs(2)
    hp = SSM_HEADS_PER_GROUP
    hd = SSM_D_INNER // SSM_HEADS
    gw = hp * hd

    @pl.when(c == 0)
    def _load_state():
        state[...] = s0_ref[0, 0]

    dt = dt_ref[0]
    da = dt * a_ref[0]
    jrow = lax.broadcasted_iota(jnp.int32, (chunk, chunk), 0)
    icol = lax.broadcasted_iota(jnp.int32, (chunk, chunk), 1)
    sign = jnp.where(d == 0, 1, -1)
    mask = (icol - jrow) * sign >= 0
    cum_t = _dot_f32_by_01(da, mask.astype(BF16))
    cum = cum_t.T
    tot = jnp.where(d == 0, cum_t[:, chunk - 1:chunk], cum_t[:, 0:1])
    to_end = jnp.exp(tot - cum_t) * dt
    ecum = jnp.exp(cum_t)
    etot = jnp.exp(tot)

    for g in range(SSM_GROUPS):
        bg = b_ref[0, :, g * SSM_STATE:(g + 1) * SSM_STATE]
        ctg = ct_ref[0, g * SSM_STATE:(g + 1) * SSM_STATE, :]
        cbt = jnp.dot(bg, ctg, preferred_element_type=F32)
        xg = xs_ref[0, g * gw:(g + 1) * gw, :].astype(F32)
        sg = state[g * gw:(g + 1) * gw, :]
        hs = slice(g * hp, (g + 1) * hp)
        y_state = (jnp.dot(sg.astype(BF16), ctg, preferred_element_type=F32)
                   .reshape(hp, hd, chunk) * ecum[hs][:, None, :])
        outs = []
        for r in range(hp):
            h = g * hp + r
            seg = cum_t[h:h + 1, :] - cum[:, h:h + 1]
            w = (cbt * jnp.exp(jnp.where(mask, seg, -jnp.inf))).astype(BF16)
            xdt = (xg[r * hd:(r + 1) * hd] * dt[h:h + 1]).astype(BF16)
            outs.append(jnp.dot(xdt, w, preferred_element_type=F32) + y_state[r])
        y_ref[0, 0, g * gw:(g + 1) * gw, :] = jnp.concatenate(outs, axis=0).astype(BF16)
        xw = (xg.reshape(hp, hd, chunk) * to_end[hs][:, None, :]).reshape(gw, chunk).astype(BF16)
        upd = jnp.dot(xw, bg, preferred_element_type=F32)
        decayed = (sg.reshape(hp, hd, SSM_STATE) * etot[hs][:, None, :]).reshape(gw, SSM_STATE)
        state[g * gw:(g + 1) * gw, :] = decayed + upd

    @pl.when(c == nc - 1)
    def _store_state():
        sout_ref[0, 0] = state[...]


def _ssd_scan(a, dtt, xst, ct, bm, s0):
    b, ch, s = xst.shape
    chunk = min(SSD_CHUNK, s)
    nc = s // chunk
    bn = SSM_GROUPS * SSM_STATE

    def cidx(d, c):
        return c + d * (nc - 1 - 2 * c)

    body = functools.partial(_scan_body, chunk=chunk)
    return pl.pallas_call(
        body,
        grid=(b, 2, nc),
        in_specs=[pl.BlockSpec((1, SSM_HEADS, 1), lambda i, d, c: (d, 0, 0)),
                  pl.BlockSpec((1, SSM_HEADS, chunk), lambda i, d, c: (i, d, cidx(d, c))),
                  pl.BlockSpec((1, ch, chunk), lambda i, d, c: (i, 0, cidx(d, c))),
                  pl.BlockSpec((1, bn, chunk), lambda i, d, c: (i, 0, cidx(d, c))),
                  pl.BlockSpec((1, chunk, bn), lambda i, d, c: (i, cidx(d, c), 0)),
                  pl.BlockSpec((1, 1, ch, SSM_STATE), lambda i, d, c: (d, i, 0, 0))],
        out_specs=[pl.BlockSpec((1, 1, ch, chunk), lambda i, d, c: (d, i, 0, cidx(d, c))),
                   pl.BlockSpec((1, 1, ch, SSM_STATE), lambda i, d, c: (d, i, 0, 0))],
        out_shape=[jax.ShapeDtypeStruct((2, b, ch, s), BF16),
                   jax.ShapeDtypeStruct((2, b, ch, SSM_STATE), F32)],
        scratch_shapes=[pltpu.VMEM((ch, SSM_STATE), F32)],
        compiler_params=_cparams("parallel", "parallel", "arbitrary"),
        name="ssd_scan",
    )(a, dtt, xst, ct, bm, s0)


def _ssm_out_body(x_ref, mod_ref, yf_ref, yb_ref, xs_ref, z_ref, dsk_ref, ng_ref, wt_ref, o_ref):
    y = (yf_ref[0, 0].astype(F32) + yb_ref[0, 0].astype(F32)
         + dsk_ref[...] * xs_ref[0].astype(F32))
    y = y * _silu(z_ref[0].astype(F32))
    ms = jnp.mean(y * y, axis=0, keepdims=True)
    yn = (y * lax.rsqrt(ms + NORM_EPS) * ng_ref[...]).astype(BF16)
    ot = jnp.dot(wt_ref[...], yn, preferred_element_type=F32)
    o_ref[0] = x_ref[0] + mod_ref[0][2:3] * ot.T


def _ssm_out(x, mod_l, mod_row, y, xst, zt, dsk, ng, wt, *, tm):
    b, s, d = x.shape
    tm = min(tm, s)
    ch = xst.shape[1]
    return pl.pallas_call(
        _ssm_out_body,
        grid=(b, s // tm),
        in_specs=[pl.BlockSpec((1, tm, d), lambda i, t: (i, t, 0)),
                  pl.BlockSpec((1, 6, d), lambda i, t: (mod_row(i), 0, 0)),
                  pl.BlockSpec((1, 1, ch, tm), lambda i, t: (0, i, 0, t)),
                  pl.BlockSpec((1, 1, ch, tm), lambda i, t: (1, i, 0, t)),
                  pl.BlockSpec((1, ch, tm), lambda i, t: (i, 0, t)),
                  pl.BlockSpec((1, ch, tm), lambda i, t: (i, 0, t)),
                  _resident((ch, 1)),
                  _resident((ch, 1)),
                  _resident((d, ch))],
        out_specs=pl.BlockSpec((1, tm, d), lambda i, t: (i, t, 0)),
        out_shape=jax.ShapeDtypeStruct((b, s, d), F32),
        compiler_params=_cparams("parallel", "parallel"),
        name="ssm_out_residual",
    )(x, mod_l, y, y, xst, zt, dsk, ng, wt)


def _final_norm_body(x_ref, g_ref, o_ref):
    x = x_ref[0]
    ms = jnp.mean(x * x, axis=-1, keepdims=True)
    o_ref[0] = x * lax.rsqrt(ms + NORM_EPS) * g_ref[...]


def _final_norm(x, g, *, tm):
    b, s, d = x.shape
    tm = min(tm, s)
    return pl.pallas_call(
        _final_norm_body,
        grid=(b, s // tm),
        in_specs=[pl.BlockSpec((1, tm, d), lambda i, t: (i, t, 0)), _resident((1, d))],
        out_specs=pl.BlockSpec((1, tm, d), lambda i, t: (i, t, 0)),
        out_shape=jax.ShapeDtypeStruct((b, s, d), F32),
        compiler_params=_cparams("parallel", "parallel"),
        name="final_rmsnorm",
    )(x, g)


def _rope_tables_t(n):
    t = jnp.arange(n)
    inv_freq = 1.0 / (ROPE_BASE ** (jnp.arange(ROPE_PAIRS, dtype=F32) / ROPE_PAIRS))
    ang_r = (t // GRID_W).astype(F32)[None, :] * inv_freq[:, None]
    ang_c = (t % GRID_W).astype(F32)[None, :] * inv_freq[:, None]
    cr, sr, cc, sc = jnp.cos(ang_r), jnp.sin(ang_r), jnp.cos(ang_c), jnp.sin(ang_c)
    return (jnp.concatenate([cr, cr, cc, cc], axis=0),
            jnp.concatenate([-sr, sr, -sc, sc], axis=0))


def _attn_layer(x, ctx, mod_l, lat_row, ctx_row, p, rope, lambda_init, with_ctx):
    wt = p["w_in"].T.astype(BF16)
    g = p["norm_g"].reshape(1, D_MODEL)
    qg = p["q_norm_g"].reshape(HEAD_DIM, 1)
    kg = p["k_norm_g"].reshape(HEAD_DIM, 1)
    cos_t, sin_t = rope
    c = ctx.shape[1]
    qt, kl, vtl = _attn_in(x, mod_l, lat_row, g, wt, cos_t, sin_t, qg, kg, rope=True, tm=512)
    qtc, kc, vtc = _attn_in(ctx, mod_l, ctx_row, g, wt, cos_t[:, :c], sin_t[:, :c], qg, kg,
                            rope=False, tm=256)
    lam_vecs = [p[k].reshape(1, HEAD_DIM) for k in ("lq1", "lk1", "lq2", "lk2")]
    sg = p["subln_g"].reshape(V_ROWS, 1)
    w_out = p["w_out"].astype(BF16)
    o = _flash(lam_vecs, sg, qt, kc, vtc, kl, vtl, lambda_init=lambda_init, tq=256, tk=1024)
    x = _proj_residual(x, o, w_out, mod_l, lat_row, gate_row=2, tm=512)
    if with_ctx:
        oc = _flash(lam_vecs, sg, qtc, kc, vtc, None, None, lambda_init=lambda_init, tq=256, tk=0)
        ctx = _proj_residual(ctx, oc, w_out, mod_l, ctx_row, gate_row=2, tm=256)
    return x, ctx


def _ssm_layer(x, ctx, mod_l, lat_row, ctx_row, p, with_ctx):
    wt = p["w_in"].T.astype(BF16)
    g = p["norm_g"].reshape(1, D_MODEL)
    dtb = p["dt_bias"].reshape(2 * SSM_HEADS, 1)
    conv_w = p["conv_w"].T
    conv_b = p["conv_b"].reshape(SSM_XBC, 1)
    a = (-jnp.exp(p["a_log"].astype(F32))).reshape(2, SSM_HEADS, 1)
    b = x.shape[0]

    def pre(v, row, tm):
        zt, xbct, dtt = _ssm_proj(v, mod_l, row, g, wt, dtb, tm=tm)
        xst, ct, bm = _ssm_conv(xbct, conv_w, conv_b, tc=512)
        return zt, xst, ct, bm, dtt

    zt_c, xst_c, ct_c, bm_c, dtt_c = pre(ctx, ctx_row, 256)
    zt_l, xst_l, ct_l, bm_l, dtt_l = pre(x, lat_row, 512)
    zero = jnp.zeros((2, b, SSM_D_INNER, SSM_STATE), F32)
    y_c, s_ctx = _ssd_scan(a, dtt_c, xst_c, ct_c, bm_c, zero)
    y_l, _ = _ssd_scan(a, dtt_l, xst_l, ct_l, bm_l, s_ctx)
    dsk = jnp.repeat(p["d_skip"], SSM_D_INNER // SSM_HEADS).reshape(SSM_D_INNER, 1)
    ng = p["out_norm_g"].reshape(SSM_D_INNER, 1)
    w_out_t = p["w_out"].T.astype(BF16)
    x = _ssm_out(x, mod_l, lat_row, y_l, xst_l, zt_l, dsk, ng, w_out_t, tm=512)
    if with_ctx:
        ctx = _ssm_out(ctx, mod_l, ctx_row, y_c, xst_c, zt_c, dsk, ng, w_out_t, tm=256)
    return x, ctx


def kernel(x, c, ctx, c_ctx, mod_w, mod_b, norm_mix_g, norm_ffn_g, attn_w_in, attn_w_out,
           diff_lq1, diff_lk1, diff_lq2, diff_lk2, diff_subln_g, gqa_q_norm_g, gqa_k_norm_g,
           ssm_w_in, ssm_conv_w, ssm_conv_b, ssm_dt_bias, ssm_a_log, ssm_d, ssm_norm_g, ssm_w_out,
           ffn_w_in, ffn_conv_w, ffn_conv_b, ffn_w_out, final_norm_g):
    b, n, d = x.shape
    mod_rows = 16
    c_rows = jnp.zeros((mod_rows, d), F32).at[:b].set(c).at[b].set(c_ctx)
    mod = _mod_all(c_rows, mod_w, mod_b).reshape(DEPTH, mod_rows, 6, d)
    lat_row = lambda i: i
    ctx_row = lambda i: b
    rope = _rope_tables_t(n)

    for layer in range(DEPTH):
        with_ctx = layer < DEPTH - 1
        mod_l = mod[layer]
        i = layer // 2
        if layer % 2 == 0:
            p = dict(w_in=attn_w_in[i], w_out=attn_w_out[i], norm_g=norm_mix_g[layer],
                     lq1=diff_lq1[i], lk1=diff_lk1[i], lq2=diff_lq2[i], lk2=diff_lk2[i],
                     subln_g=diff_subln_g[i], q_norm_g=gqa_q_norm_g[i], k_norm_g=gqa_k_norm_g[i])
            lambda_init = 0.8 - 0.6 * math.exp(-0.3 * layer)
            x, ctx = _attn_layer(x, ctx, mod_l, lat_row, ctx_row, p, rope, lambda_init, with_ctx)
        else:
            p = dict(w_in=ssm_w_in[i], norm_g=norm_mix_g[layer], conv_w=ssm_conv_w[i],
                     conv_b=ssm_conv_b[i], dt_bias=ssm_dt_bias[i], a_log=ssm_a_log[i],
                     d_skip=ssm_d[i], out_norm_g=ssm_norm_g[i], w_out=ssm_w_out[i])
            x, ctx = _ssm_layer(x, ctx, mod_l, lat_row, ctx_row, p, with_ctx)
        g = norm_ffn_g[layer].reshape(1, d)
        wv = ffn_w_in[layer][:, :D_FF].astype(BF16)
        wg = ffn_w_in[layer][:, D_FF:].astype(BF16)
        cw = ffn_conv_w[layer]
        cb = ffn_conv_b[layer].reshape(1, D_FF)
        wo = ffn_w_out[layer].astype(BF16)
        x = _ffn(x, mod_l, lat_row, g, wv, wg, cw, cb, wo, tm=256)
        if with_ctx:
            ctx = _ffn(ctx, mod_l, ctx_row, g, wv, wg, cw, cb, wo, tm=256)
    return _final_norm(x, final_norm_g.reshape(1, d), tm=512)
```

```python
import functools
import math

import jax
import jax.numpy as jnp
from jax import lax
from jax.experimental import pallas as pl
from jax.experimental.pallas import tpu as pltpu

F32 = jnp.float32
BF16 = jnp.bfloat16

D_MODEL = 1024
DEPTH = 4
GRID_W = 64
HEAD_DIM = 64
ROPE_PAIRS = HEAD_DIM // 4
ROPE_BASE = 10000.0
NORM_EPS = 1e-6
DIFF_HEADS = 4
GQA_HEADS = 8
GQA_KV_HEADS = 2
GQA_GROUP = GQA_HEADS // GQA_KV_HEADS
ATTN_IN_COLS = 2304
SSM_D_INNER = 2048
SSM_HEADS = 32
SSM_GROUPS = 4
SSM_HEADS_PER_GROUP = SSM_HEADS // SSM_GROUPS
SSM_STATE = 128
SSM_XBC = 3072
SSM_IN_COLS = 5184
D_FF = 2816
LOG2E = math.log2(math.e)

V7X_VMEM_BYTES = 64 * 1024 * 1024
VMEM_LIMIT_BYTES = V7X_VMEM_BYTES - 8 * 1024 * 1024
V7X_LANES = 128
BF16_SUBLANES = 16

V_ROWS = 2 * HEAD_DIM
V_ROWS_PADDED = V_ROWS + BF16_SUBLANES
ATTN_UNITS = DIFF_HEADS + 1
K_COLS = ATTN_UNITS * 2 * HEAD_DIM

ATTN_KEY_SUBBLOCK = 256
ATTN_STREAM_MAX_OCTAVES = 64.0

CONV_HALO = 16
SSD_CHUNK = 256

NT_DIMS = (((1,), (1,)), ((), ()))


def _cparams(*sem):
    return pltpu.CompilerParams(dimension_semantics=sem, vmem_limit_bytes=VMEM_LIMIT_BYTES)


def _resident(shape):
    nd = len(shape)
    return pl.BlockSpec(shape, lambda *_: (0,) * nd, pipeline_mode=pl.Buffered(1))


def _silu(v):
    return v * jax.nn.sigmoid(v)


def _modulate(x, g, shift, scale):
    ms = jnp.mean(x * x, axis=-1, keepdims=True)
    return (x * lax.rsqrt(ms + NORM_EPS) * g) * (1.0 + scale) + shift


def _mod_body(c_ref, w_ref, b_ref, o_ref):
    s = _silu(c_ref[...])
    o_ref[0] = jnp.dot(s, w_ref[0], preferred_element_type=F32,
                       precision=lax.Precision.HIGHEST) + b_ref[0]


def _mod_all(c_rows, mod_w, mod_b):
    rows = c_rows.shape[0]
    depth, d, cols = mod_w.shape
    tn = 2048
    return pl.pallas_call(
        _mod_body,
        grid=(depth, cols // tn),
        in_specs=[pl.BlockSpec((rows, d), lambda l, n: (0, 0)),
                  pl.BlockSpec((1, d, tn), lambda l, n: (l, 0, n)),
                  pl.BlockSpec((1, 1, tn), lambda l, n: (l, 0, n))],
        out_specs=pl.BlockSpec((1, rows, tn), lambda l, n: (l, 0, n)),
        out_shape=jax.ShapeDtypeStruct((depth, rows, cols), F32),
        compiler_params=_cparams("parallel", "parallel"),
        name="mod_vectors",
    )(c_rows, mod_w, mod_b.reshape(depth, 1, cols))


def _attn_in_body(x_ref, mod_ref, g_ref, wt_ref, cos_ref, sin_ref, qg_ref, kg_ref,
                  qt_ref, k_ref, vt_ref, *, rope):
    m = mod_ref[0]
    h = _modulate(x_ref[0], g_ref[...], m[0:1], m[1:2]).astype(BF16)
    t = lax.dot_general(wt_ref[...], h, NT_DIMS, preferred_element_type=F32)
    tm = t.shape[1]

    def rot(u):
        if not rope:
            return u
        sw = jnp.concatenate([u[:, 16:32], u[:, 0:16], u[:, 48:64], u[:, 32:48]], axis=1)
        return u * cos_ref[...][None] + sw * sin_ref[...][None]

    def qk_norm(u, g):
        ms = jnp.mean(u * u, axis=1, keepdims=True)
        return u * lax.rsqrt(ms + NORM_EPS) * g[None]

    nq = 2 * DIFF_HEADS
    qa = rot(t[0:512].reshape(nq, HEAD_DIM, tm))
    ka = rot(t[512:1024].reshape(nq, HEAD_DIM, tm))
    va = t[1024:1536]
    qb = rot(qk_norm(t[1536:2048].reshape(GQA_HEADS, HEAD_DIM, tm), qg_ref[...]))
    kb = rot(qk_norm(t[2048:2176].reshape(GQA_KV_HEADS, HEAD_DIM, tm), kg_ref[...]))
    vb = t[2176:2304]

    qs = (HEAD_DIM ** -0.5) * LOG2E
    qt_ref[0, 0:512] = (qa * qs).reshape(512, tm).astype(BF16)
    qt_ref[0, 512:1024] = (qb * qs).reshape(512, tm).astype(BF16)
    kt = jnp.concatenate([ka.reshape(512, tm), kb.reshape(128, tm)], axis=0)
    k_ref[0] = kt.T.astype(BF16)
    ones = jnp.ones((V_ROWS_PADDED - V_ROWS, tm), BF16)
    for u in range(DIFF_HEADS):
        vt_ref[0, u, 0:V_ROWS] = va[u * V_ROWS:(u + 1) * V_ROWS].astype(BF16)
        vt_ref[0, u, V_ROWS:V_ROWS_PADDED] = ones
    vt_ref[0, DIFF_HEADS, 0:V_ROWS] = vb.astype(BF16)
    vt_ref[0, DIFF_HEADS, V_ROWS:V_ROWS_PADDED] = ones


def _attn_in(x, mod_l, mod_row, g, wt, cos_t, sin_t, qg, kg, *, rope, tm):
    b, s, d = x.shape
    tm = min(tm, s)
    body = functools.partial(_attn_in_body, rope=rope)
    return pl.pallas_call(
        body,
        grid=(b, s // tm),
        in_specs=[pl.BlockSpec((1, tm, d), lambda i, t: (i, t, 0)),
                  pl.BlockSpec((1, 6, d), lambda i, t: (mod_row(i), 0, 0)),
                  _resident((1, d)),
                  _resident((ATTN_IN_COLS, d)),
                  pl.BlockSpec((HEAD_DIM, tm), lambda i, t: (0, t)),
                  pl.BlockSpec((HEAD_DIM, tm), lambda i, t: (0, t)),
                  _resident((HEAD_DIM, 1)),
                  _resident((HEAD_DIM, 1))],
        out_specs=[pl.BlockSpec((1, 1024, tm), lambda i, t: (i, 0, t)),
                   pl.BlockSpec((1, tm, K_COLS), lambda i, t: (i, t, 0)),
                   pl.BlockSpec((1, ATTN_UNITS, V_ROWS_PADDED, tm), lambda i, t: (i, 0, 0, t))],
        out_shape=[jax.ShapeDtypeStruct((b, 1024, s), BF16),
                   jax.ShapeDtypeStruct((b, s, K_COLS), BF16),
                   jax.ShapeDtypeStruct((b, ATTN_UNITS, V_ROWS_PADDED, s), BF16)],
        compiler_params=_cparams("parallel", "parallel"),
        name="attn_in_rope" if rope else "attn_in_ctx",
    )(x, mod_l, g, wt, cos_t, sin_t, qg, kg)


def _attn_unit_cols(tq):
    units = []
    for h in range(DIFF_HEADS):
        units.append((h * 128, h, h * 2 * tq, 2 * tq))
    base = DIFF_HEADS * 2 * tq
    for g in range(GQA_KV_HEADS):
        units.append((DIFF_HEADS * 128, DIFF_HEADS, base + g * GQA_GROUP * tq, GQA_GROUP * tq))
    return units


def _flash_body(*refs, tq, lambda_init, has_lat):
    if has_lat:
        (lq1, lk1, lq2, lk2, sg_ref, qt_ref, kc_ref, vtc_ref, kl_ref, vtl_ref,
         o_ref, rhs, acc, mrow, pv_new, m_chunk) = refs
    else:
        (lq1, lk1, lq2, lk2, sg_ref, qt_ref, kc_ref, vtc_ref,
         o_ref, rhs, acc, mrow) = refs
    j = pl.program_id(2)
    nj = pl.num_programs(2)
    units = _attn_unit_cols(tq)

    def exact_step(k_ref, vt_ref):
        for (kc0, vu, c0, w) in units:
            s = jnp.dot(k_ref[0, :, kc0:kc0 + 128], rhs[:, c0:c0 + w],
                        preferred_element_type=F32)
            mp = mrow[:, c0:c0 + w]
            mn = jnp.maximum(mp, jnp.max(s, axis=0, keepdims=True))
            alpha = jnp.exp2(mp - mn)
            p = jnp.exp2(s - mn).astype(BF16)
            pv = jnp.dot(vt_ref[0, vu], p, preferred_element_type=F32)
            acc[:, c0:c0 + w] = acc[:, c0:c0 + w] * alpha + pv
            mrow[:, c0:c0 + w] = mn

    def streaming_step(k_ref, vt_ref):
        nk = k_ref.shape[1]
        sub = min(nk, ATTN_KEY_SUBBLOCK)
        for (kc0, vu, c0, w) in units:
            m_used = mrow[:, c0:c0 + w]
            cm = None
            parts = []
            for r in range(nk // sub):
                s = jnp.dot(k_ref[0, r * sub:(r + 1) * sub, kc0:kc0 + 128], rhs[:, c0:c0 + w],
                            preferred_element_type=F32)
                parts.append(jnp.exp2(s - m_used).astype(BF16))
                sm = jnp.max(s.reshape(sub // 8, 8, w), axis=0)
                cm = sm if cm is None else jnp.maximum(cm, sm)
            p = jnp.concatenate(parts, axis=0)
            pv_new[:, c0:c0 + w] = jnp.dot(vt_ref[0, vu], p, preferred_element_type=F32)
            m_chunk[:, c0:c0 + w] = jnp.max(cm, axis=0, keepdims=True)

    @pl.when(j == 0)
    def _init():
        zeros = jnp.zeros((HEAD_DIM, tq), BF16)
        for h in range(DIFF_HEADS):
            c0 = h * 2 * tq
            rhs[0:64, c0:c0 + tq] = qt_ref[0, h * 128:h * 128 + 64, :]
            rhs[64:128, c0:c0 + tq] = zeros
            rhs[0:64, c0 + tq:c0 + 2 * tq] = zeros
            rhs[64:128, c0 + tq:c0 + 2 * tq] = qt_ref[0, h * 128 + 64:h * 128 + 128, :]
        base = DIFF_HEADS * 2 * tq
        for g in range(GQA_KV_HEADS):
            for r in range(GQA_GROUP):
                c0 = base + (g * GQA_GROUP + r) * tq
                hd = 512 + (g * GQA_GROUP + r) * HEAD_DIM
                rhs[g * 64:(g + 1) * 64, c0:c0 + tq] = qt_ref[0, hd:hd + HEAD_DIM, :]
                rhs[(1 - g) * 64:(2 - g) * 64, c0:c0 + tq] = zeros
        acc[...] = jnp.zeros(acc.shape, F32)
        mrow[...] = jnp.full(mrow.shape, -jnp.inf, F32)
        exact_step(kc_ref, vtc_ref)

    if has_lat:
        streaming_step(kl_ref, vtl_ref)
        excess = jnp.max(m_chunk[...] - mrow[...])
        in_range = excess <= ATTN_STREAM_MAX_OCTAVES

        @pl.when(in_range)
        def _commit():
            mp = mrow[...]
            mn = jnp.maximum(mp, m_chunk[...])
            acc[...] = (acc[...] + pv_new[...]) * jnp.exp2(mp - mn)
            mrow[...] = mn

        @pl.when(jnp.logical_not(in_range))
        def _redo():
            exact_step(kl_ref, vtl_ref)

    @pl.when(j == nj - 1)
    def _finish():
        lam = (jnp.exp(jnp.sum(lq1[...] * lk1[...], keepdims=True))
               - jnp.exp(jnp.sum(lq2[...] * lk2[...], keepdims=True)) + lambda_init)
        pieces = []
        for h in range(DIFF_HEADS):
            c0 = h * 2 * tq
            o1 = acc[0:V_ROWS, c0:c0 + tq] / acc[V_ROWS:V_ROWS + 1, c0:c0 + tq]
            o2 = acc[0:V_ROWS, c0 + tq:c0 + 2 * tq] / acc[V_ROWS:V_ROWS + 1, c0 + tq:c0 + 2 * tq]
            oh = o1 - lam * o2
            ms = jnp.mean(oh * oh, axis=0, keepdims=True)
            pieces.append(oh * lax.rsqrt(ms + NORM_EPS) * sg_ref[...] * (1.0 - lambda_init))
        base = DIFF_HEADS * 2 * tq
        for g in range(GQA_KV_HEADS):
            for r in range(GQA_GROUP):
                c0 = base + (g * GQA_GROUP + r) * tq
                pieces.append(acc[g * 64:(g + 1) * 64, c0:c0 + tq]
                              / acc[V_ROWS:V_ROWS + 1, c0:c0 + tq])
        ot = jnp.concatenate(pieces, axis=0)
        o_ref[0] = ot.T.astype(BF16)


def _flash(lam_vecs, sg, qt, kc, vtc, kl, vtl, *, lambda_init, tq, tk):
    b, _, sq = qt.shape
    c = kc.shape[1]
    has_lat = kl is not None
    tq = min(tq, sq)
    ncols = (DIFF_HEADS * 2 + GQA_HEADS) * tq
    in_specs = [_resident((1, HEAD_DIM))] * 4 + [
        _resident((V_ROWS, 1)),
        pl.BlockSpec((1, 1024, tq), lambda i, q, j: (i, 0, q)),
        pl.BlockSpec((1, c, K_COLS), lambda i, q, j: (i, 0, 0)),
        pl.BlockSpec((1, ATTN_UNITS, V_ROWS_PADDED, c), lambda i, q, j: (i, 0, 0, 0)),
    ]
    args = list(lam_vecs) + [sg, qt, kc, vtc]
    nkv = 1
    if has_lat:
        n = kl.shape[1]
        tk = min(tk, n)
        nkv = n // tk
        in_specs += [pl.BlockSpec((1, tk, K_COLS), lambda i, q, j: (i, j, 0)),
                     pl.BlockSpec((1, ATTN_UNITS, V_ROWS_PADDED, tk), lambda i, q, j: (i, 0, 0, j))]
        args += [kl, vtl]
    scratch = [pltpu.VMEM((2 * HEAD_DIM, ncols), BF16),
               pltpu.VMEM((V_ROWS_PADDED, ncols), F32),
               pltpu.VMEM((1, ncols), F32)]
    if has_lat:
        scratch += [pltpu.VMEM((V_ROWS_PADDED, ncols), F32), pltpu.VMEM((1, ncols), F32)]
    body = functools.partial(_flash_body, tq=tq, lambda_init=lambda_init, has_lat=has_lat)
    return pl.pallas_call(
        body,
        grid=(b, sq // tq, nkv),
        in_specs=in_specs,
        out_specs=pl.BlockSpec((1, tq, 1024), lambda i, q, j: (i, q, 0)),
        out_shape=jax.ShapeDtypeStruct((b, sq, 1024), BF16),
        scratch_shapes=scratch,
        compiler_params=_cparams("parallel", "parallel", "arbitrary"),
        name="attn_sweep_lat" if has_lat else "attn_sweep_ctx",
    )(*args)


def _proj_residual_body(x_ref, a_ref, w_ref, mod_ref, o_ref, *, gate_row):
    y = jnp.dot(a_ref[0], w_ref[...], preferred_element_type=F32)
    o_ref[0] = x_ref[0] + mod_ref[0][gate_row:gate_row + 1] * y


def _proj_residual(x, a, w, mod_l, mod_row, *, gate_row, tm):
    b, s, d = x.shape
    tm = min(tm, s)
    k = a.shape[2]
    body = functools.partial(_proj_residual_body, gate_row=gate_row)
    return pl.pallas_call(
        body,
        grid=(b, s // tm),
        in_specs=[pl.BlockSpec((1, tm, d), lambda i, t: (i, t, 0)),
                  pl.BlockSpec((1, tm, k), lambda i, t: (i, t, 0)),
                  _resident((k, d)),
                  pl.BlockSpec((1, 6, d), lambda i, t: (mod_row(i), 0, 0))],
        out_specs=pl.BlockSpec((1, tm, d), lambda i, t: (i, t, 0)),
        out_shape=jax.ShapeDtypeStruct((b, s, d), F32),
        compiler_params=_cparams("parallel", "parallel"),
        name="attn_out_residual",
    )(x, a, w, mod_l)


def _ffn_body(x_ref, xp_ref, xn_ref, mod_ref, g_ref, wv_ref, wg_ref, cw_ref, cb_ref, wo_ref,
              o_ref, *, tm):
    t = pl.program_id(1)
    nt = pl.num_programs(1)
    m = mod_ref[0]
    x = x_ref[0]
    xe = jnp.concatenate([xp_ref[0], x, xn_ref[0]], axis=0)
    he = _modulate(xe, g_ref[...], m[3:4], m[4:5]).astype(BF16)
    ge = jnp.dot(he, wg_ref[...], preferred_element_type=F32)
    rows = lax.broadcasted_iota(jnp.int32, (tm + 2 * CONV_HALO, 1), 0)
    lo = jnp.where(t > 0, 0, CONV_HALO)
    hi = jnp.where(t < nt - 1, tm + 2 * CONV_HALO, tm + CONV_HALO)
    ge = jnp.where((rows >= lo) & (rows < hi), ge, 0.0)
    val = jnp.dot(he[CONV_HALO:CONV_HALO + tm], wv_ref[...], preferred_element_type=F32)
    cw = cw_ref[...]
    h0 = CONV_HALO
    conv = (cb_ref[...] + cw[0:1] * ge[h0 - 1:h0 - 1 + tm] + cw[1:2] * ge[h0:h0 + tm]
            + cw[2:3] * ge[h0 + 1:h0 + 1 + tm])
    gelu = 0.5 * conv * (1.0 + lax.erf(conv * math.sqrt(0.5)))
    act = (gelu * val).astype(BF16)
    y = jnp.dot(act, wo_ref[...], preferred_element_type=F32)
    o_ref[0] = x + m[5:6] * y


def _ffn(x, mod_l, mod_row, g, wv, wg, cw, cb, wo, *, tm):
    b, s, d = x.shape
    tm = min(tm, s)
    hb = tm // CONV_HALO
    last = s // CONV_HALO - 1
    body = functools.partial(_ffn_body, tm=tm)
    return pl.pallas_call(
        body,
        grid=(b, s // tm),
        in_specs=[pl.BlockSpec((1, tm, d), lambda i, t: (i, t, 0)),
                  pl.BlockSpec((1, CONV_HALO, d), lambda i, t: (i, jnp.maximum(t * hb - 1, 0), 0)),
                  pl.BlockSpec((1, CONV_HALO, d), lambda i, t: (i, jnp.minimum((t + 1) * hb, last), 0)),
                  pl.BlockSpec((1, 6, d), lambda i, t: (mod_row(i), 0, 0)),
                  _resident((1, d)),
                  _resident((d, D_FF)),
                  _resident((d, D_FF)),
                  _resident((3, D_FF)),
                  _resident((1, D_FF)),
                  _resident((D_FF, d))],
        out_specs=pl.BlockSpec((1, tm, d), lambda i, t: (i, t, 0)),
        out_shape=jax.ShapeDtypeStruct((b, s, d), F32),
        compiler_params=_cparams("parallel", "parallel"),
        name="conv_glu_ffn",
    )(x, x, x, mod_l, g, wv, wg, cw, cb, wo)


def _ssm_proj_body(x_ref, mod_ref, g_ref, wt_ref, dtb_ref, zt_ref, xbct_ref, dtt_ref):
    m = mod_ref[0]
    h = _modulate(x_ref[0], g_ref[...], m[0:1], m[1:2]).astype(BF16)
    t = lax.dot_general(wt_ref[...], h, NT_DIMS, preferred_element_type=F32)
    zt_ref[0] = t[0:SSM_D_INNER].astype(BF16)
    xbct_ref[0] = t[SSM_D_INNER:SSM_D_INNER + SSM_XBC].astype(BF16)
    dtt_ref[0] = jax.nn.softplus(t[SSM_D_INNER + SSM_XBC:SSM_IN_COLS] + dtb_ref[...])


def _ssm_proj(x, mod_l, mod_row, g, wt, dtb, *, tm):
    b, s, d = x.shape
    tm = min(tm, s)
    return pl.pallas_call(
        _ssm_proj_body,
        grid=(b, s // tm),
        in_specs=[pl.BlockSpec((1, tm, d), lambda i, t: (i, t, 0)),
                  pl.BlockSpec((1, 6, d), lambda i, t: (mod_row(i), 0, 0)),
                  _resident((1, d)),
                  _resident((SSM_IN_COLS, d)),
                  _resident((2 * SSM_HEADS, 1))],
        out_specs=[pl.BlockSpec((1, SSM_D_INNER, tm), lambda i, t: (i, 0, t)),
                   pl.BlockSpec((1, SSM_XBC, tm), lambda i, t: (i, 0, t)),
                   pl.BlockSpec((1, 2 * SSM_HEADS, tm), lambda i, t: (i, 0, t))],
        out_shape=[jax.ShapeDtypeStruct((b, SSM_D_INNER, s), BF16),
                   jax.ShapeDtypeStruct((b, SSM_XBC, s), BF16),
                   jax.ShapeDtypeStruct((b, 2 * SSM_HEADS, s), F32)],
        compiler_params=_cparams("parallel", "parallel"),
        name="ssm_in_proj",
    )(x, mod_l, g, wt, dtb)


def _ssm_conv_body(u_ref, up_ref, un_ref, w_ref, b_ref, xst_ref, ct_ref, bm_ref, *, tc):
    t = pl.program_id(1)
    nt = pl.num_programs(1)
    u = u_ref[0].astype(F32)
    prev = jnp.where(t > 0, up_ref[0][:, V7X_LANES - 1:V7X_LANES].astype(F32), 0.0)
    nxt = jnp.where(t < nt - 1, un_ref[0][:, 0:1].astype(F32), 0.0)
    lane = lax.broadcasted_iota(jnp.int32, (1, tc), 1)
    left = jnp.where(lane == 0, prev, pltpu.roll(u, 1, 1))
    right = jnp.where(lane == tc - 1, nxt, pltpu.roll(u, tc - 1, 1))
    w = w_ref[...]
    v = _silu(b_ref[...] + w[:, 0:1] * left + w[:, 1:2] * u + w[:, 2:3] * right)
    bn = SSM_GROUPS * SSM_STATE
    xst_ref[0] = v[0:SSM_D_INNER].astype(BF16)
    bm_ref[0] = v[SSM_D_INNER:SSM_D_INNER + bn].T.astype(BF16)
    ct_ref[0] = v[SSM_D_INNER + bn:SSM_XBC].astype(BF16)


def _ssm_conv(xbct, w, bias, *, tc):
    b, ch, s = xbct.shape
    tc = min(tc, s)
    hb = tc // V7X_LANES
    last = s // V7X_LANES - 1
    bn = SSM_GROUPS * SSM_STATE
    body = functools.partial(_ssm_conv_body, tc=tc)
    return pl.pallas_call(
        body,
        grid=(b, s // tc),
        in_specs=[pl.BlockSpec((1, ch, tc), lambda i, t: (i, 0, t)),
                  pl.BlockSpec((1, ch, V7X_LANES), lambda i, t: (i, 0, jnp.maximum(t * hb - 1, 0))),
                  pl.BlockSpec((1, ch, V7X_LANES), lambda i, t: (i, 0, jnp.minimum((t + 1) * hb, last))),
                  _resident((ch, 3)),
                  _resident((ch, 1))],
        out_specs=[pl.BlockSpec((1, SSM_D_INNER, tc), lambda i, t: (i, 0, t)),
                   pl.BlockSpec((1, bn, tc), lambda i, t: (i, 0, t)),
                   pl.BlockSpec((1, tc, bn), lambda i, t: (i, t, 0))],
        out_shape=[jax.ShapeDtypeStruct((b, SSM_D_INNER, s), BF16),
                   jax.ShapeDtypeStruct((b, bn, s), BF16),
                   jax.ShapeDtypeStruct((b, s, bn), BF16)],
        compiler_params=_cparams("parallel", "parallel"),
        name="ssm_conv_silu",
    )(xbct, xbct, xbct, w, bias)


def _dot_f32_by_01(a, m01):
    hi = a.astype(BF16)
    r1 = a - hi.astype(F32)
    mid = r1.astype(BF16)
    lo = (r1 - mid.astype(F32)).astype(BF16)
    return (jnp.dot(hi, m01, preferred_element_type=F32)
            + jnp.dot(mid, m01, preferred_element_type=F32)
            + jnp.dot(lo, m01, preferred_element_type=F32))


def _scan_body(a_ref, dt_ref, xs_ref, ct_ref, b_ref, s0_ref, y_ref, sout_ref, state, *, chunk):
    d = pl.program_id(1)
    c = pl.program_id(2)
    nc = pl.num_programs(2)
    hp = SSM_HEADS_PER_GROUP
    hd = SSM_D_INNER // SSM_HEADS
    gw = hp * hd

    @pl.when(c == 0)
    def _load_state():
        state[...] = s0_ref[0, 0]

    dt = dt_ref[0]
    da = dt * a_ref[0]
    jrow = lax.broadcasted_iota(jnp.int32, (chunk, chunk), 0)
    icol = lax.broadcasted_iota(jnp.int32, (chunk, chunk), 1)
    sign = jnp.where(d == 0, 1, -1)
    mask = (icol - jrow) * sign >= 0
    cum_t = _dot_f32_by_01(da, mask.astype(BF16))
    cum = cum_t.T
    tot = jnp.where(d == 0, cum_t[:, chunk - 1:chunk], cum_t[:, 0:1])
    to_end = jnp.exp(tot - cum_t) * dt
    ecum = jnp.exp(cum_t)
    etot = jnp.exp(tot)

    for g in range(SSM_GROUPS):
        bg = b_ref[0, :, g * SSM_STATE:(g + 1) * SSM_STATE]
        ctg = ct_ref[0, g * SSM_STATE:(g + 1) * SSM_STATE, :]
        cbt = jnp.dot(bg, ctg, preferred_element_type=F32)
        xg = xs_ref[0, g * gw:(g + 1) * gw, :].astype(F32)
        sg = state[g * gw:(g + 1) * gw, :]
        hs = slice(g * hp, (g + 1) * hp)
        y_state = (jnp.dot(sg.astype(BF16), ctg, preferred_element_type=F32)
                   .reshape(hp, hd, chunk) * ecum[hs][:, None, :])
        outs = []
        for r in range(hp):
            h = g * hp + r
            seg = cum_t[h:h + 1, :] - cum[:, h:h + 1]
            w = (cbt * jnp.exp(jnp.where(mask, seg, -jnp.inf))).astype(BF16)
            xdt = (xg[r * hd:(r + 1) * hd] * dt[h:h + 1]).astype(BF16)
            outs.append(jnp.dot(xdt, w, preferred_element_type=F32) + y_state[r])
        y_ref[0, 0, g * gw:(g + 1) * gw, :] = jnp.concatenate(outs, axis=0).astype(BF16)
        xw = (xg.reshape(hp, hd, chunk) * to_end[hs][:, None, :]).reshape(gw, chunk).astype(BF16)
        upd = jnp.dot(xw, bg, preferred_element_type=F32)
        decayed = (sg.reshape(hp, hd, SSM_STATE) * etot[hs][:, None, :]).reshape(gw, SSM_STATE)
        state[g * gw:(g + 1) * gw, :] = decayed + upd

    @pl.when(c == nc - 1)
    def _store_state():
        sout_ref[0, 0] = state[...]


def _ssd_scan(a, dtt, xst, ct, bm, s0):
    b, ch, s = xst.shape
    chunk = min(SSD_CHUNK, s)
    nc = s // chunk
    bn = SSM_GROUPS * SSM_STATE

    def cidx(d, c):
        return c + d * (nc - 1 - 2 * c)

    body = functools.partial(_scan_body, chunk=chunk)
    return pl.pallas_call(
        body,
        grid=(b, 2, nc),
        in_specs=[pl.BlockSpec((1, SSM_HEADS, 1), lambda i, d, c: (d, 0, 0)),
                  pl.BlockSpec((1, SSM_HEADS, chunk), lambda i, d, c: (i, d, cidx(d, c))),
                  pl.BlockSpec((1, ch, chunk), lambda i, d, c: (i, 0, cidx(d, c))),
                  pl.BlockSpec((1, bn, chunk), lambda i, d, c: (i, 0, cidx(d, c))),
                  pl.BlockSpec((1, chunk, bn), lambda i, d, c: (i, cidx(d, c), 0)),
                  pl.BlockSpec((1, 1, ch, SSM_STATE), lambda i, d, c: (d, i, 0, 0))],
        out_specs=[pl.BlockSpec((1, 1, ch, chunk), lambda i, d, c: (d, i, 0, cidx(d, c))),
                   pl.BlockSpec((1, 1, ch, SSM_STATE), lambda i, d, c: (d, i, 0, 0))],
        out_shape=[jax.ShapeDtypeStruct((2, b, ch, s), BF16),
                   jax.ShapeDtypeStruct((2, b, ch, SSM_STATE), F32)],
        scratch_shapes=[pltpu.VMEM((ch, SSM_STATE), F32)],
        compiler_params=_cparams("parallel", "parallel", "arbitrary"),
        name="ssd_scan",
    )(a, dtt, xst, ct, bm, s0)


def _ssm_out_body(x_ref, mod_ref, yf_ref, yb_ref, xs_ref, z_ref, dsk_ref, ng_ref, wt_ref, o_ref):
    y = (yf_ref[0, 0].astype(F32) + yb_ref[0, 0].astype(F32)
         + dsk_ref[...] * xs_ref[0].astype(F32))
    y = y * _silu(z_ref[0].astype(F32))
    ms = jnp.mean(y * y, axis=0, keepdims=True)
    yn = (y * lax.rsqrt(ms + NORM_EPS) * ng_ref[...]).astype(BF16)
    ot = jnp.dot(wt_ref[...], yn, preferred_element_type=F32)
    o_ref[0] = x_ref[0] + mod_ref[0][2:3] * ot.T


def _ssm_out(x, mod_l, mod_row, y, xst, zt, dsk, ng, wt, *, tm):
    b, s, d = x.shape
    tm = min(tm, s)
    ch = xst.shape[1]
    return pl.pallas_call(
        _ssm_out_body,
        grid=(b, s // tm),
        in_specs=[pl.BlockSpec((1, tm, d), lambda i, t: (i, t, 0)),
                  pl.BlockSpec((1, 6, d), lambda i, t: (mod_row(i), 0, 0)),
                  pl.BlockSpec((1, 1, ch, tm), lambda i, t: (0, i, 0, t)),
                  pl.BlockSpec((1, 1, ch, tm), lambda i, t: (1, i, 0, t)),
                  pl.BlockSpec((1, ch, tm), lambda i, t: (i, 0, t)),
                  pl.BlockSpec((1, ch, tm), lambda i, t: (i, 0, t)),
                  _resident((ch, 1)),
                  _resident((ch, 1)),
                  _resident((d, ch))],
        out_specs=pl.BlockSpec((1, tm, d), lambda i, t: (i, t, 0)),
        out_shape=jax.ShapeDtypeStruct((b, s, d), F32),
        compiler_params=_cparams("parallel", "parallel"),
        name="ssm_out_residual",
    )(x, mod_l, y, y, xst, zt, dsk, ng, wt)


def _final_norm_body(x_ref, g_ref, o_ref):
    x = x_ref[0]
    ms = jnp.mean(x * x, axis=-1, keepdims=True)
    o_ref[0] = x * lax.rsqrt(ms + NORM_EPS) * g_ref[...]


def _final_norm(x, g, *, tm):
    b, s, d = x.shape
    tm = min(tm, s)
    return pl.pallas_call(
        _final_norm_body,
        grid=(b, s // tm),
        in_specs=[pl.BlockSpec((1, tm, d), lambda i, t: (i, t, 0)), _resident((1, d))],
        out_specs=pl.BlockSpec((1, tm, d), lambda i, t: (i, t, 0)),
        out_shape=jax.ShapeDtypeStruct((b, s, d), F32),
        compiler_params=_cparams("parallel", "parallel"),
        name="final_rmsnorm",
    )(x, g)


def _rope_tables_t(n):
    t = jnp.arange(n)
    inv_freq = 1.0 / (ROPE_BASE ** (jnp.arange(ROPE_PAIRS, dtype=F32) / ROPE_PAIRS))
    ang_r = (t // GRID_W).astype(F32)[None, :] * inv_freq[:, None]
    ang_c = (t % GRID_W).astype(F32)[None, :] * inv_freq[:, None]
    cr, sr, cc, sc = jnp.cos(ang_r), jnp.sin(ang_r), jnp.cos(ang_c), jnp.sin(ang_c)
    return (jnp.concatenate([cr, cr, cc, cc], axis=0),
            jnp.concatenate([-sr, sr, -sc, sc], axis=0))


def _attn_layer(x, ctx, mod_l, lat_row, ctx_row, p, rope, lambda_init, with_ctx):
    wt = p["w_in"].T.astype(BF16)
    g = p["norm_g"].reshape(1, D_MODEL)
    qg = p["q_norm_g"].reshape(HEAD_DIM, 1)
    kg = p["k_norm_g"].reshape(HEAD_DIM, 1)
    cos_t, sin_t = rope
    c = ctx.shape[1]
    qt, kl, vtl = _attn_in(x, mod_l, lat_row, g, wt, cos_t, sin_t, qg, kg, rope=True, tm=512)
    qtc, kc, vtc = _attn_in(ctx, mod_l, ctx_row, g, wt, cos_t[:, :c], sin_t[:, :c], qg, kg,
                            rope=False, tm=256)
    lam_vecs = [p[k].reshape(1, HEAD_DIM) for k in ("lq1", "lk1", "lq2", "lk2")]
    sg = p["subln_g"].reshape(V_ROWS, 1)
    w_out = p["w_out"].astype(BF16)
    o = _flash(lam_vecs, sg, qt, kc, vtc, kl, vtl, lambda_init=lambda_init, tq=256, tk=1024)
    x = _proj_residual(x, o, w_out, mod_l, lat_row, gate_row=2, tm=512)
    if with_ctx:
        oc = _flash(lam_vecs, sg, qtc, kc, vtc, None, None, lambda_init=lambda_init, tq=256, tk=0)
        ctx = _proj_residual(ctx, oc, w_out, mod_l, ctx_row, gate_row=2, tm=256)
    return x, ctx


def _ssm_layer(x, ctx, mod_l, lat_row, ctx_row, p, with_ctx):
    wt = p["w_in"].T.astype(BF16)
    g = p["norm_g"].reshape(1, D_MODEL)
    dtb = p["dt_bias"].reshape(2 * SSM_HEADS, 1)
    conv_w = p["conv_w"].T
    conv_b = p["conv_b"].reshape(SSM_XBC, 1)
    a = (-jnp.exp(p["a_log"].astype(F32))).reshape(2, SSM_HEADS, 1)
    b = x.shape[0]

    def pre(v, row, tm):
        zt, xbct, dtt = _ssm_proj(v, mod_l, row, g, wt, dtb, tm=tm)
        xst, ct, bm = _ssm_conv(xbct, conv_w, conv_b, tc=512)
        return zt, xst, ct, bm, dtt

    zt_c, xst_c, ct_c, bm_c, dtt_c = pre(ctx, ctx_row, 256)
    zt_l, xst_l, ct_l, bm_l, dtt_l = pre(x, lat_row, 512)
    zero = jnp.zeros((2, b, SSM_D_INNER, SSM_STATE), F32)
    y_c, s_ctx = _ssd_scan(a, dtt_c, xst_c, ct_c, bm_c, zero)
    y_l, _ = _ssd_scan(a, dtt_l, xst_l, ct_l, bm_l, s_ctx)
    dsk = jnp.repeat(p["d_skip"], SSM_D_INNER // SSM_HEADS).reshape(SSM_D_INNER, 1)
    ng = p["out_norm_g"].reshape(SSM_D_INNER, 1)
    w_out_t = p["w_out"].T.astype(BF16)
    x = _ssm_out(x, mod_l, lat_row, y_l, xst_l, zt_l, dsk, ng, w_out_t, tm=512)
    if with_ctx:
        ctx = _ssm_out(ctx, mod_l, ctx_row, y_c, xst_c, zt_c, dsk, ng, w_out_t, tm=256)
    return x, ctx


def kernel(x, c, ctx, c_ctx, mod_w, mod_b, norm_mix_g, norm_ffn_g, attn_w_in, attn_w_out,
           diff_lq1, diff_lk1, diff_lq2, diff_lk2, diff_subln_g, gqa_q_norm_g, gqa_k_norm_g,
           ssm_w_in, ssm_conv_w, ssm_conv_b, ssm_dt_bias, ssm_a_log, ssm_d, ssm_norm_g, ssm_w_out,
           ffn_w_in, ffn_conv_w, ffn_conv_b, ffn_w_out, final_norm_g):
    b, n, d = x.shape
    mod_rows = 16
    c_rows = jnp.zeros((mod_rows, d), F32).at[:b].set(c).at[b].set(c_ctx)
    mod = _mod_all(c_rows, mod_w, mod_b).reshape(DEPTH, mod_rows, 6, d)
    lat_row = lambda i: i
    ctx_row = lambda i: b
    rope = _rope_tables_t(n)

    for layer in range(DEPTH):
        with_ctx = layer < DEPTH - 1
        mod_l = mod[layer]
        i = layer // 2
        if layer % 2 == 0:
            p = dict(w_in=attn_w_in[i], w_out=attn_w_out[i], norm_g=norm_mix_g[layer],
                     lq1=diff_lq1[i], lk1=diff_lk1[i], lq2=diff_lq2[i], lk2=diff_lk2[i],
                     subln_g=diff_subln_g[i], q_norm_g=gqa_q_norm_g[i], k_norm_g=gqa_k_norm_g[i])
            lambda_init = 0.8 - 0.6 * math.exp(-0.3 * layer)
            x, ctx = _attn_layer(x, ctx, mod_l, lat_row, ctx_row, p, rope, lambda_init, with_ctx)
        else:
            p = dict(w_in=ssm_w_in[i], norm_g=norm_mix_g[layer], conv_w=ssm_conv_w[i],
                     conv_b=ssm_conv_b[i], dt_bias=ssm_dt_bias[i], a_log=ssm_a_log[i],
                     d_skip=ssm_d[i], out_norm_g=ssm_norm_g[i], w_out=ssm_w_out[i])
            x, ctx = _ssm_layer(x, ctx, mod_l, lat_row, ctx_row, p, with_ctx)
        g = norm_ffn_g[layer].reshape(1, d)
        wv = ffn_w_in[layer][:, :D_FF].astype(BF16)
        wg = ffn_w_in[layer][:, D_FF:].astype(BF16)
        cw = ffn_conv_w[layer]
        cb = ffn_conv_b[layer].reshape(1, D_FF)
        wo = ffn_w_out[layer].astype(BF16)
        x = _ffn(x, mod_l, lat_row, g, wv, wg, cw, cb, wo, tm=256)
        if with_ctx:
            ctx = _ffn(ctx, mod_l, ctx_row, g, wv, wg, cw, cb, wo, tm=256)
    return _final_norm(x, final_norm_g.reshape(1, d), tm=512)
```

```python
import functools
import math

import jax
import jax.numpy as jnp
from jax import lax
from jax.experimental import pallas as pl
from jax.experimental.pallas import tpu as pltpu

F32 = jnp.float32
BF16 = jnp.bfloat16

D_MODEL = 1024
DEPTH = 4
GRID_W = 64
HEAD_DIM = 64
ROPE_PAIRS = HEAD_DIM // 4
ROPE_BASE = 10000.0
NORM_EPS = 1e-6
DIFF_HEADS = 4
GQA_HEADS = 8
GQA_KV_HEADS = 2
GQA_GROUP = GQA_HEADS // GQA_KV_HEADS
ATTN_IN_COLS = 2304
SSM_D_INNER = 2048
SSM_HEADS = 32
SSM_GROUPS = 4
SSM_HEADS_PER_GROUP = SSM_HEADS // SSM_GROUPS
SSM_STATE = 128
SSM_XBC = 3072
SSM_IN_COLS = 5184
D_FF = 2816
LOG2E = math.log2(math.e)

V7X_VMEM_BYTES = 64 * 1024 * 1024
VMEM_LIMIT_BYTES = V7X_VMEM_BYTES - 8 * 1024 * 1024
V7X_LANES = 128
BF16_SUBLANES = 16

V_ROWS = 2 * HEAD_DIM
V_ROWS_PADDED = V_ROWS + BF16_SUBLANES
ATTN_UNITS = DIFF_HEADS + 1
K_COLS = ATTN_UNITS * 2 * HEAD_DIM

ATTN_KEY_SUBBLOCK = 256
ATTN_STREAM_MAX_OCTAVES = 64.0

CONV_HALO = 16
SSD_CHUNK = 256

NT_DIMS = (((1,), (1,)), ((), ()))


def _cparams(*sem):
    return pltpu.CompilerParams(dimension_semantics=sem, vmem_limit_bytes=VMEM_LIMIT_BYTES)


def _resident(shape):
    nd = len(shape)
    return pl.BlockSpec(shape, lambda *_: (0,) * nd, pipeline_mode=pl.Buffered(1))


def _silu(v):
    return v * jax.nn.sigmoid(v)


def _modulate(x, g, shift, scale):
    ms = jnp.mean(x * x, axis=-1, keepdims=True)
    return (x * lax.rsqrt(ms + NORM_EPS) * g) * (1.0 + scale) + shift


def _mod_body(c_ref, w_ref, b_ref, o_ref):
    s = _silu(c_ref[...])
    o_ref[0] = jnp.dot(s, w_ref[0], preferred_element_type=F32,
                       precision=lax.Precision.HIGHEST) + b_ref[0]


def _mod_all(c_rows, mod_w, mod_b):
    rows = c_rows.shape[0]
    depth, d, cols = mod_w.shape
    tn = 2048
    return pl.pallas_call(
        _mod_body,
        grid=(depth, cols // tn),
        in_specs=[pl.BlockSpec((rows, d), lambda l, n: (0, 0)),
                  pl.BlockSpec((1, d, tn), lambda l, n: (l, 0, n)),
                  pl.BlockSpec((1, 1, tn), lambda l, n: (l, 0, n))],
        out_specs=pl.BlockSpec((1, rows, tn), lambda l, n: (l, 0, n)),
        out_shape=jax.ShapeDtypeStruct((depth, rows, cols), F32),
        compiler_params=_cparams("parallel", "parallel"),
        name="mod_vectors",
    )(c_rows, mod_w, mod_b.reshape(depth, 1, cols))


def _attn_in_body(x_ref, mod_ref, g_ref, wt_ref, cos_ref, sin_ref, qg_ref, kg_ref,
                  qt_ref, k_ref, vt_ref, *, rope):
    m = mod_ref[0]
    h = _modulate(x_ref[0], g_ref[...], m[0:1], m[1:2]).astype(BF16)
    t = lax.dot_general(wt_ref[...], h, NT_DIMS, preferred_element_type=F32)
    tm = t.shape[1]

    def rot(u):
        if not rope:
            return u
        sw = jnp.concatenate([u[:, 16:32], u[:, 0:16], u[:, 48:64], u[:, 32:48]], axis=1)
        return u * cos_ref[...][None] + sw * sin_ref[...][None]

    def qk_norm(u, g):
        ms = jnp.mean(u * u, axis=1, keepdims=True)
        return u * lax.rsqrt(ms + NORM_EPS) * g[None]

    nq = 2 * DIFF_HEADS
    qa = rot(t[0:512].reshape(nq, HEAD_DIM, tm))
    ka = rot(t[512:1024].reshape(nq, HEAD_DIM, tm))
    va = t[1024:1536]
    qb = rot(qk_norm(t[1536:2048].reshape(GQA_HEADS, HEAD_DIM, tm), qg_ref[...]))
    kb = rot(qk_norm(t[2048:2176].reshape(GQA_KV_HEADS, HEAD_DIM, tm), kg_ref[...]))
    vb = t[2176:2304]

    qs = (HEAD_DIM ** -0.5) * LOG2E
    qt_ref[0, 0:512] = (qa * qs).reshape(512, tm).astype(BF16)
    qt_ref[0, 512:1024] = (qb * qs).reshape(512, tm).astype(BF16)
    kt = jnp.concatenate([ka.reshape(512, tm), kb.reshape(128, tm)], axis=0)
    k_ref[0] = kt.T.astype(BF16)
    ones = jnp.ones((V_ROWS_PADDED - V_ROWS, tm), BF16)
    for u in range(DIFF_HEADS):
        vt_ref[0, u, 0:V_ROWS] = va[u * V_ROWS:(u + 1) * V_ROWS].astype(BF16)
        vt_ref[0, u, V_ROWS:V_ROWS_PADDED] = ones
    vt_ref[0, DIFF_HEADS, 0:V_ROWS] = vb.astype(BF16)
    vt_ref[0, DIFF_HEADS, V_ROWS:V_ROWS_PADDED] = ones


def _attn_in(x, mod_l, mod_row, g, wt, cos_t, sin_t, qg, kg, *, rope, tm):
    b, s, d = x.shape
    tm = min(tm, s)
    body = functools.partial(_attn_in_body, rope=rope)
    return pl.pallas_call(
        body,
        grid=(b, s // tm),
        in_specs=[pl.BlockSpec((1, tm, d), lambda i, t: (i, t, 0)),
                  pl.BlockSpec((1, 6, d), lambda i, t: (mod_row(i), 0, 0)),
                  _resident((1, d)),
                  _resident((ATTN_IN_COLS, d)),
                  pl.BlockSpec((HEAD_DIM, tm), lambda i, t: (0, t)),
                  pl.BlockSpec((HEAD_DIM, tm), lambda i, t: (0, t)),
                  _resident((HEAD_DIM, 1)),
                  _resident((HEAD_DIM, 1))],
        out_specs=[pl.BlockSpec((1, 1024, tm), lambda i, t: (i, 0, t)),
                   pl.BlockSpec((1, tm, K_COLS), lambda i, t: (i, t, 0)),
                   pl.BlockSpec((1, ATTN_UNITS, V_ROWS_PADDED, tm), lambda i, t: (i, 0, 0, t))],
        out_shape=[jax.ShapeDtypeStruct((b, 1024, s), BF16),
                   jax.ShapeDtypeStruct((b, s, K_COLS), BF16),
                   jax.ShapeDtypeStruct((b, ATTN_UNITS, V_ROWS_PADDED, s), BF16)],
        compiler_params=_cparams("parallel", "parallel"),
        name="attn_in_rope" if rope else "attn_in_ctx",
    )(x, mod_l, g, wt, cos_t, sin_t, qg, kg)


def _attn_unit_cols(tq):
    units = []
    for h in range(DIFF_HEADS):
        units.append((h * 128, h, h * 2 * tq, 2 * tq))
    base = DIFF_HEADS * 2 * tq
    for g in range(GQA_KV_HEADS):
        units.append((DIFF_HEADS * 128, DIFF_HEADS, base + g * GQA_GROUP * tq, GQA_GROUP * tq))
    return units


def _flash_body(*refs, tq, lambda_init, has_lat):
    if has_lat:
        (lq1, lk1, lq2, lk2, sg_ref, qt_ref, kc_ref, vtc_ref, kl_ref, vtl_ref,
         o_ref, rhs, acc, mrow, pv_new, m_chunk) = refs
    else:
        (lq1, lk1, lq2, lk2, sg_ref, qt_ref, kc_ref, vtc_ref,
         o_ref, rhs, acc, mrow) = refs
    j = pl.program_id(2)
    nj = pl.num_programs(2)
    units = _attn_unit_cols(tq)

    def exact_step(k_ref, vt_ref):
        for (kc0, vu, c0, w) in units:
            s = jnp.dot(k_ref[0, :, kc0:kc0 + 128], rhs[:, c0:c0 + w],
                        preferred_element_type=F32)
            mp = mrow[:, c0:c0 + w]
            mn = jnp.maximum(mp, jnp.max(s, axis=0, keepdims=True))
            alpha = jnp.exp2(mp - mn)
            p = jnp.exp2(s - mn).astype(BF16)
            pv = jnp.dot(vt_ref[0, vu], p, preferred_element_type=F32)
            acc[:, c0:c0 + w] = acc[:, c0:c0 + w] * alpha + pv
            mrow[:, c0:c0 + w] = mn

    def streaming_step(k_ref, vt_ref):
        nk = k_ref.shape[1]
        sub = min(nk, ATTN_KEY_SUBBLOCK)
        for (kc0, vu, c0, w) in units:
            m_used = mrow[:, c0:c0 + w]
            cm = None
            parts = []
            for r in range(nk // sub):
                s = jnp.dot(k_ref[0, r * sub:(r + 1) * sub, kc0:kc0 + 128], rhs[:, c0:c0 + w],
                            preferred_element_type=F32)
                parts.append(jnp.exp2(s - m_used).astype(BF16))
                sm = jnp.max(s.reshape(sub // 8, 8, w), axis=0)
                cm = sm if cm is None else jnp.maximum(cm, sm)
            p = jnp.concatenate(parts, axis=0)
            pv_new[:, c0:c0 + w] = jnp.dot(vt_ref[0, vu], p, preferred_element_type=F32)
            m_chunk[:, c0:c0 + w] = jnp.max(cm, axis=0, keepdims=True)

    @pl.when(j == 0)
    def _init():
        zeros = jnp.zeros((HEAD_DIM, tq), BF16)
        for h in range(DIFF_HEADS):
            c0 = h * 2 * tq
            rhs[0:64, c0:c0 + tq] = qt_ref[0, h * 128:h * 128 + 64, :]
            rhs[64:128, c0:c0 + tq] = zeros
            rhs[0:64, c0 + tq:c0 + 2 * tq] = zeros
            rhs[64:128, c0 + tq:c0 + 2 * tq] = qt_ref[0, h * 128 + 64:h * 128 + 128, :]
        base = DIFF_HEADS * 2 * tq
        for g in range(GQA_KV_HEADS):
            for r in range(GQA_GROUP):
                c0 = base + (g * GQA_GROUP + r) * tq
                hd = 512 + (g * GQA_GROUP + r) * HEAD_DIM
                rhs[g * 64:(g + 1) * 64, c0:c0 + tq] = qt_ref[0, hd:hd + HEAD_DIM, :]
                rhs[(1 - g) * 64:(2 - g) * 64, c0:c0 + tq] = zeros
        acc[...] = jnp.zeros(acc.shape, F32)
        mrow[...] = jnp.full(mrow.shape, -jnp.inf, F32)
        exact_step(kc_ref, vtc_ref)

    if has_lat:
        streaming_step(kl_ref, vtl_ref)
        excess = jnp.max(m_chunk[...] - mrow[...])
        in_range = excess <= ATTN_STREAM_MAX_OCTAVES

        @pl.when(in_range)
        def _commit():
            mp = mrow[...]
            mn = jnp.maximum(mp, m_chunk[...])
            acc[...] = (acc[...] + pv_new[...]) * jnp.exp2(mp - mn)
            mrow[...] = mn

        @pl.when(jnp.logical_not(in_range))
        def _redo():
            exact_step(kl_ref, vtl_ref)

    @pl.when(j == nj - 1)
    def _finish():
        lam = (jnp.exp(jnp.sum(lq1[...] * lk1[...], keepdims=True))
               - jnp.exp(jnp.sum(lq2[...] * lk2[...], keepdims=True)) + lambda_init)
        pieces = []
        for h in range(DIFF_HEADS):
            c0 = h * 2 * tq
            o1 = acc[0:V_ROWS, c0:c0 + tq] / acc[V_ROWS:V_ROWS + 1, c0:c0 + tq]
            o2 = acc[0:V_ROWS, c0 + tq:c0 + 2 * tq] / acc[V_ROWS:V_ROWS + 1, c0 + tq:c0 + 2 * tq]
            oh = o1 - lam * o2
            ms = jnp.mean(oh * oh, axis=0, keepdims=True)
            pieces.append(oh * lax.rsqrt(ms + NORM_EPS) * sg_ref[...] * (1.0 - lambda_init))
        base = DIFF_HEADS * 2 * tq
        for g in range(GQA_KV_HEADS):
            for r in range(GQA_GROUP):
                c0 = base + (g * GQA_GROUP + r) * tq
                pieces.append(acc[g * 64:(g + 1) * 64, c0:c0 + tq]
                              / acc[V_ROWS:V_ROWS + 1, c0:c0 + tq])
        ot = jnp.concatenate(pieces, axis=0)
        o_ref[0] = ot.T.astype(BF16)


def _flash(lam_vecs, sg, qt, kc, vtc, kl, vtl, *, lambda_init, tq, tk):
    b, _, sq = qt.shape
    c = kc.shape[1]
    has_lat = kl is not None
    tq = min(tq, sq)
    ncols = (DIFF_HEADS * 2 + GQA_HEADS) * tq
    in_specs = [_resident((1, HEAD_DIM))] * 4 + [
        _resident((V_ROWS, 1)),
        pl.BlockSpec((1, 1024, tq), lambda i, q, j: (i, 0, q)),
        pl.BlockSpec((1, c, K_COLS), lambda i, q, j: (i, 0, 0)),
        pl.BlockSpec((1, ATTN_UNITS, V_ROWS_PADDED, c), lambda i, q, j: (i, 0, 0, 0)),
    ]
    args = list(lam_vecs) + [sg, qt, kc, vtc]
    nkv = 1
    if has_lat:
        n = kl.shape[1]
        tk = min(tk, n)
        nkv = n // tk
        in_specs += [pl.BlockSpec((1, tk, K_COLS), lambda i, q, j: (i, j, 0)),
                     pl.BlockSpec((1, ATTN_UNITS, V_ROWS_PADDED, tk), lambda i, q, j: (i, 0, 0, j))]
        args += [kl, vtl]
    scratch = [pltpu.VMEM((2 * HEAD_DIM, ncols), BF16),
               pltpu.VMEM((V_ROWS_PADDED, ncols), F32),
               pltpu.VMEM((1, ncols), F32)]
    if has_lat:
        scratch += [pltpu.VMEM((V_ROWS_PADDED, ncols), F32), pltpu.VMEM((1, ncols), F32)]
    body = functools.partial(_flash_body, tq=tq, lambda_init=lambda_init, has_lat=has_lat)
    return pl.pallas_call(
        body,
        grid=(b, sq // tq, nkv),
        in_specs=in_specs,
        out_specs=pl.BlockSpec((1, tq, 1024), lambda i, q, j: (i, q, 0)),
        out_shape=jax.ShapeDtypeStruct((b, sq, 1024), BF16),
        scratch_shapes=scratch,
        compiler_params=_cparams("parallel", "parallel", "arbitrary"),
        name="attn_sweep_lat" if has_lat else "attn_sweep_ctx",
    )(*args)


def _proj_residual_body(x_ref, a_ref, w_ref, mod_ref, o_ref, *, gate_row):
    y = jnp.dot(a_ref[0], w_ref[...], preferred_element_type=F32)
    o_ref[0] = x_ref[0] + mod_ref[0][gate_row:gate_row + 1] * y


def _proj_residual(x, a, w, mod_l, mod_row, *, gate_row, tm):
    b, s, d = x.shape
    tm = min(tm, s)
    k = a.shape[2]
    body = functools.partial(_proj_residual_body, gate_row=gate_row)
    return pl.pallas_call(
        body,
        grid=(b, s // tm),
        in_specs=[pl.BlockSpec((1, tm, d), lambda i, t: (i, t, 0)),
                  pl.BlockSpec((1, tm, k), lambda i, t: (i, t, 0)),
                  _resident((k, d)),
                  pl.BlockSpec((1, 6, d), lambda i, t: (mod_row(i), 0, 0))],
        out_specs=pl.BlockSpec((1, tm, d), lambda i, t: (i, t, 0)),
        out_shape=jax.ShapeDtypeStruct((b, s, d), F32),
        compiler_params=_cparams("parallel", "parallel"),
        name="attn_out_residual",
    )(x, a, w, mod_l)


def _ffn_body(x_ref, xp_ref, xn_ref, mod_ref, g_ref, wv_ref, wg_ref, cw_ref, cb_ref, wo_ref,
              o_ref, *, tm):
    t = pl.program_id(1)
    nt = pl.num_programs(1)
    m = mod_ref[0]
    x = x_ref[0]
    xe = jnp.concatenate([xp_ref[0], x, xn_ref[0]], axis=0)
    he = _modulate(xe, g_ref[...], m[3:4], m[4:5]).astype(BF16)
    ge = jnp.dot(he, wg_ref[...], preferred_element_type=F32)
    rows = lax.broadcasted_iota(jnp.int32, (tm + 2 * CONV_HALO, 1), 0)
    lo = jnp.where(t > 0, 0, CONV_HALO)
    hi = jnp.where(t < nt - 1, tm + 2 * CONV_HALO, tm + CONV_HALO)
    ge = jnp.where((rows >= lo) & (rows < hi), ge, 0.0)
    val = jnp.dot(he[CONV_HALO:CONV_HALO + tm], wv_ref[...], preferred_element_type=F32)
    cw = cw_ref[...]
    h0 = CONV_HALO
    conv = (cb_ref[...] + cw[0:1] * ge[h0 - 1:h0 - 1 + tm] + cw[1:2] * ge[h0:h0 + tm]
            + cw[2:3] * ge[h0 + 1:h0 + 1 + tm])
    gelu = 0.5 * conv * (1.0 + lax.erf(conv * math.sqrt(0.5)))
    act = (gelu * val).astype(BF16)
    y = jnp.dot(act, wo_ref[...], preferred_element_type=F32)
    o_ref[0] = x + m[5:6] * y


def _ffn(x, mod_l, mod_row, g, wv, wg, cw, cb, wo, *, tm):
    b, s, d = x.shape
    tm = min(tm, s)
    hb = tm // CONV_HALO
    last = s // CONV_HALO - 1
    body = functools.partial(_ffn_body, tm=tm)
    return pl.pallas_call(
        body,
        grid=(b, s // tm),
        in_specs=[pl.BlockSpec((1, tm, d), lambda i, t: (i, t, 0)),
                  pl.BlockSpec((1, CONV_HALO, d), lambda i, t: (i, jnp.maximum(t * hb - 1, 0), 0)),
                  pl.BlockSpec((1, CONV_HALO, d), lambda i, t: (i, jnp.minimum((t + 1) * hb, last), 0)),
                  pl.BlockSpec((1, 6, d), lambda i, t: (mod_row(i), 0, 0)),
                  _resident((1, d)),
                  _resident((d, D_FF)),
                  _resident((d, D_FF)),
                  _resident((3, D_FF)),
                  _resident((1, D_FF)),
                  _resident((D_FF, d))],
        out_specs=pl.BlockSpec((1, tm, d), lambda i, t: (i, t, 0)),
        out_shape=jax.ShapeDtypeStruct((b, s, d), F32),
        compiler_params=_cparams("parallel", "parallel"),
        name="conv_glu_ffn",
    )(x, x, x, mod_l, g, wv, wg, cw, cb, wo)


def _ssm_proj_body(x_ref, mod_ref, g_ref, wt_ref, dtb_ref, zt_ref, xbct_ref, dtt_ref):
    m = mod_ref[0]
    h = _modulate(x_ref[0], g_ref[...], m[0:1], m[1:2]).astype(BF16)
    t = lax.dot_general(wt_ref[...], h, NT_DIMS, preferred_element_type=F32)
    zt_ref[0] = t[0:SSM_D_INNER].astype(BF16)
    xbct_ref[0] = t[SSM_D_INNER:SSM_D_INNER + SSM_XBC].astype(BF16)
    dtt_ref[0] = jax.nn.softplus(t[SSM_D_INNER + SSM_XBC:SSM_IN_COLS] + dtb_ref[...])


def _ssm_proj(x, mod_l, mod_row, g, wt, dtb, *, tm):
    b, s, d = x.shape
    tm = min(tm, s)
    return pl.pallas_call(
        _ssm_proj_body,
        grid=(b, s // tm),
        in_specs=[pl.BlockSpec((1, tm, d), lambda i, t: (i, t, 0)),
                  pl.BlockSpec((1, 6, d), lambda i, t: (mod_row(i), 0, 0)),
                  _resident((1, d)),
                  _resident((SSM_IN_COLS, d)),
                  _resident((2 * SSM_HEADS, 1))],
        out_specs=[pl.BlockSpec((1, SSM_D_INNER, tm), lambda i, t: (i, 0, t)),
                   pl.BlockSpec((1, SSM_XBC, tm), lambda i, t: (i, 0, t)),
                   pl.BlockSpec((1, 2 * SSM_HEADS, tm), lambda i, t: (i, 0, t))],
        out_shape=[jax.ShapeDtypeStruct((b, SSM_D_INNER, s), BF16),
                   jax.ShapeDtypeStruct((b, SSM_XBC, s), BF16),
                   jax.ShapeDtypeStruct((b, 2 * SSM_HEADS, s), F32)],
        compiler_params=_cparams("parallel", "parallel"),
        name="ssm_in_proj",
    )(x, mod_l, g, wt, dtb)


def _ssm_conv_body(u_ref, up_ref, un_ref, w_ref, b_ref, xst_ref, ct_ref, bm_ref, *, tc):
    t = pl.program_id(1)
    nt = pl.num_programs(1)
    u = u_ref[0].astype(F32)
    prev = jnp.where(t > 0, up_ref[0][:, V7X_LANES - 1:V7X_LANES].astype(F32), 0.0)
    nxt = jnp.where(t < nt - 1, un_ref[0][:, 0:1].astype(F32), 0.0)
    lane = lax.broadcasted_iota(jnp.int32, (1, tc), 1)
    packed = pltpu.bitcast(u_ref[0], jnp.uint32)
    rolled_l = pltpu.bitcast(pltpu.roll(packed, 1, 1), BF16).astype(F32)
    rolled_r = pltpu.bitcast(pltpu.roll(packed, tc - 1, 1), BF16).astype(F32)
    left = jnp.where(lane == 0, prev, rolled_l)
    right = jnp.where(lane == tc - 1, nxt, rolled_r)
    w = w_ref[...]
    v = _silu(b_ref[...] + w[:, 0:1] * left + w[:, 1:2] * u + w[:, 2:3] * right)
    bn = SSM_GROUPS * SSM_STATE
    xst_ref[0] = v[0:SSM_D_INNER].astype(BF16)
    bm_ref[0] = v[SSM_D_INNER:SSM_D_INNER + bn].T.astype(BF16)
    ct_ref[0] = v[SSM_D_INNER + bn:SSM_XBC].astype(BF16)


def _ssm_conv(xbct, w, bias, *, tc):
    b, ch, s = xbct.shape
    tc = min(tc, s)
    hb = tc // V7X_LANES
    last = s // V7X_LANES - 1
    bn = SSM_GROUPS * SSM_STATE
    body = functools.partial(_ssm_conv_body, tc=tc)
    return pl.pallas_call(
        body,
        grid=(b, s // tc),
        in_specs=[pl.BlockSpec((1, ch, tc), lambda i, t: (i, 0, t)),
                  pl.BlockSpec((1, ch, V7X_LANES), lambda i, t: (i, 0, jnp.maximum(t * hb - 1, 0))),
                  pl.BlockSpec((1, ch, V7X_LANES), lambda i, t: (i, 0, jnp.minimum((t + 1) * hb, last))),
                  _resident((ch, 3)),
                  _resident((ch, 1))],
        out_specs=[pl.BlockSpec((1, SSM_D_INNER, tc), lambda i, t: (i, 0, t)),
                   pl.BlockSpec((1, bn, tc), lambda i, t: (i, 0, t)),
                   pl.BlockSpec((1, tc, bn), lambda i, t: (i, t, 0))],
        out_shape=[jax.ShapeDtypeStruct((b, SSM_D_INNER, s), BF16),
                   jax.ShapeDtypeStruct((b, bn, s), BF16),
                   jax.ShapeDtypeStruct((b, s, bn), BF16)],
        compiler_params=_cparams("parallel", "parallel"),
        name="ssm_conv_silu",
    )(xbct, xbct, xbct, w, bias)


def _dot_f32_by_01(a, m01):
    hi = a.astype(BF16)
    r1 = a - hi.astype(F32)
    mid = r1.astype(BF16)
    lo = (r1 - mid.astype(F32)).astype(BF16)
    return (jnp.dot(hi, m01, preferred_element_type=F32)
            + jnp.dot(mid, m01, preferred_element_type=F32)
            + jnp.dot(lo, m01, preferred_element_type=F32))


def _scan_body(a_ref, dt_ref, xs_ref, ct_ref, b_ref, s0_ref, y_ref, sout_ref, state, *, chunk):
    d = pl.program_id(1)
    c = pl.program_id(2)
    nc = pl.num_programs(2)
    hp = SSM_HEADS_PER_GROUP
    hd = SSM_D_INNER // SSM_HEADS
    gw = hp * hd

    @pl.when(c == 0)
    def _load_state():
        state[...] = s0_ref[0, 0]

    dt = dt_ref[0]
    da = dt * a_ref[0]
    jrow = lax.broadcasted_iota(jnp.int32, (chunk, chunk), 0)
    icol = lax.broadcasted_iota(jnp.int32, (chunk, chunk), 1)
    sign = jnp.where(d == 0, 1, -1)
    mask = (icol - jrow) * sign >= 0
    cum_t = _dot_f32_by_01(da, mask.astype(BF16))
    not_yet = jnp.where(mask, 0.0, -jnp.inf)
    cum2_t = cum_t * LOG2E
    src_term = (cum2_t - jnp.log2(dt)).T
    tot = jnp.where(d == 0, cum_t[:, chunk - 1:chunk], cum_t[:, 0:1])
    to_end = jnp.exp(tot - cum_t) * dt
    ecum = jnp.exp(cum_t)
    etot = jnp.exp(tot)

    for g in range(SSM_GROUPS):
        bg = b_ref[0, :, g * SSM_STATE:(g + 1) * SSM_STATE]
        ctg = ct_ref[0, g * SSM_STATE:(g + 1) * SSM_STATE, :]
        cbt = jnp.dot(bg, ctg, preferred_element_type=F32)
        xg = xs_ref[0, g * gw:(g + 1) * gw, :]
        sg = state[g * gw:(g + 1) * gw, :]
        hs = slice(g * hp, (g + 1) * hp)
        y_state = (jnp.dot(sg.astype(BF16), ctg, preferred_element_type=F32)
                   .reshape(hp, hd, chunk) * ecum[hs][:, None, :])
        outs = []
        for r in range(hp):
            h = g * hp + r
            seg = cum2_t[h:h + 1, :] - src_term[:, h:h + 1]
            w = (cbt * jnp.exp2(seg + not_yet)).astype(BF16)
            outs.append(jnp.dot(xg[r * hd:(r + 1) * hd], w, preferred_element_type=F32)
                        + y_state[r])
        y_ref[0, 0, g * gw:(g + 1) * gw, :] = jnp.concatenate(outs, axis=0).astype(BF16)
        xw = ((xg.astype(F32).reshape(hp, hd, chunk) * to_end[hs][:, None, :])
              .reshape(gw, chunk).astype(BF16))
        upd = jnp.dot(xw, bg, preferred_element_type=F32)
        decayed = (sg.reshape(hp, hd, SSM_STATE) * etot[hs][:, None, :]).reshape(gw, SSM_STATE)
        state[g * gw:(g + 1) * gw, :] = decayed + upd

    @pl.when(c == nc - 1)
    def _store_state():
        sout_ref[0, 0] = state[...]


def _ssd_scan(a, dtt, xst, ct, bm, s0):
    b, ch, s = xst.shape
    chunk = min(SSD_CHUNK, s)
    nc = s // chunk
    bn = SSM_GROUPS * SSM_STATE

    def cidx(d, c):
        return c + d * (nc - 1 - 2 * c)

    body = functools.partial(_scan_body, chunk=chunk)
    return pl.pallas_call(
        body,
        grid=(b, 2, nc),
        in_specs=[pl.BlockSpec((1, SSM_HEADS, 1), lambda i, d, c: (d, 0, 0)),
                  pl.BlockSpec((1, SSM_HEADS, chunk), lambda i, d, c: (i, d, cidx(d, c))),
                  pl.BlockSpec((1, ch, chunk), lambda i, d, c: (i, 0, cidx(d, c))),
                  pl.BlockSpec((1, bn, chunk), lambda i, d, c: (i, 0, cidx(d, c))),
                  pl.BlockSpec((1, chunk, bn), lambda i, d, c: (i, cidx(d, c), 0)),
                  pl.BlockSpec((1, 1, ch, SSM_STATE), lambda i, d, c: (d, i, 0, 0))],
        out_specs=[pl.BlockSpec((1, 1, ch, chunk), lambda i, d, c: (d, i, 0, cidx(d, c))),
                   pl.BlockSpec((1, 1, ch, SSM_STATE), lambda i, d, c: (d, i, 0, 0))],
        out_shape=[jax.ShapeDtypeStruct((2, b, ch, s), BF16),
                   jax.ShapeDtypeStruct((2, b, ch, SSM_STATE), F32)],
        scratch_shapes=[pltpu.VMEM((ch, SSM_STATE), F32)],
        compiler_params=_cparams("parallel", "parallel", "arbitrary"),
        name="ssd_scan",
    )(a, dtt, xst, ct, bm, s0)


def _ssm_out_body(x_ref, mod_ref, yf_ref, yb_ref, xs_ref, z_ref, dsk_ref, ng_ref, wt_ref, o_ref):
    y = (yf_ref[0, 0].astype(F32) + yb_ref[0, 0].astype(F32)
         + dsk_ref[...] * xs_ref[0].astype(F32))
    y = y * _silu(z_ref[0].astype(F32))
    ms = jnp.mean(y * y, axis=0, keepdims=True)
    yn = (y * lax.rsqrt(ms + NORM_EPS) * ng_ref[...]).astype(BF16)
    ot = jnp.dot(wt_ref[...], yn, preferred_element_type=F32)
    o_ref[0] = x_ref[0] + mod_ref[0][2:3] * ot.T


def _ssm_out(x, mod_l, mod_row, y, xst, zt, dsk, ng, wt, *, tm):
    b, s, d = x.shape
    tm = min(tm, s)
    ch = xst.shape[1]
    return pl.pallas_call(
        _ssm_out_body,
        grid=(b, s // tm),
        in_specs=[pl.BlockSpec((1, tm, d), lambda i, t: (i, t, 0)),
                  pl.BlockSpec((1, 6, d), lambda i, t: (mod_row(i), 0, 0)),
                  pl.BlockSpec((1, 1, ch, tm), lambda i, t: (0, i, 0, t)),
                  pl.BlockSpec((1, 1, ch, tm), lambda i, t: (1, i, 0, t)),
                  pl.BlockSpec((1, ch, tm), lambda i, t: (i, 0, t)),
                  pl.BlockSpec((1, ch, tm), lambda i, t: (i, 0, t)),
                  _resident((ch, 1)),
                  _resident((ch, 1)),
                  _resident((d, ch))],
        out_specs=pl.BlockSpec((1, tm, d), lambda i, t: (i, t, 0)),
        out_shape=jax.ShapeDtypeStruct((b, s, d), F32),
        compiler_params=_cparams("parallel", "parallel"),
        name="ssm_out_residual",
    )(x, mod_l, y, y, xst, zt, dsk, ng, wt)


def _final_norm_body(x_ref, g_ref, o_ref):
    x = x_ref[0]
    ms = jnp.mean(x * x, axis=-1, keepdims=True)
    o_ref[0] = x * lax.rsqrt(ms + NORM_EPS) * g_ref[...]


def _final_norm(x, g, *, tm):
    b, s, d = x.shape
    tm = min(tm, s)
    return pl.pallas_call(
        _final_norm_body,
        grid=(b, s // tm),
        in_specs=[pl.BlockSpec((1, tm, d), lambda i, t: (i, t, 0)), _resident((1, d))],
        out_specs=pl.BlockSpec((1, tm, d), lambda i, t: (i, t, 0)),
        out_shape=jax.ShapeDtypeStruct((b, s, d), F32),
        compiler_params=_cparams("parallel", "parallel"),
        name="final_rmsnorm",
    )(x, g)


def _rope_tables_t(n):
    t = jnp.arange(n)
    inv_freq = 1.0 / (ROPE_BASE ** (jnp.arange(ROPE_PAIRS, dtype=F32) / ROPE_PAIRS))
    ang_r = (t // GRID_W).astype(F32)[None, :] * inv_freq[:, None]
    ang_c = (t % GRID_W).astype(F32)[None, :] * inv_freq[:, None]
    cr, sr, cc, sc = jnp.cos(ang_r), jnp.sin(ang_r), jnp.cos(ang_c), jnp.sin(ang_c)
    return (jnp.concatenate([cr, cr, cc, cc], axis=0),
            jnp.concatenate([-sr, sr, -sc, sc], axis=0))


def _attn_layer(x, ctx, mod_l, lat_row, ctx_row, p, rope, lambda_init, with_ctx):
    wt = p["w_in"].T.astype(BF16)
    g = p["norm_g"].reshape(1, D_MODEL)
    qg = p["q_norm_g"].reshape(HEAD_DIM, 1)
    kg = p["k_norm_g"].reshape(HEAD_DIM, 1)
    cos_t, sin_t = rope
    c = ctx.shape[1]
    qt, kl, vtl = _attn_in(x, mod_l, lat_row, g, wt, cos_t, sin_t, qg, kg, rope=True, tm=512)
    qtc, kc, vtc = _attn_in(ctx, mod_l, ctx_row, g, wt, cos_t[:, :c], sin_t[:, :c], qg, kg,
                            rope=False, tm=256)
    lam_vecs = [p[k].reshape(1, HEAD_DIM) for k in ("lq1", "lk1", "lq2", "lk2")]
    sg = p["subln_g"].reshape(V_ROWS, 1)
    w_out = p["w_out"].astype(BF16)
    o = _flash(lam_vecs, sg, qt, kc, vtc, kl, vtl, lambda_init=lambda_init, tq=256, tk=2048)
    x = _proj_residual(x, o, w_out, mod_l, lat_row, gate_row=2, tm=512)
    if with_ctx:
        oc = _flash(lam_vecs, sg, qtc, kc, vtc, None, None, lambda_init=lambda_init, tq=256, tk=0)
        ctx = _proj_residual(ctx, oc, w_out, mod_l, ctx_row, gate_row=2, tm=256)
    return x, ctx


def _ssm_layer(x, ctx, mod_l, lat_row, ctx_row, p, with_ctx):
    wt = p["w_in"].T.astype(BF16)
    g = p["norm_g"].reshape(1, D_MODEL)
    dtb = p["dt_bias"].reshape(2 * SSM_HEADS, 1)
    conv_w = p["conv_w"].T
    conv_b = p["conv_b"].reshape(SSM_XBC, 1)
    a = (-jnp.exp(p["a_log"].astype(F32))).reshape(2, SSM_HEADS, 1)
    b = x.shape[0]

    def pre(v, row, tm):
        zt, xbct, dtt = _ssm_proj(v, mod_l, row, g, wt, dtb, tm=tm)
        xst, ct, bm = _ssm_conv(xbct, conv_w, conv_b, tc=512)
        return zt, xst, ct, bm, dtt

    zt_c, xst_c, ct_c, bm_c, dtt_c = pre(ctx, ctx_row, 256)
    zt_l, xst_l, ct_l, bm_l, dtt_l = pre(x, lat_row, 512)
    zero = jnp.zeros((2, b, SSM_D_INNER, SSM_STATE), F32)
    y_c, s_ctx = _ssd_scan(a, dtt_c, xst_c, ct_c, bm_c, zero)
    y_l, _ = _ssd_scan(a, dtt_l, xst_l, ct_l, bm_l, s_ctx)
    dsk = jnp.repeat(p["d_skip"], SSM_D_INNER // SSM_HEADS).reshape(SSM_D_INNER, 1)
    ng = p["out_norm_g"].reshape(SSM_D_INNER, 1)
    w_out_t = p["w_out"].T.astype(BF16)
    x = _ssm_out(x, mod_l, lat_row, y_l, xst_l, zt_l, dsk, ng, w_out_t, tm=512)
    if with_ctx:
        ctx = _ssm_out(ctx, mod_l, ctx_row, y_c, xst_c, zt_c, dsk, ng, w_out_t, tm=256)
    return x, ctx


def kernel(x, c, ctx, c_ctx, mod_w, mod_b, norm_mix_g, norm_ffn_g, attn_w_in, attn_w_out,
           diff_lq1, diff_lk1, diff_lq2, diff_lk2, diff_subln_g, gqa_q_norm_g, gqa_k_norm_g,
           ssm_w_in, ssm_conv_w, ssm_conv_b, ssm_dt_bias, ssm_a_log, ssm_d, ssm_norm_g, ssm_w_out,
           ffn_w_in, ffn_conv_w, ffn_conv_b, ffn_w_out, final_norm_g):
    b, n, d = x.shape
    mod_rows = 16
    c_rows = jnp.zeros((mod_rows, d), F32).at[:b].set(c).at[b].set(c_ctx)
    mod = _mod_all(c_rows, mod_w, mod_b).reshape(DEPTH, mod_rows, 6, d)
    lat_row = lambda i: i
    ctx_row = lambda i: b
    rope = _rope_tables_t(n)

    for layer in range(DEPTH):
        with_ctx = layer < DEPTH - 1
        mod_l = mod[layer]
        i = layer // 2
        if layer % 2 == 0:
            p = dict(w_in=attn_w_in[i], w_out=attn_w_out[i], norm_g=norm_mix_g[layer],
                     lq1=diff_lq1[i], lk1=diff_lk1[i], lq2=diff_lq2[i], lk2=diff_lk2[i],
                     subln_g=diff_subln_g[i], q_norm_g=gqa_q_norm_g[i], k_norm_g=gqa_k_norm_g[i])
            lambda_init = 0.8 - 0.6 * math.exp(-0.3 * layer)
            x, ctx = _attn_layer(x, ctx, mod_l, lat_row, ctx_row, p, rope, lambda_init, with_ctx)
        else:
            p = dict(w_in=ssm_w_in[i], norm_g=norm_mix_g[layer], conv_w=ssm_conv_w[i],
                     conv_b=ssm_conv_b[i], dt_bias=ssm_dt_bias[i], a_log=ssm_a_log[i],
                     d_skip=ssm_d[i], out_norm_g=ssm_norm_g[i], w_out=ssm_w_out[i])
            x, ctx = _ssm_layer(x, ctx, mod_l, lat_row, ctx_row, p, with_ctx)
        g = norm_ffn_g[layer].reshape(1, d)
        wv = ffn_w_in[layer][:, :D_FF].astype(BF16)
        wg = ffn_w_in[layer][:, D_FF:].astype(BF16)
        cw = ffn_conv_w[layer]
        cb = ffn_conv_b[layer].reshape(1, D_FF)
        wo = ffn_w_out[layer].astype(BF16)
        x = _ffn(x, mod_l, lat_row, g, wv, wg, cw, cb, wo, tm=512)
        if with_ctx:
            ctx = _ffn(ctx, mod_l, ctx_row, g, wv, wg, cw, cb, wo, tm=256)
    return _final_norm(x, final_norm_g.reshape(1, d), tm=512)
```

```python
import functools
import math

import jax
import jax.numpy as jnp
from jax import lax
from jax.experimental import pallas as pl
from jax.experimental.pallas import tpu as pltpu

F32 = jnp.float32
BF16 = jnp.bfloat16

D_MODEL = 1024
DEPTH = 4
GRID_W = 64
HEAD_DIM = 64
ROPE_PAIRS = HEAD_DIM // 4
ROPE_BASE = 10000.0
NORM_EPS = 1e-6
DIFF_HEADS = 4
GQA_HEADS = 8
GQA_KV_HEADS = 2
GQA_GROUP = GQA_HEADS // GQA_KV_HEADS
ATTN_IN_COLS = 2304
SSM_D_INNER = 2048
SSM_HEADS = 32
SSM_GROUPS = 4
SSM_HEADS_PER_GROUP = SSM_HEADS // SSM_GROUPS
SSM_STATE = 128
SSM_XBC = 3072
SSM_IN_COLS = 5184
D_FF = 2816
LOG2E = math.log2(math.e)

V7X_VMEM_BYTES = 64 * 1024 * 1024
VMEM_LIMIT_BYTES = V7X_VMEM_BYTES - 8 * 1024 * 1024
V7X_LANES = 128
BF16_SUBLANES = 16

V_ROWS = 2 * HEAD_DIM
V_ROWS_PADDED = V_ROWS + BF16_SUBLANES
GV_ROWS_PADDED = HEAD_DIM + BF16_SUBLANES
K_COLS = (DIFF_HEADS + 1) * 2 * HEAD_DIM

ATTN_KEY_SUBBLOCK = 256
ATTN_STREAM_MAX_OCTAVES = 64.0

CONV_HALO = 16
SSD_CHUNK = 256

NT_DIMS = (((1,), (1,)), ((), ()))


def _cparams(*sem):
    return pltpu.CompilerParams(dimension_semantics=sem, vmem_limit_bytes=VMEM_LIMIT_BYTES)


def _resident(shape):
    nd = len(shape)
    return pl.BlockSpec(shape, lambda *_: (0,) * nd, pipeline_mode=pl.Buffered(1))


def _silu(v):
    return v * jax.nn.sigmoid(v)


def _modulate(x, g, shift, scale):
    ms = jnp.mean(x * x, axis=-1, keepdims=True)
    return (x * lax.rsqrt(ms + NORM_EPS) * g) * (1.0 + scale) + shift


def _mod_body(c_ref, w_ref, b_ref, o_ref):
    s = _silu(c_ref[...])
    o_ref[0] = jnp.dot(s, w_ref[0], preferred_element_type=F32,
                       precision=lax.Precision.HIGHEST) + b_ref[0]


def _mod_all(c_rows, mod_w, mod_b):
    rows = c_rows.shape[0]
    depth, d, cols = mod_w.shape
    tn = 2048
    return pl.pallas_call(
        _mod_body,
        grid=(depth, cols // tn),
        in_specs=[pl.BlockSpec((rows, d), lambda l, n: (0, 0)),
                  pl.BlockSpec((1, d, tn), lambda l, n: (l, 0, n)),
                  pl.BlockSpec((1, 1, tn), lambda l, n: (l, 0, n))],
        out_specs=pl.BlockSpec((1, rows, tn), lambda l, n: (l, 0, n)),
        out_shape=jax.ShapeDtypeStruct((depth, rows, cols), F32),
        compiler_params=_cparams("parallel", "parallel"),
        name="mod_vectors",
    )(c_rows, mod_w, mod_b.reshape(depth, 1, cols))


def _attn_in_body(x_ref, mod_ref, g_ref, wt_ref, cos_ref, sin_ref, qg_ref, kg_ref,
                  qt_ref, k_ref, vta_ref, vtb_ref, *, rope):
    m = mod_ref[0]
    h = _modulate(x_ref[0], g_ref[...], m[0:1], m[1:2]).astype(BF16)
    t = lax.dot_general(wt_ref[...], h, NT_DIMS, preferred_element_type=F32)
    tm = t.shape[1]

    def rot(u):
        if not rope:
            return u
        sw = jnp.concatenate([u[:, 16:32], u[:, 0:16], u[:, 48:64], u[:, 32:48]], axis=1)
        return u * cos_ref[...][None] + sw * sin_ref[...][None]

    def qk_norm(u, g):
        ms = jnp.mean(u * u, axis=1, keepdims=True)
        return u * lax.rsqrt(ms + NORM_EPS) * g[None]

    nq = 2 * DIFF_HEADS
    qa = rot(t[0:512].reshape(nq, HEAD_DIM, tm))
    ka = rot(t[512:1024].reshape(nq, HEAD_DIM, tm))
    va = t[1024:1536]
    qb = rot(qk_norm(t[1536:2048].reshape(GQA_HEADS, HEAD_DIM, tm), qg_ref[...]))
    kb = rot(qk_norm(t[2048:2176].reshape(GQA_KV_HEADS, HEAD_DIM, tm), kg_ref[...]))
    vb = t[2176:2304]

    qs = (HEAD_DIM ** -0.5) * LOG2E
    qt_ref[0, 0:512] = (qa * qs).reshape(512, tm).astype(BF16)
    qt_ref[0, 512:1024] = (qb * qs).reshape(512, tm).astype(BF16)
    kt = jnp.concatenate([ka.reshape(512, tm), kb.reshape(128, tm)], axis=0)
    k_ref[0] = kt.T.astype(BF16)
    ones = jnp.ones((BF16_SUBLANES, tm), BF16)
    for u in range(DIFF_HEADS):
        vta_ref[0, u, 0:V_ROWS] = va[u * V_ROWS:(u + 1) * V_ROWS].astype(BF16)
        vta_ref[0, u, V_ROWS:V_ROWS_PADDED] = ones
    for g in range(GQA_KV_HEADS):
        vtb_ref[0, g, 0:HEAD_DIM] = vb[g * HEAD_DIM:(g + 1) * HEAD_DIM].astype(BF16)
        vtb_ref[0, g, HEAD_DIM:GV_ROWS_PADDED] = ones


def _attn_in(x, mod_l, mod_row, g, wt, cos_t, sin_t, qg, kg, *, rope, tm):
    b, s, d = x.shape
    tm = min(tm, s)
    body = functools.partial(_attn_in_body, rope=rope)
    return pl.pallas_call(
        body,
        grid=(b, s // tm),
        in_specs=[pl.BlockSpec((1, tm, d), lambda i, t: (i, t, 0)),
                  pl.BlockSpec((1, 6, d), lambda i, t: (mod_row(i), 0, 0)),
                  _resident((1, d)),
                  _resident((ATTN_IN_COLS, d)),
                  pl.BlockSpec((HEAD_DIM, tm), lambda i, t: (0, t)),
                  pl.BlockSpec((HEAD_DIM, tm), lambda i, t: (0, t)),
                  _resident((HEAD_DIM, 1)),
                  _resident((HEAD_DIM, 1))],
        out_specs=[pl.BlockSpec((1, 1024, tm), lambda i, t: (i, 0, t)),
                   pl.BlockSpec((1, tm, K_COLS), lambda i, t: (i, t, 0)),
                   pl.BlockSpec((1, DIFF_HEADS, V_ROWS_PADDED, tm), lambda i, t: (i, 0, 0, t)),
                   pl.BlockSpec((1, GQA_KV_HEADS, GV_ROWS_PADDED, tm), lambda i, t: (i, 0, 0, t))],
        out_shape=[jax.ShapeDtypeStruct((b, 1024, s), BF16),
                   jax.ShapeDtypeStruct((b, s, K_COLS), BF16),
                   jax.ShapeDtypeStruct((b, DIFF_HEADS, V_ROWS_PADDED, s), BF16),
                   jax.ShapeDtypeStruct((b, GQA_KV_HEADS, GV_ROWS_PADDED, s), BF16)],
        compiler_params=_cparams("parallel", "parallel"),
        name="attn_in_rope" if rope else "attn_in_ctx",
    )(x, mod_l, g, wt, cos_t, sin_t, qg, kg)


def _attn_unit_cols(tq):
    units = []
    for h in range(DIFF_HEADS):
        units.append((h * 128, h, V_ROWS_PADDED, h * 2 * tq, 2 * tq))
    base = DIFF_HEADS * 2 * tq
    for g in range(GQA_KV_HEADS):
        units.append((DIFF_HEADS * 128, g, GV_ROWS_PADDED, base + g * GQA_GROUP * tq,
                      GQA_GROUP * tq))
    return units


def _flash_body(*refs, tq, lambda_init, has_lat):
    if has_lat:
        (lq1, lk1, lq2, lk2, sg_ref, x_ref, mod_ref, wo_ref, qt_ref, kc_ref, vtac_ref, vtbc_ref,
         kl_ref, vtal_ref, vtbl_ref, o_ref, rhs, acc, mrow, pv_new, m_chunk) = refs
    else:
        (lq1, lk1, lq2, lk2, sg_ref, x_ref, mod_ref, wo_ref, qt_ref, kc_ref, vtac_ref, vtbc_ref,
         o_ref, rhs, acc, mrow) = refs
    j = pl.program_id(2)
    nj = pl.num_programs(2)
    units = _attn_unit_cols(tq)
    diff_cols = DIFF_HEADS * 2 * tq

    def values(vta_ref, vtb_ref, u, vu):
        return vta_ref[0, vu] if u < DIFF_HEADS else vtb_ref[0, vu]

    def exact_step(k_ref, vta_ref, vtb_ref):
        for u, (kc0, vu, vr, c0, w) in enumerate(units):
            s = jnp.dot(k_ref[0, :, kc0:kc0 + 128], rhs[:, c0:c0 + w],
                        preferred_element_type=F32)
            mp = mrow[:, c0:c0 + w]
            mn = jnp.maximum(mp, jnp.max(s, axis=0, keepdims=True))
            alpha = jnp.exp2(mp - mn)
            p = jnp.exp2(s - mn).astype(BF16)
            pv = jnp.dot(values(vta_ref, vtb_ref, u, vu), p, preferred_element_type=F32)
            acc[0:vr, c0:c0 + w] = acc[0:vr, c0:c0 + w] * alpha + pv
            mrow[:, c0:c0 + w] = mn

    def streaming_step(k_ref, vta_ref, vtb_ref):
        nk = k_ref.shape[1]
        sub = min(nk, ATTN_KEY_SUBBLOCK)
        for u, (kc0, vu, vr, c0, w) in enumerate(units):
            m_used = mrow[:, c0:c0 + w]
            cm = None
            parts = []
            for r in range(nk // sub):
                s = jnp.dot(k_ref[0, r * sub:(r + 1) * sub, kc0:kc0 + 128], rhs[:, c0:c0 + w],
                            preferred_element_type=F32)
                parts.append(jnp.exp2(s - m_used).astype(BF16))
                sm = jnp.max(s.reshape(sub // 8, 8, w), axis=0)
                cm = sm if cm is None else jnp.maximum(cm, sm)
            p = jnp.concatenate(parts, axis=0)
            pv_new[0:vr, c0:c0 + w] = jnp.dot(values(vta_ref, vtb_ref, u, vu), p,
                                              preferred_element_type=F32)
            m_chunk[:, c0:c0 + w] = jnp.max(cm, axis=0, keepdims=True)

    @pl.when(j == 0)
    def _init():
        zeros = jnp.zeros((HEAD_DIM, tq), BF16)
        for h in range(DIFF_HEADS):
            c0 = h * 2 * tq
            rhs[0:64, c0:c0 + tq] = qt_ref[0, h * 128:h * 128 + 64, :]
            rhs[64:128, c0:c0 + tq] = zeros
            rhs[0:64, c0 + tq:c0 + 2 * tq] = zeros
            rhs[64:128, c0 + tq:c0 + 2 * tq] = qt_ref[0, h * 128 + 64:h * 128 + 128, :]
        base = DIFF_HEADS * 2 * tq
        for g in range(GQA_KV_HEADS):
            for r in range(GQA_GROUP):
                c0 = base + (g * GQA_GROUP + r) * tq
                hd = 512 + (g * GQA_GROUP + r) * HEAD_DIM
                rhs[g * 64:(g + 1) * 64, c0:c0 + tq] = qt_ref[0, hd:hd + HEAD_DIM, :]
                rhs[(1 - g) * 64:(2 - g) * 64, c0:c0 + tq] = zeros
        acc[...] = jnp.zeros(acc.shape, F32)
        mrow[...] = jnp.full(mrow.shape, -jnp.inf, F32)
        exact_step(kc_ref, vtac_ref, vtbc_ref)

    if has_lat:
        streaming_step(kl_ref, vtal_ref, vtbl_ref)
        excess = jnp.max(m_chunk[...] - mrow[...])
        in_range = excess <= ATTN_STREAM_MAX_OCTAVES

        @pl.when(in_range)
        def _commit():
            mp = mrow[...]
            mn = jnp.maximum(mp, m_chunk[...])
            alpha = jnp.exp2(mp - mn)
            for rows, cols in ((V_ROWS_PADDED, slice(0, diff_cols)),
                               (GV_ROWS_PADDED, slice(diff_cols, acc.shape[1]))):
                acc[0:rows, cols] = (acc[0:rows, cols] + pv_new[0:rows, cols]) * alpha[:, cols]
            mrow[...] = mn

        @pl.when(jnp.logical_not(in_range))
        def _redo():
            exact_step(kl_ref, vtal_ref, vtbl_ref)

    @pl.when(j == nj - 1)
    def _finish():
        lam = (jnp.exp(jnp.sum(lq1[...] * lk1[...], keepdims=True))
               - jnp.exp(jnp.sum(lq2[...] * lk2[...], keepdims=True)) + lambda_init)
        pieces = []
        for h in range(DIFF_HEADS):
            c0 = h * 2 * tq
            o1 = acc[0:V_ROWS, c0:c0 + tq] / acc[V_ROWS:V_ROWS + 1, c0:c0 + tq]
            o2 = acc[0:V_ROWS, c0 + tq:c0 + 2 * tq] / acc[V_ROWS:V_ROWS + 1, c0 + tq:c0 + 2 * tq]
            oh = o1 - lam * o2
            ms = jnp.mean(oh * oh, axis=0, keepdims=True)
            pieces.append(oh * lax.rsqrt(ms + NORM_EPS) * sg_ref[...] * (1.0 - lambda_init))
        base = DIFF_HEADS * 2 * tq
        for g in range(GQA_KV_HEADS):
            for r in range(GQA_GROUP):
                c0 = base + (g * GQA_GROUP + r) * tq
                pieces.append(acc[0:HEAD_DIM, c0:c0 + tq] / acc[HEAD_DIM:HEAD_DIM + 1, c0:c0 + tq])
        o = jnp.concatenate(pieces, axis=0).T.astype(BF16)
        y = jnp.dot(o, wo_ref[...], preferred_element_type=F32)
        o_ref[0] = x_ref[0] + mod_ref[0][2:3] * y


def _flash(x, mod_l, mod_row, w_out, lam_vecs, sg, qt, ctx_kv, lat_kv, *, lambda_init, tq, tk):
    b, _, sq = qt.shape
    d = x.shape[2]
    kc, vtac, vtbc = ctx_kv
    c = kc.shape[1]
    has_lat = lat_kv is not None
    tq = min(tq, sq)
    ncols = (DIFF_HEADS * 2 + GQA_HEADS) * tq
    in_specs = [_resident((1, HEAD_DIM))] * 4 + [
        _resident((V_ROWS, 1)),
        pl.BlockSpec((1, tq, d), lambda i, q, j: (i, q, 0)),
        pl.BlockSpec((1, 6, d), lambda i, q, j: (mod_row(i), 0, 0)),
        _resident((1024, d)),
        pl.BlockSpec((1, 1024, tq), lambda i, q, j: (i, 0, q)),
        pl.BlockSpec((1, c, K_COLS), lambda i, q, j: (i, 0, 0)),
        pl.BlockSpec((1, DIFF_HEADS, V_ROWS_PADDED, c), lambda i, q, j: (i, 0, 0, 0)),
        pl.BlockSpec((1, GQA_KV_HEADS, GV_ROWS_PADDED, c), lambda i, q, j: (i, 0, 0, 0)),
    ]
    args = list(lam_vecs) + [sg, x, mod_l, w_out, qt, kc, vtac, vtbc]
    nkv = 1
    if has_lat:
        kl, vtal, vtbl = lat_kv
        n = kl.shape[1]
        tk = min(tk, n)
        nkv = n // tk
        in_specs += [pl.BlockSpec((1, tk, K_COLS), lambda i, q, j: (i, j, 0)),
                     pl.BlockSpec((1, DIFF_HEADS, V_ROWS_PADDED, tk), lambda i, q, j: (i, 0, 0, j)),
                     pl.BlockSpec((1, GQA_KV_HEADS, GV_ROWS_PADDED, tk), lambda i, q, j: (i, 0, 0, j))]
        args += [kl, vtal, vtbl]
    scratch = [pltpu.VMEM((2 * HEAD_DIM, ncols), BF16),
               pltpu.VMEM((V_ROWS_PADDED, ncols), F32),
               pltpu.VMEM((1, ncols), F32)]
    if has_lat:
        scratch += [pltpu.VMEM((V_ROWS_PADDED, ncols), F32), pltpu.VMEM((1, ncols), F32)]
    body = functools.partial(_flash_body, tq=tq, lambda_init=lambda_init, has_lat=has_lat)
    return pl.pallas_call(
        body,
        grid=(b, sq // tq, nkv),
        in_specs=in_specs,
        out_specs=pl.BlockSpec((1, tq, d), lambda i, q, j: (i, q, 0)),
        out_shape=jax.ShapeDtypeStruct((b, sq, d), F32),
        scratch_shapes=scratch,
        compiler_params=_cparams("parallel", "parallel", "arbitrary"),
        name="attn_sweep_lat" if has_lat else "attn_sweep_ctx",
    )(*args)


def _ffn_body(x_ref, xp_ref, xn_ref, mod_ref, g_ref, wv_ref, wg_ref, cw_ref, cb_ref, wo_ref,
              fg_ref, o_ref, *, tm, final_norm):
    t = pl.program_id(1)
    nt = pl.num_programs(1)
    m = mod_ref[0]
    x = x_ref[0]
    xe = jnp.concatenate([xp_ref[0], x, xn_ref[0]], axis=0)
    he = _modulate(xe, g_ref[...], m[3:4], m[4:5]).astype(BF16)
    ge = jnp.dot(he, wg_ref[...], preferred_element_type=F32)
    rows = lax.broadcasted_iota(jnp.int32, (tm + 2 * CONV_HALO, 1), 0)
    lo = jnp.where(t > 0, 0, CONV_HALO)
    hi = jnp.where(t < nt - 1, tm + 2 * CONV_HALO, tm + CONV_HALO)
    ge = jnp.where((rows >= lo) & (rows < hi), ge, 0.0)
    val = jnp.dot(he[CONV_HALO:CONV_HALO + tm], wv_ref[...], preferred_element_type=F32)
    cw = cw_ref[...]
    h0 = CONV_HALO
    conv = (cb_ref[...] + cw[0:1] * ge[h0 - 1:h0 - 1 + tm] + cw[1:2] * ge[h0:h0 + tm]
            + cw[2:3] * ge[h0 + 1:h0 + 1 + tm])
    gelu = 0.5 * conv * (1.0 + lax.erf(conv * math.sqrt(0.5)))
    act = (gelu * val).astype(BF16)
    y = jnp.dot(act, wo_ref[...], preferred_element_type=F32)
    out = x + m[5:6] * y
    if final_norm:
        ms = jnp.mean(out * out, axis=-1, keepdims=True)
        out = out * lax.rsqrt(ms + NORM_EPS) * fg_ref[...]
    o_ref[0] = out


def _ffn(x, mod_l, mod_row, g, wv, wg, cw, cb, wo, fg, *, tm, final_norm):
    b, s, d = x.shape
    tm = min(tm, s)
    hb = tm // CONV_HALO
    last = s // CONV_HALO - 1
    body = functools.partial(_ffn_body, tm=tm, final_norm=final_norm)
    return pl.pallas_call(
        body,
        grid=(b, s // tm),
        in_specs=[pl.BlockSpec((1, tm, d), lambda i, t: (i, t, 0)),
                  pl.BlockSpec((1, CONV_HALO, d), lambda i, t: (i, jnp.maximum(t * hb - 1, 0), 0)),
                  pl.BlockSpec((1, CONV_HALO, d), lambda i, t: (i, jnp.minimum((t + 1) * hb, last), 0)),
                  pl.BlockSpec((1, 6, d), lambda i, t: (mod_row(i), 0, 0)),
                  _resident((1, d)),
                  _resident((d, D_FF)),
                  _resident((d, D_FF)),
                  _resident((3, D_FF)),
                  _resident((1, D_FF)),
                  _resident((D_FF, d)),
                  _resident((1, d))],
        out_specs=pl.BlockSpec((1, tm, d), lambda i, t: (i, t, 0)),
        out_shape=jax.ShapeDtypeStruct((b, s, d), F32),
        compiler_params=_cparams("parallel", "parallel"),
        name="conv_glu_ffn",
    )(x, x, x, mod_l, g, wv, wg, cw, cb, wo, fg)


def _ssm_proj_body(x_ref, mod_ref, g_ref, wt_ref, dtb_ref, zt_ref, xbct_ref, dtt_ref):
    m = mod_ref[0]
    h = _modulate(x_ref[0], g_ref[...], m[0:1], m[1:2]).astype(BF16)
    t = lax.dot_general(wt_ref[...], h, NT_DIMS, preferred_element_type=F32)
    zt_ref[0] = t[0:SSM_D_INNER].astype(BF16)
    xbct_ref[0] = t[SSM_D_INNER:SSM_D_INNER + SSM_XBC].astype(BF16)
    dtt_ref[0] = jax.nn.softplus(t[SSM_D_INNER + SSM_XBC:SSM_IN_COLS] + dtb_ref[...])


def _ssm_proj(x, mod_l, mod_row, g, wt, dtb, *, tm):
    b, s, d = x.shape
    tm = min(tm, s)
    return pl.pallas_call(
        _ssm_proj_body,
        grid=(b, s // tm),
        in_specs=[pl.BlockSpec((1, tm, d), lambda i, t: (i, t, 0)),
                  pl.BlockSpec((1, 6, d), lambda i, t: (mod_row(i), 0, 0)),
                  _resident((1, d)),
                  _resident((SSM_IN_COLS, d)),
                  _resident((2 * SSM_HEADS, 1))],
        out_specs=[pl.BlockSpec((1, SSM_D_INNER, tm), lambda i, t: (i, 0, t)),
                   pl.BlockSpec((1, SSM_XBC, tm), lambda i, t: (i, 0, t)),
                   pl.BlockSpec((1, 2 * SSM_HEADS, tm), lambda i, t: (i, 0, t))],
        out_shape=[jax.ShapeDtypeStruct((b, SSM_D_INNER, s), BF16),
                   jax.ShapeDtypeStruct((b, SSM_XBC, s), BF16),
                   jax.ShapeDtypeStruct((b, 2 * SSM_HEADS, s), F32)],
        compiler_params=_cparams("parallel", "parallel"),
        name="ssm_in_proj",
    )(x, mod_l, g, wt, dtb)


def _ssm_conv_body(u_ref, up_ref, un_ref, w_ref, b_ref, xst_ref, ct_ref, bm_ref, *, tc):
    t = pl.program_id(1)
    nt = pl.num_programs(1)
    u = u_ref[0].astype(F32)
    prev = jnp.where(t > 0, up_ref[0][:, V7X_LANES - 1:V7X_LANES].astype(F32), 0.0)
    nxt = jnp.where(t < nt - 1, un_ref[0][:, 0:1].astype(F32), 0.0)
    lane = lax.broadcasted_iota(jnp.int32, (1, tc), 1)
    packed = pltpu.bitcast(u_ref[0], jnp.uint32)
    rolled_l = pltpu.bitcast(pltpu.roll(packed, 1, 1), BF16).astype(F32)
    rolled_r = pltpu.bitcast(pltpu.roll(packed, tc - 1, 1), BF16).astype(F32)
    left = jnp.where(lane == 0, prev, rolled_l)
    right = jnp.where(lane == tc - 1, nxt, rolled_r)
    def lanes(a):
        return jnp.concatenate([a] * (tc // V7X_LANES), axis=1)

    v = _silu(lanes(b_ref[...]) + lanes(w_ref[0]) * left + lanes(w_ref[1]) * u
              + lanes(w_ref[2]) * right)
    bn = SSM_GROUPS * SSM_STATE
    xst_ref[0] = v[0:SSM_D_INNER].astype(BF16)
    bm_ref[0] = v[SSM_D_INNER:SSM_D_INNER + bn].T.astype(BF16)
    ct_ref[0] = v[SSM_D_INNER + bn:SSM_XBC].astype(BF16)


def _ssm_conv(xbct, w, bias, *, tc):
    b, ch, s = xbct.shape
    tc = min(tc, s)
    hb = tc // V7X_LANES
    last = s // V7X_LANES - 1
    bn = SSM_GROUPS * SSM_STATE
    body = functools.partial(_ssm_conv_body, tc=tc)
    return pl.pallas_call(
        body,
        grid=(b, s // tc),
        in_specs=[pl.BlockSpec((1, ch, tc), lambda i, t: (i, 0, t)),
                  pl.BlockSpec((1, ch, V7X_LANES), lambda i, t: (i, 0, jnp.maximum(t * hb - 1, 0))),
                  pl.BlockSpec((1, ch, V7X_LANES), lambda i, t: (i, 0, jnp.minimum((t + 1) * hb, last))),
                  _resident((3, ch, V7X_LANES)),
                  _resident((ch, V7X_LANES))],
        out_specs=[pl.BlockSpec((1, SSM_D_INNER, tc), lambda i, t: (i, 0, t)),
                   pl.BlockSpec((1, bn, tc), lambda i, t: (i, 0, t)),
                   pl.BlockSpec((1, tc, bn), lambda i, t: (i, t, 0))],
        out_shape=[jax.ShapeDtypeStruct((b, SSM_D_INNER, s), BF16),
                   jax.ShapeDtypeStruct((b, bn, s), BF16),
                   jax.ShapeDtypeStruct((b, s, bn), BF16)],
        compiler_params=_cparams("parallel", "parallel"),
        name="ssm_conv_silu",
    )(xbct, xbct, xbct, w, bias)


def _dot_f32_by_01(a, m01):
    hi = a.astype(BF16)
    r1 = a - hi.astype(F32)
    mid = r1.astype(BF16)
    lo = (r1 - mid.astype(F32)).astype(BF16)
    return (jnp.dot(hi, m01, preferred_element_type=F32)
            + jnp.dot(mid, m01, preferred_element_type=F32)
            + jnp.dot(lo, m01, preferred_element_type=F32))


def _scan_body(a_ref, dt_ref, xs_ref, ct_ref, b_ref, s0_ref, y_ref, sout_ref, state, *, chunk):
    d = pl.program_id(1)
    c = pl.program_id(2)
    nc = pl.num_programs(2)
    hp = SSM_HEADS_PER_GROUP
    hd = SSM_D_INNER // SSM_HEADS
    gw = hp * hd

    @pl.when(c == 0)
    def _load_state():
        state[...] = s0_ref[0, 0]

    dt = dt_ref[0]
    da = dt * a_ref[0]
    jrow = lax.broadcasted_iota(jnp.int32, (chunk, chunk), 0)
    icol = lax.broadcasted_iota(jnp.int32, (chunk, chunk), 1)
    sign = jnp.where(d == 0, 1, -1)
    mask = (icol - jrow) * sign >= 0
    cum_t = _dot_f32_by_01(da, mask.astype(BF16))
    not_yet = jnp.where(mask, 0.0, -jnp.inf)
    cum2_t = cum_t * LOG2E
    src_term = (cum2_t - jnp.log2(dt)).T
    tot = jnp.where(d == 0, cum_t[:, chunk - 1:chunk], cum_t[:, 0:1])
    to_end = jnp.exp(tot - cum_t) * dt
    ecum = jnp.exp(cum_t)
    etot = jnp.exp(tot)

    for g in range(SSM_GROUPS):
        bg = b_ref[0, :, g * SSM_STATE:(g + 1) * SSM_STATE]
        ctg = ct_ref[0, g * SSM_STATE:(g + 1) * SSM_STATE, :]
        cbt = jnp.dot(bg, ctg, preferred_element_type=F32)
        xg = xs_ref[0, g * gw:(g + 1) * gw, :]
        sg = state[g * gw:(g + 1) * gw, :]
        hs = slice(g * hp, (g + 1) * hp)
        y_state = (jnp.dot(sg.astype(BF16), ctg, preferred_element_type=F32)
                   .reshape(hp, hd, chunk) * ecum[hs][:, None, :])
        outs = []
        for r in range(hp):
            h = g * hp + r
            seg = cum2_t[h:h + 1, :] - src_term[:, h:h + 1]
            w = (cbt * jnp.exp2(seg + not_yet)).astype(BF16)
            outs.append(jnp.dot(xg[r * hd:(r + 1) * hd], w, preferred_element_type=F32)
                        + y_state[r])
        y_ref[0, 0, g * gw:(g + 1) * gw, :] = jnp.concatenate(outs, axis=0).astype(BF16)
        xw = ((xg.astype(F32).reshape(hp, hd, chunk) * to_end[hs][:, None, :])
              .reshape(gw, chunk).astype(BF16))
        upd = jnp.dot(xw, bg, preferred_element_type=F32)
        decayed = (sg.reshape(hp, hd, SSM_STATE) * etot[hs][:, None, :]).reshape(gw, SSM_STATE)
        state[g * gw:(g + 1) * gw, :] = decayed + upd

    @pl.when(c == nc - 1)
    def _store_state():
        sout_ref[0, 0] = state[...]


def _ssd_scan(a, dtt, xst, ct, bm, s0):
    b, ch, s = xst.shape
    chunk = min(SSD_CHUNK, s)
    nc = s // chunk
    bn = SSM_GROUPS * SSM_STATE

    def cidx(d, c):
        return c + d * (nc - 1 - 2 * c)

    body = functools.partial(_scan_body, chunk=chunk)
    return pl.pallas_call(
        body,
        grid=(b, 2, nc),
        in_specs=[pl.BlockSpec((1, SSM_HEADS, 1), lambda i, d, c: (d, 0, 0)),
                  pl.BlockSpec((1, SSM_HEADS, chunk), lambda i, d, c: (i, d, cidx(d, c))),
                  pl.BlockSpec((1, ch, chunk), lambda i, d, c: (i, 0, cidx(d, c))),
                  pl.BlockSpec((1, bn, chunk), lambda i, d, c: (i, 0, cidx(d, c))),
                  pl.BlockSpec((1, chunk, bn), lambda i, d, c: (i, cidx(d, c), 0)),
                  pl.BlockSpec((1, 1, ch, SSM_STATE), lambda i, d, c: (d, i, 0, 0))],
        out_specs=[pl.BlockSpec((1, 1, ch, chunk), lambda i, d, c: (d, i, 0, cidx(d, c))),
                   pl.BlockSpec((1, 1, ch, SSM_STATE), lambda i, d, c: (d, i, 0, 0))],
        out_shape=[jax.ShapeDtypeStruct((2, b, ch, s), BF16),
                   jax.ShapeDtypeStruct((2, b, ch, SSM_STATE), F32)],
        scratch_shapes=[pltpu.VMEM((ch, SSM_STATE), F32)],
        compiler_params=_cparams("parallel", "parallel", "arbitrary"),
        name="ssd_scan",
    )(a, dtt, xst, ct, bm, s0)


def _ssm_out_body(x_ref, mod_ref, yf_ref, yb_ref, xs_ref, z_ref, dsk_ref, ng_ref, wt_ref, o_ref):
    y = (yf_ref[0, 0].astype(F32) + yb_ref[0, 0].astype(F32)
         + dsk_ref[...] * xs_ref[0].astype(F32))
    y = y * _silu(z_ref[0].astype(F32))
    ms = jnp.mean(y * y, axis=0, keepdims=True)
    yn = (y * lax.rsqrt(ms + NORM_EPS) * ng_ref[...]).astype(BF16)
    ot = jnp.dot(wt_ref[...], yn, preferred_element_type=F32)
    o_ref[0] = x_ref[0] + mod_ref[0][2:3] * ot.T


def _ssm_out(x, mod_l, mod_row, y, xst, zt, dsk, ng, wt, *, tm):
    b, s, d = x.shape
    tm = min(tm, s)
    ch = xst.shape[1]
    return pl.pallas_call(
        _ssm_out_body,
        grid=(b, s // tm),
        in_specs=[pl.BlockSpec((1, tm, d), lambda i, t: (i, t, 0)),
                  pl.BlockSpec((1, 6, d), lambda i, t: (mod_row(i), 0, 0)),
                  pl.BlockSpec((1, 1, ch, tm), lambda i, t: (0, i, 0, t)),
                  pl.BlockSpec((1, 1, ch, tm), lambda i, t: (1, i, 0, t)),
                  pl.BlockSpec((1, ch, tm), lambda i, t: (i, 0, t)),
                  pl.BlockSpec((1, ch, tm), lambda i, t: (i, 0, t)),
                  _resident((ch, 1)),
                  _resident((ch, 1)),
                  _resident((d, ch))],
        out_specs=pl.BlockSpec((1, tm, d), lambda i, t: (i, t, 0)),
        out_shape=jax.ShapeDtypeStruct((b, s, d), F32),
        compiler_params=_cparams("parallel", "parallel"),
        name="ssm_out_residual",
    )(x, mod_l, y, y, xst, zt, dsk, ng, wt)


def _rope_tables_t(n):
    t = jnp.arange(n)
    inv_freq = 1.0 / (ROPE_BASE ** (jnp.arange(ROPE_PAIRS, dtype=F32) / ROPE_PAIRS))
    ang_r = (t // GRID_W).astype(F32)[None, :] * inv_freq[:, None]
    ang_c = (t % GRID_W).astype(F32)[None, :] * inv_freq[:, None]
    cr, sr, cc, sc = jnp.cos(ang_r), jnp.sin(ang_r), jnp.cos(ang_c), jnp.sin(ang_c)
    return (jnp.concatenate([cr, cr, cc, cc], axis=0),
            jnp.concatenate([-sr, sr, -sc, sc], axis=0))


def _attn_layer(x, ctx, mod_l, lat_row, ctx_row, p, rope, lambda_init, with_ctx):
    wt = p["w_in"].T.astype(BF16)
    g = p["norm_g"].reshape(1, D_MODEL)
    qg = p["q_norm_g"].reshape(HEAD_DIM, 1)
    kg = p["k_norm_g"].reshape(HEAD_DIM, 1)
    cos_t, sin_t = rope
    c = ctx.shape[1]
    qt, *lat_kv = _attn_in(x, mod_l, lat_row, g, wt, cos_t, sin_t, qg, kg, rope=True, tm=512)
    qtc, *ctx_kv = _attn_in(ctx, mod_l, ctx_row, g, wt, cos_t[:, :c], sin_t[:, :c], qg, kg,
                            rope=False, tm=256)
    lam_vecs = [p[k].reshape(1, HEAD_DIM) for k in ("lq1", "lk1", "lq2", "lk2")]
    sg = p["subln_g"].reshape(V_ROWS, 1)
    w_out = p["w_out"].astype(BF16)
    x = _flash(x, mod_l, lat_row, w_out, lam_vecs, sg, qt, ctx_kv, lat_kv,
               lambda_init=lambda_init, tq=256, tk=2048)
    if with_ctx:
        ctx = _flash(ctx, mod_l, ctx_row, w_out, lam_vecs, sg, qtc, ctx_kv, None,
                     lambda_init=lambda_init, tq=256, tk=0)
    return x, ctx


def _ssm_layer(x, ctx, mod_l, lat_row, ctx_row, p, with_ctx):
    wt = p["w_in"].T.astype(BF16)
    g = p["norm_g"].reshape(1, D_MODEL)
    dtb = p["dt_bias"].reshape(2 * SSM_HEADS, 1)
    conv_w = jnp.broadcast_to(p["conv_w"][:, :, None], (3, SSM_XBC, V7X_LANES))
    conv_b = jnp.broadcast_to(p["conv_b"][:, None], (SSM_XBC, V7X_LANES))
    a = (-jnp.exp(p["a_log"].astype(F32))).reshape(2, SSM_HEADS, 1)
    b = x.shape[0]

    def pre(v, row, tm):
        zt, xbct, dtt = _ssm_proj(v, mod_l, row, g, wt, dtb, tm=tm)
        xst, ct, bm = _ssm_conv(xbct, conv_w, conv_b, tc=512)
        return zt, xst, ct, bm, dtt

    zt_c, xst_c, ct_c, bm_c, dtt_c = pre(ctx, ctx_row, 256)
    zt_l, xst_l, ct_l, bm_l, dtt_l = pre(x, lat_row, 512)
    zero = jnp.zeros((2, b, SSM_D_INNER, SSM_STATE), F32)
    y_c, s_ctx = _ssd_scan(a, dtt_c, xst_c, ct_c, bm_c, zero)
    y_l, _ = _ssd_scan(a, dtt_l, xst_l, ct_l, bm_l, s_ctx)
    dsk = jnp.repeat(p["d_skip"], SSM_D_INNER // SSM_HEADS).reshape(SSM_D_INNER, 1)
    ng = p["out_norm_g"].reshape(SSM_D_INNER, 1)
    w_out_t = p["w_out"].T.astype(BF16)
    x = _ssm_out(x, mod_l, lat_row, y_l, xst_l, zt_l, dsk, ng, w_out_t, tm=512)
    if with_ctx:
        ctx = _ssm_out(ctx, mod_l, ctx_row, y_c, xst_c, zt_c, dsk, ng, w_out_t, tm=256)
    return x, ctx


def kernel(x, c, ctx, c_ctx, mod_w, mod_b, norm_mix_g, norm_ffn_g, attn_w_in, attn_w_out,
           diff_lq1, diff_lk1, diff_lq2, diff_lk2, diff_subln_g, gqa_q_norm_g, gqa_k_norm_g,
           ssm_w_in, ssm_conv_w, ssm_conv_b, ssm_dt_bias, ssm_a_log, ssm_d, ssm_norm_g, ssm_w_out,
           ffn_w_in, ffn_conv_w, ffn_conv_b, ffn_w_out, final_norm_g):
    b, n, d = x.shape
    mod_rows = 16
    c_rows = jnp.zeros((mod_rows, d), F32).at[:b].set(c).at[b].set(c_ctx)
    mod = _mod_all(c_rows, mod_w, mod_b).reshape(DEPTH, mod_rows, 6, d)
    lat_row = lambda i: i
    ctx_row = lambda i: b
    rope = _rope_tables_t(n)

    for layer in range(DEPTH):
        with_ctx = layer < DEPTH - 1
        mod_l = mod[layer]
        i = layer // 2
        if layer % 2 == 0:
            p = dict(w_in=attn_w_in[i], w_out=attn_w_out[i], norm_g=norm_mix_g[layer],
                     lq1=diff_lq1[i], lk1=diff_lk1[i], lq2=diff_lq2[i], lk2=diff_lk2[i],
                     subln_g=diff_subln_g[i], q_norm_g=gqa_q_norm_g[i], k_norm_g=gqa_k_norm_g[i])
            lambda_init = 0.8 - 0.6 * math.exp(-0.3 * layer)
            x, ctx = _attn_layer(x, ctx, mod_l, lat_row, ctx_row, p, rope, lambda_init, with_ctx)
        else:
            p = dict(w_in=ssm_w_in[i], norm_g=norm_mix_g[layer], conv_w=ssm_conv_w[i],
                     conv_b=ssm_conv_b[i], dt_bias=ssm_dt_bias[i], a_log=ssm_a_log[i],
                     d_skip=ssm_d[i], out_norm_g=ssm_norm_g[i], w_out=ssm_w_out[i])
            x, ctx = _ssm_layer(x, ctx, mod_l, lat_row, ctx_row, p, with_ctx)
        g = norm_ffn_g[layer].reshape(1, d)
        wv = ffn_w_in[layer][:, :D_FF].astype(BF16)
        wg = ffn_w_in[layer][:, D_FF:].astype(BF16)
        cw = ffn_conv_w[layer]
        cb = ffn_conv_b[layer].reshape(1, D_FF)
        wo = ffn_w_out[layer].astype(BF16)
        fg = final_norm_g.reshape(1, d)
        x = _ffn(x, mod_l, lat_row, g, wv, wg, cw, cb, wo, fg, tm=256,
                 final_norm=layer == DEPTH - 1)
        if with_ctx:
            ctx = _ffn(ctx, mod_l, ctx_row, g, wv, wg, cw, cb, wo, fg, tm=256, final_norm=False)
    return x
```

```python
import functools
import math

import jax
import jax.numpy as jnp
from jax import lax
from jax.experimental import pallas as pl
from jax.experimental.pallas import tpu as pltpu

F32 = jnp.float32
BF16 = jnp.bfloat16

D_MODEL = 1024
DEPTH = 4
GRID_W = 64
HEAD_DIM = 64
ROPE_PAIRS = HEAD_DIM // 4
ROPE_BASE = 10000.0
NORM_EPS = 1e-6
DIFF_HEADS = 4
GQA_HEADS = 8
GQA_KV_HEADS = 2
GQA_GROUP = GQA_HEADS // GQA_KV_HEADS
ATTN_IN_COLS = 2304
SSM_D_INNER = 2048
SSM_HEADS = 32
SSM_GROUPS = 4
SSM_HEADS_PER_GROUP = SSM_HEADS // SSM_GROUPS
SSM_STATE = 128
SSM_XBC = 3072
SSM_IN_COLS = 5184
D_FF = 2816
LOG2E = math.log2(math.e)

V7X_VMEM_BYTES = 64 * 1024 * 1024
VMEM_LIMIT_BYTES = V7X_VMEM_BYTES - 8 * 1024 * 1024
V7X_LANES = 128
BF16_SUBLANES = 16

V_ROWS = 2 * HEAD_DIM
V_ROWS_PADDED = V_ROWS + BF16_SUBLANES
GV_ROWS_PADDED = HEAD_DIM + BF16_SUBLANES
K_COLS = (DIFF_HEADS + 1) * 2 * HEAD_DIM

ATTN_KEY_SUBBLOCK = 256
ATTN_STREAM_MAX_OCTAVES = 64.0

CONV_HALO = 16
SSD_CHUNK = 256

NT_DIMS = (((1,), (1,)), ((), ()))


def _cparams(*sem):
    return pltpu.CompilerParams(dimension_semantics=sem, vmem_limit_bytes=VMEM_LIMIT_BYTES)


def _resident(shape):
    nd = len(shape)
    return pl.BlockSpec(shape, lambda *_: (0,) * nd, pipeline_mode=pl.Buffered(1))


def _silu(v):
    return v * jax.nn.sigmoid(v)


def _modulate(x, g, shift, scale):
    ms = jnp.mean(x * x, axis=-1, keepdims=True)
    return (x * lax.rsqrt(ms + NORM_EPS) * g) * (1.0 + scale) + shift


def _mod_body(c_ref, w_ref, b_ref, o_ref):
    s = _silu(c_ref[...])
    o_ref[0] = jnp.dot(s, w_ref[0], preferred_element_type=F32,
                       precision=lax.Precision.HIGHEST) + b_ref[0]


def _mod_all(c_rows, mod_w, mod_b):
    rows = c_rows.shape[0]
    depth, d, cols = mod_w.shape
    tn = 2048
    return pl.pallas_call(
        _mod_body,
        grid=(depth, cols // tn),
        in_specs=[pl.BlockSpec((rows, d), lambda l, n: (0, 0)),
                  pl.BlockSpec((1, d, tn), lambda l, n: (l, 0, n)),
                  pl.BlockSpec((1, 1, tn), lambda l, n: (l, 0, n))],
        out_specs=pl.BlockSpec((1, rows, tn), lambda l, n: (l, 0, n)),
        out_shape=jax.ShapeDtypeStruct((depth, rows, cols), F32),
        compiler_params=_cparams("parallel", "parallel"),
        name="mod_vectors",
    )(c_rows, mod_w, mod_b.reshape(depth, 1, cols))


def _attn_in_body(x_ref, mod_ref, g_ref, wt_ref, cos_ref, sin_ref, qg_ref, kg_ref,
                  qt_ref, k_ref, vta_ref, vtb_ref, *, rope):
    m = mod_ref[0]
    h = _modulate(x_ref[0], g_ref[...], m[0:1], m[1:2]).astype(BF16)
    t = lax.dot_general(wt_ref[...], h, NT_DIMS, preferred_element_type=F32)
    tm = t.shape[1]

    def rot(u):
        if not rope:
            return u
        sw = jnp.concatenate([u[:, 16:32], u[:, 0:16], u[:, 48:64], u[:, 32:48]], axis=1)
        return u * cos_ref[...][None] + sw * sin_ref[...][None]

    def qk_norm(u, g):
        ms = jnp.mean(u * u, axis=1, keepdims=True)
        return u * lax.rsqrt(ms + NORM_EPS) * g[None]

    nq = 2 * DIFF_HEADS
    qa = rot(t[0:512].reshape(nq, HEAD_DIM, tm))
    ka = rot(t[512:1024].reshape(nq, HEAD_DIM, tm))
    va = t[1024:1536]
    qb = rot(qk_norm(t[1536:2048].reshape(GQA_HEADS, HEAD_DIM, tm), qg_ref[...]))
    kb = rot(qk_norm(t[2048:2176].reshape(GQA_KV_HEADS, HEAD_DIM, tm), kg_ref[...]))
    vb = t[2176:2304]

    qs = (HEAD_DIM ** -0.5) * LOG2E
    qt_ref[0, 0:512] = (qa * qs).reshape(512, tm).astype(BF16)
    qt_ref[0, 512:1024] = (qb * qs).reshape(512, tm).astype(BF16)
    kt = jnp.concatenate([ka.reshape(512, tm), kb.reshape(128, tm)], axis=0)
    k_ref[0] = kt.T.astype(BF16)
    ones = jnp.ones((BF16_SUBLANES, tm), BF16)
    for u in range(DIFF_HEADS):
        vta_ref[0, u, 0:V_ROWS] = va[u * V_ROWS:(u + 1) * V_ROWS].astype(BF16)
        vta_ref[0, u, V_ROWS:V_ROWS_PADDED] = ones
    for g in range(GQA_KV_HEADS):
        vtb_ref[0, g, 0:HEAD_DIM] = vb[g * HEAD_DIM:(g + 1) * HEAD_DIM].astype(BF16)
        vtb_ref[0, g, HEAD_DIM:GV_ROWS_PADDED] = ones


def _attn_in(x, mod_l, mod_row, g, wt, cos_t, sin_t, qg, kg, *, rope, tm):
    b, s, d = x.shape
    tm = min(tm, s)
    body = functools.partial(_attn_in_body, rope=rope)
    return pl.pallas_call(
        body,
        grid=(b, s // tm),
        in_specs=[pl.BlockSpec((1, tm, d), lambda i, t: (i, t, 0)),
                  pl.BlockSpec((1, 6, d), lambda i, t: (mod_row(i), 0, 0)),
                  _resident((1, d)),
                  _resident((ATTN_IN_COLS, d)),
                  pl.BlockSpec((HEAD_DIM, tm), lambda i, t: (0, t)),
                  pl.BlockSpec((HEAD_DIM, tm), lambda i, t: (0, t)),
                  _resident((HEAD_DIM, 1)),
                  _resident((HEAD_DIM, 1))],
        out_specs=[pl.BlockSpec((1, 1024, tm), lambda i, t: (i, 0, t)),
                   pl.BlockSpec((1, tm, K_COLS), lambda i, t: (i, t, 0)),
                   pl.BlockSpec((1, DIFF_HEADS, V_ROWS_PADDED, tm), lambda i, t: (i, 0, 0, t)),
                   pl.BlockSpec((1, GQA_KV_HEADS, GV_ROWS_PADDED, tm), lambda i, t: (i, 0, 0, t))],
        out_shape=[jax.ShapeDtypeStruct((b, 1024, s), BF16),
                   jax.ShapeDtypeStruct((b, s, K_COLS), BF16),
                   jax.ShapeDtypeStruct((b, DIFF_HEADS, V_ROWS_PADDED, s), BF16),
                   jax.ShapeDtypeStruct((b, GQA_KV_HEADS, GV_ROWS_PADDED, s), BF16)],
        compiler_params=_cparams("parallel", "parallel"),
        name="attn_in_rope" if rope else "attn_in_ctx",
    )(x, mod_l, g, wt, cos_t, sin_t, qg, kg)


def _attn_unit_cols(tq):
    units = []
    for h in range(DIFF_HEADS):
        units.append((h * 128, h, V_ROWS_PADDED, h * 2 * tq, 2 * tq))
    base = DIFF_HEADS * 2 * tq
    for g in range(GQA_KV_HEADS):
        units.append((DIFF_HEADS * 128, g, GV_ROWS_PADDED, base + g * GQA_GROUP * tq,
                      GQA_GROUP * tq))
    return units


def _flash_body(*refs, tq, lambda_init, has_lat):
    if has_lat:
        (lq1, lk1, lq2, lk2, sg_ref, x_ref, mod_ref, wo_ref, qt_ref, kc_ref, vtac_ref, vtbc_ref,
         kl_ref, vtal_ref, vtbl_ref, o_ref, rhs, acc, mrow, pv_new, m_chunk) = refs
    else:
        (lq1, lk1, lq2, lk2, sg_ref, x_ref, mod_ref, wo_ref, qt_ref, kc_ref, vtac_ref, vtbc_ref,
         o_ref, rhs, acc, mrow, pv_new, m_chunk) = refs
    j = pl.program_id(2)
    nj = pl.num_programs(2)
    units = _attn_unit_cols(tq)
    diff_cols = DIFF_HEADS * 2 * tq

    def values(vta_ref, vtb_ref, u, vu):
        return vta_ref[0, vu] if u < DIFF_HEADS else vtb_ref[0, vu]

    def exact_step(k_ref, vta_ref, vtb_ref):
        for u, (kc0, vu, vr, c0, w) in enumerate(units):
            s = jnp.dot(k_ref[0, :, kc0:kc0 + 128], rhs[:, c0:c0 + w],
                        preferred_element_type=F32)
            mp = mrow[:, c0:c0 + w]
            mn = jnp.maximum(mp, jnp.max(s, axis=0, keepdims=True))
            alpha = jnp.exp2(mp - mn)
            p = jnp.exp2(s - mn).astype(BF16)
            pv = jnp.dot(values(vta_ref, vtb_ref, u, vu), p, preferred_element_type=F32)
            acc[0:vr, c0:c0 + w] = acc[0:vr, c0:c0 + w] * alpha + pv
            mrow[:, c0:c0 + w] = mn

    def streaming_step(k_ref, vta_ref, vtb_ref):
        nk = k_ref.shape[1]
        sub = min(nk, ATTN_KEY_SUBBLOCK)
        for u, (kc0, vu, vr, c0, w) in enumerate(units):
            m_used = mrow[:, c0:c0 + w]
            cm = None
            parts = []
            for r in range(nk // sub):
                s = jnp.dot(k_ref[0, r * sub:(r + 1) * sub, kc0:kc0 + 128], rhs[:, c0:c0 + w],
                            preferred_element_type=F32)
                parts.append(jnp.exp2(s - m_used).astype(BF16))
                sm = jnp.max(s.reshape(sub // 8, 8, w), axis=0)
                cm = sm if cm is None else jnp.maximum(cm, sm)
            p = jnp.concatenate(parts, axis=0)
            pv_new[0:vr, c0:c0 + w] = jnp.dot(values(vta_ref, vtb_ref, u, vu), p,
                                              preferred_element_type=F32)
            m_chunk[:, c0:c0 + w] = jnp.max(cm, axis=0, keepdims=True)

    @pl.when(j == 0)
    def _init():
        zeros = jnp.zeros((HEAD_DIM, tq), BF16)
        for h in range(DIFF_HEADS):
            c0 = h * 2 * tq
            rhs[0:64, c0:c0 + tq] = qt_ref[0, h * 128:h * 128 + 64, :]
            rhs[64:128, c0:c0 + tq] = zeros
            rhs[0:64, c0 + tq:c0 + 2 * tq] = zeros
            rhs[64:128, c0 + tq:c0 + 2 * tq] = qt_ref[0, h * 128 + 64:h * 128 + 128, :]
        base = DIFF_HEADS * 2 * tq
        for g in range(GQA_KV_HEADS):
            for r in range(GQA_GROUP):
                c0 = base + (g * GQA_GROUP + r) * tq
                hd = 512 + (g * GQA_GROUP + r) * HEAD_DIM
                rhs[g * 64:(g + 1) * 64, c0:c0 + tq] = qt_ref[0, hd:hd + HEAD_DIM, :]
                rhs[(1 - g) * 64:(2 - g) * 64, c0:c0 + tq] = zeros
        acc[...] = jnp.zeros(acc.shape, F32)
        mrow[...] = jnp.zeros(mrow.shape, F32)

    def stream_chunk(k_ref, vta_ref, vtb_ref, first):
        streaming_step(k_ref, vta_ref, vtb_ref)
        dev = m_chunk[...] - mrow[...]
        in_range = jnp.max(jnp.abs(dev) if first else dev) <= ATTN_STREAM_MAX_OCTAVES

        @pl.when(in_range)
        def _commit():
            mp = mrow[...]
            mn = jnp.maximum(mp, m_chunk[...])
            alpha = jnp.exp2(mp - mn)
            for rows, cols in ((V_ROWS_PADDED, slice(0, diff_cols)),
                               (GV_ROWS_PADDED, slice(diff_cols, acc.shape[1]))):
                acc[0:rows, cols] = (acc[0:rows, cols] + pv_new[0:rows, cols]) * alpha[:, cols]
            mrow[...] = mn

        @pl.when(jnp.logical_not(in_range))
        def _redo():
            if first:
                mrow[...] = jnp.full(mrow.shape, -jnp.inf, F32)
            exact_step(k_ref, vta_ref, vtb_ref)

    @pl.when(j == 0)
    def _context_keys():
        stream_chunk(kc_ref, vtac_ref, vtbc_ref, True)

    if has_lat:
        stream_chunk(kl_ref, vtal_ref, vtbl_ref, False)

    @pl.when(j == nj - 1)
    def _finish():
        lam = (jnp.exp(jnp.sum(lq1[...] * lk1[...], keepdims=True))
               - jnp.exp(jnp.sum(lq2[...] * lk2[...], keepdims=True)) + lambda_init)
        pieces = []
        for h in range(DIFF_HEADS):
            c0 = h * 2 * tq
            o1 = acc[0:V_ROWS, c0:c0 + tq] / acc[V_ROWS:V_ROWS + 1, c0:c0 + tq]
            o2 = acc[0:V_ROWS, c0 + tq:c0 + 2 * tq] / acc[V_ROWS:V_ROWS + 1, c0 + tq:c0 + 2 * tq]
            oh = o1 - lam * o2
            ms = jnp.mean(oh * oh, axis=0, keepdims=True)
            pieces.append(oh * lax.rsqrt(ms + NORM_EPS) * sg_ref[...] * (1.0 - lambda_init))
        base = DIFF_HEADS * 2 * tq
        for g in range(GQA_KV_HEADS):
            for r in range(GQA_GROUP):
                c0 = base + (g * GQA_GROUP + r) * tq
                pieces.append(acc[0:HEAD_DIM, c0:c0 + tq] / acc[HEAD_DIM:HEAD_DIM + 1, c0:c0 + tq])
        o = jnp.concatenate(pieces, axis=0).T.astype(BF16)
        y = jnp.dot(o, wo_ref[...], preferred_element_type=F32)
        o_ref[0] = x_ref[0] + mod_ref[0][2:3] * y


def _flash(x, mod_l, mod_row, w_out, lam_vecs, sg, qt, ctx_kv, lat_kv, *, lambda_init, tq, tk):
    b, _, sq = qt.shape
    d = x.shape[2]
    kc, vtac, vtbc = ctx_kv
    c = kc.shape[1]
    has_lat = lat_kv is not None
    tq = min(tq, sq)
    ncols = (DIFF_HEADS * 2 + GQA_HEADS) * tq
    in_specs = [_resident((1, HEAD_DIM))] * 4 + [
        _resident((V_ROWS, 1)),
        pl.BlockSpec((1, tq, d), lambda i, q, j: (i, q, 0)),
        pl.BlockSpec((1, 6, d), lambda i, q, j: (mod_row(i), 0, 0)),
        _resident((1024, d)),
        pl.BlockSpec((1, 1024, tq), lambda i, q, j: (i, 0, q)),
        pl.BlockSpec((1, c, K_COLS), lambda i, q, j: (i, 0, 0)),
        pl.BlockSpec((1, DIFF_HEADS, V_ROWS_PADDED, c), lambda i, q, j: (i, 0, 0, 0)),
        pl.BlockSpec((1, GQA_KV_HEADS, GV_ROWS_PADDED, c), lambda i, q, j: (i, 0, 0, 0)),
    ]
    args = list(lam_vecs) + [sg, x, mod_l, w_out, qt, kc, vtac, vtbc]
    nkv = 1
    if has_lat:
        kl, vtal, vtbl = lat_kv
        n = kl.shape[1]
        tk = min(tk, n)
        nkv = n // tk
        in_specs += [pl.BlockSpec((1, tk, K_COLS), lambda i, q, j: (i, j, 0)),
                     pl.BlockSpec((1, DIFF_HEADS, V_ROWS_PADDED, tk), lambda i, q, j: (i, 0, 0, j)),
                     pl.BlockSpec((1, GQA_KV_HEADS, GV_ROWS_PADDED, tk), lambda i, q, j: (i, 0, 0, j))]
        args += [kl, vtal, vtbl]
    scratch = [pltpu.VMEM((2 * HEAD_DIM, ncols), BF16),
               pltpu.VMEM((V_ROWS_PADDED, ncols), F32),
               pltpu.VMEM((1, ncols), F32),
               pltpu.VMEM((V_ROWS_PADDED, ncols), F32),
               pltpu.VMEM((1, ncols), F32)]
    body = functools.partial(_flash_body, tq=tq, lambda_init=lambda_init, has_lat=has_lat)
    return pl.pallas_call(
        body,
        grid=(b, sq // tq, nkv),
        in_specs=in_specs,
        out_specs=pl.BlockSpec((1, tq, d), lambda i, q, j: (i, q, 0)),
        out_shape=jax.ShapeDtypeStruct((b, sq, d), F32),
        scratch_shapes=scratch,
        compiler_params=_cparams("parallel", "parallel", "arbitrary"),
        name="attn_sweep_lat" if has_lat else "attn_sweep_ctx",
    )(*args)


def _ffn_body(x_ref, xp_ref, xn_ref, mod_ref, g_ref, wv_ref, wg_ref, cw_ref, cb_ref, wo_ref,
              fg_ref, o_ref, *, tm, final_norm):
    t = pl.program_id(1)
    nt = pl.num_programs(1)
    m = mod_ref[0]
    x = x_ref[0]
    xe = jnp.concatenate([xp_ref[0], x, xn_ref[0]], axis=0)
    he = _modulate(xe, g_ref[...], m[3:4], m[4:5]).astype(BF16)
    ge = jnp.dot(he, wg_ref[...], preferred_element_type=F32)
    rows = lax.broadcasted_iota(jnp.int32, (tm + 2 * CONV_HALO, 1), 0)
    lo = jnp.where(t > 0, 0, CONV_HALO)
    hi = jnp.where(t < nt - 1, tm + 2 * CONV_HALO, tm + CONV_HALO)
    ge = jnp.where((rows >= lo) & (rows < hi), ge, 0.0)
    val = jnp.dot(he[CONV_HALO:CONV_HALO + tm], wv_ref[...], preferred_element_type=F32)
    cw = cw_ref[...]
    h0 = CONV_HALO
    conv = (cb_ref[...] + cw[0:1] * ge[h0 - 1:h0 - 1 + tm] + cw[1:2] * ge[h0:h0 + tm]
            + cw[2:3] * ge[h0 + 1:h0 + 1 + tm])
    gelu = 0.5 * conv * (1.0 + lax.erf(conv * math.sqrt(0.5)))
    act = (gelu * val).astype(BF16)
    y = jnp.dot(act, wo_ref[...], preferred_element_type=F32)
    out = x + m[5:6] * y
    if final_norm:
        ms = jnp.mean(out * out, axis=-1, keepdims=True)
        out = out * lax.rsqrt(ms + NORM_EPS) * fg_ref[...]
    o_ref[0] = out


def _ffn(x, mod_l, mod_row, g, wv, wg, cw, cb, wo, fg, *, tm, final_norm):
    b, s, d = x.shape
    tm = min(tm, s)
    hb = tm // CONV_HALO
    last = s // CONV_HALO - 1
    body = functools.partial(_ffn_body, tm=tm, final_norm=final_norm)
    return pl.pallas_call(
        body,
        grid=(b, s // tm),
        in_specs=[pl.BlockSpec((1, tm, d), lambda i, t: (i, t, 0)),
                  pl.BlockSpec((1, CONV_HALO, d), lambda i, t: (i, jnp.maximum(t * hb - 1, 0), 0)),
                  pl.BlockSpec((1, CONV_HALO, d), lambda i, t: (i, jnp.minimum((t + 1) * hb, last), 0)),
                  pl.BlockSpec((1, 6, d), lambda i, t: (mod_row(i), 0, 0)),
                  _resident((1, d)),
                  _resident((d, D_FF)),
                  _resident((d, D_FF)),
                  _resident((3, D_FF)),
                  _resident((1, D_FF)),
                  _resident((D_FF, d)),
                  _resident((1, d))],
        out_specs=pl.BlockSpec((1, tm, d), lambda i, t: (i, t, 0)),
        out_shape=jax.ShapeDtypeStruct((b, s, d), F32),
        compiler_params=_cparams("parallel", "parallel"),
        name="conv_glu_ffn",
    )(x, x, x, mod_l, g, wv, wg, cw, cb, wo, fg)


def _ssm_proj_body(x_ref, mod_ref, g_ref, wt_ref, dtb_ref, zt_ref, xbct_ref, dtt_ref):
    m = mod_ref[0]
    h = _modulate(x_ref[0], g_ref[...], m[0:1], m[1:2]).astype(BF16)
    t = lax.dot_general(wt_ref[...], h, NT_DIMS, preferred_element_type=F32)
    zt_ref[0] = t[0:SSM_D_INNER].astype(BF16)
    xbct_ref[0] = t[SSM_D_INNER:SSM_D_INNER + SSM_XBC].astype(BF16)
    dtt_ref[0] = jax.nn.softplus(t[SSM_D_INNER + SSM_XBC:SSM_IN_COLS] + dtb_ref[...])


def _ssm_proj(x, mod_l, mod_row, g, wt, dtb, *, tm):
    b, s, d = x.shape
    tm = min(tm, s)
    return pl.pallas_call(
        _ssm_proj_body,
        grid=(b, s // tm),
        in_specs=[pl.BlockSpec((1, tm, d), lambda i, t: (i, t, 0)),
                  pl.BlockSpec((1, 6, d), lambda i, t: (mod_row(i), 0, 0)),
                  _resident((1, d)),
                  _resident((SSM_IN_COLS, d)),
                  _resident((2 * SSM_HEADS, 1))],
        out_specs=[pl.BlockSpec((1, SSM_D_INNER, tm), lambda i, t: (i, 0, t)),
                   pl.BlockSpec((1, SSM_XBC, tm), lambda i, t: (i, 0, t)),
                   pl.BlockSpec((1, 2 * SSM_HEADS, tm), lambda i, t: (i, 0, t))],
        out_shape=[jax.ShapeDtypeStruct((b, SSM_D_INNER, s), BF16),
                   jax.ShapeDtypeStruct((b, SSM_XBC, s), BF16),
                   jax.ShapeDtypeStruct((b, 2 * SSM_HEADS, s), F32)],
        compiler_params=_cparams("parallel", "parallel"),
        name="ssm_in_proj",
    )(x, mod_l, g, wt, dtb)


def _ssm_conv_body(u_ref, up_ref, un_ref, w_ref, b_ref, xst_ref, ct_ref, bm_ref, *, tc):
    t = pl.program_id(1)
    nt = pl.num_programs(1)
    u = u_ref[0].astype(F32)
    prev = jnp.where(t > 0, up_ref[0][:, V7X_LANES - 1:V7X_LANES].astype(F32), 0.0)
    nxt = jnp.where(t < nt - 1, un_ref[0][:, 0:1].astype(F32), 0.0)
    lane = lax.broadcasted_iota(jnp.int32, (1, tc), 1)
    packed = pltpu.bitcast(u_ref[0], jnp.uint32)
    rolled_l = pltpu.bitcast(pltpu.roll(packed, 1, 1), BF16).astype(F32)
    rolled_r = pltpu.bitcast(pltpu.roll(packed, tc - 1, 1), BF16).astype(F32)
    left = jnp.where(lane == 0, prev, rolled_l)
    right = jnp.where(lane == tc - 1, nxt, rolled_r)
    def lanes(a):
        return jnp.concatenate([a] * (tc // V7X_LANES), axis=1)

    v = _silu(lanes(b_ref[...]) + lanes(w_ref[0]) * left + lanes(w_ref[1]) * u
              + lanes(w_ref[2]) * right)
    bn = SSM_GROUPS * SSM_STATE
    xst_ref[0] = v[0:SSM_D_INNER].astype(BF16)
    bm_ref[0] = v[SSM_D_INNER:SSM_D_INNER + bn].T.astype(BF16)
    ct_ref[0] = v[SSM_D_INNER + bn:SSM_XBC].astype(BF16)


def _ssm_conv(xbct, w, bias, *, tc):
    b, ch, s = xbct.shape
    tc = min(tc, s)
    hb = tc // V7X_LANES
    last = s // V7X_LANES - 1
    bn = SSM_GROUPS * SSM_STATE
    body = functools.partial(_ssm_conv_body, tc=tc)
    return pl.pallas_call(
        body,
        grid=(b, s // tc),
        in_specs=[pl.BlockSpec((1, ch, tc), lambda i, t: (i, 0, t)),
                  pl.BlockSpec((1, ch, V7X_LANES), lambda i, t: (i, 0, jnp.maximum(t * hb - 1, 0))),
                  pl.BlockSpec((1, ch, V7X_LANES), lambda i, t: (i, 0, jnp.minimum((t + 1) * hb, last))),
                  _resident((3, ch, V7X_LANES)),
                  _resident((ch, V7X_LANES))],
        out_specs=[pl.BlockSpec((1, SSM_D_INNER, tc), lambda i, t: (i, 0, t)),
                   pl.BlockSpec((1, bn, tc), lambda i, t: (i, 0, t)),
                   pl.BlockSpec((1, tc, bn), lambda i, t: (i, t, 0))],
        out_shape=[jax.ShapeDtypeStruct((b, SSM_D_INNER, s), BF16),
                   jax.ShapeDtypeStruct((b, bn, s), BF16),
                   jax.ShapeDtypeStruct((b, s, bn), BF16)],
        compiler_params=_cparams("parallel", "parallel"),
        name="ssm_conv_silu",
    )(xbct, xbct, xbct, w, bias)


def _dot_f32_by_01(a, m01):
    hi = a.astype(BF16)
    r1 = a - hi.astype(F32)
    mid = r1.astype(BF16)
    lo = (r1 - mid.astype(F32)).astype(BF16)
    return (jnp.dot(hi, m01, preferred_element_type=F32)
            + jnp.dot(mid, m01, preferred_element_type=F32)
            + jnp.dot(lo, m01, preferred_element_type=F32))


def _scan_order_mask(d, chunk):
    jrow = lax.broadcasted_iota(jnp.int32, (chunk, chunk), 0)
    icol = lax.broadcasted_iota(jnp.int32, (chunk, chunk), 1)
    sign = jnp.where(d == 0, 1, -1)
    return (icol - jrow) * sign >= 0


def _ssd_decay_body(a_ref, dt_ref, cum2_ref, src_ref, toend_ref, ecum_ref, etot_ref, *, chunk):
    d = pl.program_id(1)
    mask01 = _scan_order_mask(d, chunk).astype(BF16)
    for s in range(dt_ref.shape[2] // chunk):
        tok = slice(s * chunk, (s + 1) * chunk)
        dt = dt_ref[0, :, tok]
        cum_t = _dot_f32_by_01(dt * a_ref[0], mask01)
        cum2_t = cum_t * LOG2E
        cum2_ref[0, 0, :, tok] = cum2_t
        src_ref[0, 0, tok, :] = (cum2_t - jnp.log2(dt)).T
        tot = jnp.where(d == 0, cum_t[:, chunk - 1:chunk], cum_t[:, 0:1])
        toend_ref[0, 0, :, tok] = jnp.exp(tot - cum_t) * dt
        ecum_ref[0, 0, :, tok] = jnp.exp(cum_t)
        etot_ref[0, 0, :, tok] = jnp.broadcast_to(jnp.exp(tot), cum_t.shape)


def _ssd_decay(a, dtt, chunk):
    b, _, s = dtt.shape
    span = min(s, 8 * chunk)
    body = functools.partial(_ssd_decay_body, chunk=chunk)
    by_head = pl.BlockSpec((1, 1, SSM_HEADS, span), lambda i, d, g: (i, d, 0, g))
    return pl.pallas_call(
        body,
        grid=(b, 2, s // span),
        in_specs=[pl.BlockSpec((1, SSM_HEADS, 1), lambda i, d, g: (d, 0, 0)),
                  pl.BlockSpec((1, SSM_HEADS, span), lambda i, d, g: (i, d, g))],
        out_specs=[by_head,
                   pl.BlockSpec((1, 1, span, SSM_HEADS), lambda i, d, g: (i, d, g, 0)),
                   by_head, by_head, by_head],
        out_shape=[jax.ShapeDtypeStruct((b, 2, SSM_HEADS, s), F32),
                   jax.ShapeDtypeStruct((b, 2, s, SSM_HEADS), F32)]
        + [jax.ShapeDtypeStruct((b, 2, SSM_HEADS, s), F32)] * 3,
        compiler_params=_cparams("parallel", "parallel", "parallel"),
        name="ssd_decay_factors",
    )(a, dtt)


def _scan_body(cum2_ref, src_ref, toend_ref, ecum_ref, etot_ref, xs_ref, ct_ref, b_ref, s0_ref,
               y_ref, sout_ref, state, *, chunk):
    d = pl.program_id(1)
    c = pl.program_id(2)
    nc = pl.num_programs(2)
    hp = SSM_HEADS_PER_GROUP
    hd = SSM_D_INNER // SSM_HEADS
    gw = hp * hd

    @pl.when(c == 0)
    def _load_state():
        state[...] = s0_ref[0, 0]

    not_yet = jnp.where(_scan_order_mask(d, chunk), 0.0, -jnp.inf)
    cum2_t = cum2_ref[0, 0]
    src_term = src_ref[0, 0]
    to_end = toend_ref[0, 0]
    ecum = ecum_ref[0, 0]
    etot = etot_ref[0, 0][:, 0:1]

    for g in range(SSM_GROUPS):
        bg = b_ref[0, :, g * SSM_STATE:(g + 1) * SSM_STATE]
        ctg = ct_ref[0, g * SSM_STATE:(g + 1) * SSM_STATE, :]
        cbt = jnp.dot(bg, ctg, preferred_element_type=F32)
        xg = xs_ref[0, g * gw:(g + 1) * gw, :]
        sg = state[g * gw:(g + 1) * gw, :]
        hs = slice(g * hp, (g + 1) * hp)
        y_state = (jnp.dot(sg.astype(BF16), ctg, preferred_element_type=F32)
                   .reshape(hp, hd, chunk) * ecum[hs][:, None, :])
        outs = []
        for r in range(hp):
            h = g * hp + r
            seg = cum2_t[h:h + 1, :] - src_term[:, h:h + 1]
            w = (cbt * jnp.exp2(seg + not_yet)).astype(BF16)
            outs.append(jnp.dot(xg[r * hd:(r + 1) * hd], w, preferred_element_type=F32)
                        + y_state[r])
        y_ref[0, 0, g * gw:(g + 1) * gw, :] = jnp.concatenate(outs, axis=0).astype(BF16)
        xw = ((xg.astype(F32).reshape(hp, hd, chunk) * to_end[hs][:, None, :])
              .reshape(gw, chunk).astype(BF16))
        upd = jnp.dot(xw, bg, preferred_element_type=F32)
        decayed = (sg.reshape(hp, hd, SSM_STATE) * etot[hs][:, None, :]).reshape(gw, SSM_STATE)
        state[g * gw:(g + 1) * gw, :] = decayed + upd

    @pl.when(c == nc - 1)
    def _store_state():
        sout_ref[0, 0] = state[...]


def _ssd_scan(a, dtt, xst, ct, bm, s0):
    b, ch, s = xst.shape
    chunk = min(SSD_CHUNK, s)
    nc = s // chunk
    bn = SSM_GROUPS * SSM_STATE
    cum2, src, toend, ecum, etot = _ssd_decay(a, dtt, chunk)

    def cidx(d, c):
        return c + d * (nc - 1 - 2 * c)

    by_head = pl.BlockSpec((1, 1, SSM_HEADS, chunk), lambda i, d, c: (i, d, 0, cidx(d, c)))
    body = functools.partial(_scan_body, chunk=chunk)
    return pl.pallas_call(
        body,
        grid=(b, 2, nc),
        in_specs=[by_head,
                  pl.BlockSpec((1, 1, chunk, SSM_HEADS), lambda i, d, c: (i, d, cidx(d, c), 0)),
                  by_head, by_head, by_head,
                  pl.BlockSpec((1, ch, chunk), lambda i, d, c: (i, 0, cidx(d, c))),
                  pl.BlockSpec((1, bn, chunk), lambda i, d, c: (i, 0, cidx(d, c))),
                  pl.BlockSpec((1, chunk, bn), lambda i, d, c: (i, cidx(d, c), 0)),
                  pl.BlockSpec((1, 1, ch, SSM_STATE), lambda i, d, c: (d, i, 0, 0))],
        out_specs=[pl.BlockSpec((1, 1, ch, chunk), lambda i, d, c: (d, i, 0, cidx(d, c))),
                   pl.BlockSpec((1, 1, ch, SSM_STATE), lambda i, d, c: (d, i, 0, 0))],
        out_shape=[jax.ShapeDtypeStruct((2, b, ch, s), BF16),
                   jax.ShapeDtypeStruct((2, b, ch, SSM_STATE), F32)],
        scratch_shapes=[pltpu.VMEM((ch, SSM_STATE), F32)],
        compiler_params=_cparams("parallel", "parallel", "arbitrary"),
        name="ssd_scan",
    )(cum2, src, toend, ecum, etot, xst, ct, bm, s0)


def _ssm_out_body(x_ref, mod_ref, yf_ref, yb_ref, xs_ref, z_ref, dsk_ref, ng_ref, wt_ref, o_ref):
    y = (yf_ref[0, 0].astype(F32) + yb_ref[0, 0].astype(F32)
         + dsk_ref[...] * xs_ref[0].astype(F32))
    y = y * _silu(z_ref[0].astype(F32))
    ms = jnp.mean(y * y, axis=0, keepdims=True)
    yn = (y * lax.rsqrt(ms + NORM_EPS) * ng_ref[...]).astype(BF16)
    ot = jnp.dot(wt_ref[...], yn, preferred_element_type=F32)
    o_ref[0] = x_ref[0] + mod_ref[0][2:3] * ot.T


def _ssm_out(x, mod_l, mod_row, y, xst, zt, dsk, ng, wt, *, tm):
    b, s, d = x.shape
    tm = min(tm, s)
    ch = xst.shape[1]
    return pl.pallas_call(
        _ssm_out_body,
        grid=(b, s // tm),
        in_specs=[pl.BlockSpec((1, tm, d), lambda i, t: (i, t, 0)),
                  pl.BlockSpec((1, 6, d), lambda i, t: (mod_row(i), 0, 0)),
                  pl.BlockSpec((1, 1, ch, tm), lambda i, t: (0, i, 0, t)),
                  pl.BlockSpec((1, 1, ch, tm), lambda i, t: (1, i, 0, t)),
                  pl.BlockSpec((1, ch, tm), lambda i, t: (i, 0, t)),
                  pl.BlockSpec((1, ch, tm), lambda i, t: (i, 0, t)),
                  _resident((ch, 1)),
                  _resident((ch, 1)),
                  _resident((d, ch))],
        out_specs=pl.BlockSpec((1, tm, d), lambda i, t: (i, t, 0)),
        out_shape=jax.ShapeDtypeStruct((b, s, d), F32),
        compiler_params=_cparams("parallel", "parallel"),
        name="ssm_out_residual",
    )(x, mod_l, y, y, xst, zt, dsk, ng, wt)


def _rope_tables_t(n):
    t = jnp.arange(n)
    inv_freq = 1.0 / (ROPE_BASE ** (jnp.arange(ROPE_PAIRS, dtype=F32) / ROPE_PAIRS))
    ang_r = (t // GRID_W).astype(F32)[None, :] * inv_freq[:, None]
    ang_c = (t % GRID_W).astype(F32)[None, :] * inv_freq[:, None]
    cr, sr, cc, sc = jnp.cos(ang_r), jnp.sin(ang_r), jnp.cos(ang_c), jnp.sin(ang_c)
    return (jnp.concatenate([cr, cr, cc, cc], axis=0),
            jnp.concatenate([-sr, sr, -sc, sc], axis=0))


def _attn_layer(x, ctx, mod_l, lat_row, ctx_row, p, rope, lambda_init, with_ctx):
    wt = p["w_in"].T.astype(BF16)
    g = p["norm_g"].reshape(1, D_MODEL)
    qg = p["q_norm_g"].reshape(HEAD_DIM, 1)
    kg = p["k_norm_g"].reshape(HEAD_DIM, 1)
    cos_t, sin_t = rope
    c = ctx.shape[1]
    qt, *lat_kv = _attn_in(x, mod_l, lat_row, g, wt, cos_t, sin_t, qg, kg, rope=True, tm=512)
    qtc, *ctx_kv = _attn_in(ctx, mod_l, ctx_row, g, wt, cos_t[:, :c], sin_t[:, :c], qg, kg,
                            rope=False, tm=256)
    lam_vecs = [p[k].reshape(1, HEAD_DIM) for k in ("lq1", "lk1", "lq2", "lk2")]
    sg = p["subln_g"].reshape(V_ROWS, 1)
    w_out = p["w_out"].astype(BF16)
    x = _flash(x, mod_l, lat_row, w_out, lam_vecs, sg, qt, ctx_kv, lat_kv,
               lambda_init=lambda_init, tq=256, tk=2048)
    if with_ctx:
        ctx = _flash(ctx, mod_l, ctx_row, w_out, lam_vecs, sg, qtc, ctx_kv, None,
                     lambda_init=lambda_init, tq=256, tk=0)
    return x, ctx


def _ssm_layer(x, ctx, mod_l, lat_row, ctx_row, p, with_ctx):
    wt = p["w_in"].T.astype(BF16)
    g = p["norm_g"].reshape(1, D_MODEL)
    dtb = p["dt_bias"].reshape(2 * SSM_HEADS, 1)
    conv_w = jnp.broadcast_to(p["conv_w"][:, :, None], (3, SSM_XBC, V7X_LANES))
    conv_b = jnp.broadcast_to(p["conv_b"][:, None], (SSM_XBC, V7X_LANES))
    a = (-jnp.exp(p["a_log"].astype(F32))).reshape(2, SSM_HEADS, 1)
    b = x.shape[0]

    def pre(v, row, tm):
        zt, xbct, dtt = _ssm_proj(v, mod_l, row, g, wt, dtb, tm=tm)
        xst, ct, bm = _ssm_conv(xbct, conv_w, conv_b, tc=512)
        return zt, xst, ct, bm, dtt

    zt_c, xst_c, ct_c, bm_c, dtt_c = pre(ctx, ctx_row, 256)
    zt_l, xst_l, ct_l, bm_l, dtt_l = pre(x, lat_row, 512)
    zero = jnp.zeros((2, b, SSM_D_INNER, SSM_STATE), F32)
    y_c, s_ctx = _ssd_scan(a, dtt_c, xst_c, ct_c, bm_c, zero)
    y_l, _ = _ssd_scan(a, dtt_l, xst_l, ct_l, bm_l, s_ctx)
    dsk = jnp.repeat(p["d_skip"], SSM_D_INNER // SSM_HEADS).reshape(SSM_D_INNER, 1)
    ng = p["out_norm_g"].reshape(SSM_D_INNER, 1)
    w_out_t = p["w_out"].T.astype(BF16)
    x = _ssm_out(x, mod_l, lat_row, y_l, xst_l, zt_l, dsk, ng, w_out_t, tm=512)
    if with_ctx:
        ctx = _ssm_out(ctx, mod_l, ctx_row, y_c, xst_c, zt_c, dsk, ng, w_out_t, tm=256)
    return x, ctx


def kernel(x, c, ctx, c_ctx, mod_w, mod_b, norm_mix_g, norm_ffn_g, attn_w_in, attn_w_out,
           diff_lq1, diff_lk1, diff_lq2, diff_lk2, diff_subln_g, gqa_q_norm_g, gqa_k_norm_g,
           ssm_w_in, ssm_conv_w, ssm_conv_b, ssm_dt_bias, ssm_a_log, ssm_d, ssm_norm_g, ssm_w_out,
           ffn_w_in, ffn_conv_w, ffn_conv_b, ffn_w_out, final_norm_g):
    b, n, d = x.shape
    mod_rows = 16
    c_rows = jnp.zeros((mod_rows, d), F32).at[:b].set(c).at[b].set(c_ctx)
    mod = _mod_all(c_rows, mod_w, mod_b).reshape(DEPTH, mod_rows, 6, d)
    lat_row = lambda i: i
    ctx_row = lambda i: b
    rope = _rope_tables_t(n)

    for layer in range(DEPTH):
        with_ctx = layer < DEPTH - 1
        mod_l = mod[layer]
        i = layer // 2
        if layer % 2 == 0:
            p = dict(w_in=attn_w_in[i], w_out=attn_w_out[i], norm_g=norm_mix_g[layer],
                     lq1=diff_lq1[i], lk1=diff_lk1[i], lq2=diff_lq2[i], lk2=diff_lk2[i],
                     subln_g=diff_subln_g[i], q_norm_g=gqa_q_norm_g[i], k_norm_g=gqa_k_norm_g[i])
            lambda_init = 0.8 - 0.6 * math.exp(-0.3 * layer)
            x, ctx = _attn_layer(x, ctx, mod_l, lat_row, ctx_row, p, rope, lambda_init, with_ctx)
        else:
            p = dict(w_in=ssm_w_in[i], norm_g=norm_mix_g[layer], conv_w=ssm_conv_w[i],
                     conv_b=ssm_conv_b[i], dt_bias=ssm_dt_bias[i], a_log=ssm_a_log[i],
                     d_skip=ssm_d[i], out_norm_g=ssm_norm_g[i], w_out=ssm_w_out[i])
            x, ctx = _ssm_layer(x, ctx, mod_l, lat_row, ctx_row, p, with_ctx)
        g = norm_ffn_g[layer].reshape(1, d)
        wv = ffn_w_in[layer][:, :D_FF].astype(BF16)
        wg = ffn_w_in[layer][:, D_FF:].astype(BF16)
        cw = ffn_conv_w[layer]
        cb = ffn_conv_b[layer].reshape(1, D_FF)
        wo = ffn_w_out[layer].astype(BF16)
        fg = final_norm_g.reshape(1, d)
        x = _ffn(x, mod_l, lat_row, g, wv, wg, cw, cb, wo, fg, tm=256,
                 final_norm=layer == DEPTH - 1)
        if with_ctx:
            ctx = _ffn(ctx, mod_l, ctx_row, g, wv, wg, cw, cb, wo, fg, tm=256, final_norm=False)
    return x
```

```python
import functools
import math

import jax
import jax.numpy as jnp
from jax import lax
from jax.experimental import pallas as pl
from jax.experimental.pallas import tpu as pltpu

F32 = jnp.float32
BF16 = jnp.bfloat16

D_MODEL = 1024
DEPTH = 4
GRID_W = 64
HEAD_DIM = 64
ROPE_PAIRS = HEAD_DIM // 4
ROPE_BASE = 10000.0
NORM_EPS = 1e-6
DIFF_HEADS = 4
GQA_HEADS = 8
GQA_KV_HEADS = 2
GQA_GROUP = GQA_HEADS // GQA_KV_HEADS
ATTN_IN_COLS = 2304
SSM_D_INNER = 2048
SSM_HEADS = 32
SSM_GROUPS = 4
SSM_HEADS_PER_GROUP = SSM_HEADS // SSM_GROUPS
SSM_STATE = 128
SSM_XBC = 3072
SSM_IN_COLS = 5184
D_FF = 2816
LOG2E = math.log2(math.e)

V7X_VMEM_BYTES = 64 * 1024 * 1024
VMEM_LIMIT_BYTES = V7X_VMEM_BYTES - 8 * 1024 * 1024
V7X_LANES = 128
BF16_SUBLANES = 16

V_ROWS = 2 * HEAD_DIM
V_ROWS_PADDED = V_ROWS + BF16_SUBLANES
GV_ROWS_PADDED = V_ROWS_PADDED
K_COLS = (DIFF_HEADS + 1) * 2 * HEAD_DIM

ATTN_KEY_SUBBLOCK = 256
ATTN_STREAM_MAX_OCTAVES = 64.0

CONV_HALO = 16
SSD_CHUNK = 256

NT_DIMS = (((1,), (1,)), ((), ()))


def _cparams(*sem):
    return pltpu.CompilerParams(dimension_semantics=sem, vmem_limit_bytes=VMEM_LIMIT_BYTES)


def _resident(shape):
    nd = len(shape)
    return pl.BlockSpec(shape, lambda *_: (0,) * nd, pipeline_mode=pl.Buffered(1))


def _silu(v):
    return v * jax.nn.sigmoid(v)


def _modulate(x, g, shift, scale):
    ms = jnp.mean(x * x, axis=-1, keepdims=True)
    return (x * lax.rsqrt(ms + NORM_EPS) * g) * (1.0 + scale) + shift


def _mod_body(c_ref, w_ref, b_ref, o_ref):
    s = _silu(c_ref[...])
    o_ref[0] = jnp.dot(s, w_ref[0], preferred_element_type=F32,
                       precision=lax.Precision.HIGHEST) + b_ref[0]


def _mod_all(c_rows, mod_w, mod_b):
    rows = c_rows.shape[0]
    depth, d, cols = mod_w.shape
    tn = 2048
    return pl.pallas_call(
        _mod_body,
        grid=(depth, cols // tn),
        in_specs=[pl.BlockSpec((rows, d), lambda l, n: (0, 0)),
                  pl.BlockSpec((1, d, tn), lambda l, n: (l, 0, n)),
                  pl.BlockSpec((1, 1, tn), lambda l, n: (l, 0, n))],
        out_specs=pl.BlockSpec((1, rows, tn), lambda l, n: (l, 0, n)),
        out_shape=jax.ShapeDtypeStruct((depth, rows, cols), F32),
        compiler_params=_cparams("parallel", "parallel"),
        name="mod_vectors",
    )(c_rows, mod_w, mod_b.reshape(depth, 1, cols))


def _attn_in_body(x_ref, mod_ref, g_ref, wt_ref, cos_ref, sin_ref, qg_ref, kg_ref,
                  qt_ref, k_ref, vta_ref, vtb_ref, *, rope):
    m = mod_ref[0]
    h = _modulate(x_ref[0], g_ref[...], m[0:1], m[1:2]).astype(BF16)
    t = lax.dot_general(wt_ref[...], h, NT_DIMS, preferred_element_type=F32)
    tm = t.shape[1]

    def rot(u):
        if not rope:
            return u
        sw = jnp.concatenate([u[:, 16:32], u[:, 0:16], u[:, 48:64], u[:, 32:48]], axis=1)
        return u * cos_ref[...][None] + sw * sin_ref[...][None]

    def qk_norm(u, g):
        ms = jnp.mean(u * u, axis=1, keepdims=True)
        return u * lax.rsqrt(ms + NORM_EPS) * g[None]

    nq = 2 * DIFF_HEADS
    qa = rot(t[0:512].reshape(nq, HEAD_DIM, tm))
    ka = rot(t[512:1024].reshape(nq, HEAD_DIM, tm))
    va = t[1024:1536]
    qb = rot(qk_norm(t[1536:2048].reshape(GQA_HEADS, HEAD_DIM, tm), qg_ref[...]))
    kb = rot(qk_norm(t[2048:2176].reshape(GQA_KV_HEADS, HEAD_DIM, tm), kg_ref[...]))
    vb = t[2176:2304]

    qs = (HEAD_DIM ** -0.5) * LOG2E
    qt_ref[0, 0:512] = (qa * qs).reshape(512, tm).astype(BF16)
    qt_ref[0, 512:1024] = (qb * qs).reshape(512, tm).astype(BF16)
    kt = jnp.concatenate([ka.reshape(512, tm), kb.reshape(128, tm)], axis=0)
    k_ref[0] = kt.T.astype(BF16)
    ones = jnp.ones((BF16_SUBLANES, tm), BF16)
    for u in range(DIFF_HEADS):
        vta_ref[0, u, 0:V_ROWS] = va[u * V_ROWS:(u + 1) * V_ROWS].astype(BF16)
        vta_ref[0, u, V_ROWS:V_ROWS_PADDED] = ones
    for g in range(GQA_KV_HEADS):
        vtb_ref[0, g, 0:HEAD_DIM] = vb[g * HEAD_DIM:(g + 1) * HEAD_DIM].astype(BF16)
        vtb_ref[0, g, HEAD_DIM:HEAD_DIM + BF16_SUBLANES] = ones
        vtb_ref[0, g, HEAD_DIM + BF16_SUBLANES:GV_ROWS_PADDED] = jnp.zeros(
            (GV_ROWS_PADDED - HEAD_DIM - BF16_SUBLANES, tm), BF16)


def _attn_in(x, mod_l, mod_row, g, wt, cos_t, sin_t, qg, kg, *, rope, tm):
    b, s, d = x.shape
    tm = min(tm, s)
    body = functools.partial(_attn_in_body, rope=rope)
    return pl.pallas_call(
        body,
        grid=(b, s // tm),
        in_specs=[pl.BlockSpec((1, tm, d), lambda i, t: (i, t, 0)),
                  pl.BlockSpec((1, 6, d), lambda i, t: (mod_row(i), 0, 0)),
                  _resident((1, d)),
                  _resident((ATTN_IN_COLS, d)),
                  pl.BlockSpec((HEAD_DIM, tm), lambda i, t: (0, t)),
                  pl.BlockSpec((HEAD_DIM, tm), lambda i, t: (0, t)),
                  _resident((HEAD_DIM, 1)),
                  _resident((HEAD_DIM, 1))],
        out_specs=[pl.BlockSpec((1, 1024, tm), lambda i, t: (i, 0, t)),
                   pl.BlockSpec((1, tm, K_COLS), lambda i, t: (i, t, 0)),
                   pl.BlockSpec((1, DIFF_HEADS, V_ROWS_PADDED, tm), lambda i, t: (i, 0, 0, t)),
                   pl.BlockSpec((1, GQA_KV_HEADS, GV_ROWS_PADDED, tm), lambda i, t: (i, 0, 0, t))],
        out_shape=[jax.ShapeDtypeStruct((b, 1024, s), BF16),
                   jax.ShapeDtypeStruct((b, s, K_COLS), BF16),
                   jax.ShapeDtypeStruct((b, DIFF_HEADS, V_ROWS_PADDED, s), BF16),
                   jax.ShapeDtypeStruct((b, GQA_KV_HEADS, GV_ROWS_PADDED, s), BF16)],
        compiler_params=_cparams("parallel", "parallel"),
        name="attn_in_rope" if rope else "attn_in_ctx",
    )(x, mod_l, g, wt, cos_t, sin_t, qg, kg)


def _attn_unit_cols(tq):
    units = []
    for h in range(DIFF_HEADS):
        units.append((h * 128, h, V_ROWS_PADDED, h * 2 * tq, 2 * tq))
    base = DIFF_HEADS * 2 * tq
    for g in range(GQA_KV_HEADS):
        units.append((DIFF_HEADS * 128, g, GV_ROWS_PADDED, base + g * GQA_GROUP * tq,
                      GQA_GROUP * tq))
    return units


def _flash_body(*refs, tq, lambda_init, has_lat):
    if has_lat:
        (lq1, lk1, lq2, lk2, sg_ref, x_ref, mod_ref, wo_ref, qt_ref, kc_ref, vtac_ref, vtbc_ref,
         kl_ref, vtal_ref, vtbl_ref, o_ref, rhs, acc, mrow, pv_new, m_chunk) = refs
    else:
        (lq1, lk1, lq2, lk2, sg_ref, x_ref, mod_ref, wo_ref, qt_ref, kc_ref, vtac_ref, vtbc_ref,
         o_ref, rhs, acc, mrow, pv_new, m_chunk) = refs
    j = pl.program_id(2)
    nj = pl.num_programs(2)
    units = _attn_unit_cols(tq)
    diff_cols = DIFF_HEADS * 2 * tq

    def values(vta_ref, vtb_ref, u, vu):
        return vta_ref[0, vu] if u < DIFF_HEADS else vtb_ref[0, vu]

    def exact_step(k_ref, vta_ref, vtb_ref):
        for u, (kc0, vu, vr, c0, w) in enumerate(units):
            s = jnp.dot(k_ref[0, :, kc0:kc0 + 128], rhs[:, c0:c0 + w],
                        preferred_element_type=F32)
            mp = mrow[:, c0:c0 + w]
            mn = jnp.maximum(mp, jnp.max(s, axis=0, keepdims=True))
            alpha = jnp.exp2(mp - mn)
            p = jnp.exp2(s - mn).astype(BF16)
            pv = jnp.dot(values(vta_ref, vtb_ref, u, vu), p, preferred_element_type=F32)
            acc[0:vr, c0:c0 + w] = acc[0:vr, c0:c0 + w] * alpha + pv
            mrow[:, c0:c0 + w] = mn

    def streaming_step(k_ref, vta_ref, vtb_ref):
        nk = k_ref.shape[1]
        sub = min(nk, ATTN_KEY_SUBBLOCK)
        for u, (kc0, vu, vr, c0, w) in enumerate(units):
            m_used = mrow[:, c0:c0 + w]
            cm = None
            parts = []
            for r in range(nk // sub):
                s = jnp.dot(k_ref[0, r * sub:(r + 1) * sub, kc0:kc0 + 128], rhs[:, c0:c0 + w],
                            preferred_element_type=F32)
                parts.append(jnp.exp2(s - m_used).astype(BF16))
                sm = jnp.max(s.reshape(sub // 8, 8, w), axis=0)
                cm = sm if cm is None else jnp.maximum(cm, sm)
            p = jnp.concatenate(parts, axis=0)
            pv_new[0:vr, c0:c0 + w] = jnp.dot(values(vta_ref, vtb_ref, u, vu), p,
                                              preferred_element_type=F32)
            m_chunk[:, c0:c0 + w] = jnp.max(cm, axis=0, keepdims=True)

    @pl.when(j == 0)
    def _init():
        zeros = jnp.zeros((HEAD_DIM, tq), BF16)
        for h in range(DIFF_HEADS):
            c0 = h * 2 * tq
            rhs[0:64, c0:c0 + tq] = qt_ref[0, h * 128:h * 128 + 64, :]
            rhs[64:128, c0:c0 + tq] = zeros
            rhs[0:64, c0 + tq:c0 + 2 * tq] = zeros
            rhs[64:128, c0 + tq:c0 + 2 * tq] = qt_ref[0, h * 128 + 64:h * 128 + 128, :]
        base = DIFF_HEADS * 2 * tq
        for g in range(GQA_KV_HEADS):
            for r in range(GQA_GROUP):
                c0 = base + (g * GQA_GROUP + r) * tq
                hd = 512 + (g * GQA_GROUP + r) * HEAD_DIM
                rhs[g * 64:(g + 1) * 64, c0:c0 + tq] = qt_ref[0, hd:hd + HEAD_DIM, :]
                rhs[(1 - g) * 64:(2 - g) * 64, c0:c0 + tq] = zeros
        acc[...] = jnp.zeros(acc.shape, F32)
        mrow[...] = jnp.zeros(mrow.shape, F32)

    def stream_chunk(k_ref, vta_ref, vtb_ref, first):
        streaming_step(k_ref, vta_ref, vtb_ref)
        dev = m_chunk[...] - mrow[...]
        in_range = jnp.max(jnp.abs(dev) if first else dev) <= ATTN_STREAM_MAX_OCTAVES

        @pl.when(in_range)
        def _commit():
            mp = mrow[...]
            mn = jnp.maximum(mp, m_chunk[...])
            alpha = jnp.exp2(mp - mn)
            for rows, cols in ((V_ROWS_PADDED, slice(0, diff_cols)),
                               (GV_ROWS_PADDED, slice(diff_cols, acc.shape[1]))):
                acc[0:rows, cols] = (acc[0:rows, cols] + pv_new[0:rows, cols]) * alpha[:, cols]
            mrow[...] = mn

        @pl.when(jnp.logical_not(in_range))
        def _redo():
            if first:
                mrow[...] = jnp.full(mrow.shape, -jnp.inf, F32)
            exact_step(k_ref, vta_ref, vtb_ref)

    @pl.when(j == 0)
    def _context_keys():
        stream_chunk(kc_ref, vtac_ref, vtbc_ref, True)

    if has_lat:
        stream_chunk(kl_ref, vtal_ref, vtbl_ref, False)

    @pl.when(j == nj - 1)
    def _finish():
        lam = (jnp.exp(jnp.sum(lq1[...] * lk1[...], keepdims=True))
               - jnp.exp(jnp.sum(lq2[...] * lk2[...], keepdims=True)) + lambda_init)
        pieces = []
        for h in range(DIFF_HEADS):
            c0 = h * 2 * tq
            o1 = acc[0:V_ROWS, c0:c0 + tq] / acc[V_ROWS:V_ROWS + 1, c0:c0 + tq]
            o2 = acc[0:V_ROWS, c0 + tq:c0 + 2 * tq] / acc[V_ROWS:V_ROWS + 1, c0 + tq:c0 + 2 * tq]
            oh = o1 - lam * o2
            ms = jnp.mean(oh * oh, axis=0, keepdims=True)
            pieces.append(oh * lax.rsqrt(ms + NORM_EPS) * sg_ref[...] * (1.0 - lambda_init))
        base = DIFF_HEADS * 2 * tq
        for g in range(GQA_KV_HEADS):
            for r in range(GQA_GROUP):
                c0 = base + (g * GQA_GROUP + r) * tq
                pieces.append(acc[0:HEAD_DIM, c0:c0 + tq] / acc[HEAD_DIM:HEAD_DIM + 1, c0:c0 + tq])
        o = jnp.concatenate(pieces, axis=0).T.astype(BF16)
        y = jnp.dot(o, wo_ref[...], preferred_element_type=F32)
        o_ref[0] = x_ref[0] + mod_ref[0][2:3] * y


def _flash(x, mod_l, mod_row, w_out, lam_vecs, sg, qt, ctx_kv, lat_kv, *, lambda_init, tq, tk):
    b, _, sq = qt.shape
    d = x.shape[2]
    kc, vtac, vtbc = ctx_kv
    c = kc.shape[1]
    has_lat = lat_kv is not None
    tq = min(tq, sq)
    ncols = (DIFF_HEADS * 2 + GQA_HEADS) * tq
    in_specs = [_resident((1, HEAD_DIM))] * 4 + [
        _resident((V_ROWS, 1)),
        pl.BlockSpec((1, tq, d), lambda i, q, j: (i, q, 0)),
        pl.BlockSpec((1, 6, d), lambda i, q, j: (mod_row(i), 0, 0)),
        _resident((1024, d)),
        pl.BlockSpec((1, 1024, tq), lambda i, q, j: (i, 0, q)),
        pl.BlockSpec((1, c, K_COLS), lambda i, q, j: (i, 0, 0)),
        pl.BlockSpec((1, DIFF_HEADS, V_ROWS_PADDED, c), lambda i, q, j: (i, 0, 0, 0)),
        pl.BlockSpec((1, GQA_KV_HEADS, GV_ROWS_PADDED, c), lambda i, q, j: (i, 0, 0, 0)),
    ]
    args = list(lam_vecs) + [sg, x, mod_l, w_out, qt, kc, vtac, vtbc]
    nkv = 1
    if has_lat:
        kl, vtal, vtbl = lat_kv
        n = kl.shape[1]
        tk = min(tk, n)
        nkv = n // tk
        in_specs += [pl.BlockSpec((1, tk, K_COLS), lambda i, q, j: (i, j, 0)),
                     pl.BlockSpec((1, DIFF_HEADS, V_ROWS_PADDED, tk), lambda i, q, j: (i, 0, 0, j)),
                     pl.BlockSpec((1, GQA_KV_HEADS, GV_ROWS_PADDED, tk), lambda i, q, j: (i, 0, 0, j))]
        args += [kl, vtal, vtbl]
    scratch = [pltpu.VMEM((2 * HEAD_DIM, ncols), BF16),
               pltpu.VMEM((V_ROWS_PADDED, ncols), F32),
               pltpu.VMEM((1, ncols), F32),
               pltpu.VMEM((V_ROWS_PADDED, ncols), F32),
               pltpu.VMEM((1, ncols), F32)]
    body = functools.partial(_flash_body, tq=tq, lambda_init=lambda_init, has_lat=has_lat)
    return pl.pallas_call(
        body,
        grid=(b, sq // tq, nkv),
        in_specs=in_specs,
        out_specs=pl.BlockSpec((1, tq, d), lambda i, q, j: (i, q, 0)),
        out_shape=jax.ShapeDtypeStruct((b, sq, d), F32),
        scratch_shapes=scratch,
        compiler_params=_cparams("parallel", "parallel", "arbitrary"),
        name="attn_sweep_lat" if has_lat else "attn_sweep_ctx",
    )(*args)


def _ffn_body(x_ref, xp_ref, xn_ref, mod_ref, g_ref, wv_ref, wg_ref, cw_ref, cb_ref, wo_ref,
              fg_ref, o_ref, *, tm, final_norm):
    t = pl.program_id(1)
    nt = pl.num_programs(1)
    m = mod_ref[0]
    x = x_ref[0]
    xe = jnp.concatenate([xp_ref[0], x, xn_ref[0]], axis=0)
    he = _modulate(xe, g_ref[...], m[3:4], m[4:5]).astype(BF16)
    ge = jnp.dot(he, wg_ref[...], preferred_element_type=F32)
    rows = lax.broadcasted_iota(jnp.int32, (tm + 2 * CONV_HALO, 1), 0)
    lo = jnp.where(t > 0, 0, CONV_HALO)
    hi = jnp.where(t < nt - 1, tm + 2 * CONV_HALO, tm + CONV_HALO)
    ge = jnp.where((rows >= lo) & (rows < hi), ge, 0.0)
    val = jnp.dot(he[CONV_HALO:CONV_HALO + tm], wv_ref[...], preferred_element_type=F32)
    cw = cw_ref[...]
    h0 = CONV_HALO
    conv = (cb_ref[...] + cw[0:1] * ge[h0 - 1:h0 - 1 + tm] + cw[1:2] * ge[h0:h0 + tm]
            + cw[2:3] * ge[h0 + 1:h0 + 1 + tm])
    gelu = 0.5 * conv * (1.0 + lax.erf(conv * math.sqrt(0.5)))
    act = (gelu * val).astype(BF16)
    y = jnp.dot(act, wo_ref[...], preferred_element_type=F32)
    out = x + m[5:6] * y
    if final_norm:
        ms = jnp.mean(out * out, axis=-1, keepdims=True)
        out = out * lax.rsqrt(ms + NORM_EPS) * fg_ref[...]
    o_ref[0] = out


def _ffn(x, mod_l, mod_row, g, wv, wg, cw, cb, wo, fg, *, tm, final_norm):
    b, s, d = x.shape
    tm = min(tm, s)
    hb = tm // CONV_HALO
    last = s // CONV_HALO - 1
    body = functools.partial(_ffn_body, tm=tm, final_norm=final_norm)
    return pl.pallas_call(
        body,
        grid=(b, s // tm),
        in_specs=[pl.BlockSpec((1, tm, d), lambda i, t: (i, t, 0)),
                  pl.BlockSpec((1, CONV_HALO, d), lambda i, t: (i, jnp.maximum(t * hb - 1, 0), 0)),
                  pl.BlockSpec((1, CONV_HALO, d), lambda i, t: (i, jnp.minimum((t + 1) * hb, last), 0)),
                  pl.BlockSpec((1, 6, d), lambda i, t: (mod_row(i), 0, 0)),
                  _resident((1, d)),
                  _resident((d, D_FF)),
                  _resident((d, D_FF)),
                  _resident((3, D_FF)),
                  _resident((1, D_FF)),
                  _resident((D_FF, d)),
                  _resident((1, d))],
        out_specs=pl.BlockSpec((1, tm, d), lambda i, t: (i, t, 0)),
        out_shape=jax.ShapeDtypeStruct((b, s, d), F32),
        compiler_params=_cparams("parallel", "parallel"),
        name="conv_glu_ffn",
    )(x, x, x, mod_l, g, wv, wg, cw, cb, wo, fg)


def _ssm_proj_body(x_ref, mod_ref, g_ref, wt_ref, dtb_ref, zt_ref, xbct_ref, dtt_ref):
    m = mod_ref[0]
    h = _modulate(x_ref[0], g_ref[...], m[0:1], m[1:2]).astype(BF16)
    t = lax.dot_general(wt_ref[...], h, NT_DIMS, preferred_element_type=F32)
    zt_ref[0] = t[0:SSM_D_INNER].astype(BF16)
    xbct_ref[0] = t[SSM_D_INNER:SSM_D_INNER + SSM_XBC].astype(BF16)
    dtt_ref[0] = jax.nn.softplus(t[SSM_D_INNER + SSM_XBC:SSM_IN_COLS] + dtb_ref[...])


def _ssm_proj(x, mod_l, mod_row, g, wt, dtb, *, tm):
    b, s, d = x.shape
    tm = min(tm, s)
    return pl.pallas_call(
        _ssm_proj_body,
        grid=(b, s // tm),
        in_specs=[pl.BlockSpec((1, tm, d), lambda i, t: (i, t, 0)),
                  pl.BlockSpec((1, 6, d), lambda i, t: (mod_row(i), 0, 0)),
                  _resident((1, d)),
                  _resident((SSM_IN_COLS, d)),
                  _resident((2 * SSM_HEADS, 1))],
        out_specs=[pl.BlockSpec((1, SSM_D_INNER, tm), lambda i, t: (i, 0, t)),
                   pl.BlockSpec((1, SSM_XBC, tm), lambda i, t: (i, 0, t)),
                   pl.BlockSpec((1, 2 * SSM_HEADS, tm), lambda i, t: (i, 0, t))],
        out_shape=[jax.ShapeDtypeStruct((b, SSM_D_INNER, s), BF16),
                   jax.ShapeDtypeStruct((b, SSM_XBC, s), BF16),
                   jax.ShapeDtypeStruct((b, 2 * SSM_HEADS, s), F32)],
        compiler_params=_cparams("parallel", "parallel"),
        name="ssm_in_proj",
    )(x, mod_l, g, wt, dtb)


def _ssm_conv_body(u_ref, up_ref, un_ref, w_ref, b_ref, xst_ref, ct_ref, bm_ref, *, tc):
    t = pl.program_id(1)
    nt = pl.num_programs(1)
    u = u_ref[0].astype(F32)
    prev = jnp.where(t > 0, up_ref[0][:, V7X_LANES - 1:V7X_LANES].astype(F32), 0.0)
    nxt = jnp.where(t < nt - 1, un_ref[0][:, 0:1].astype(F32), 0.0)
    lane = lax.broadcasted_iota(jnp.int32, (1, tc), 1)
    packed = pltpu.bitcast(u_ref[0], jnp.uint32)
    rolled_l = pltpu.bitcast(pltpu.roll(packed, 1, 1), BF16).astype(F32)
    rolled_r = pltpu.bitcast(pltpu.roll(packed, tc - 1, 1), BF16).astype(F32)
    left = jnp.where(lane == 0, prev, rolled_l)
    right = jnp.where(lane == tc - 1, nxt, rolled_r)
    def lanes(a):
        return jnp.concatenate([a] * (tc // V7X_LANES), axis=1)

    v = _silu(lanes(b_ref[...]) + lanes(w_ref[0]) * left + lanes(w_ref[1]) * u
              + lanes(w_ref[2]) * right)
    bn = SSM_GROUPS * SSM_STATE
    xst_ref[0] = v[0:SSM_D_INNER].astype(BF16)
    bm_ref[0] = v[SSM_D_INNER:SSM_D_INNER + bn].T.astype(BF16)
    ct_ref[0] = v[SSM_D_INNER + bn:SSM_XBC].astype(BF16)


def _ssm_conv(xbct, w, bias, *, tc):
    b, ch, s = xbct.shape
    tc = min(tc, s)
    hb = tc // V7X_LANES
    last = s // V7X_LANES - 1
    bn = SSM_GROUPS * SSM_STATE
    body = functools.partial(_ssm_conv_body, tc=tc)
    return pl.pallas_call(
        body,
        grid=(b, s // tc),
        in_specs=[pl.BlockSpec((1, ch, tc), lambda i, t: (i, 0, t)),
                  pl.BlockSpec((1, ch, V7X_LANES), lambda i, t: (i, 0, jnp.maximum(t * hb - 1, 0))),
                  pl.BlockSpec((1, ch, V7X_LANES), lambda i, t: (i, 0, jnp.minimum((t + 1) * hb, last))),
                  _resident((3, ch, V7X_LANES)),
                  _resident((ch, V7X_LANES))],
        out_specs=[pl.BlockSpec((1, SSM_D_INNER, tc), lambda i, t: (i, 0, t)),
                   pl.BlockSpec((1, bn, tc), lambda i, t: (i, 0, t)),
                   pl.BlockSpec((1, tc, bn), lambda i, t: (i, t, 0))],
        out_shape=[jax.ShapeDtypeStruct((b, SSM_D_INNER, s), BF16),
                   jax.ShapeDtypeStruct((b, bn, s), BF16),
                   jax.ShapeDtypeStruct((b, s, bn), BF16)],
        compiler_params=_cparams("parallel", "parallel"),
        name="ssm_conv_silu",
    )(xbct, xbct, xbct, w, bias)


def _dot_f32_by_01(a, m01):
    hi = a.astype(BF16)
    r1 = a - hi.astype(F32)
    mid = r1.astype(BF16)
    lo = (r1 - mid.astype(F32)).astype(BF16)
    return (jnp.dot(hi, m01, preferred_element_type=F32)
            + jnp.dot(mid, m01, preferred_element_type=F32)
            + jnp.dot(lo, m01, preferred_element_type=F32))


def _scan_order_mask(d, chunk):
    jrow = lax.broadcasted_iota(jnp.int32, (chunk, chunk), 0)
    icol = lax.broadcasted_iota(jnp.int32, (chunk, chunk), 1)
    sign = jnp.where(d == 0, 1, -1)
    return (icol - jrow) * sign >= 0


def _ssd_decay_body(a_ref, dt_ref, cum2_ref, src_ref, toend_ref, ecum_ref, etot_ref, *, chunk):
    d = pl.program_id(1)
    mask01 = _scan_order_mask(d, chunk).astype(BF16)
    for s in range(dt_ref.shape[2] // chunk):
        tok = slice(s * chunk, (s + 1) * chunk)
        dt = dt_ref[0, :, tok]
        cum_t = _dot_f32_by_01(dt * a_ref[0], mask01)
        cum2_t = cum_t * LOG2E
        cum2_ref[0, 0, :, tok] = cum2_t
        src_ref[0, 0, tok, :] = (cum2_t - jnp.log2(dt)).T
        tot = jnp.where(d == 0, cum_t[:, chunk - 1:chunk], cum_t[:, 0:1])
        toend_ref[0, 0, :, tok] = jnp.exp(tot - cum_t) * dt
        ecum_ref[0, 0, :, tok] = jnp.exp(cum_t)
        etot_ref[0, 0, :, tok] = jnp.broadcast_to(jnp.exp(tot), cum_t.shape)


def _ssd_decay(a, dtt, chunk):
    b, _, s = dtt.shape
    span = min(s, 8 * chunk)
    body = functools.partial(_ssd_decay_body, chunk=chunk)
    by_head = pl.BlockSpec((1, 1, SSM_HEADS, span), lambda i, d, g: (i, d, 0, g))
    return pl.pallas_call(
        body,
        grid=(b, 2, s // span),
        in_specs=[pl.BlockSpec((1, SSM_HEADS, 1), lambda i, d, g: (d, 0, 0)),
                  pl.BlockSpec((1, SSM_HEADS, span), lambda i, d, g: (i, d, g))],
        out_specs=[by_head,
                   pl.BlockSpec((1, 1, span, SSM_HEADS), lambda i, d, g: (i, d, g, 0)),
                   by_head, by_head, by_head],
        out_shape=[jax.ShapeDtypeStruct((b, 2, SSM_HEADS, s), F32),
                   jax.ShapeDtypeStruct((b, 2, s, SSM_HEADS), F32)]
        + [jax.ShapeDtypeStruct((b, 2, SSM_HEADS, s), F32)] * 3,
        compiler_params=_cparams("parallel", "parallel", "parallel"),
        name="ssd_decay_factors",
    )(a, dtt)


def _scan_body(cum2_ref, src_ref, toend_ref, ecum_ref, etot_ref, xs_ref, ct_ref, b_ref, s0_ref,
               y_ref, sout_ref, state, *, chunk):
    d = pl.program_id(1)
    c = pl.program_id(2)
    nc = pl.num_programs(2)
    hp = SSM_HEADS_PER_GROUP
    hd = SSM_D_INNER // SSM_HEADS
    gw = hp * hd

    @pl.when(c == 0)
    def _load_state():
        state[...] = s0_ref[0, 0]

    not_yet = jnp.where(_scan_order_mask(d, chunk), 0.0, -jnp.inf)
    cum2_t = cum2_ref[0, 0]
    src_term = src_ref[0, 0]
    to_end = toend_ref[0, 0]
    ecum = ecum_ref[0, 0]
    etot = etot_ref[0, 0][:, 0:1]

    for g in range(SSM_GROUPS):
        bg = b_ref[0, :, g * SSM_STATE:(g + 1) * SSM_STATE]
        ctg = ct_ref[0, g * SSM_STATE:(g + 1) * SSM_STATE, :]
        cbt = jnp.dot(bg, ctg, preferred_element_type=F32)
        xg = xs_ref[0, g * gw:(g + 1) * gw, :]
        sg = state[g * gw:(g + 1) * gw, :]
        hs = slice(g * hp, (g + 1) * hp)
        y_state = (jnp.dot(sg.astype(BF16), ctg, preferred_element_type=F32)
                   .reshape(hp, hd, chunk) * ecum[hs][:, None, :])
        outs = []
        for r in range(hp):
            h = g * hp + r
            seg = cum2_t[h:h + 1, :] - src_term[:, h:h + 1]
            w = (cbt * jnp.exp2(seg + not_yet)).astype(BF16)
            outs.append(jnp.dot(xg[r * hd:(r + 1) * hd], w, preferred_element_type=F32)
                        + y_state[r])
        y_ref[0, 0, g * gw:(g + 1) * gw, :] = jnp.concatenate(outs, axis=0).astype(BF16)
        xw = ((xg.astype(F32).reshape(hp, hd, chunk) * to_end[hs][:, None, :])
              .reshape(gw, chunk).astype(BF16))
        upd = jnp.dot(xw, bg, preferred_element_type=F32)
        decayed = (sg.reshape(hp, hd, SSM_STATE) * etot[hs][:, None, :]).reshape(gw, SSM_STATE)
        state[g * gw:(g + 1) * gw, :] = decayed + upd

    @pl.when(c == nc - 1)
    def _store_state():
        sout_ref[0, 0] = state[...]


def _ssd_scan(a, dtt, xst, ct, bm, s0):
    b, ch, s = xst.shape
    chunk = min(SSD_CHUNK, s)
    nc = s // chunk
    bn = SSM_GROUPS * SSM_STATE
    cum2, src, toend, ecum, etot = _ssd_decay(a, dtt, chunk)

    def cidx(d, c):
        return c + d * (nc - 1 - 2 * c)

    by_head = pl.BlockSpec((1, 1, SSM_HEADS, chunk), lambda i, d, c: (i, d, 0, cidx(d, c)))
    body = functools.partial(_scan_body, chunk=chunk)
    return pl.pallas_call(
        body,
        grid=(b, 2, nc),
        in_specs=[by_head,
                  pl.BlockSpec((1, 1, chunk, SSM_HEADS), lambda i, d, c: (i, d, cidx(d, c), 0)),
                  by_head, by_head, by_head,
                  pl.BlockSpec((1, ch, chunk), lambda i, d, c: (i, 0, cidx(d, c))),
                  pl.BlockSpec((1, bn, chunk), lambda i, d, c: (i, 0, cidx(d, c))),
                  pl.BlockSpec((1, chunk, bn), lambda i, d, c: (i, cidx(d, c), 0)),
                  pl.BlockSpec((1, 1, ch, SSM_STATE), lambda i, d, c: (d, i, 0, 0))],
        out_specs=[pl.BlockSpec((1, 1, ch, chunk), lambda i, d, c: (d, i, 0, cidx(d, c))),
                   pl.BlockSpec((1, 1, ch, SSM_STATE), lambda i, d, c: (d, i, 0, 0))],
        out_shape=[jax.ShapeDtypeStruct((2, b, ch, s), BF16),
                   jax.ShapeDtypeStruct((2, b, ch, SSM_STATE), F32)],
        scratch_shapes=[pltpu.VMEM((ch, SSM_STATE), F32)],
        compiler_params=_cparams("parallel", "parallel", "arbitrary"),
        name="ssd_scan",
    )(cum2, src, toend, ecum, etot, xst, ct, bm, s0)


def _ssm_out_body(x_ref, mod_ref, yf_ref, yb_ref, xs_ref, z_ref, dsk_ref, ng_ref, wt_ref, o_ref):
    y = (yf_ref[0, 0].astype(F32) + yb_ref[0, 0].astype(F32)
         + dsk_ref[...] * xs_ref[0].astype(F32))
    y = y * _silu(z_ref[0].astype(F32))
    ms = jnp.mean(y * y, axis=0, keepdims=True)
    yn = (y * lax.rsqrt(ms + NORM_EPS) * ng_ref[...]).astype(BF16)
    ot = jnp.dot(wt_ref[...], yn, preferred_element_type=F32)
    o_ref[0] = x_ref[0] + mod_ref[0][2:3] * ot.T


def _ssm_out(x, mod_l, mod_row, y, xst, zt, dsk, ng, wt, *, tm):
    b, s, d = x.shape
    tm = min(tm, s)
    ch = xst.shape[1]
    return pl.pallas_call(
        _ssm_out_body,
        grid=(b, s // tm),
        in_specs=[pl.BlockSpec((1, tm, d), lambda i, t: (i, t, 0)),
                  pl.BlockSpec((1, 6, d), lambda i, t: (mod_row(i), 0, 0)),
                  pl.BlockSpec((1, 1, ch, tm), lambda i, t: (0, i, 0, t)),
                  pl.BlockSpec((1, 1, ch, tm), lambda i, t: (1, i, 0, t)),
                  pl.BlockSpec((1, ch, tm), lambda i, t: (i, 0, t)),
                  pl.BlockSpec((1, ch, tm), lambda i, t: (i, 0, t)),
                  _resident((ch, 1)),
                  _resident((ch, 1)),
                  _resident((d, ch))],
        out_specs=pl.BlockSpec((1, tm, d), lambda i, t: (i, t, 0)),
        out_shape=jax.ShapeDtypeStruct((b, s, d), F32),
        compiler_params=_cparams("parallel", "parallel"),
        name="ssm_out_residual",
    )(x, mod_l, y, y, xst, zt, dsk, ng, wt)


def _rope_tables_t(n):
    t = jnp.arange(n)
    inv_freq = 1.0 / (ROPE_BASE ** (jnp.arange(ROPE_PAIRS, dtype=F32) / ROPE_PAIRS))
    ang_r = (t // GRID_W).astype(F32)[None, :] * inv_freq[:, None]
    ang_c = (t % GRID_W).astype(F32)[None, :] * inv_freq[:, None]
    cr, sr, cc, sc = jnp.cos(ang_r), jnp.sin(ang_r), jnp.cos(ang_c), jnp.sin(ang_c)
    return (jnp.concatenate([cr, cr, cc, cc], axis=0),
            jnp.concatenate([-sr, sr, -sc, sc], axis=0))


def _attn_layer(x, ctx, mod_l, lat_row, ctx_row, p, rope, lambda_init, with_ctx):
    wt = p["w_in"].T.astype(BF16)
    g = p["norm_g"].reshape(1, D_MODEL)
    qg = p["q_norm_g"].reshape(HEAD_DIM, 1)
    kg = p["k_norm_g"].reshape(HEAD_DIM, 1)
    cos_t, sin_t = rope
    c = ctx.shape[1]
    qt, *lat_kv = _attn_in(x, mod_l, lat_row, g, wt, cos_t, sin_t, qg, kg, rope=True, tm=512)
    qtc, *ctx_kv = _attn_in(ctx, mod_l, ctx_row, g, wt, cos_t[:, :c], sin_t[:, :c], qg, kg,
                            rope=False, tm=256)
    lam_vecs = [p[k].reshape(1, HEAD_DIM) for k in ("lq1", "lk1", "lq2", "lk2")]
    sg = p["subln_g"].reshape(V_ROWS, 1)
    w_out = p["w_out"].astype(BF16)
    x = _flash(x, mod_l, lat_row, w_out, lam_vecs, sg, qt, ctx_kv, lat_kv,
               lambda_init=lambda_init, tq=256, tk=2048)
    if with_ctx:
        ctx = _flash(ctx, mod_l, ctx_row, w_out, lam_vecs, sg, qtc, ctx_kv, None,
                     lambda_init=lambda_init, tq=256, tk=0)
    return x, ctx


def _ssm_layer(x, ctx, mod_l, lat_row, ctx_row, p, with_ctx):
    wt = p["w_in"].T.astype(BF16)
    g = p["norm_g"].reshape(1, D_MODEL)
    dtb = p["dt_bias"].reshape(2 * SSM_HEADS, 1)
    conv_w = jnp.broadcast_to(p["conv_w"][:, :, None], (3, SSM_XBC, V7X_LANES))
    conv_b = jnp.broadcast_to(p["conv_b"][:, None], (SSM_XBC, V7X_LANES))
    a = (-jnp.exp(p["a_log"].astype(F32))).reshape(2, SSM_HEADS, 1)
    b = x.shape[0]

    def pre(v, row, tm):
        zt, xbct, dtt = _ssm_proj(v, mod_l, row, g, wt, dtb, tm=tm)
        xst, ct, bm = _ssm_conv(xbct, conv_w, conv_b, tc=512)
        return zt, xst, ct, bm, dtt

    zt_c, xst_c, ct_c, bm_c, dtt_c = pre(ctx, ctx_row, 256)
    zt_l, xst_l, ct_l, bm_l, dtt_l = pre(x, lat_row, 512)
    zero = jnp.zeros((2, b, SSM_D_INNER, SSM_STATE), F32)
    y_c, s_ctx = _ssd_scan(a, dtt_c, xst_c, ct_c, bm_c, zero)
    y_l, _ = _ssd_scan(a, dtt_l, xst_l, ct_l, bm_l, s_ctx)
    dsk = jnp.repeat(p["d_skip"], SSM_D_INNER // SSM_HEADS).reshape(SSM_D_INNER, 1)
    ng = p["out_norm_g"].reshape(SSM_D_INNER, 1)
    w_out_t = p["w_out"].T.astype(BF16)
    x = _ssm_out(x, mod_l, lat_row, y_l, xst_l, zt_l, dsk, ng, w_out_t, tm=512)
    if with_ctx:
        ctx = _ssm_out(ctx, mod_l, ctx_row, y_c, xst_c, zt_c, dsk, ng, w_out_t, tm=256)
    return x, ctx


def kernel(x, c, ctx, c_ctx, mod_w, mod_b, norm_mix_g, norm_ffn_g, attn_w_in, attn_w_out,
           diff_lq1, diff_lk1, diff_lq2, diff_lk2, diff_subln_g, gqa_q_norm_g, gqa_k_norm_g,
           ssm_w_in, ssm_conv_w, ssm_conv_b, ssm_dt_bias, ssm_a_log, ssm_d, ssm_norm_g, ssm_w_out,
           ffn_w_in, ffn_conv_w, ffn_conv_b, ffn_w_out, final_norm_g):
    b, n, d = x.shape
    mod_rows = 16
    c_rows = jnp.zeros((mod_rows, d), F32).at[:b].set(c).at[b].set(c_ctx)
    mod = _mod_all(c_rows, mod_w, mod_b).reshape(DEPTH, mod_rows, 6, d)
    lat_row = lambda i: i
    ctx_row = lambda i: b
    rope = _rope_tables_t(n)

    for layer in range(DEPTH):
        with_ctx = layer < DEPTH - 1
        mod_l = mod[layer]
        i = layer // 2
        if layer % 2 == 0:
            p = dict(w_in=attn_w_in[i], w_out=attn_w_out[i], norm_g=norm_mix_g[layer],
                     lq1=diff_lq1[i], lk1=diff_lk1[i], lq2=diff_lq2[i], lk2=diff_lk2[i],
                     subln_g=diff_subln_g[i], q_norm_g=gqa_q_norm_g[i], k_norm_g=gqa_k_norm_g[i])
            lambda_init = 0.8 - 0.6 * math.exp(-0.3 * layer)
            x, ctx = _attn_layer(x, ctx, mod_l, lat_row, ctx_row, p, rope, lambda_init, with_ctx)
        else:
            p = dict(w_in=ssm_w_in[i], norm_g=norm_mix_g[layer], conv_w=ssm_conv_w[i],
                     conv_b=ssm_conv_b[i], dt_bias=ssm_dt_bias[i], a_log=ssm_a_log[i],
                     d_skip=ssm_d[i], out_norm_g=ssm_norm_g[i], w_out=ssm_w_out[i])
            x, ctx = _ssm_layer(x, ctx, mod_l, lat_row, ctx_row, p, with_ctx)
        g = norm_ffn_g[layer].reshape(1, d)
        wv = ffn_w_in[layer][:, :D_FF].astype(BF16)
        wg = ffn_w_in[layer][:, D_FF:].astype(BF16)
        cw = ffn_conv_w[layer]
        cb = ffn_conv_b[layer].reshape(1, D_FF)
        wo = ffn_w_out[layer].astype(BF16)
        fg = final_norm_g.reshape(1, d)
        x = _ffn(x, mod_l, lat_row, g, wv, wg, cw, cb, wo, fg, tm=256,
                 final_norm=layer == DEPTH - 1)
        if with_ctx:
            ctx = _ffn(ctx, mod_l, ctx_row, g, wv, wg, cw, cb, wo, fg, tm=256, final_norm=False)
    return x
```

```python
import functools
import math

import jax
import jax.numpy as jnp
from jax import lax
from jax.experimental import pallas as pl
from jax.experimental.pallas import tpu as pltpu

F32 = jnp.float32
BF16 = jnp.bfloat16

D_MODEL = 1024
DEPTH = 4
GRID_W = 64
HEAD_DIM = 64
ROPE_PAIRS = HEAD_DIM // 4
ROPE_BASE = 10000.0
NORM_EPS = 1e-6
DIFF_HEADS = 4
GQA_HEADS = 8
GQA_KV_HEADS = 2
GQA_GROUP = GQA_HEADS // GQA_KV_HEADS
ATTN_IN_COLS = 2304
SSM_D_INNER = 2048
SSM_HEADS = 32
SSM_GROUPS = 4
SSM_HEADS_PER_GROUP = SSM_HEADS // SSM_GROUPS
SSM_STATE = 128
SSM_XBC = 3072
SSM_IN_COLS = 5184
D_FF = 2816
LOG2E = math.log2(math.e)

V7X_VMEM_BYTES = 64 * 1024 * 1024
VMEM_LIMIT_BYTES = V7X_VMEM_BYTES - 8 * 1024 * 1024
V7X_LANES = 128
BF16_SUBLANES = 16

V_ROWS = 2 * HEAD_DIM
V_ROWS_PADDED = V_ROWS + BF16_SUBLANES
GV_ROWS_PADDED = HEAD_DIM + BF16_SUBLANES
K_COLS = (DIFF_HEADS + 1) * 2 * HEAD_DIM

ATTN_KEY_SUBBLOCK = 256
ATTN_STREAM_MAX_OCTAVES = 64.0

CONV_HALO = 16
SSD_CHUNK = 256

NT_DIMS = (((1,), (1,)), ((), ()))


def _cparams(*sem):
    return pltpu.CompilerParams(dimension_semantics=sem, vmem_limit_bytes=VMEM_LIMIT_BYTES)


def _resident(shape):
    nd = len(shape)
    return pl.BlockSpec(shape, lambda *_: (0,) * nd, pipeline_mode=pl.Buffered(1))


def _silu(v):
    return v * jax.nn.sigmoid(v)


def _modulate(x, g, shift, scale):
    ms = jnp.mean(x * x, axis=-1, keepdims=True)
    return (x * lax.rsqrt(ms + NORM_EPS) * g) * (1.0 + scale) + shift


def _mod_body(c_ref, w_ref, b_ref, o_ref):
    s = _silu(c_ref[...])
    o_ref[0] = jnp.dot(s, w_ref[0], preferred_element_type=F32,
                       precision=lax.Precision.HIGHEST) + b_ref[0]


def _mod_all(c_rows, mod_w, mod_b):
    rows = c_rows.shape[0]
    depth, d, cols = mod_w.shape
    tn = 2048
    return pl.pallas_call(
        _mod_body,
        grid=(depth, cols // tn),
        in_specs=[pl.BlockSpec((rows, d), lambda l, n: (0, 0)),
                  pl.BlockSpec((1, d, tn), lambda l, n: (l, 0, n)),
                  pl.BlockSpec((1, 1, tn), lambda l, n: (l, 0, n))],
        out_specs=pl.BlockSpec((1, rows, tn), lambda l, n: (l, 0, n)),
        out_shape=jax.ShapeDtypeStruct((depth, rows, cols), F32),
        compiler_params=_cparams("parallel", "parallel"),
        name="mod_vectors",
    )(c_rows, mod_w, mod_b.reshape(depth, 1, cols))


def _attn_in_body(x_ref, mod_ref, g_ref, wt_ref, cos_ref, sin_ref, qg_ref, kg_ref,
                  qt_ref, k_ref, vta_ref, vtb_ref, *, rope):
    m = mod_ref[0]
    h = _modulate(x_ref[0], g_ref[...], m[0:1], m[1:2]).astype(BF16)
    t = lax.dot_general(wt_ref[...], h, NT_DIMS, preferred_element_type=F32)
    tm = t.shape[1]

    def rot(u):
        if not rope:
            return u
        sw = jnp.concatenate([u[:, 16:32], u[:, 0:16], u[:, 48:64], u[:, 32:48]], axis=1)
        return u * cos_ref[...][None] + sw * sin_ref[...][None]

    def qk_norm(u, g):
        ms = jnp.mean(u * u, axis=1, keepdims=True)
        return u * lax.rsqrt(ms + NORM_EPS) * g[None]

    nq = 2 * DIFF_HEADS
    qa = rot(t[0:512].reshape(nq, HEAD_DIM, tm))
    ka = rot(t[512:1024].reshape(nq, HEAD_DIM, tm))
    va = t[1024:1536]
    qb = rot(qk_norm(t[1536:2048].reshape(GQA_HEADS, HEAD_DIM, tm), qg_ref[...]))
    kb = rot(qk_norm(t[2048:2176].reshape(GQA_KV_HEADS, HEAD_DIM, tm), kg_ref[...]))
    vb = t[2176:2304]

    qs = (HEAD_DIM ** -0.5) * LOG2E
    qt_ref[0, 0:512] = (qa * qs).reshape(512, tm).astype(BF16)
    qt_ref[0, 512:1024] = (qb * qs).reshape(512, tm).astype(BF16)
    kt = jnp.concatenate([ka.reshape(512, tm), kb.reshape(128, tm)], axis=0)
    k_ref[0] = kt.T.astype(BF16)
    ones = jnp.ones((BF16_SUBLANES, tm), BF16)
    for u in range(DIFF_HEADS):
        vta_ref[0, u, 0:V_ROWS] = va[u * V_ROWS:(u + 1) * V_ROWS].astype(BF16)
        vta_ref[0, u, V_ROWS:V_ROWS_PADDED] = ones
    for g in range(GQA_KV_HEADS):
        vtb_ref[0, g, 0:HEAD_DIM] = vb[g * HEAD_DIM:(g + 1) * HEAD_DIM].astype(BF16)
        vtb_ref[0, g, HEAD_DIM:GV_ROWS_PADDED] = ones


def _attn_in(x, mod_l, mod_row, g, wt, cos_t, sin_t, qg, kg, *, rope, tm):
    b, s, d = x.shape
    tm = min(tm, s)
    body = functools.partial(_attn_in_body, rope=rope)
    return pl.pallas_call(
        body,
        grid=(b, s // tm),
        in_specs=[pl.BlockSpec((1, tm, d), lambda i, t: (i, t, 0)),
                  pl.BlockSpec((1, 6, d), lambda i, t: (mod_row(i), 0, 0)),
                  _resident((1, d)),
                  _resident((ATTN_IN_COLS, d)),
                  pl.BlockSpec((HEAD_DIM, tm), lambda i, t: (0, t)),
                  pl.BlockSpec((HEAD_DIM, tm), lambda i, t: (0, t)),
                  _resident((HEAD_DIM, 1)),
                  _resident((HEAD_DIM, 1))],
        out_specs=[pl.BlockSpec((1, 1024, tm), lambda i, t: (i, 0, t)),
                   pl.BlockSpec((1, tm, K_COLS), lambda i, t: (i, t, 0)),
                   pl.BlockSpec((1, DIFF_HEADS, V_ROWS_PADDED, tm), lambda i, t: (i, 0, 0, t)),
                   pl.BlockSpec((1, GQA_KV_HEADS, GV_ROWS_PADDED, tm), lambda i, t: (i, 0, 0, t))],
        out_shape=[jax.ShapeDtypeStruct((b, 1024, s), BF16),
                   jax.ShapeDtypeStruct((b, s, K_COLS), BF16),
                   jax.ShapeDtypeStruct((b, DIFF_HEADS, V_ROWS_PADDED, s), BF16),
                   jax.ShapeDtypeStruct((b, GQA_KV_HEADS, GV_ROWS_PADDED, s), BF16)],
        compiler_params=_cparams("parallel", "parallel"),
        name="attn_in_rope" if rope else "attn_in_ctx",
    )(x, mod_l, g, wt, cos_t, sin_t, qg, kg)


def _attn_unit_cols(tq):
    units = []
    for h in range(DIFF_HEADS):
        units.append((h * 128, h, V_ROWS_PADDED, h * 2 * tq, 2 * tq))
    base = DIFF_HEADS * 2 * tq
    for g in range(GQA_KV_HEADS):
        units.append((DIFF_HEADS * 128, g, GV_ROWS_PADDED, base + g * GQA_GROUP * tq,
                      GQA_GROUP * tq))
    return units


def _flash_body(*refs, tq, lambda_init, has_lat):
    if has_lat:
        (lq1, lk1, lq2, lk2, sg_ref, x_ref, mod_ref, wo_ref, qt_ref, kc_ref, vtac_ref, vtbc_ref,
         kl_ref, vtal_ref, vtbl_ref, o_ref, rhs, acc, mrow, pv_new, m_chunk) = refs
    else:
        (lq1, lk1, lq2, lk2, sg_ref, x_ref, mod_ref, wo_ref, qt_ref, kc_ref, vtac_ref, vtbc_ref,
         o_ref, rhs, acc, mrow, pv_new, m_chunk) = refs
    j = pl.program_id(2)
    nj = pl.num_programs(2)
    units = _attn_unit_cols(tq)
    diff_cols = DIFF_HEADS * 2 * tq

    def values(vta_ref, vtb_ref, u, vu):
        return vta_ref[0, vu] if u < DIFF_HEADS else vtb_ref[0, vu]

    def exact_step(k_ref, vta_ref, vtb_ref):
        for u, (kc0, vu, vr, c0, w) in enumerate(units):
            s = jnp.dot(k_ref[0, :, kc0:kc0 + 128], rhs[:, c0:c0 + w],
                        preferred_element_type=F32)
            mp = mrow[:, c0:c0 + w]
            mn = jnp.maximum(mp, jnp.max(s, axis=0, keepdims=True))
            alpha = jnp.exp2(mp - mn)
            p = jnp.exp2(s - mn).astype(BF16)
            pv = jnp.dot(values(vta_ref, vtb_ref, u, vu), p, preferred_element_type=F32)
            acc[0:vr, c0:c0 + w] = acc[0:vr, c0:c0 + w] * alpha + pv
            mrow[:, c0:c0 + w] = mn

    def streaming_step(k_ref, vta_ref, vtb_ref):
        nk = k_ref.shape[1]
        sub = min(nk, ATTN_KEY_SUBBLOCK)
        for u, (kc0, vu, vr, c0, w) in enumerate(units):
            m_used = mrow[:, c0:c0 + w]
            cm = None
            parts = []
            for r in range(nk // sub):
                s = jnp.dot(k_ref[0, r * sub:(r + 1) * sub, kc0:kc0 + 128], rhs[:, c0:c0 + w],
                            preferred_element_type=F32)
                parts.append(jnp.exp2(s - m_used).astype(BF16))
                sm = jnp.max(s.reshape(sub // 8, 8, w), axis=0)
                cm = sm if cm is None else jnp.maximum(cm, sm)
            p = jnp.concatenate(parts, axis=0)
            pv_new[0:vr, c0:c0 + w] = jnp.dot(values(vta_ref, vtb_ref, u, vu), p,
                                              preferred_element_type=F32)
            m_chunk[:, c0:c0 + w] = jnp.max(cm, axis=0, keepdims=True)

    @pl.when(j == 0)
    def _init():
        zeros = jnp.zeros((HEAD_DIM, tq), BF16)
        for h in range(DIFF_HEADS):
            c0 = h * 2 * tq
            rhs[0:64, c0:c0 + tq] = qt_ref[0, h * 128:h * 128 + 64, :]
            rhs[64:128, c0:c0 + tq] = zeros
            rhs[0:64, c0 + tq:c0 + 2 * tq] = zeros
            rhs[64:128, c0 + tq:c0 + 2 * tq] = qt_ref[0, h * 128 + 64:h * 128 + 128, :]
        base = DIFF_HEADS * 2 * tq
        for g in range(GQA_KV_HEADS):
            for r in range(GQA_GROUP):
                c0 = base + (g * GQA_GROUP + r) * tq
                hd = 512 + (g * GQA_GROUP + r) * HEAD_DIM
                rhs[g * 64:(g + 1) * 64, c0:c0 + tq] = qt_ref[0, hd:hd + HEAD_DIM, :]
                rhs[(1 - g) * 64:(2 - g) * 64, c0:c0 + tq] = zeros
        acc[...] = jnp.zeros(acc.shape, F32)
        mrow[...] = jnp.zeros(mrow.shape, F32)

    row_blocks = ((V_ROWS_PADDED, slice(0, diff_cols)),
                  (GV_ROWS_PADDED, slice(diff_cols, acc.shape[1])))

    def nothing_pending():
        for rows, cols in row_blocks:
            pv_new[0:rows, cols] = jnp.zeros((rows, cols.stop - cols.start), F32)
        m_chunk[...] = jnp.full(m_chunk.shape, -jnp.inf, F32)

    def commit():
        mp = mrow[...]
        mn = jnp.maximum(mp, m_chunk[...])
        mrow[...] = mn
        alpha = jnp.exp2(mp - mn)
        for rows, cols in row_blocks:
            acc[0:rows, cols] = (acc[0:rows, cols] + pv_new[0:rows, cols]) * alpha[:, cols]

    def stream_chunk(k_ref, vta_ref, vtb_ref, first):
        streaming_step(k_ref, vta_ref, vtb_ref)
        dev = m_chunk[...] - mrow[...]
        in_range = jnp.max(jnp.abs(dev) if first else dev) <= ATTN_STREAM_MAX_OCTAVES

        @pl.when(jnp.logical_not(in_range))
        def _redo():
            if first:
                mrow[...] = jnp.full(mrow.shape, -jnp.inf, F32)
            exact_step(k_ref, vta_ref, vtb_ref)
            nothing_pending()

    @pl.when(j == 0)
    def _context_keys():
        stream_chunk(kc_ref, vtac_ref, vtbc_ref, True)

    if has_lat:
        commit()
        stream_chunk(kl_ref, vtal_ref, vtbl_ref, False)

    @pl.when(j == nj - 1)
    def _finish():
        commit()
        lam = (jnp.exp(jnp.sum(lq1[...] * lk1[...], keepdims=True))
               - jnp.exp(jnp.sum(lq2[...] * lk2[...], keepdims=True)) + lambda_init)
        pieces = []
        for h in range(DIFF_HEADS):
            c0 = h * 2 * tq
            o1 = acc[0:V_ROWS, c0:c0 + tq] / acc[V_ROWS:V_ROWS + 1, c0:c0 + tq]
            o2 = acc[0:V_ROWS, c0 + tq:c0 + 2 * tq] / acc[V_ROWS:V_ROWS + 1, c0 + tq:c0 + 2 * tq]
            oh = o1 - lam * o2
            ms = jnp.mean(oh * oh, axis=0, keepdims=True)
            pieces.append(oh * lax.rsqrt(ms + NORM_EPS) * sg_ref[...] * (1.0 - lambda_init))
        base = DIFF_HEADS * 2 * tq
        for g in range(GQA_KV_HEADS):
            for r in range(GQA_GROUP):
                c0 = base + (g * GQA_GROUP + r) * tq
                pieces.append(acc[0:HEAD_DIM, c0:c0 + tq] / acc[HEAD_DIM:HEAD_DIM + 1, c0:c0 + tq])
        o = jnp.concatenate(pieces, axis=0).T.astype(BF16)
        y = jnp.dot(o, wo_ref[...], preferred_element_type=F32)
        o_ref[0] = x_ref[0] + mod_ref[0][2:3] * y


def _flash(x, mod_l, mod_row, w_out, lam_vecs, sg, qt, ctx_kv, lat_kv, *, lambda_init, tq, tk):
    b, _, sq = qt.shape
    d = x.shape[2]
    kc, vtac, vtbc = ctx_kv
    c = kc.shape[1]
    has_lat = lat_kv is not None
    tq = min(tq, sq)
    ncols = (DIFF_HEADS * 2 + GQA_HEADS) * tq
    in_specs = [_resident((1, HEAD_DIM))] * 4 + [
        _resident((V_ROWS, 1)),
        pl.BlockSpec((1, tq, d), lambda i, q, j: (i, q, 0)),
        pl.BlockSpec((1, 6, d), lambda i, q, j: (mod_row(i), 0, 0)),
        _resident((1024, d)),
        pl.BlockSpec((1, 1024, tq), lambda i, q, j: (i, 0, q)),
        pl.BlockSpec((1, c, K_COLS), lambda i, q, j: (i, 0, 0)),
        pl.BlockSpec((1, DIFF_HEADS, V_ROWS_PADDED, c), lambda i, q, j: (i, 0, 0, 0)),
        pl.BlockSpec((1, GQA_KV_HEADS, GV_ROWS_PADDED, c), lambda i, q, j: (i, 0, 0, 0)),
    ]
    args = list(lam_vecs) + [sg, x, mod_l, w_out, qt, kc, vtac, vtbc]
    nkv = 1
    if has_lat:
        kl, vtal, vtbl = lat_kv
        n = kl.shape[1]
        tk = min(tk, n)
        nkv = n // tk
        in_specs += [pl.BlockSpec((1, tk, K_COLS), lambda i, q, j: (i, j, 0)),
                     pl.BlockSpec((1, DIFF_HEADS, V_ROWS_PADDED, tk), lambda i, q, j: (i, 0, 0, j)),
                     pl.BlockSpec((1, GQA_KV_HEADS, GV_ROWS_PADDED, tk), lambda i, q, j: (i, 0, 0, j))]
        args += [kl, vtal, vtbl]
    scratch = [pltpu.VMEM((2 * HEAD_DIM, ncols), BF16),
               pltpu.VMEM((V_ROWS_PADDED, ncols), F32),
               pltpu.VMEM((1, ncols), F32),
               pltpu.VMEM((V_ROWS_PADDED, ncols), F32),
               pltpu.VMEM((1, ncols), F32)]
    body = functools.partial(_flash_body, tq=tq, lambda_init=lambda_init, has_lat=has_lat)
    return pl.pallas_call(
        body,
        grid=(b, sq // tq, nkv),
        in_specs=in_specs,
        out_specs=pl.BlockSpec((1, tq, d), lambda i, q, j: (i, q, 0)),
        out_shape=jax.ShapeDtypeStruct((b, sq, d), F32),
        scratch_shapes=scratch,
        compiler_params=_cparams("parallel", "parallel", "arbitrary"),
        name="attn_sweep_lat" if has_lat else "attn_sweep_ctx",
    )(*args)


def _ffn_body(x_ref, xp_ref, xn_ref, mod_ref, g_ref, wv_ref, wg_ref, cw_ref, cb_ref, wo_ref,
              fg_ref, o_ref, *, tm, final_norm):
    t = pl.program_id(1)
    nt = pl.num_programs(1)
    m = mod_ref[0]
    x = x_ref[0]
    xe = jnp.concatenate([xp_ref[0], x, xn_ref[0]], axis=0)
    he = _modulate(xe, g_ref[...], m[3:4], m[4:5]).astype(BF16)
    ge = jnp.dot(he, wg_ref[...], preferred_element_type=F32)
    rows = lax.broadcasted_iota(jnp.int32, (tm + 2 * CONV_HALO, 1), 0)
    lo = jnp.where(t > 0, 0, CONV_HALO)
    hi = jnp.where(t < nt - 1, tm + 2 * CONV_HALO, tm + CONV_HALO)
    ge = jnp.where((rows >= lo) & (rows < hi), ge, 0.0)
    val = jnp.dot(he[CONV_HALO:CONV_HALO + tm], wv_ref[...], preferred_element_type=F32)
    cw = cw_ref[...]
    h0 = CONV_HALO
    conv = (cb_ref[...] + cw[0:1] * ge[h0 - 1:h0 - 1 + tm] + cw[1:2] * ge[h0:h0 + tm]
            + cw[2:3] * ge[h0 + 1:h0 + 1 + tm])
    gelu = 0.5 * conv * (1.0 + lax.erf(conv * math.sqrt(0.5)))
    act = (gelu * val).astype(BF16)
    y = jnp.dot(act, wo_ref[...], preferred_element_type=F32)
    out = x + m[5:6] * y
    if final_norm:
        ms = jnp.mean(out * out, axis=-1, keepdims=True)
        out = out * lax.rsqrt(ms + NORM_EPS) * fg_ref[...]
    o_ref[0] = out


def _ffn(x, mod_l, mod_row, g, wv, wg, cw, cb, wo, fg, *, tm, final_norm):
    b, s, d = x.shape
    tm = min(tm, s)
    hb = tm // CONV_HALO
    last = s // CONV_HALO - 1
    body = functools.partial(_ffn_body, tm=tm, final_norm=final_norm)
    return pl.pallas_call(
        body,
        grid=(b, s // tm),
        in_specs=[pl.BlockSpec((1, tm, d), lambda i, t: (i, t, 0)),
                  pl.BlockSpec((1, CONV_HALO, d), lambda i, t: (i, jnp.maximum(t * hb - 1, 0), 0)),
                  pl.BlockSpec((1, CONV_HALO, d), lambda i, t: (i, jnp.minimum((t + 1) * hb, last), 0)),
                  pl.BlockSpec((1, 6, d), lambda i, t: (mod_row(i), 0, 0)),
                  _resident((1, d)),
                  _resident((d, D_FF)),
                  _resident((d, D_FF)),
                  _resident((3, D_FF)),
                  _resident((1, D_FF)),
                  _resident((D_FF, d)),
                  _resident((1, d))],
        out_specs=pl.BlockSpec((1, tm, d), lambda i, t: (i, t, 0)),
        out_shape=jax.ShapeDtypeStruct((b, s, d), F32),
        compiler_params=_cparams("parallel", "parallel"),
        name="conv_glu_ffn",
    )(x, x, x, mod_l, g, wv, wg, cw, cb, wo, fg)


def _ssm_proj_body(x_ref, mod_ref, g_ref, wt_ref, dtb_ref, zt_ref, xbct_ref, dtt_ref):
    m = mod_ref[0]
    h = _modulate(x_ref[0], g_ref[...], m[0:1], m[1:2]).astype(BF16)
    t = lax.dot_general(wt_ref[...], h, NT_DIMS, preferred_element_type=F32)
    zt_ref[0] = t[0:SSM_D_INNER].astype(BF16)
    xbct_ref[0] = t[SSM_D_INNER:SSM_D_INNER + SSM_XBC].astype(BF16)
    dtt_ref[0] = jax.nn.softplus(t[SSM_D_INNER + SSM_XBC:SSM_IN_COLS] + dtb_ref[...])


def _ssm_proj(x, mod_l, mod_row, g, wt, dtb, *, tm):
    b, s, d = x.shape
    tm = min(tm, s)
    return pl.pallas_call(
        _ssm_proj_body,
        grid=(b, s // tm),
        in_specs=[pl.BlockSpec((1, tm, d), lambda i, t: (i, t, 0)),
                  pl.BlockSpec((1, 6, d), lambda i, t: (mod_row(i), 0, 0)),
                  _resident((1, d)),
                  _resident((SSM_IN_COLS, d)),
                  _resident((2 * SSM_HEADS, 1))],
        out_specs=[pl.BlockSpec((1, SSM_D_INNER, tm), lambda i, t: (i, 0, t)),
                   pl.BlockSpec((1, SSM_XBC, tm), lambda i, t: (i, 0, t)),
                   pl.BlockSpec((1, 2 * SSM_HEADS, tm), lambda i, t: (i, 0, t))],
        out_shape=[jax.ShapeDtypeStruct((b, SSM_D_INNER, s), BF16),
                   jax.ShapeDtypeStruct((b, SSM_XBC, s), BF16),
                   jax.ShapeDtypeStruct((b, 2 * SSM_HEADS, s), F32)],
        compiler_params=_cparams("parallel", "parallel"),
        name="ssm_in_proj",
    )(x, mod_l, g, wt, dtb)


def _ssm_conv_body(u_ref, up_ref, un_ref, w_ref, b_ref, xst_ref, ct_ref, bm_ref, *, tc):
    t = pl.program_id(1)
    nt = pl.num_programs(1)
    u = u_ref[0].astype(F32)
    prev = jnp.where(t > 0, up_ref[0][:, V7X_LANES - 1:V7X_LANES].astype(F32), 0.0)
    nxt = jnp.where(t < nt - 1, un_ref[0][:, 0:1].astype(F32), 0.0)
    lane = lax.broadcasted_iota(jnp.int32, (1, tc), 1)
    packed = pltpu.bitcast(u_ref[0], jnp.uint32)
    rolled_l = pltpu.bitcast(pltpu.roll(packed, 1, 1), BF16).astype(F32)
    rolled_r = pltpu.bitcast(pltpu.roll(packed, tc - 1, 1), BF16).astype(F32)
    left = jnp.where(lane == 0, prev, rolled_l)
    right = jnp.where(lane == tc - 1, nxt, rolled_r)
    def lanes(a):
        return jnp.concatenate([a] * (tc // V7X_LANES), axis=1)

    v = _silu(lanes(b_ref[...]) + lanes(w_ref[0]) * left + lanes(w_ref[1]) * u
              + lanes(w_ref[2]) * right)
    bn = SSM_GROUPS * SSM_STATE
    xst_ref[0] = v[0:SSM_D_INNER].astype(BF16)
    bm_ref[0] = v[SSM_D_INNER:SSM_D_INNER + bn].T.astype(BF16)
    ct_ref[0] = v[SSM_D_INNER + bn:SSM_XBC].astype(BF16)


def _ssm_conv(xbct, w, bias, *, tc):
    b, ch, s = xbct.shape
    tc = min(tc, s)
    hb = tc // V7X_LANES
    last = s // V7X_LANES - 1
    bn = SSM_GROUPS * SSM_STATE
    body = functools.partial(_ssm_conv_body, tc=tc)
    return pl.pallas_call(
        body,
        grid=(b, s // tc),
        in_specs=[pl.BlockSpec((1, ch, tc), lambda i, t: (i, 0, t)),
                  pl.BlockSpec((1, ch, V7X_LANES), lambda i, t: (i, 0, jnp.maximum(t * hb - 1, 0))),
                  pl.BlockSpec((1, ch, V7X_LANES), lambda i, t: (i, 0, jnp.minimum((t + 1) * hb, last))),
                  _resident((3, ch, V7X_LANES)),
                  _resident((ch, V7X_LANES))],
        out_specs=[pl.BlockSpec((1, SSM_D_INNER, tc), lambda i, t: (i, 0, t)),
                   pl.BlockSpec((1, bn, tc), lambda i, t: (i, 0, t)),
                   pl.BlockSpec((1, tc, bn), lambda i, t: (i, t, 0))],
        out_shape=[jax.ShapeDtypeStruct((b, SSM_D_INNER, s), BF16),
                   jax.ShapeDtypeStruct((b, bn, s), BF16),
                   jax.ShapeDtypeStruct((b, s, bn), BF16)],
        compiler_params=_cparams("parallel", "parallel"),
        name="ssm_conv_silu",
    )(xbct, xbct, xbct, w, bias)


def _dot_f32_by_01(a, m01):
    hi = a.astype(BF16)
    r1 = a - hi.astype(F32)
    mid = r1.astype(BF16)
    lo = (r1 - mid.astype(F32)).astype(BF16)
    return (jnp.dot(hi, m01, preferred_element_type=F32)
            + jnp.dot(mid, m01, preferred_element_type=F32)
            + jnp.dot(lo, m01, preferred_element_type=F32))


def _scan_order_mask(d, chunk):
    jrow = lax.broadcasted_iota(jnp.int32, (chunk, chunk), 0)
    icol = lax.broadcasted_iota(jnp.int32, (chunk, chunk), 1)
    sign = jnp.where(d == 0, 1, -1)
    return (icol - jrow) * sign >= 0


def _ssd_decay_body(a_ref, dt_ref, cum2_ref, src_ref, toend_ref, ecum_ref, etot_ref, *, chunk):
    d = pl.program_id(1)
    mask01 = _scan_order_mask(d, chunk).astype(BF16)
    for s in range(dt_ref.shape[2] // chunk):
        tok = slice(s * chunk, (s + 1) * chunk)
        dt = dt_ref[0, :, tok]
        cum_t = _dot_f32_by_01(dt * a_ref[0], mask01)
        cum2_t = cum_t * LOG2E
        cum2_ref[0, 0, :, tok] = cum2_t
        src_ref[0, 0, tok, :] = (cum2_t - jnp.log2(dt)).T
        tot = jnp.where(d == 0, cum_t[:, chunk - 1:chunk], cum_t[:, 0:1])
        toend_ref[0, 0, :, tok] = jnp.exp(tot - cum_t) * dt
        ecum_ref[0, 0, :, tok] = jnp.exp(cum_t)
        etot_ref[0, 0, :, tok] = jnp.broadcast_to(jnp.exp(tot), cum_t.shape)


def _ssd_decay(a, dtt, chunk):
    b, _, s = dtt.shape
    span = min(s, 8 * chunk)
    body = functools.partial(_ssd_decay_body, chunk=chunk)
    by_head = pl.BlockSpec((1, 1, SSM_HEADS, span), lambda i, d, g: (i, d, 0, g))
    return pl.pallas_call(
        body,
        grid=(b, 2, s // span),
        in_specs=[pl.BlockSpec((1, SSM_HEADS, 1), lambda i, d, g: (d, 0, 0)),
                  pl.BlockSpec((1, SSM_HEADS, span), lambda i, d, g: (i, d, g))],
        out_specs=[by_head,
                   pl.BlockSpec((1, 1, span, SSM_HEADS), lambda i, d, g: (i, d, g, 0)),
                   by_head, by_head, by_head],
        out_shape=[jax.ShapeDtypeStruct((b, 2, SSM_HEADS, s), F32),
                   jax.ShapeDtypeStruct((b, 2, s, SSM_HEADS), F32)]
        + [jax.ShapeDtypeStruct((b, 2, SSM_HEADS, s), F32)] * 3,
        compiler_params=_cparams("parallel", "parallel", "parallel"),
        name="ssd_decay_factors",
    )(a, dtt)


def _scan_body(cum2_ref, src_ref, toend_ref, ecum_ref, etot_ref, xs_ref, ct_ref, b_ref, s0_ref,
               y_ref, sout_ref, state, *, chunk):
    d = pl.program_id(1)
    c = pl.program_id(2)
    nc = pl.num_programs(2)
    hp = SSM_HEADS_PER_GROUP
    hd = SSM_D_INNER // SSM_HEADS
    gw = hp * hd

    @pl.when(c == 0)
    def _load_state():
        state[...] = s0_ref[0, 0]

    not_yet = jnp.where(_scan_order_mask(d, chunk), 0.0, -jnp.inf)
    cum2_t = cum2_ref[0, 0]
    src_term = src_ref[0, 0]
    to_end = toend_ref[0, 0]
    ecum = ecum_ref[0, 0]
    etot = etot_ref[0, 0][:, 0:1]

    for g in range(SSM_GROUPS):
        bg = b_ref[0, :, g * SSM_STATE:(g + 1) * SSM_STATE]
        ctg = ct_ref[0, g * SSM_STATE:(g + 1) * SSM_STATE, :]
        cbt = jnp.dot(bg, ctg, preferred_element_type=F32)
        xg = xs_ref[0, g * gw:(g + 1) * gw, :]
        sg = state[g * gw:(g + 1) * gw, :]
        hs = slice(g * hp, (g + 1) * hp)
        y_state = (jnp.dot(sg.astype(BF16), ctg, preferred_element_type=F32)
                   .reshape(hp, hd, chunk) * ecum[hs][:, None, :])
        outs = []
        for r in range(hp):
            h = g * hp + r
            seg = cum2_t[h:h + 1, :] - src_term[:, h:h + 1]
            w = (cbt * jnp.exp2(seg + not_yet)).astype(BF16)
            outs.append(jnp.dot(xg[r * hd:(r + 1) * hd], w, preferred_element_type=F32)
                        + y_state[r])
        y_ref[0, 0, g * gw:(g + 1) * gw, :] = jnp.concatenate(outs, axis=0).astype(BF16)
        xw = ((xg.astype(F32).reshape(hp, hd, chunk) * to_end[hs][:, None, :])
              .reshape(gw, chunk).astype(BF16))
        upd = jnp.dot(xw, bg, preferred_element_type=F32)
        decayed = (sg.reshape(hp, hd, SSM_STATE) * etot[hs][:, None, :]).reshape(gw, SSM_STATE)
        state[g * gw:(g + 1) * gw, :] = decayed + upd

    @pl.when(c == nc - 1)
    def _store_state():
        sout_ref[0, 0] = state[...]


def _ssd_scan(a, dtt, xst, ct, bm, s0):
    b, ch, s = xst.shape
    chunk = min(SSD_CHUNK, s)
    nc = s // chunk
    bn = SSM_GROUPS * SSM_STATE
    cum2, src, toend, ecum, etot = _ssd_decay(a, dtt, chunk)

    def cidx(d, c):
        return c + d * (nc - 1 - 2 * c)

    by_head = pl.BlockSpec((1, 1, SSM_HEADS, chunk), lambda i, d, c: (i, d, 0, cidx(d, c)))
    body = functools.partial(_scan_body, chunk=chunk)
    return pl.pallas_call(
        body,
        grid=(b, 2, nc),
        in_specs=[by_head,
                  pl.BlockSpec((1, 1, chunk, SSM_HEADS), lambda i, d, c: (i, d, cidx(d, c), 0)),
                  by_head, by_head, by_head,
                  pl.BlockSpec((1, ch, chunk), lambda i, d, c: (i, 0, cidx(d, c))),
                  pl.BlockSpec((1, bn, chunk), lambda i, d, c: (i, 0, cidx(d, c))),
                  pl.BlockSpec((1, chunk, bn), lambda i, d, c: (i, cidx(d, c), 0)),
                  pl.BlockSpec((1, 1, ch, SSM_STATE), lambda i, d, c: (d, i, 0, 0))],
        out_specs=[pl.BlockSpec((1, 1, ch, chunk), lambda i, d, c: (d, i, 0, cidx(d, c))),
                   pl.BlockSpec((1, 1, ch, SSM_STATE), lambda i, d, c: (d, i, 0, 0))],
        out_shape=[jax.ShapeDtypeStruct((2, b, ch, s), BF16),
                   jax.ShapeDtypeStruct((2, b, ch, SSM_STATE), F32)],
        scratch_shapes=[pltpu.VMEM((ch, SSM_STATE), F32)],
        compiler_params=_cparams("parallel", "parallel", "arbitrary"),
        name="ssd_scan",
    )(cum2, src, toend, ecum, etot, xst, ct, bm, s0)


def _ssm_out_body(x_ref, mod_ref, yf_ref, yb_ref, xs_ref, z_ref, dsk_ref, ng_ref, wt_ref, o_ref):
    y = (yf_ref[0, 0].astype(F32) + yb_ref[0, 0].astype(F32)
         + dsk_ref[...] * xs_ref[0].astype(F32))
    y = y * _silu(z_ref[0].astype(F32))
    ms = jnp.mean(y * y, axis=0, keepdims=True)
    yn = (y * lax.rsqrt(ms + NORM_EPS) * ng_ref[...]).astype(BF16)
    ot = jnp.dot(wt_ref[...], yn, preferred_element_type=F32)
    o_ref[0] = x_ref[0] + mod_ref[0][2:3] * ot.T


def _ssm_out(x, mod_l, mod_row, y, xst, zt, dsk, ng, wt, *, tm):
    b, s, d = x.shape
    tm = min(tm, s)
    ch = xst.shape[1]
    return pl.pallas_call(
        _ssm_out_body,
        grid=(b, s // tm),
        in_specs=[pl.BlockSpec((1, tm, d), lambda i, t: (i, t, 0)),
                  pl.BlockSpec((1, 6, d), lambda i, t: (mod_row(i), 0, 0)),
                  pl.BlockSpec((1, 1, ch, tm), lambda i, t: (0, i, 0, t)),
                  pl.BlockSpec((1, 1, ch, tm), lambda i, t: (1, i, 0, t)),
                  pl.BlockSpec((1, ch, tm), lambda i, t: (i, 0, t)),
                  pl.BlockSpec((1, ch, tm), lambda i, t: (i, 0, t)),
                  _resident((ch, 1)),
                  _resident((ch, 1)),
                  _resident((d, ch))],
        out_specs=pl.BlockSpec((1, tm, d), lambda i, t: (i, t, 0)),
        out_shape=jax.ShapeDtypeStruct((b, s, d), F32),
        compiler_params=_cparams("parallel", "parallel"),
        name="ssm_out_residual",
    )(x, mod_l, y, y, xst, zt, dsk, ng, wt)


def _rope_tables_t(n):
    t = jnp.arange(n)
    inv_freq = 1.0 / (ROPE_BASE ** (jnp.arange(ROPE_PAIRS, dtype=F32) / ROPE_PAIRS))
    ang_r = (t // GRID_W).astype(F32)[None, :] * inv_freq[:, None]
    ang_c = (t % GRID_W).astype(F32)[None, :] * inv_freq[:, None]
    cr, sr, cc, sc = jnp.cos(ang_r), jnp.sin(ang_r), jnp.cos(ang_c), jnp.sin(ang_c)
    return (jnp.concatenate([cr, cr, cc, cc], axis=0),
            jnp.concatenate([-sr, sr, -sc, sc], axis=0))


def _attn_layer(x, ctx, mod_l, lat_row, ctx_row, p, rope, lambda_init, with_ctx):
    wt = p["w_in"].T.astype(BF16)
    g = p["norm_g"].reshape(1, D_MODEL)
    qg = p["q_norm_g"].reshape(HEAD_DIM, 1)
    kg = p["k_norm_g"].reshape(HEAD_DIM, 1)
    cos_t, sin_t = rope
    c = ctx.shape[1]
    qt, *lat_kv = _attn_in(x, mod_l, lat_row, g, wt, cos_t, sin_t, qg, kg, rope=True, tm=512)
    qtc, *ctx_kv = _attn_in(ctx, mod_l, ctx_row, g, wt, cos_t[:, :c], sin_t[:, :c], qg, kg,
                            rope=False, tm=256)
    lam_vecs = [p[k].reshape(1, HEAD_DIM) for k in ("lq1", "lk1", "lq2", "lk2")]
    sg = p["subln_g"].reshape(V_ROWS, 1)
    w_out = p["w_out"].astype(BF16)
    x = _flash(x, mod_l, lat_row, w_out, lam_vecs, sg, qt, ctx_kv, lat_kv,
               lambda_init=lambda_init, tq=256, tk=2048)
    if with_ctx:
        ctx = _flash(ctx, mod_l, ctx_row, w_out, lam_vecs, sg, qtc, ctx_kv, None,
                     lambda_init=lambda_init, tq=256, tk=0)
    return x, ctx


def _ssm_layer(x, ctx, mod_l, lat_row, ctx_row, p, with_ctx):
    wt = p["w_in"].T.astype(BF16)
    g = p["norm_g"].reshape(1, D_MODEL)
    dtb = p["dt_bias"].reshape(2 * SSM_HEADS, 1)
    conv_w = jnp.broadcast_to(p["conv_w"][:, :, None], (3, SSM_XBC, V7X_LANES))
    conv_b = jnp.broadcast_to(p["conv_b"][:, None], (SSM_XBC, V7X_LANES))
    a = (-jnp.exp(p["a_log"].astype(F32))).reshape(2, SSM_HEADS, 1)
    b = x.shape[0]

    def pre(v, row, tm):
        zt, xbct, dtt = _ssm_proj(v, mod_l, row, g, wt, dtb, tm=tm)
        xst, ct, bm = _ssm_conv(xbct, conv_w, conv_b, tc=512)
        return zt, xst, ct, bm, dtt

    zt_c, xst_c, ct_c, bm_c, dtt_c = pre(ctx, ctx_row, 256)
    zt_l, xst_l, ct_l, bm_l, dtt_l = pre(x, lat_row, 512)
    zero = jnp.zeros((2, b, SSM_D_INNER, SSM_STATE), F32)
    y_c, s_ctx = _ssd_scan(a, dtt_c, xst_c, ct_c, bm_c, zero)
    y_l, _ = _ssd_scan(a, dtt_l, xst_l, ct_l, bm_l, s_ctx)
    dsk = jnp.repeat(p["d_skip"], SSM_D_INNER // SSM_HEADS).reshape(SSM_D_INNER, 1)
    ng = p["out_norm_g"].reshape(SSM_D_INNER, 1)
    w_out_t = p["w_out"].T.astype(BF16)
    x = _ssm_out(x, mod_l, lat_row, y_l, xst_l, zt_l, dsk, ng, w_out_t, tm=512)
    if with_ctx:
        ctx = _ssm_out(ctx, mod_l, ctx_row, y_c, xst_c, zt_c, dsk, ng, w_out_t, tm=256)
    return x, ctx


def kernel(x, c, ctx, c_ctx, mod_w, mod_b, norm_mix_g, norm_ffn_g, attn_w_in, attn_w_out,
           diff_lq1, diff_lk1, diff_lq2, diff_lk2, diff_subln_g, gqa_q_norm_g, gqa_k_norm_g,
           ssm_w_in, ssm_conv_w, ssm_conv_b, ssm_dt_bias, ssm_a_log, ssm_d, ssm_norm_g, ssm_w_out,
           ffn_w_in, ffn_conv_w, ffn_conv_b, ffn_w_out, final_norm_g):
    b, n, d = x.shape
    mod_rows = 16
    c_rows = jnp.zeros((mod_rows, d), F32).at[:b].set(c).at[b].set(c_ctx)
    mod = _mod_all(c_rows, mod_w, mod_b).reshape(DEPTH, mod_rows, 6, d)
    lat_row = lambda i: i
    ctx_row = lambda i: b
    rope = _rope_tables_t(n)

    for layer in range(DEPTH):
        with_ctx = layer < DEPTH - 1
        mod_l = mod[layer]
        i = layer // 2
        if layer % 2 == 0:
            p = dict(w_in=attn_w_in[i], w_out=attn_w_out[i], norm_g=norm_mix_g[layer],
                     lq1=diff_lq1[i], lk1=diff_lk1[i], lq2=diff_lq2[i], lk2=diff_lk2[i],
                     subln_g=diff_subln_g[i], q_norm_g=gqa_q_norm_g[i], k_norm_g=gqa_k_norm_g[i])
            lambda_init = 0.8 - 0.6 * math.exp(-0.3 * layer)
            x, ctx = _attn_layer(x, ctx, mod_l, lat_row, ctx_row, p, rope, lambda_init, with_ctx)
        else:
            p = dict(w_in=ssm_w_in[i], norm_g=norm_mix_g[layer], conv_w=ssm_conv_w[i],
                     conv_b=ssm_conv_b[i], dt_bias=ssm_dt_bias[i], a_log=ssm_a_log[i],
                     d_skip=ssm_d[i], out_norm_g=ssm_norm_g[i], w_out=ssm_w_out[i])
            x, ctx = _ssm_layer(x, ctx, mod_l, lat_row, ctx_row, p, with_ctx)
        g = norm_ffn_g[layer].reshape(1, d)
        wv = ffn_w_in[layer][:, :D_FF].astype(BF16)
        wg = ffn_w_in[layer][:, D_FF:].astype(BF16)
        cw = ffn_conv_w[layer]
        cb = ffn_conv_b[layer].reshape(1, D_FF)
        wo = ffn_w_out[layer].astype(BF16)
        fg = final_norm_g.reshape(1, d)
        x = _ffn(x, mod_l, lat_row, g, wv, wg, cw, cb, wo, fg, tm=256,
                 final_norm=layer == DEPTH - 1)
        if with_ctx:
            ctx = _ffn(ctx, mod_l, ctx_row, g, wv, wg, cw, cb, wo, fg, tm=256, final_norm=False)
    return x
```

```python
import functools
import math

import jax
import jax.numpy as jnp
from jax import lax
from jax.experimental import pallas as pl
from jax.experimental.pallas import tpu as pltpu

F32 = jnp.float32
BF16 = jnp.bfloat16

D_MODEL = 1024
DEPTH = 4
GRID_W = 64
HEAD_DIM = 64
ROPE_PAIRS = HEAD_DIM // 4
ROPE_BASE = 10000.0
NORM_EPS = 1e-6
DIFF_HEADS = 4
GQA_HEADS = 8
GQA_KV_HEADS = 2
GQA_GROUP = GQA_HEADS // GQA_KV_HEADS
ATTN_IN_COLS = 2304
SSM_D_INNER = 2048
SSM_HEADS = 32
SSM_GROUPS = 4
SSM_HEADS_PER_GROUP = SSM_HEADS // SSM_GROUPS
SSM_STATE = 128
SSM_XBC = 3072
SSM_IN_COLS = 5184
D_FF = 2816
LOG2E = math.log2(math.e)

V7X_VMEM_BYTES = 64 * 1024 * 1024
VMEM_LIMIT_BYTES = V7X_VMEM_BYTES - 8 * 1024 * 1024
V7X_LANES = 128
BF16_SUBLANES = 16

V_ROWS = 2 * HEAD_DIM
V_ROWS_PADDED = V_ROWS + BF16_SUBLANES
GV_ROWS_PADDED = HEAD_DIM + BF16_SUBLANES
K_COLS = (DIFF_HEADS + 1) * 2 * HEAD_DIM

ATTN_KEY_SUBBLOCK = 256
ATTN_STREAM_MAX_OCTAVES = 64.0

CONV_HALO = 16
SSD_CHUNK = 256

NT_DIMS = (((1,), (1,)), ((), ()))


def _cparams(*sem):
    return pltpu.CompilerParams(dimension_semantics=sem, vmem_limit_bytes=VMEM_LIMIT_BYTES)


def _resident(shape):
    nd = len(shape)
    return pl.BlockSpec(shape, lambda *_: (0,) * nd, pipeline_mode=pl.Buffered(1))


def _silu(v):
    return v * jax.nn.sigmoid(v)


def _modulate(x, g, shift, scale):
    ms = jnp.mean(x * x, axis=-1, keepdims=True)
    return (x * lax.rsqrt(ms + NORM_EPS) * g) * (1.0 + scale) + shift


def _mod_body(c_ref, w_ref, b_ref, o_ref):
    s = _silu(c_ref[...])
    o_ref[0] = jnp.dot(s, w_ref[0], preferred_element_type=F32,
                       precision=lax.Precision.HIGHEST) + b_ref[0]


def _mod_all(c_rows, mod_w, mod_b):
    rows = c_rows.shape[0]
    depth, d, cols = mod_w.shape
    tn = 2048
    return pl.pallas_call(
        _mod_body,
        grid=(depth, cols // tn),
        in_specs=[pl.BlockSpec((rows, d), lambda l, n: (0, 0)),
                  pl.BlockSpec((1, d, tn), lambda l, n: (l, 0, n)),
                  pl.BlockSpec((1, 1, tn), lambda l, n: (l, 0, n))],
        out_specs=pl.BlockSpec((1, rows, tn), lambda l, n: (l, 0, n)),
        out_shape=jax.ShapeDtypeStruct((depth, rows, cols), F32),
        compiler_params=_cparams("parallel", "parallel"),
        name="mod_vectors",
    )(c_rows, mod_w, mod_b.reshape(depth, 1, cols))


def _attn_in_body(x_ref, mod_ref, g_ref, wt_ref, cos_ref, sin_ref, qg_ref, kg_ref,
                  qt_ref, k_ref, vta_ref, vtb_ref, *, rope):
    m = mod_ref[0]
    h = _modulate(x_ref[0], g_ref[...], m[0:1], m[1:2]).astype(BF16)
    t = lax.dot_general(wt_ref[...], h, NT_DIMS, preferred_element_type=F32)
    tm = t.shape[1]

    def rot(u):
        if not rope:
            return u
        sw = jnp.concatenate([u[:, 16:32], u[:, 0:16], u[:, 48:64], u[:, 32:48]], axis=1)
        return u * cos_ref[...][None] + sw * sin_ref[...][None]

    def qk_norm(u, g):
        ms = jnp.mean(u * u, axis=1, keepdims=True)
        return u * lax.rsqrt(ms + NORM_EPS) * g[None]

    nq = 2 * DIFF_HEADS
    qa = rot(t[0:512].reshape(nq, HEAD_DIM, tm))
    ka = rot(t[512:1024].reshape(nq, HEAD_DIM, tm))
    va = t[1024:1536]
    qb = rot(qk_norm(t[1536:2048].reshape(GQA_HEADS, HEAD_DIM, tm), qg_ref[...]))
    kb = rot(qk_norm(t[2048:2176].reshape(GQA_KV_HEADS, HEAD_DIM, tm), kg_ref[...]))
    vb = t[2176:2304]

    qs = (HEAD_DIM ** -0.5) * LOG2E
    qt_ref[0, 0:512] = (qa * qs).reshape(512, tm).astype(BF16)
    qt_ref[0, 512:1024] = (qb * qs).reshape(512, tm).astype(BF16)
    kt = jnp.concatenate([ka.reshape(512, tm), kb.reshape(128, tm)], axis=0)
    k_ref[0] = kt.T.astype(BF16)
    ones = jnp.ones((BF16_SUBLANES, tm), BF16)
    for u in range(DIFF_HEADS):
        vta_ref[0, u, 0:V_ROWS] = va[u * V_ROWS:(u + 1) * V_ROWS].astype(BF16)
        vta_ref[0, u, V_ROWS:V_ROWS_PADDED] = ones
    for g in range(GQA_KV_HEADS):
        vtb_ref[0, g, 0:HEAD_DIM] = vb[g * HEAD_DIM:(g + 1) * HEAD_DIM].astype(BF16)
        vtb_ref[0, g, HEAD_DIM:GV_ROWS_PADDED] = ones


def _attn_in(x, mod_l, mod_row, g, wt, cos_t, sin_t, qg, kg, *, rope, tm):
    b, s, d = x.shape
    tm = min(tm, s)
    body = functools.partial(_attn_in_body, rope=rope)
    return pl.pallas_call(
        body,
        grid=(b, s // tm),
        in_specs=[pl.BlockSpec((1, tm, d), lambda i, t: (i, t, 0)),
                  pl.BlockSpec((1, 6, d), lambda i, t: (mod_row(i), 0, 0)),
                  _resident((1, d)),
                  _resident((ATTN_IN_COLS, d)),
                  pl.BlockSpec((HEAD_DIM, tm), lambda i, t: (0, t)),
                  pl.BlockSpec((HEAD_DIM, tm), lambda i, t: (0, t)),
                  _resident((HEAD_DIM, 1)),
                  _resident((HEAD_DIM, 1))],
        out_specs=[pl.BlockSpec((1, 1024, tm), lambda i, t: (i, 0, t)),
                   pl.BlockSpec((1, tm, K_COLS), lambda i, t: (i, t, 0)),
                   pl.BlockSpec((1, DIFF_HEADS, V_ROWS_PADDED, tm), lambda i, t: (i, 0, 0, t)),
                   pl.BlockSpec((1, GQA_KV_HEADS, GV_ROWS_PADDED, tm), lambda i, t: (i, 0, 0, t))],
        out_shape=[jax.ShapeDtypeStruct((b, 1024, s), BF16),
                   jax.ShapeDtypeStruct((b, s, K_COLS), BF16),
                   jax.ShapeDtypeStruct((b, DIFF_HEADS, V_ROWS_PADDED, s), BF16),
                   jax.ShapeDtypeStruct((b, GQA_KV_HEADS, GV_ROWS_PADDED, s), BF16)],
        compiler_params=_cparams("parallel", "parallel"),
        name="attn_in_rope" if rope else "attn_in_ctx",
    )(x, mod_l, g, wt, cos_t, sin_t, qg, kg)


def _attn_unit_cols(tq):
    units = []
    for h in range(DIFF_HEADS):
        units.append((h * 128, h, V_ROWS_PADDED, h * 2 * tq, 2 * tq))
    base = DIFF_HEADS * 2 * tq
    for g in range(GQA_KV_HEADS):
        units.append((DIFF_HEADS * 128, g, GV_ROWS_PADDED, base + g * GQA_GROUP * tq,
                      GQA_GROUP * tq))
    return units


def _flash_body(*refs, tq, lambda_init, has_lat):
    if has_lat:
        (lq1, lk1, lq2, lk2, sg_ref, x_ref, mod_ref, wo_ref, qt_ref, kc_ref, vtac_ref, vtbc_ref,
         kl_ref, vtal_ref, vtbl_ref, o_ref, rhs, acc, mrow, pv_new, m_chunk) = refs
    else:
        (lq1, lk1, lq2, lk2, sg_ref, x_ref, mod_ref, wo_ref, qt_ref, kc_ref, vtac_ref, vtbc_ref,
         o_ref, rhs, acc, mrow, pv_new, m_chunk) = refs
    j = pl.program_id(2)
    nj = pl.num_programs(2)
    units = _attn_unit_cols(tq)
    diff_cols = DIFF_HEADS * 2 * tq

    def values(vta_ref, vtb_ref, u, vu):
        return vta_ref[0, vu] if u < DIFF_HEADS else vtb_ref[0, vu]

    def exact_step(k_ref, vta_ref, vtb_ref):
        for u, (kc0, vu, vr, c0, w) in enumerate(units):
            s = jnp.dot(k_ref[0, :, kc0:kc0 + 128], rhs[:, c0:c0 + w],
                        preferred_element_type=F32)
            mp = mrow[:, c0:c0 + w]
            mn = jnp.maximum(mp, jnp.max(s, axis=0, keepdims=True))
            alpha = jnp.exp2(mp - mn)
            p = jnp.exp2(s - mn).astype(BF16)
            pv = jnp.dot(values(vta_ref, vtb_ref, u, vu), p, preferred_element_type=F32)
            acc[0:vr, c0:c0 + w] = acc[0:vr, c0:c0 + w] * alpha + pv
            mrow[:, c0:c0 + w] = mn

    def streaming_step(k_ref, vta_ref, vtb_ref):
        nk = k_ref.shape[1]
        sub = min(nk, ATTN_KEY_SUBBLOCK)
        for u, (kc0, vu, vr, c0, w) in enumerate(units):
            m_used = mrow[:, c0:c0 + w]
            cm = None
            parts = []
            for r in range(nk // sub):
                s = jnp.dot(k_ref[0, r * sub:(r + 1) * sub, kc0:kc0 + 128], rhs[:, c0:c0 + w],
                            preferred_element_type=F32)
                parts.append(jnp.exp2(s - m_used).astype(BF16))
                sm = jnp.max(s.reshape(sub // 8, 8, w), axis=0)
                cm = sm if cm is None else jnp.maximum(cm, sm)
            p = jnp.concatenate(parts, axis=0)
            pv_new[0:vr, c0:c0 + w] = jnp.dot(values(vta_ref, vtb_ref, u, vu), p,
                                              preferred_element_type=F32)
            m_chunk[:, c0:c0 + w] = jnp.max(cm, axis=0, keepdims=True)

    @pl.when(j == 0)
    def _init():
        zeros = jnp.zeros((HEAD_DIM, tq), BF16)
        for h in range(DIFF_HEADS):
            c0 = h * 2 * tq
            rhs[0:64, c0:c0 + tq] = qt_ref[0, h * 128:h * 128 + 64, :]
            rhs[64:128, c0:c0 + tq] = zeros
            rhs[0:64, c0 + tq:c0 + 2 * tq] = zeros
            rhs[64:128, c0 + tq:c0 + 2 * tq] = qt_ref[0, h * 128 + 64:h * 128 + 128, :]
        base = DIFF_HEADS * 2 * tq
        for g in range(GQA_KV_HEADS):
            for r in range(GQA_GROUP):
                c0 = base + (g * GQA_GROUP + r) * tq
                hd = 512 + (g * GQA_GROUP + r) * HEAD_DIM
                rhs[g * 64:(g + 1) * 64, c0:c0 + tq] = qt_ref[0, hd:hd + HEAD_DIM, :]
                rhs[(1 - g) * 64:(2 - g) * 64, c0:c0 + tq] = zeros
        acc[...] = jnp.zeros(acc.shape, F32)
        mrow[...] = jnp.zeros(mrow.shape, F32)

    row_blocks = ((V_ROWS_PADDED, slice(0, diff_cols)),
                  (GV_ROWS_PADDED, slice(diff_cols, acc.shape[1])))

    def nothing_pending():
        for rows, cols in row_blocks:
            pv_new[0:rows, cols] = jnp.zeros((rows, cols.stop - cols.start), F32)
        m_chunk[...] = jnp.full(m_chunk.shape, -jnp.inf, F32)

    def commit():
        mp = mrow[...]
        mn = jnp.maximum(mp, m_chunk[...])
        mrow[...] = mn
        alpha = jnp.exp2(mp - mn)
        for rows, cols in row_blocks:
            acc[0:rows, cols] = (acc[0:rows, cols] + pv_new[0:rows, cols]) * alpha[:, cols]

    def stream_chunk(k_ref, vta_ref, vtb_ref, first):
        streaming_step(k_ref, vta_ref, vtb_ref)
        dev = m_chunk[...] - mrow[...]
        in_range = jnp.max(jnp.abs(dev) if first else dev) <= ATTN_STREAM_MAX_OCTAVES

        @pl.when(jnp.logical_not(in_range))
        def _redo():
            if first:
                mrow[...] = jnp.full(mrow.shape, -jnp.inf, F32)
            exact_step(k_ref, vta_ref, vtb_ref)
            nothing_pending()

    @pl.when(j == 0)
    def _context_keys():
        stream_chunk(kc_ref, vtac_ref, vtbc_ref, True)

    if has_lat:
        commit()
        stream_chunk(kl_ref, vtal_ref, vtbl_ref, False)

    @pl.when(j == nj - 1)
    def _finish():
        commit()
        lam = (jnp.exp(jnp.sum(lq1[...] * lk1[...], keepdims=True))
               - jnp.exp(jnp.sum(lq2[...] * lk2[...], keepdims=True)) + lambda_init)
        pieces = []
        for h in range(DIFF_HEADS):
            c0 = h * 2 * tq
            o1 = acc[0:V_ROWS, c0:c0 + tq] / acc[V_ROWS:V_ROWS + 1, c0:c0 + tq]
            o2 = acc[0:V_ROWS, c0 + tq:c0 + 2 * tq] / acc[V_ROWS:V_ROWS + 1, c0 + tq:c0 + 2 * tq]
            oh = o1 - lam * o2
            ms = jnp.mean(oh * oh, axis=0, keepdims=True)
            pieces.append(oh * lax.rsqrt(ms + NORM_EPS) * sg_ref[...] * (1.0 - lambda_init))
        base = DIFF_HEADS * 2 * tq
        for g in range(GQA_KV_HEADS):
            for r in range(GQA_GROUP):
                c0 = base + (g * GQA_GROUP + r) * tq
                pieces.append(acc[0:HEAD_DIM, c0:c0 + tq] / acc[HEAD_DIM:HEAD_DIM + 1, c0:c0 + tq])
        o = jnp.concatenate(pieces, axis=0).T.astype(BF16)
        y = jnp.dot(o, wo_ref[...], preferred_element_type=F32)
        o_ref[0] = x_ref[0] + mod_ref[0][2:3] * y


def _flash(x, mod_l, mod_row, w_out, lam_vecs, sg, qt, ctx_kv, lat_kv, *, lambda_init, tq, tk):
    b, _, sq = qt.shape
    d = x.shape[2]
    kc, vtac, vtbc = ctx_kv
    c = kc.shape[1]
    has_lat = lat_kv is not None
    tq = min(tq, sq)
    ncols = (DIFF_HEADS * 2 + GQA_HEADS) * tq
    in_specs = [_resident((1, HEAD_DIM))] * 4 + [
        _resident((V_ROWS, 1)),
        pl.BlockSpec((1, tq, d), lambda i, q, j: (i, q, 0)),
        pl.BlockSpec((1, 6, d), lambda i, q, j: (mod_row(i), 0, 0)),
        _resident((1024, d)),
        pl.BlockSpec((1, 1024, tq), lambda i, q, j: (i, 0, q)),
        pl.BlockSpec((1, c, K_COLS), lambda i, q, j: (i, 0, 0)),
        pl.BlockSpec((1, DIFF_HEADS, V_ROWS_PADDED, c), lambda i, q, j: (i, 0, 0, 0)),
        pl.BlockSpec((1, GQA_KV_HEADS, GV_ROWS_PADDED, c), lambda i, q, j: (i, 0, 0, 0)),
    ]
    args = list(lam_vecs) + [sg, x, mod_l, w_out, qt, kc, vtac, vtbc]
    nkv = 1
    if has_lat:
        kl, vtal, vtbl = lat_kv
        n = kl.shape[1]
        tk = min(tk, n)
        nkv = n // tk
        in_specs += [pl.BlockSpec((1, tk, K_COLS), lambda i, q, j: (i, j, 0)),
                     pl.BlockSpec((1, DIFF_HEADS, V_ROWS_PADDED, tk), lambda i, q, j: (i, 0, 0, j)),
                     pl.BlockSpec((1, GQA_KV_HEADS, GV_ROWS_PADDED, tk), lambda i, q, j: (i, 0, 0, j))]
        args += [kl, vtal, vtbl]
    scratch = [pltpu.VMEM((2 * HEAD_DIM, ncols), BF16),
               pltpu.VMEM((V_ROWS_PADDED, ncols), F32),
               pltpu.VMEM((1, ncols), F32),
               pltpu.VMEM((V_ROWS_PADDED, ncols), F32),
               pltpu.VMEM((1, ncols), F32)]
    body = functools.partial(_flash_body, tq=tq, lambda_init=lambda_init, has_lat=has_lat)
    return pl.pallas_call(
        body,
        grid=(b, sq // tq, nkv),
        in_specs=in_specs,
        out_specs=pl.BlockSpec((1, tq, d), lambda i, q, j: (i, q, 0)),
        out_shape=jax.ShapeDtypeStruct((b, sq, d), F32),
        scratch_shapes=scratch,
        compiler_params=_cparams("parallel", "parallel", "arbitrary"),
        name="attn_sweep_lat" if has_lat else "attn_sweep_ctx",
    )(*args)


def _ffn_body(x_ref, xp_ref, xn_ref, mod_ref, g_ref, wv_ref, wg_ref, cw_ref, cb_ref, wo_ref,
              fg_ref, o_ref, *, tm, final_norm):
    t = pl.program_id(1)
    nt = pl.num_programs(1)
    m = mod_ref[0]
    x = x_ref[0]
    xe = jnp.concatenate([xp_ref[0], x, xn_ref[0]], axis=0)
    he = _modulate(xe, g_ref[...], m[3:4], m[4:5]).astype(BF16)
    ge = jnp.dot(he, wg_ref[...], preferred_element_type=F32)
    rows = lax.broadcasted_iota(jnp.int32, (tm + 2 * CONV_HALO, 1), 0)
    lo = jnp.where(t > 0, 0, CONV_HALO)
    hi = jnp.where(t < nt - 1, tm + 2 * CONV_HALO, tm + CONV_HALO)
    ge = jnp.where((rows >= lo) & (rows < hi), ge, 0.0)
    val = jnp.dot(he[CONV_HALO:CONV_HALO + tm], wv_ref[...], preferred_element_type=F32)
    cw = cw_ref[...]
    h0 = CONV_HALO
    conv = (cb_ref[...] + cw[0:1] * ge[h0 - 1:h0 - 1 + tm] + cw[1:2] * ge[h0:h0 + tm]
            + cw[2:3] * ge[h0 + 1:h0 + 1 + tm])
    gelu = 0.5 * conv * (1.0 + lax.erf(conv * math.sqrt(0.5)))
    act = (gelu * val).astype(BF16)
    y = jnp.dot(act, wo_ref[...], preferred_element_type=F32)
    out = x + m[5:6] * y
    if final_norm:
        ms = jnp.mean(out * out, axis=-1, keepdims=True)
        out = out * lax.rsqrt(ms + NORM_EPS) * fg_ref[...]
    o_ref[0] = out


def _ffn(x, mod_l, mod_row, g, wv, wg, cw, cb, wo, fg, *, tm, final_norm):
    b, s, d = x.shape
    tm = min(tm, s)
    hb = tm // CONV_HALO
    last = s // CONV_HALO - 1
    body = functools.partial(_ffn_body, tm=tm, final_norm=final_norm)
    return pl.pallas_call(
        body,
        grid=(b, s // tm),
        in_specs=[pl.BlockSpec((1, tm, d), lambda i, t: (i, t, 0)),
                  pl.BlockSpec((1, CONV_HALO, d), lambda i, t: (i, jnp.maximum(t * hb - 1, 0), 0)),
                  pl.BlockSpec((1, CONV_HALO, d), lambda i, t: (i, jnp.minimum((t + 1) * hb, last), 0)),
                  pl.BlockSpec((1, 6, d), lambda i, t: (mod_row(i), 0, 0)),
                  _resident((1, d)),
                  _resident((d, D_FF)),
                  _resident((d, D_FF)),
                  _resident((3, D_FF)),
                  _resident((1, D_FF)),
                  _resident((D_FF, d)),
                  _resident((1, d))],
        out_specs=pl.BlockSpec((1, tm, d), lambda i, t: (i, t, 0)),
        out_shape=jax.ShapeDtypeStruct((b, s, d), F32),
        compiler_params=_cparams("parallel", "parallel"),
        name="conv_glu_ffn",
    )(x, x, x, mod_l, g, wv, wg, cw, cb, wo, fg)


def _ssm_proj_body(x_ref, mod_ref, g_ref, wt_ref, dtb_ref, zt_ref, xbct_ref, dtt_ref):
    m = mod_ref[0]
    h = _modulate(x_ref[0], g_ref[...], m[0:1], m[1:2]).astype(BF16)
    t = lax.dot_general(wt_ref[...], h, NT_DIMS, preferred_element_type=F32)
    zt_ref[0] = t[0:SSM_D_INNER].astype(BF16)
    xbct_ref[0] = t[SSM_D_INNER:SSM_D_INNER + SSM_XBC].astype(BF16)
    dtt_ref[0] = jax.nn.softplus(t[SSM_D_INNER + SSM_XBC:SSM_IN_COLS] + dtb_ref[...])


def _ssm_proj(x, mod_l, mod_row, g, wt, dtb, *, tm):
    b, s, d = x.shape
    tm = min(tm, s)
    return pl.pallas_call(
        _ssm_proj_body,
        grid=(b, s // tm),
        in_specs=[pl.BlockSpec((1, tm, d), lambda i, t: (i, t, 0)),
                  pl.BlockSpec((1, 6, d), lambda i, t: (mod_row(i), 0, 0)),
                  _resident((1, d)),
                  _resident((SSM_IN_COLS, d)),
                  _resident((2 * SSM_HEADS, 1))],
        out_specs=[pl.BlockSpec((1, SSM_D_INNER, tm), lambda i, t: (i, 0, t)),
                   pl.BlockSpec((1, SSM_XBC, tm), lambda i, t: (i, 0, t)),
                   pl.BlockSpec((1, 2 * SSM_HEADS, tm), lambda i, t: (i, 0, t))],
        out_shape=[jax.ShapeDtypeStruct((b, SSM_D_INNER, s), BF16),
                   jax.ShapeDtypeStruct((b, SSM_XBC, s), BF16),
                   jax.ShapeDtypeStruct((b, 2 * SSM_HEADS, s), F32)],
        compiler_params=_cparams("parallel", "parallel"),
        name="ssm_in_proj",
    )(x, mod_l, g, wt, dtb)


def _ssm_conv_body(u_ref, up_ref, un_ref, w_ref, b_ref, xst_ref, ct_ref, bm_ref, *, tc):
    t = pl.program_id(1)
    nt = pl.num_programs(1)
    u = u_ref[0].astype(F32)
    prev = jnp.where(t > 0, up_ref[0][:, V7X_LANES - 1:V7X_LANES].astype(F32), 0.0)
    nxt = jnp.where(t < nt - 1, un_ref[0][:, 0:1].astype(F32), 0.0)
    lane = lax.broadcasted_iota(jnp.int32, (1, tc), 1)
    packed = pltpu.bitcast(u_ref[0], jnp.uint32)
    rolled_l = pltpu.bitcast(pltpu.roll(packed, 1, 1), BF16).astype(F32)
    rolled_r = pltpu.bitcast(pltpu.roll(packed, tc - 1, 1), BF16).astype(F32)
    left = jnp.where(lane == 0, prev, rolled_l)
    right = jnp.where(lane == tc - 1, nxt, rolled_r)
    def lanes(a):
        return jnp.concatenate([a] * (tc // V7X_LANES), axis=1)

    v = _silu(lanes(b_ref[...]) + lanes(w_ref[0]) * left + lanes(w_ref[1]) * u
              + lanes(w_ref[2]) * right)
    bn = SSM_GROUPS * SSM_STATE
    xst_ref[0] = v[0:SSM_D_INNER].astype(BF16)
    bm_ref[0] = v[SSM_D_INNER:SSM_D_INNER + bn].T.astype(BF16)
    ct_ref[0] = v[SSM_D_INNER + bn:SSM_XBC].astype(BF16)


def _ssm_conv(xbct, w, bias, *, tc):
    b, ch, s = xbct.shape
    tc = min(tc, s)
    hb = tc // V7X_LANES
    last = s // V7X_LANES - 1
    bn = SSM_GROUPS * SSM_STATE
    body = functools.partial(_ssm_conv_body, tc=tc)
    return pl.pallas_call(
        body,
        grid=(b, s // tc),
        in_specs=[pl.BlockSpec((1, ch, tc), lambda i, t: (i, 0, t)),
                  pl.BlockSpec((1, ch, V7X_LANES), lambda i, t: (i, 0, jnp.maximum(t * hb - 1, 0))),
                  pl.BlockSpec((1, ch, V7X_LANES), lambda i, t: (i, 0, jnp.minimum((t + 1) * hb, last))),
                  _resident((3, ch, V7X_LANES)),
                  _resident((ch, V7X_LANES))],
        out_specs=[pl.BlockSpec((1, SSM_D_INNER, tc), lambda i, t: (i, 0, t)),
                   pl.BlockSpec((1, bn, tc), lambda i, t: (i, 0, t)),
                   pl.BlockSpec((1, tc, bn), lambda i, t: (i, t, 0))],
        out_shape=[jax.ShapeDtypeStruct((b, SSM_D_INNER, s), BF16),
                   jax.ShapeDtypeStruct((b, bn, s), BF16),
                   jax.ShapeDtypeStruct((b, s, bn), BF16)],
        compiler_params=_cparams("parallel", "parallel"),
        name="ssm_conv_silu",
    )(xbct, xbct, xbct, w, bias)


def _dot_f32_by_01(a, m01):
    hi = a.astype(BF16)
    r1 = a - hi.astype(F32)
    mid = r1.astype(BF16)
    lo = (r1 - mid.astype(F32)).astype(BF16)
    return (jnp.dot(hi, m01, preferred_element_type=F32)
            + jnp.dot(mid, m01, preferred_element_type=F32)
            + jnp.dot(lo, m01, preferred_element_type=F32))


def _scan_order_mask(d, chunk):
    jrow = lax.broadcasted_iota(jnp.int32, (chunk, chunk), 0)
    icol = lax.broadcasted_iota(jnp.int32, (chunk, chunk), 1)
    sign = jnp.where(d == 0, 1, -1)
    return (icol - jrow) * sign >= 0


def _ssd_decay_body(a_ref, dt_ref, cum2_ref, src_ref, toend_ref, ecum_ref, etot_ref, *, chunk):
    d = pl.program_id(1)
    mask01 = _scan_order_mask(d, chunk).astype(BF16)
    for s in range(dt_ref.shape[2] // chunk):
        tok = slice(s * chunk, (s + 1) * chunk)
        dt = dt_ref[0, :, tok]
        cum_t = _dot_f32_by_01(dt * a_ref[0], mask01)
        cum2_t = cum_t * LOG2E
        cum2_ref[0, 0, :, tok] = cum2_t
        src_ref[0, 0, tok, :] = (cum2_t - jnp.log2(dt)).T
        tot = jnp.where(d == 0, cum_t[:, chunk - 1:chunk], cum_t[:, 0:1])
        toend_ref[0, 0, :, tok] = jnp.exp(tot - cum_t) * dt
        ecum_ref[0, 0, :, tok] = jnp.exp(cum_t)
        etot_ref[0, 0, :, tok] = jnp.broadcast_to(jnp.exp(tot), cum_t.shape)


def _ssd_decay(a, dtt, chunk):
    b, _, s = dtt.shape
    span = min(s, 8 * chunk)
    body = functools.partial(_ssd_decay_body, chunk=chunk)
    by_head = pl.BlockSpec((1, 1, SSM_HEADS, span), lambda i, d, g: (i, d, 0, g))
    return pl.pallas_call(
        body,
        grid=(b, 2, s // span),
        in_specs=[pl.BlockSpec((1, SSM_HEADS, 1), lambda i, d, g: (d, 0, 0)),
                  pl.BlockSpec((1, SSM_HEADS, span), lambda i, d, g: (i, d, g))],
        out_specs=[by_head,
                   pl.BlockSpec((1, 1, span, SSM_HEADS), lambda i, d, g: (i, d, g, 0)),
                   by_head, by_head, by_head],
        out_shape=[jax.ShapeDtypeStruct((b, 2, SSM_HEADS, s), F32),
                   jax.ShapeDtypeStruct((b, 2, s, SSM_HEADS), F32)]
        + [jax.ShapeDtypeStruct((b, 2, SSM_HEADS, s), F32)] * 3,
        compiler_params=_cparams("parallel", "parallel", "parallel"),
        name="ssd_decay_factors",
    )(a, dtt)


def _scan_body(cum2_ref, src_ref, toend_ref, ecum_ref, etot_ref, xs_ref, ct_ref, b_ref, s0_ref,
               y_ref, sout_ref, state, *, chunk):
    d = pl.program_id(1)
    c = pl.program_id(2)
    nc = pl.num_programs(2)
    hp = SSM_HEADS_PER_GROUP
    hd = SSM_D_INNER // SSM_HEADS
    gw = hp * hd

    @pl.when(c == 0)
    def _load_state():
        state[...] = s0_ref[0, 0]

    not_yet = jnp.where(_scan_order_mask(d, chunk), 0.0, -jnp.inf)
    cum2_t = cum2_ref[0, 0]
    src_term = src_ref[0, 0]
    to_end = toend_ref[0, 0]
    ecum = ecum_ref[0, 0]
    etot = etot_ref[0, 0][:, 0:1]

    for g in range(SSM_GROUPS):
        bg = b_ref[0, :, g * SSM_STATE:(g + 1) * SSM_STATE]
        ctg = ct_ref[0, g * SSM_STATE:(g + 1) * SSM_STATE, :]
        cbt = jnp.dot(bg, ctg, preferred_element_type=F32)
        xg = xs_ref[0, g * gw:(g + 1) * gw, :]
        sg = state[g * gw:(g + 1) * gw, :]
        hs = slice(g * hp, (g + 1) * hp)
        y_state = (jnp.dot(sg.astype(BF16), ctg, preferred_element_type=F32)
                   .reshape(hp, hd, chunk) * ecum[hs][:, None, :])
        outs = []
        for r in range(hp):
            h = g * hp + r
            seg = cum2_t[h:h + 1, :] - src_term[:, h:h + 1]
            w = (cbt * jnp.exp2(seg + not_yet)).astype(BF16)
            outs.append(jnp.dot(xg[r * hd:(r + 1) * hd], w, preferred_element_type=F32)
                        + y_state[r])
        y_ref[0, 0, g * gw:(g + 1) * gw, :] = jnp.concatenate(outs, axis=0).astype(BF16)
        xw = ((xg.astype(F32).reshape(hp, hd, chunk) * to_end[hs][:, None, :])
              .reshape(gw, chunk).astype(BF16))
        upd = jnp.dot(xw, bg, preferred_element_type=F32)
        decayed = (sg.reshape(hp, hd, SSM_STATE) * etot[hs][:, None, :]).reshape(gw, SSM_STATE)
        state[g * gw:(g + 1) * gw, :] = decayed + upd

    @pl.when(c == nc - 1)
    def _store_state():
        sout_ref[0, 0] = state[...]


def _ssd_scan(a, dtt, xst, ct, bm, s0):
    b, ch, s = xst.shape
    chunk = min(SSD_CHUNK, s)
    nc = s // chunk
    bn = SSM_GROUPS * SSM_STATE
    cum2, src, toend, ecum, etot = _ssd_decay(a, dtt, chunk)

    def cidx(d, c):
        return c + d * (nc - 1 - 2 * c)

    by_head = pl.BlockSpec((1, 1, SSM_HEADS, chunk), lambda i, d, c: (i, d, 0, cidx(d, c)))
    body = functools.partial(_scan_body, chunk=chunk)
    return pl.pallas_call(
        body,
        grid=(b, 2, nc),
        in_specs=[by_head,
                  pl.BlockSpec((1, 1, chunk, SSM_HEADS), lambda i, d, c: (i, d, cidx(d, c), 0)),
                  by_head, by_head, by_head,
                  pl.BlockSpec((1, ch, chunk), lambda i, d, c: (i, 0, cidx(d, c))),
                  pl.BlockSpec((1, bn, chunk), lambda i, d, c: (i, 0, cidx(d, c))),
                  pl.BlockSpec((1, chunk, bn), lambda i, d, c: (i, cidx(d, c), 0)),
                  pl.BlockSpec((1, 1, ch, SSM_STATE), lambda i, d, c: (d, i, 0, 0))],
        out_specs=[pl.BlockSpec((1, 1, ch, chunk), lambda i, d, c: (d, i, 0, cidx(d, c))),
                   pl.BlockSpec((1, 1, ch, SSM_STATE), lambda i, d, c: (d, i, 0, 0))],
        out_shape=[jax.ShapeDtypeStruct((2, b, ch, s), BF16),
                   jax.ShapeDtypeStruct((2, b, ch, SSM_STATE), F32)],
        scratch_shapes=[pltpu.VMEM((ch, SSM_STATE), F32)],
        compiler_params=_cparams("parallel", "parallel", "arbitrary"),
        name="ssd_scan",
    )(cum2, src, toend, ecum, etot, xst, ct, bm, s0)


def _ssm_out_body(x_ref, mod_ref, yf_ref, yb_ref, xs_ref, z_ref, dsk_ref, ng_ref, wt_ref, o_ref):
    y = (yf_ref[0, 0].astype(F32) + yb_ref[0, 0].astype(F32)
         + dsk_ref[...] * xs_ref[0].astype(F32))
    y = y * _silu(z_ref[0].astype(F32))
    ms = jnp.mean(y * y, axis=0, keepdims=True)
    yn = (y * lax.rsqrt(ms + NORM_EPS) * ng_ref[...]).astype(BF16)
    ot = jnp.dot(wt_ref[...], yn, preferred_element_type=F32)
    o_ref[0] = x_ref[0] + mod_ref[0][2:3] * ot.T


def _ssm_out(x, mod_l, mod_row, y, xst, zt, dsk, ng, wt, *, tm):
    b, s, d = x.shape
    tm = min(tm, s)
    ch = xst.shape[1]
    return pl.pallas_call(
        _ssm_out_body,
        grid=(b, s // tm),
        in_specs=[pl.BlockSpec((1, tm, d), lambda i, t: (i, t, 0)),
                  pl.BlockSpec((1, 6, d), lambda i, t: (mod_row(i), 0, 0)),
                  pl.BlockSpec((1, 1, ch, tm), lambda i, t: (0, i, 0, t)),
                  pl.BlockSpec((1, 1, ch, tm), lambda i, t: (1, i, 0, t)),
                  pl.BlockSpec((1, ch, tm), lambda i, t: (i, 0, t)),
                  pl.BlockSpec((1, ch, tm), lambda i, t: (i, 0, t)),
                  _resident((ch, 1)),
                  _resident((ch, 1)),
                  _resident((d, ch))],
        out_specs=pl.BlockSpec((1, tm, d), lambda i, t: (i, t, 0)),
        out_shape=jax.ShapeDtypeStruct((b, s, d), F32),
        compiler_params=_cparams("parallel", "parallel"),
        name="ssm_out_residual",
    )(x, mod_l, y, y, xst, zt, dsk, ng, wt)


def _rope_tables_t(n):
    t = jnp.arange(n)
    inv_freq = 1.0 / (ROPE_BASE ** (jnp.arange(ROPE_PAIRS, dtype=F32) / ROPE_PAIRS))
    ang_r = (t // GRID_W).astype(F32)[None, :] * inv_freq[:, None]
    ang_c = (t % GRID_W).astype(F32)[None, :] * inv_freq[:, None]
    cr, sr, cc, sc = jnp.cos(ang_r), jnp.sin(ang_r), jnp.cos(ang_c), jnp.sin(ang_c)
    return (jnp.concatenate([cr, cr, cc, cc], axis=0),
            jnp.concatenate([-sr, sr, -sc, sc], axis=0))


def _attn_layer(x, ctx, mod_l, lat_row, ctx_row, p, rope, lambda_init, with_ctx):
    wt = p["w_in"].T.astype(BF16)
    g = p["norm_g"].reshape(1, D_MODEL)
    qg = p["q_norm_g"].reshape(HEAD_DIM, 1)
    kg = p["k_norm_g"].reshape(HEAD_DIM, 1)
    cos_t, sin_t = rope
    c = ctx.shape[1]
    qt, *lat_kv = _attn_in(x, mod_l, lat_row, g, wt, cos_t, sin_t, qg, kg, rope=True, tm=512)
    qtc, *ctx_kv = _attn_in(ctx, mod_l, ctx_row, g, wt, cos_t[:, :c], sin_t[:, :c], qg, kg,
                            rope=False, tm=256)
    lam_vecs = [p[k].reshape(1, HEAD_DIM) for k in ("lq1", "lk1", "lq2", "lk2")]
    sg = p["subln_g"].reshape(V_ROWS, 1)
    w_out = p["w_out"].astype(BF16)
    x = _flash(x, mod_l, lat_row, w_out, lam_vecs, sg, qt, ctx_kv, lat_kv,
               lambda_init=lambda_init, tq=256, tk=1024)
    if with_ctx:
        ctx = _flash(ctx, mod_l, ctx_row, w_out, lam_vecs, sg, qtc, ctx_kv, None,
                     lambda_init=lambda_init, tq=256, tk=0)
    return x, ctx


def _ssm_layer(x, ctx, mod_l, lat_row, ctx_row, p, with_ctx):
    wt = p["w_in"].T.astype(BF16)
    g = p["norm_g"].reshape(1, D_MODEL)
    dtb = p["dt_bias"].reshape(2 * SSM_HEADS, 1)
    conv_w = jnp.broadcast_to(p["conv_w"][:, :, None], (3, SSM_XBC, V7X_LANES))
    conv_b = jnp.broadcast_to(p["conv_b"][:, None], (SSM_XBC, V7X_LANES))
    a = (-jnp.exp(p["a_log"].astype(F32))).reshape(2, SSM_HEADS, 1)
    b = x.shape[0]

    def pre(v, row, tm):
        zt, xbct, dtt = _ssm_proj(v, mod_l, row, g, wt, dtb, tm=tm)
        xst, ct, bm = _ssm_conv(xbct, conv_w, conv_b, tc=512)
        return zt, xst, ct, bm, dtt

    zt_c, xst_c, ct_c, bm_c, dtt_c = pre(ctx, ctx_row, 256)
    zt_l, xst_l, ct_l, bm_l, dtt_l = pre(x, lat_row, 512)
    zero = jnp.zeros((2, b, SSM_D_INNER, SSM_STATE), F32)
    y_c, s_ctx = _ssd_scan(a, dtt_c, xst_c, ct_c, bm_c, zero)
    y_l, _ = _ssd_scan(a, dtt_l, xst_l, ct_l, bm_l, s_ctx)
    dsk = jnp.repeat(p["d_skip"], SSM_D_INNER // SSM_HEADS).reshape(SSM_D_INNER, 1)
    ng = p["out_norm_g"].reshape(SSM_D_INNER, 1)
    w_out_t = p["w_out"].T.astype(BF16)
    x = _ssm_out(x, mod_l, lat_row, y_l, xst_l, zt_l, dsk, ng, w_out_t, tm=512)
    if with_ctx:
        ctx = _ssm_out(ctx, mod_l, ctx_row, y_c, xst_c, zt_c, dsk, ng, w_out_t, tm=256)
    return x, ctx


def kernel(x, c, ctx, c_ctx, mod_w, mod_b, norm_mix_g, norm_ffn_g, attn_w_in, attn_w_out,
           diff_lq1, diff_lk1, diff_lq2, diff_lk2, diff_subln_g, gqa_q_norm_g, gqa_k_norm_g,
           ssm_w_in, ssm_conv_w, ssm_conv_b, ssm_dt_bias, ssm_a_log, ssm_d, ssm_norm_g, ssm_w_out,
           ffn_w_in, ffn_conv_w, ffn_conv_b, ffn_w_out, final_norm_g):
    b, n, d = x.shape
    mod_rows = 16
    c_rows = jnp.zeros((mod_rows, d), F32).at[:b].set(c).at[b].set(c_ctx)
    mod = _mod_all(c_rows, mod_w, mod_b).reshape(DEPTH, mod_rows, 6, d)
    lat_row = lambda i: i
    ctx_row = lambda i: b
    rope = _rope_tables_t(n)

    for layer in range(DEPTH):
        with_ctx = layer < DEPTH - 1
        mod_l = mod[layer]
        i = layer // 2
        if layer % 2 == 0:
            p = dict(w_in=attn_w_in[i], w_out=attn_w_out[i], norm_g=norm_mix_g[layer],
                     lq1=diff_lq1[i], lk1=diff_lk1[i], lq2=diff_lq2[i], lk2=diff_lk2[i],
                     subln_g=diff_subln_g[i], q_norm_g=gqa_q_norm_g[i], k_norm_g=gqa_k_norm_g[i])
            lambda_init = 0.8 - 0.6 * math.exp(-0.3 * layer)
            x, ctx = _attn_layer(x, ctx, mod_l, lat_row, ctx_row, p, rope, lambda_init, with_ctx)
        else:
            p = dict(w_in=ssm_w_in[i], norm_g=norm_mix_g[layer], conv_w=ssm_conv_w[i],
                     conv_b=ssm_conv_b[i], dt_bias=ssm_dt_bias[i], a_log=ssm_a_log[i],
                     d_skip=ssm_d[i], out_norm_g=ssm_norm_g[i], w_out=ssm_w_out[i])
            x, ctx = _ssm_layer(x, ctx, mod_l, lat_row, ctx_row, p, with_ctx)
        g = norm_ffn_g[layer].reshape(1, d)
        wv = ffn_w_in[layer][:, :D_FF].astype(BF16)
        wg = ffn_w_in[layer][:, D_FF:].astype(BF16)
        cw = ffn_conv_w[layer]
        cb = ffn_conv_b[layer].reshape(1, D_FF)
        wo = ffn_w_out[layer].astype(BF16)
        fg = final_norm_g.reshape(1, d)
        x = _ffn(x, mod_l, lat_row, g, wv, wg, cw, cb, wo, fg, tm=256,
                 final_norm=layer == DEPTH - 1)
        if with_ctx:
            ctx = _ffn(ctx, mod_l, ctx_row, g, wv, wg, cw, cb, wo, fg, tm=256, final_norm=False)
    return x
```

```python
import functools
import math

import jax
import jax.numpy as jnp
from jax import lax
from jax.experimental import pallas as pl
from jax.experimental.pallas import tpu as pltpu

F32 = jnp.float32
BF16 = jnp.bfloat16

D_MODEL = 1024
DEPTH = 4
GRID_W = 64
HEAD_DIM = 64
ROPE_PAIRS = HEAD_DIM // 4
ROPE_BASE = 10000.0
NORM_EPS = 1e-6
DIFF_HEADS = 4
GQA_HEADS = 8
GQA_KV_HEADS = 2
GQA_GROUP = GQA_HEADS // GQA_KV_HEADS
ATTN_IN_COLS = 2304
SSM_D_INNER = 2048
SSM_HEADS = 32
SSM_GROUPS = 4
SSM_HEADS_PER_GROUP = SSM_HEADS // SSM_GROUPS
SSM_STATE = 128
SSM_XBC = 3072
SSM_IN_COLS = 5184
D_FF = 2816
LOG2E = math.log2(math.e)

V7X_VMEM_BYTES = 64 * 1024 * 1024
VMEM_LIMIT_BYTES = V7X_VMEM_BYTES - 8 * 1024 * 1024
V7X_LANES = 128
BF16_SUBLANES = 16

V_ROWS = 2 * HEAD_DIM
V_ROWS_PADDED = V_ROWS + BF16_SUBLANES
GV_ROWS_PADDED = HEAD_DIM + BF16_SUBLANES
K_COLS = (DIFF_HEADS + 1) * 2 * HEAD_DIM

ATTN_KEY_SUBBLOCK = 256
ATTN_STREAM_MAX_OCTAVES = 64.0

CONV_HALO = 16
SSD_CHUNK = 256

NT_DIMS = (((1,), (1,)), ((), ()))


def _cparams(*sem):
    return pltpu.CompilerParams(dimension_semantics=sem, vmem_limit_bytes=VMEM_LIMIT_BYTES)


def _resident(shape):
    nd = len(shape)
    return pl.BlockSpec(shape, lambda *_: (0,) * nd, pipeline_mode=pl.Buffered(1))


def _silu(v):
    return v * jax.nn.sigmoid(v)


def _modulate(x, g, shift, scale):
    ms = jnp.mean(x * x, axis=-1, keepdims=True)
    return (x * lax.rsqrt(ms + NORM_EPS) * g) * (1.0 + scale) + shift


def _mod_body(c_ref, w_ref, b_ref, o_ref):
    s = _silu(c_ref[...])
    o_ref[0] = jnp.dot(s, w_ref[0], preferred_element_type=F32,
                       precision=lax.Precision.HIGHEST) + b_ref[0]


def _mod_all(c_rows, mod_w, mod_b):
    rows = c_rows.shape[0]
    depth, d, cols = mod_w.shape
    tn = 2048
    return pl.pallas_call(
        _mod_body,
        grid=(depth, cols // tn),
        in_specs=[pl.BlockSpec((rows, d), lambda l, n: (0, 0)),
                  pl.BlockSpec((1, d, tn), lambda l, n: (l, 0, n)),
                  pl.BlockSpec((1, 1, tn), lambda l, n: (l, 0, n))],
        out_specs=pl.BlockSpec((1, rows, tn), lambda l, n: (l, 0, n)),
        out_shape=jax.ShapeDtypeStruct((depth, rows, cols), F32),
        compiler_params=_cparams("parallel", "parallel"),
        name="mod_vectors",
    )(c_rows, mod_w, mod_b.reshape(depth, 1, cols))


def _attn_in_body(x_ref, mod_ref, g_ref, wt_ref, cos_ref, sin_ref, qg_ref, kg_ref,
                  qt_ref, k_ref, vta_ref, vtb_ref, *, rope):
    m = mod_ref[0]
    h = _modulate(x_ref[0], g_ref[...], m[0:1], m[1:2]).astype(BF16)
    t = lax.dot_general(wt_ref[...], h, NT_DIMS, preferred_element_type=F32)
    tm = t.shape[1]

    def rot(u):
        if not rope:
            return u
        sw = jnp.concatenate([u[:, 16:32], u[:, 0:16], u[:, 48:64], u[:, 32:48]], axis=1)
        return u * cos_ref[...][None] + sw * sin_ref[...][None]

    def qk_norm(u, g):
        ms = jnp.mean(u * u, axis=1, keepdims=True)
        return u * lax.rsqrt(ms + NORM_EPS) * g[None]

    nq = 2 * DIFF_HEADS
    qa = rot(t[0:512].reshape(nq, HEAD_DIM, tm))
    ka = rot(t[512:1024].reshape(nq, HEAD_DIM, tm))
    va = t[1024:1536]
    qb = rot(qk_norm(t[1536:2048].reshape(GQA_HEADS, HEAD_DIM, tm), qg_ref[...]))
    kb = rot(qk_norm(t[2048:2176].reshape(GQA_KV_HEADS, HEAD_DIM, tm), kg_ref[...]))
    vb = t[2176:2304]

    qs = (HEAD_DIM ** -0.5) * LOG2E
    qt_ref[0, 0:512] = (qa * qs).reshape(512, tm).astype(BF16)
    qt_ref[0, 512:1024] = (qb * qs).reshape(512, tm).astype(BF16)
    kt = jnp.concatenate([ka.reshape(512, tm), kb.reshape(128, tm)], axis=0)
    k_ref[0] = kt.T.astype(BF16)
    ones = jnp.ones((BF16_SUBLANES, tm), BF16)
    for u in range(DIFF_HEADS):
        vta_ref[0, u, 0:V_ROWS] = va[u * V_ROWS:(u + 1) * V_ROWS].astype(BF16)
        vta_ref[0, u, V_ROWS:V_ROWS_PADDED] = ones
    for g in range(GQA_KV_HEADS):
        vtb_ref[0, g, 0:HEAD_DIM] = vb[g * HEAD_DIM:(g + 1) * HEAD_DIM].astype(BF16)
        vtb_ref[0, g, HEAD_DIM:GV_ROWS_PADDED] = ones


def _attn_in(x, mod_l, mod_row, g, wt, cos_t, sin_t, qg, kg, *, rope, tm):
    b, s, d = x.shape
    tm = min(tm, s)
    body = functools.partial(_attn_in_body, rope=rope)
    return pl.pallas_call(
        body,
        grid=(b, s // tm),
        in_specs=[pl.BlockSpec((1, tm, d), lambda i, t: (i, t, 0)),
                  pl.BlockSpec((1, 6, d), lambda i, t: (mod_row(i), 0, 0)),
                  _resident((1, d)),
                  _resident((ATTN_IN_COLS, d)),
                  pl.BlockSpec((HEAD_DIM, tm), lambda i, t: (0, t)),
                  pl.BlockSpec((HEAD_DIM, tm), lambda i, t: (0, t)),
                  _resident((HEAD_DIM, 1)),
                  _resident((HEAD_DIM, 1))],
        out_specs=[pl.BlockSpec((1, 1024, tm), lambda i, t: (i, 0, t)),
                   pl.BlockSpec((1, tm, K_COLS), lambda i, t: (i, t, 0)),
                   pl.BlockSpec((1, DIFF_HEADS, V_ROWS_PADDED, tm), lambda i, t: (i, 0, 0, t)),
                   pl.BlockSpec((1, GQA_KV_HEADS, GV_ROWS_PADDED, tm), lambda i, t: (i, 0, 0, t))],
        out_shape=[jax.ShapeDtypeStruct((b, 1024, s), BF16),
                   jax.ShapeDtypeStruct((b, s, K_COLS), BF16),
                   jax.ShapeDtypeStruct((b, DIFF_HEADS, V_ROWS_PADDED, s), BF16),
                   jax.ShapeDtypeStruct((b, GQA_KV_HEADS, GV_ROWS_PADDED, s), BF16)],
        compiler_params=_cparams("parallel", "parallel"),
        name="attn_in_rope" if rope else "attn_in_ctx",
    )(x, mod_l, g, wt, cos_t, sin_t, qg, kg)


def _attn_unit_cols(tq):
    units = []
    for h in range(DIFF_HEADS):
        units.append((h * 128, h, V_ROWS_PADDED, h * 2 * tq, 2 * tq))
    base = DIFF_HEADS * 2 * tq
    for g in range(GQA_KV_HEADS):
        for pair in range(GQA_GROUP // 2):
            units.append((DIFF_HEADS * 128, g, GV_ROWS_PADDED,
                          base + (g * GQA_GROUP + 2 * pair) * tq, 2 * tq))
    return units


def _flash_body(*refs, tq, lambda_init, has_lat):
    if has_lat:
        (lq1, lk1, lq2, lk2, sg_ref, x_ref, mod_ref, wo_ref, qt_ref, kc_ref, vtac_ref, vtbc_ref,
         kl_ref, vtal_ref, vtbl_ref, o_ref, rhs, acc, mrow, pv_new, m_chunk) = refs
    else:
        (lq1, lk1, lq2, lk2, sg_ref, x_ref, mod_ref, wo_ref, qt_ref, kc_ref, vtac_ref, vtbc_ref,
         o_ref, rhs, acc, mrow, pv_new, m_chunk) = refs
    j = pl.program_id(2)
    nj = pl.num_programs(2)
    units = _attn_unit_cols(tq)
    diff_cols = DIFF_HEADS * 2 * tq

    def values(vta_ref, vtb_ref, u, vu):
        return vta_ref[0, vu] if u < DIFF_HEADS else vtb_ref[0, vu]

    def exact_step(k_ref, vta_ref, vtb_ref):
        for u, (kc0, vu, vr, c0, w) in enumerate(units):
            s = jnp.dot(k_ref[0, :, kc0:kc0 + 128], rhs[:, c0:c0 + w],
                        preferred_element_type=F32)
            mp = mrow[:, c0:c0 + w]
            mn = jnp.maximum(mp, jnp.max(s, axis=0, keepdims=True))
            alpha = jnp.exp2(mp - mn)
            p = jnp.exp2(s - mn).astype(BF16)
            pv = jnp.dot(values(vta_ref, vtb_ref, u, vu), p, preferred_element_type=F32)
            acc[0:vr, c0:c0 + w] = acc[0:vr, c0:c0 + w] * alpha + pv
            mrow[:, c0:c0 + w] = mn

    def streaming_step(k_ref, vta_ref, vtb_ref):
        nk = k_ref.shape[1]
        sub = min(nk, ATTN_KEY_SUBBLOCK)
        for u, (kc0, vu, vr, c0, w) in enumerate(units):
            m_used = mrow[:, c0:c0 + w]
            cm = None
            parts = []
            for r in range(nk // sub):
                s = jnp.dot(k_ref[0, r * sub:(r + 1) * sub, kc0:kc0 + 128], rhs[:, c0:c0 + w],
                            preferred_element_type=F32)
                parts.append(jnp.exp2(s - m_used).astype(BF16))
                sm = jnp.max(s.reshape(sub // 8, 8, w), axis=0)
                cm = sm if cm is None else jnp.maximum(cm, sm)
            p = jnp.concatenate(parts, axis=0)
            pv_new[0:vr, c0:c0 + w] = jnp.dot(values(vta_ref, vtb_ref, u, vu), p,
                                              preferred_element_type=F32)
            m_chunk[:, c0:c0 + w] = jnp.max(cm, axis=0, keepdims=True)

    @pl.when(j == 0)
    def _init():
        zeros = jnp.zeros((HEAD_DIM, tq), BF16)
        for h in range(DIFF_HEADS):
            c0 = h * 2 * tq
            rhs[0:64, c0:c0 + tq] = qt_ref[0, h * 128:h * 128 + 64, :]
            rhs[64:128, c0:c0 + tq] = zeros
            rhs[0:64, c0 + tq:c0 + 2 * tq] = zeros
            rhs[64:128, c0 + tq:c0 + 2 * tq] = qt_ref[0, h * 128 + 64:h * 128 + 128, :]
        base = DIFF_HEADS * 2 * tq
        for g in range(GQA_KV_HEADS):
            for r in range(GQA_GROUP):
                c0 = base + (g * GQA_GROUP + r) * tq
                hd = 512 + (g * GQA_GROUP + r) * HEAD_DIM
                rhs[g * 64:(g + 1) * 64, c0:c0 + tq] = qt_ref[0, hd:hd + HEAD_DIM, :]
                rhs[(1 - g) * 64:(2 - g) * 64, c0:c0 + tq] = zeros
        acc[...] = jnp.zeros(acc.shape, F32)
        mrow[...] = jnp.zeros(mrow.shape, F32)

    row_blocks = ((V_ROWS_PADDED, slice(0, diff_cols)),
                  (GV_ROWS_PADDED, slice(diff_cols, acc.shape[1])))

    def nothing_pending():
        for rows, cols in row_blocks:
            pv_new[0:rows, cols] = jnp.zeros((rows, cols.stop - cols.start), F32)
        m_chunk[...] = jnp.full(m_chunk.shape, -jnp.inf, F32)

    def commit():
        mp = mrow[...]
        mn = jnp.maximum(mp, m_chunk[...])
        mrow[...] = mn
        alpha = jnp.exp2(mp - mn)
        for rows, cols in row_blocks:
            acc[0:rows, cols] = (acc[0:rows, cols] + pv_new[0:rows, cols]) * alpha[:, cols]

    def stream_chunk(k_ref, vta_ref, vtb_ref, first):
        streaming_step(k_ref, vta_ref, vtb_ref)
        dev = m_chunk[...] - mrow[...]
        in_range = jnp.max(jnp.abs(dev) if first else dev) <= ATTN_STREAM_MAX_OCTAVES

        @pl.when(jnp.logical_not(in_range))
        def _redo():
            if first:
                mrow[...] = jnp.full(mrow.shape, -jnp.inf, F32)
            exact_step(k_ref, vta_ref, vtb_ref)
            nothing_pending()

    @pl.when(j == 0)
    def _context_keys():
        stream_chunk(kc_ref, vtac_ref, vtbc_ref, True)

    if has_lat:
        commit()
        stream_chunk(kl_ref, vtal_ref, vtbl_ref, False)

    @pl.when(j == nj - 1)
    def _finish():
        commit()
        lam = (jnp.exp(jnp.sum(lq1[...] * lk1[...], keepdims=True))
               - jnp.exp(jnp.sum(lq2[...] * lk2[...], keepdims=True)) + lambda_init)
        pieces = []
        for h in range(DIFF_HEADS):
            c0 = h * 2 * tq
            o1 = acc[0:V_ROWS, c0:c0 + tq] / acc[V_ROWS:V_ROWS + 1, c0:c0 + tq]
            o2 = acc[0:V_ROWS, c0 + tq:c0 + 2 * tq] / acc[V_ROWS:V_ROWS + 1, c0 + tq:c0 + 2 * tq]
            oh = o1 - lam * o2
            ms = jnp.mean(oh * oh, axis=0, keepdims=True)
            pieces.append(oh * lax.rsqrt(ms + NORM_EPS) * sg_ref[...] * (1.0 - lambda_init))
        base = DIFF_HEADS * 2 * tq
        for g in range(GQA_KV_HEADS):
            for r in range(GQA_GROUP):
                c0 = base + (g * GQA_GROUP + r) * tq
                pieces.append(acc[0:HEAD_DIM, c0:c0 + tq] / acc[HEAD_DIM:HEAD_DIM + 1, c0:c0 + tq])
        o = jnp.concatenate(pieces, axis=0).T.astype(BF16)
        y = jnp.dot(o, wo_ref[...], preferred_element_type=F32)
        o_ref[0] = x_ref[0] + mod_ref[0][2:3] * y


def _flash(x, mod_l, mod_row, w_out, lam_vecs, sg, qt, ctx_kv, lat_kv, *, lambda_init, tq, tk):
    b, _, sq = qt.shape
    d = x.shape[2]
    kc, vtac, vtbc = ctx_kv
    c = kc.shape[1]
    has_lat = lat_kv is not None
    tq = min(tq, sq)
    ncols = (DIFF_HEADS * 2 + GQA_HEADS) * tq
    in_specs = [_resident((1, HEAD_DIM))] * 4 + [
        _resident((V_ROWS, 1)),
        pl.BlockSpec((1, tq, d), lambda i, q, j: (i, q, 0)),
        pl.BlockSpec((1, 6, d), lambda i, q, j: (mod_row(i), 0, 0)),
        _resident((1024, d)),
        pl.BlockSpec((1, 1024, tq), lambda i, q, j: (i, 0, q)),
        pl.BlockSpec((1, c, K_COLS), lambda i, q, j: (i, 0, 0)),
        pl.BlockSpec((1, DIFF_HEADS, V_ROWS_PADDED, c), lambda i, q, j: (i, 0, 0, 0)),
        pl.BlockSpec((1, GQA_KV_HEADS, GV_ROWS_PADDED, c), lambda i, q, j: (i, 0, 0, 0)),
    ]
    args = list(lam_vecs) + [sg, x, mod_l, w_out, qt, kc, vtac, vtbc]
    nkv = 1
    if has_lat:
        kl, vtal, vtbl = lat_kv
        n = kl.shape[1]
        tk = min(tk, n)
        nkv = n // tk
        in_specs += [pl.BlockSpec((1, tk, K_COLS), lambda i, q, j: (i, j, 0)),
                     pl.BlockSpec((1, DIFF_HEADS, V_ROWS_PADDED, tk), lambda i, q, j: (i, 0, 0, j)),
                     pl.BlockSpec((1, GQA_KV_HEADS, GV_ROWS_PADDED, tk), lambda i, q, j: (i, 0, 0, j))]
        args += [kl, vtal, vtbl]
    scratch = [pltpu.VMEM((2 * HEAD_DIM, ncols), BF16),
               pltpu.VMEM((V_ROWS_PADDED, ncols), F32),
               pltpu.VMEM((1, ncols), F32),
               pltpu.VMEM((V_ROWS_PADDED, ncols), F32),
               pltpu.VMEM((1, ncols), F32)]
    body = functools.partial(_flash_body, tq=tq, lambda_init=lambda_init, has_lat=has_lat)
    return pl.pallas_call(
        body,
        grid=(b, sq // tq, nkv),
        in_specs=in_specs,
        out_specs=pl.BlockSpec((1, tq, d), lambda i, q, j: (i, q, 0)),
        out_shape=jax.ShapeDtypeStruct((b, sq, d), F32),
        scratch_shapes=scratch,
        compiler_params=_cparams("parallel", "parallel", "arbitrary"),
        name="attn_sweep_lat" if has_lat else "attn_sweep_ctx",
    )(*args)


def _ffn_body(x_ref, xp_ref, xn_ref, mod_ref, g_ref, wv_ref, wg_ref, cw_ref, cb_ref, wo_ref,
              fg_ref, o_ref, *, tm, final_norm):
    t = pl.program_id(1)
    nt = pl.num_programs(1)
    m = mod_ref[0]
    x = x_ref[0]
    xe = jnp.concatenate([xp_ref[0], x, xn_ref[0]], axis=0)
    he = _modulate(xe, g_ref[...], m[3:4], m[4:5]).astype(BF16)
    ge = jnp.dot(he, wg_ref[...], preferred_element_type=F32)
    rows = lax.broadcasted_iota(jnp.int32, (tm + 2 * CONV_HALO, 1), 0)
    lo = jnp.where(t > 0, 0, CONV_HALO)
    hi = jnp.where(t < nt - 1, tm + 2 * CONV_HALO, tm + CONV_HALO)
    ge = jnp.where((rows >= lo) & (rows < hi), ge, 0.0)
    val = jnp.dot(he[CONV_HALO:CONV_HALO + tm], wv_ref[...], preferred_element_type=F32)
    cw = cw_ref[...]
    h0 = CONV_HALO
    conv = (cb_ref[...] + cw[0:1] * ge[h0 - 1:h0 - 1 + tm] + cw[1:2] * ge[h0:h0 + tm]
            + cw[2:3] * ge[h0 + 1:h0 + 1 + tm])
    gelu = 0.5 * conv * (1.0 + lax.erf(conv * math.sqrt(0.5)))
    act = (gelu * val).astype(BF16)
    y = jnp.dot(act, wo_ref[...], preferred_element_type=F32)
    out = x + m[5:6] * y
    if final_norm:
        ms = jnp.mean(out * out, axis=-1, keepdims=True)
        out = out * lax.rsqrt(ms + NORM_EPS) * fg_ref[...]
    o_ref[0] = out


def _ffn(x, mod_l, mod_row, g, wv, wg, cw, cb, wo, fg, *, tm, final_norm):
    b, s, d = x.shape
    tm = min(tm, s)
    hb = tm // CONV_HALO
    last = s // CONV_HALO - 1
    body = functools.partial(_ffn_body, tm=tm, final_norm=final_norm)
    return pl.pallas_call(
        body,
        grid=(b, s // tm),
        in_specs=[pl.BlockSpec((1, tm, d), lambda i, t: (i, t, 0)),
                  pl.BlockSpec((1, CONV_HALO, d), lambda i, t: (i, jnp.maximum(t * hb - 1, 0), 0)),
                  pl.BlockSpec((1, CONV_HALO, d), lambda i, t: (i, jnp.minimum((t + 1) * hb, last), 0)),
                  pl.BlockSpec((1, 6, d), lambda i, t: (mod_row(i), 0, 0)),
                  _resident((1, d)),
                  _resident((d, D_FF)),
                  _resident((d, D_FF)),
                  _resident((3, D_FF)),
                  _resident((1, D_FF)),
                  _resident((D_FF, d)),
                  _resident((1, d))],
        out_specs=pl.BlockSpec((1, tm, d), lambda i, t: (i, t, 0)),
        out_shape=jax.ShapeDtypeStruct((b, s, d), F32),
        compiler_params=_cparams("parallel", "parallel"),
        name="conv_glu_ffn",
    )(x, x, x, mod_l, g, wv, wg, cw, cb, wo, fg)


def _ssm_proj_body(x_ref, mod_ref, g_ref, wt_ref, dtb_ref, zt_ref, xbct_ref, dtt_ref):
    m = mod_ref[0]
    h = _modulate(x_ref[0], g_ref[...], m[0:1], m[1:2]).astype(BF16)
    t = lax.dot_general(wt_ref[...], h, NT_DIMS, preferred_element_type=F32)
    zt_ref[0] = t[0:SSM_D_INNER].astype(BF16)
    xbct_ref[0] = t[SSM_D_INNER:SSM_D_INNER + SSM_XBC].astype(BF16)
    dtt_ref[0] = jax.nn.softplus(t[SSM_D_INNER + SSM_XBC:SSM_IN_COLS] + dtb_ref[...])


def _ssm_proj(x, mod_l, mod_row, g, wt, dtb, *, tm):
    b, s, d = x.shape
    tm = min(tm, s)
    return pl.pallas_call(
        _ssm_proj_body,
        grid=(b, s // tm),
        in_specs=[pl.BlockSpec((1, tm, d), lambda i, t: (i, t, 0)),
                  pl.BlockSpec((1, 6, d), lambda i, t: (mod_row(i), 0, 0)),
                  _resident((1, d)),
                  _resident((SSM_IN_COLS, d)),
                  _resident((2 * SSM_HEADS, 1))],
        out_specs=[pl.BlockSpec((1, SSM_D_INNER, tm), lambda i, t: (i, 0, t)),
                   pl.BlockSpec((1, SSM_XBC, tm), lambda i, t: (i, 0, t)),
                   pl.BlockSpec((1, 2 * SSM_HEADS, tm), lambda i, t: (i, 0, t))],
        out_shape=[jax.ShapeDtypeStruct((b, SSM_D_INNER, s), BF16),
                   jax.ShapeDtypeStruct((b, SSM_XBC, s), BF16),
                   jax.ShapeDtypeStruct((b, 2 * SSM_HEADS, s), F32)],
        compiler_params=_cparams("parallel", "parallel"),
        name="ssm_in_proj",
    )(x, mod_l, g, wt, dtb)


def _ssm_conv_body(u_ref, up_ref, un_ref, w_ref, b_ref, xst_ref, ct_ref, bm_ref, *, tc):
    t = pl.program_id(1)
    nt = pl.num_programs(1)
    u = u_ref[0].astype(F32)
    prev = jnp.where(t > 0, up_ref[0][:, V7X_LANES - 1:V7X_LANES].astype(F32), 0.0)
    nxt = jnp.where(t < nt - 1, un_ref[0][:, 0:1].astype(F32), 0.0)
    lane = lax.broadcasted_iota(jnp.int32, (1, tc), 1)
    packed = pltpu.bitcast(u_ref[0], jnp.uint32)
    rolled_l = pltpu.bitcast(pltpu.roll(packed, 1, 1), BF16).astype(F32)
    rolled_r = pltpu.bitcast(pltpu.roll(packed, tc - 1, 1), BF16).astype(F32)
    left = jnp.where(lane == 0, prev, rolled_l)
    right = jnp.where(lane == tc - 1, nxt, rolled_r)
    def lanes(a):
        return jnp.concatenate([a] * (tc // V7X_LANES), axis=1)

    v = _silu(lanes(b_ref[...]) + lanes(w_ref[0]) * left + lanes(w_ref[1]) * u
              + lanes(w_ref[2]) * right)
    bn = SSM_GROUPS * SSM_STATE
    xst_ref[0] = v[0:SSM_D_INNER].astype(BF16)
    bm_ref[0] = v[SSM_D_INNER:SSM_D_INNER + bn].T.astype(BF16)
    ct_ref[0] = v[SSM_D_INNER + bn:SSM_XBC].astype(BF16)


def _ssm_conv(xbct, w, bias, *, tc):
    b, ch, s = xbct.shape
    tc = min(tc, s)
    hb = tc // V7X_LANES
    last = s // V7X_LANES - 1
    bn = SSM_GROUPS * SSM_STATE
    body = functools.partial(_ssm_conv_body, tc=tc)
    return pl.pallas_call(
        body,
        grid=(b, s // tc),
        in_specs=[pl.BlockSpec((1, ch, tc), lambda i, t: (i, 0, t)),
                  pl.BlockSpec((1, ch, V7X_LANES), lambda i, t: (i, 0, jnp.maximum(t * hb - 1, 0))),
                  pl.BlockSpec((1, ch, V7X_LANES), lambda i, t: (i, 0, jnp.minimum((t + 1) * hb, last))),
                  _resident((3, ch, V7X_LANES)),
                  _resident((ch, V7X_LANES))],
        out_specs=[pl.BlockSpec((1, SSM_D_INNER, tc), lambda i, t: (i, 0, t)),
                   pl.BlockSpec((1, bn, tc), lambda i, t: (i, 0, t)),
                   pl.BlockSpec((1, tc, bn), lambda i, t: (i, t, 0))],
        out_shape=[jax.ShapeDtypeStruct((b, SSM_D_INNER, s), BF16),
                   jax.ShapeDtypeStruct((b, bn, s), BF16),
                   jax.ShapeDtypeStruct((b, s, bn), BF16)],
        compiler_params=_cparams("parallel", "parallel"),
        name="ssm_conv_silu",
    )(xbct, xbct, xbct, w, bias)


def _dot_f32_by_01(a, m01):
    hi = a.astype(BF16)
    r1 = a - hi.astype(F32)
    mid = r1.astype(BF16)
    lo = (r1 - mid.astype(F32)).astype(BF16)
    return (jnp.dot(hi, m01, preferred_element_type=F32)
            + jnp.dot(mid, m01, preferred_element_type=F32)
            + jnp.dot(lo, m01, preferred_element_type=F32))


def _scan_order_mask(d, chunk):
    jrow = lax.broadcasted_iota(jnp.int32, (chunk, chunk), 0)
    icol = lax.broadcasted_iota(jnp.int32, (chunk, chunk), 1)
    sign = jnp.where(d == 0, 1, -1)
    return (icol - jrow) * sign >= 0


def _ssd_decay_body(a_ref, dt_ref, cum2_ref, src_ref, toend_ref, ecum_ref, etot_ref, *, chunk):
    d = pl.program_id(1)
    mask01 = _scan_order_mask(d, chunk).astype(BF16)
    for s in range(dt_ref.shape[2] // chunk):
        tok = slice(s * chunk, (s + 1) * chunk)
        dt = dt_ref[0, :, tok]
        cum_t = _dot_f32_by_01(dt * a_ref[0], mask01)
        cum2_t = cum_t * LOG2E
        cum2_ref[0, 0, :, tok] = cum2_t
        src_ref[0, 0, tok, :] = (cum2_t - jnp.log2(dt)).T
        tot = jnp.where(d == 0, cum_t[:, chunk - 1:chunk], cum_t[:, 0:1])
        toend_ref[0, 0, :, tok] = jnp.exp(tot - cum_t) * dt
        ecum_ref[0, 0, :, tok] = jnp.exp(cum_t)
        etot_ref[0, 0, :, tok] = jnp.broadcast_to(jnp.exp(tot), cum_t.shape)


def _ssd_decay(a, dtt, chunk):
    b, _, s = dtt.shape
    span = min(s, 8 * chunk)
    body = functools.partial(_ssd_decay_body, chunk=chunk)
    by_head = pl.BlockSpec((1, 1, SSM_HEADS, span), lambda i, d, g: (i, d, 0, g))
    return pl.pallas_call(
        body,
        grid=(b, 2, s // span),
        in_specs=[pl.BlockSpec((1, SSM_HEADS, 1), lambda i, d, g: (d, 0, 0)),
                  pl.BlockSpec((1, SSM_HEADS, span), lambda i, d, g: (i, d, g))],
        out_specs=[by_head,
                   pl.BlockSpec((1, 1, span, SSM_HEADS), lambda i, d, g: (i, d, g, 0)),
                   by_head, by_head, by_head],
        out_shape=[jax.ShapeDtypeStruct((b, 2, SSM_HEADS, s), F32),
                   jax.ShapeDtypeStruct((b, 2, s, SSM_HEADS), F32)]
        + [jax.ShapeDtypeStruct((b, 2, SSM_HEADS, s), F32)] * 3,
        compiler_params=_cparams("parallel", "parallel", "parallel"),
        name="ssd_decay_factors",
    )(a, dtt)


def _scan_body(cum2_ref, src_ref, toend_ref, ecum_ref, etot_ref, xs_ref, ct_ref, b_ref, s0_ref,
               y_ref, sout_ref, state, *, chunk):
    d = pl.program_id(1)
    c = pl.program_id(2)
    nc = pl.num_programs(2)
    hp = SSM_HEADS_PER_GROUP
    hd = SSM_D_INNER // SSM_HEADS
    gw = hp * hd

    @pl.when(c == 0)
    def _load_state():
        state[...] = s0_ref[0, 0]

    not_yet = jnp.where(_scan_order_mask(d, chunk), 0.0, -jnp.inf)
    cum2_t = cum2_ref[0, 0]
    src_term = src_ref[0, 0]
    to_end = toend_ref[0, 0]
    ecum = ecum_ref[0, 0]
    etot = etot_ref[0, 0][:, 0:1]

    for g in range(SSM_GROUPS):
        bg = b_ref[0, :, g * SSM_STATE:(g + 1) * SSM_STATE]
        ctg = ct_ref[0, g * SSM_STATE:(g + 1) * SSM_STATE, :]
        cbt = jnp.dot(bg, ctg, preferred_element_type=F32)
        xg = xs_ref[0, g * gw:(g + 1) * gw, :]
        sg = state[g * gw:(g + 1) * gw, :]
        hs = slice(g * hp, (g + 1) * hp)
        y_state = (jnp.dot(sg.astype(BF16), ctg, preferred_element_type=F32)
                   .reshape(hp, hd, chunk) * ecum[hs][:, None, :])
        outs = []
        for r in range(hp):
            h = g * hp + r
            seg = cum2_t[h:h + 1, :] - src_term[:, h:h + 1]
            w = (cbt * jnp.exp2(seg + not_yet)).astype(BF16)
            outs.append(jnp.dot(xg[r * hd:(r + 1) * hd], w, preferred_element_type=F32)
                        + y_state[r])
        y_ref[0, 0, g * gw:(g + 1) * gw, :] = jnp.concatenate(outs, axis=0).astype(BF16)
        xw = ((xg.astype(F32).reshape(hp, hd, chunk) * to_end[hs][:, None, :])
              .reshape(gw, chunk).astype(BF16))
        upd = jnp.dot(xw, bg, preferred_element_type=F32)
        decayed = (sg.reshape(hp, hd, SSM_STATE) * etot[hs][:, None, :]).reshape(gw, SSM_STATE)
        state[g * gw:(g + 1) * gw, :] = decayed + upd

    @pl.when(c == nc - 1)
    def _store_state():
        sout_ref[0, 0] = state[...]


def _ssd_scan(a, dtt, xst, ct, bm, s0):
    b, ch, s = xst.shape
    chunk = min(SSD_CHUNK, s)
    nc = s // chunk
    bn = SSM_GROUPS * SSM_STATE
    cum2, src, toend, ecum, etot = _ssd_decay(a, dtt, chunk)

    def cidx(d, c):
        return c + d * (nc - 1 - 2 * c)

    by_head = pl.BlockSpec((1, 1, SSM_HEADS, chunk), lambda i, d, c: (i, d, 0, cidx(d, c)))
    body = functools.partial(_scan_body, chunk=chunk)
    return pl.pallas_call(
        body,
        grid=(b, 2, nc),
        in_specs=[by_head,
                  pl.BlockSpec((1, 1, chunk, SSM_HEADS), lambda i, d, c: (i, d, cidx(d, c), 0)),
                  by_head, by_head, by_head,
                  pl.BlockSpec((1, ch, chunk), lambda i, d, c: (i, 0, cidx(d, c))),
                  pl.BlockSpec((1, bn, chunk), lambda i, d, c: (i, 0, cidx(d, c))),
                  pl.BlockSpec((1, chunk, bn), lambda i, d, c: (i, cidx(d, c), 0)),
                  pl.BlockSpec((1, 1, ch, SSM_STATE), lambda i, d, c: (d, i, 0, 0))],
        out_specs=[pl.BlockSpec((1, 1, ch, chunk), lambda i, d, c: (d, i, 0, cidx(d, c))),
                   pl.BlockSpec((1, 1, ch, SSM_STATE), lambda i, d, c: (d, i, 0, 0))],
        out_shape=[jax.ShapeDtypeStruct((2, b, ch, s), BF16),
                   jax.ShapeDtypeStruct((2, b, ch, SSM_STATE), F32)],
        scratch_shapes=[pltpu.VMEM((ch, SSM_STATE), F32)],
        compiler_params=_cparams("parallel", "parallel", "arbitrary"),
        name="ssd_scan",
    )(cum2, src, toend, ecum, etot, xst, ct, bm, s0)


def _ssm_out_body(x_ref, mod_ref, yf_ref, yb_ref, xs_ref, z_ref, dsk_ref, ng_ref, wt_ref, o_ref):
    y = (yf_ref[0, 0].astype(F32) + yb_ref[0, 0].astype(F32)
         + dsk_ref[...] * xs_ref[0].astype(F32))
    y = y * _silu(z_ref[0].astype(F32))
    ms = jnp.mean(y * y, axis=0, keepdims=True)
    yn = (y * lax.rsqrt(ms + NORM_EPS) * ng_ref[...]).astype(BF16)
    ot = jnp.dot(wt_ref[...], yn, preferred_element_type=F32)
    o_ref[0] = x_ref[0] + mod_ref[0][2:3] * ot.T


def _ssm_out(x, mod_l, mod_row, y, xst, zt, dsk, ng, wt, *, tm):
    b, s, d = x.shape
    tm = min(tm, s)
    ch = xst.shape[1]
    return pl.pallas_call(
        _ssm_out_body,
        grid=(b, s // tm),
        in_specs=[pl.BlockSpec((1, tm, d), lambda i, t: (i, t, 0)),
                  pl.BlockSpec((1, 6, d), lambda i, t: (mod_row(i), 0, 0)),
                  pl.BlockSpec((1, 1, ch, tm), lambda i, t: (0, i, 0, t)),
                  pl.BlockSpec((1, 1, ch, tm), lambda i, t: (1, i, 0, t)),
                  pl.BlockSpec((1, ch, tm), lambda i, t: (i, 0, t)),
                  pl.BlockSpec((1, ch, tm), lambda i, t: (i, 0, t)),
                  _resident((ch, 1)),
                  _resident((ch, 1)),
                  _resident((d, ch))],
        out_specs=pl.BlockSpec((1, tm, d), lambda i, t: (i, t, 0)),
        out_shape=jax.ShapeDtypeStruct((b, s, d), F32),
        compiler_params=_cparams("parallel", "parallel"),
        name="ssm_out_residual",
    )(x, mod_l, y, y, xst, zt, dsk, ng, wt)


def _rope_tables_t(n):
    t = jnp.arange(n)
    inv_freq = 1.0 / (ROPE_BASE ** (jnp.arange(ROPE_PAIRS, dtype=F32) / ROPE_PAIRS))
    ang_r = (t // GRID_W).astype(F32)[None, :] * inv_freq[:, None]
    ang_c = (t % GRID_W).astype(F32)[None, :] * inv_freq[:, None]
    cr, sr, cc, sc = jnp.cos(ang_r), jnp.sin(ang_r), jnp.cos(ang_c), jnp.sin(ang_c)
    return (jnp.concatenate([cr, cr, cc, cc], axis=0),
            jnp.concatenate([-sr, sr, -sc, sc], axis=0))


def _attn_layer(x, ctx, mod_l, lat_row, ctx_row, p, rope, lambda_init, with_ctx):
    wt = p["w_in"].T.astype(BF16)
    g = p["norm_g"].reshape(1, D_MODEL)
    qg = p["q_norm_g"].reshape(HEAD_DIM, 1)
    kg = p["k_norm_g"].reshape(HEAD_DIM, 1)
    cos_t, sin_t = rope
    c = ctx.shape[1]
    qt, *lat_kv = _attn_in(x, mod_l, lat_row, g, wt, cos_t, sin_t, qg, kg, rope=True, tm=512)
    qtc, *ctx_kv = _attn_in(ctx, mod_l, ctx_row, g, wt, cos_t[:, :c], sin_t[:, :c], qg, kg,
                            rope=False, tm=256)
    lam_vecs = [p[k].reshape(1, HEAD_DIM) for k in ("lq1", "lk1", "lq2", "lk2")]
    sg = p["subln_g"].reshape(V_ROWS, 1)
    w_out = p["w_out"].astype(BF16)
    x = _flash(x, mod_l, lat_row, w_out, lam_vecs, sg, qt, ctx_kv, lat_kv,
               lambda_init=lambda_init, tq=512, tk=2048)
    if with_ctx:
        ctx = _flash(ctx, mod_l, ctx_row, w_out, lam_vecs, sg, qtc, ctx_kv, None,
                     lambda_init=lambda_init, tq=256, tk=0)
    return x, ctx


def _ssm_layer(x, ctx, mod_l, lat_row, ctx_row, p, with_ctx):
    wt = p["w_in"].T.astype(BF16)
    g = p["norm_g"].reshape(1, D_MODEL)
    dtb = p["dt_bias"].reshape(2 * SSM_HEADS, 1)
    conv_w = jnp.broadcast_to(p["conv_w"][:, :, None], (3, SSM_XBC, V7X_LANES))
    conv_b = jnp.broadcast_to(p["conv_b"][:, None], (SSM_XBC, V7X_LANES))
    a = (-jnp.exp(p["a_log"].astype(F32))).reshape(2, SSM_HEADS, 1)
    b = x.shape[0]

    def pre(v, row, tm):
        zt, xbct, dtt = _ssm_proj(v, mod_l, row, g, wt, dtb, tm=tm)
        xst, ct, bm = _ssm_conv(xbct, conv_w, conv_b, tc=512)
        return zt, xst, ct, bm, dtt

    zt_c, xst_c, ct_c, bm_c, dtt_c = pre(ctx, ctx_row, 256)
    zt_l, xst_l, ct_l, bm_l, dtt_l = pre(x, lat_row, 512)
    zero = jnp.zeros((2, b, SSM_D_INNER, SSM_STATE), F32)
    y_c, s_ctx = _ssd_scan(a, dtt_c, xst_c, ct_c, bm_c, zero)
    y_l, _ = _ssd_scan(a, dtt_l, xst_l, ct_l, bm_l, s_ctx)
    dsk = jnp.repeat(p["d_skip"], SSM_D_INNER // SSM_HEADS).reshape(SSM_D_INNER, 1)
    ng = p["out_norm_g"].reshape(SSM_D_INNER, 1)
    w_out_t = p["w_out"].T.astype(BF16)
    x = _ssm_out(x, mod_l, lat_row, y_l, xst_l, zt_l, dsk, ng, w_out_t, tm=512)
    if with_ctx:
        ctx = _ssm_out(ctx, mod_l, ctx_row, y_c, xst_c, zt_c, dsk, ng, w_out_t, tm=256)
    return x, ctx


def kernel(x, c, ctx, c_ctx, mod_w, mod_b, norm_mix_g, norm_ffn_g, attn_w_in, attn_w_out,
           diff_lq1, diff_lk1, diff_lq2, diff_lk2, diff_subln_g, gqa_q_norm_g, gqa_k_norm_g,
           ssm_w_in, ssm_conv_w, ssm_conv_b, ssm_dt_bias, ssm_a_log, ssm_d, ssm_norm_g, ssm_w_out,
           ffn_w_in, ffn_conv_w, ffn_conv_b, ffn_w_out, final_norm_g):
    b, n, d = x.shape
    mod_rows = 16
    c_rows = jnp.zeros((mod_rows, d), F32).at[:b].set(c).at[b].set(c_ctx)
    mod = _mod_all(c_rows, mod_w, mod_b).reshape(DEPTH, mod_rows, 6, d)
    lat_row = lambda i: i
    ctx_row = lambda i: b
    rope = _rope_tables_t(n)

    for layer in range(DEPTH):
        with_ctx = layer < DEPTH - 1
        mod_l = mod[layer]
        i = layer // 2
        if layer % 2 == 0:
            p = dict(w_in=attn_w_in[i], w_out=attn_w_out[i], norm_g=norm_mix_g[layer],
                     lq1=diff_lq1[i], lk1=diff_lk1[i], lq2=diff_lq2[i], lk2=diff_lk2[i],
                     subln_g=diff_subln_g[i], q_norm_g=gqa_q_norm_g[i], k_norm_g=gqa_k_norm_g[i])
            lambda_init = 0.8 - 0.6 * math.exp(-0.3 * layer)
            x, ctx = _attn_layer(x, ctx, mod_l, lat_row, ctx_row, p, rope, lambda_init, with_ctx)
        else:
            p = dict(w_in=ssm_w_in[i], norm_g=norm_mix_g[layer], conv_w=ssm_conv_w[i],
                     conv_b=ssm_conv_b[i], dt_bias=ssm_dt_bias[i], a_log=ssm_a_log[i],
                     d_skip=ssm_d[i], out_norm_g=ssm_norm_g[i], w_out=ssm_w_out[i])
            x, ctx = _ssm_layer(x, ctx, mod_l, lat_row, ctx_row, p, with_ctx)
        g = norm_ffn_g[layer].reshape(1, d)
        wv = ffn_w_in[layer][:, :D_FF].astype(BF16)
        wg = ffn_w_in[layer][:, D_FF:].astype(BF16)
        cw = ffn_conv_w[layer]
        cb = ffn_conv_b[layer].reshape(1, D_FF)
        wo = ffn_w_out[layer].astype(BF16)
        fg = final_norm_g.reshape(1, d)
        x = _ffn(x, mod_l, lat_row, g, wv, wg, cw, cb, wo, fg, tm=256,
                 final_norm=layer == DEPTH - 1)
        if with_ctx:
            ctx = _ffn(ctx, mod_l, ctx_row, g, wv, wg, cw, cb, wo, fg, tm=256, final_norm=False)
    return x
```

```python
import functools
import math

import jax
import jax.numpy as jnp
from jax import lax
from jax.experimental import pallas as pl
from jax.experimental.pallas import tpu as pltpu

F32 = jnp.float32
BF16 = jnp.bfloat16

D_MODEL = 1024
DEPTH = 4
GRID_W = 64
HEAD_DIM = 64
ROPE_PAIRS = HEAD_DIM // 4
ROPE_BASE = 10000.0
NORM_EPS = 1e-6
DIFF_HEADS = 4
GQA_HEADS = 8
GQA_KV_HEADS = 2
GQA_GROUP = GQA_HEADS // GQA_KV_HEADS
ATTN_IN_COLS = 2304
SSM_D_INNER = 2048
SSM_HEADS = 32
SSM_GROUPS = 4
SSM_HEADS_PER_GROUP = SSM_HEADS // SSM_GROUPS
SSM_STATE = 128
SSM_XBC = 3072
SSM_IN_COLS = 5184
D_FF = 2816
LOG2E = math.log2(math.e)

V7X_VMEM_BYTES = 64 * 1024 * 1024
VMEM_LIMIT_BYTES = V7X_VMEM_BYTES - 8 * 1024 * 1024
V7X_LANES = 128
BF16_SUBLANES = 16

V_ROWS = 2 * HEAD_DIM
V_ROWS_PADDED = V_ROWS + BF16_SUBLANES
GV_ROWS_PADDED = HEAD_DIM + BF16_SUBLANES
K_COLS = (DIFF_HEADS + 1) * 2 * HEAD_DIM

ATTN_KEY_SUBBLOCK = 256
ATTN_STREAM_MAX_OCTAVES = 64.0

CONV_HALO = 16
SSD_CHUNK = 256

NT_DIMS = (((1,), (1,)), ((), ()))


def _cparams(*sem):
    return pltpu.CompilerParams(dimension_semantics=sem, vmem_limit_bytes=VMEM_LIMIT_BYTES)


def _resident(shape):
    nd = len(shape)
    return pl.BlockSpec(shape, lambda *_: (0,) * nd, pipeline_mode=pl.Buffered(1))


def _silu(v):
    return v * jax.nn.sigmoid(v)


def _modulate(x, g, shift, scale):
    ms = jnp.mean(x * x, axis=-1, keepdims=True)
    return (x * lax.rsqrt(ms + NORM_EPS) * g) * (1.0 + scale) + shift


def _mod_body(c_ref, w_ref, b_ref, o_ref):
    s = _silu(c_ref[...])
    o_ref[0] = jnp.dot(s, w_ref[0], preferred_element_type=F32,
                       precision=lax.Precision.HIGHEST) + b_ref[0]


def _mod_all(c_rows, mod_w, mod_b):
    rows = c_rows.shape[0]
    depth, d, cols = mod_w.shape
    tn = 2048
    return pl.pallas_call(
        _mod_body,
        grid=(depth, cols // tn),
        in_specs=[pl.BlockSpec((rows, d), lambda l, n: (0, 0)),
                  pl.BlockSpec((1, d, tn), lambda l, n: (l, 0, n)),
                  pl.BlockSpec((1, 1, tn), lambda l, n: (l, 0, n))],
        out_specs=pl.BlockSpec((1, rows, tn), lambda l, n: (l, 0, n)),
        out_shape=jax.ShapeDtypeStruct((depth, rows, cols), F32),
        compiler_params=_cparams("parallel", "parallel"),
        name="mod_vectors",
    )(c_rows, mod_w, mod_b.reshape(depth, 1, cols))


def _attn_in_body(x_ref, mod_ref, g_ref, wt_ref, cos_ref, sin_ref, qg_ref, kg_ref,
                  qt_ref, k_ref, vta_ref, vtb_ref, *, rope):
    m = mod_ref[0]
    h = _modulate(x_ref[0], g_ref[...], m[0:1], m[1:2]).astype(BF16)
    t = lax.dot_general(wt_ref[...], h, NT_DIMS, preferred_element_type=F32)
    tm = t.shape[1]

    def rot(u):
        if not rope:
            return u
        sw = jnp.concatenate([u[:, 16:32], u[:, 0:16], u[:, 48:64], u[:, 32:48]], axis=1)
        return u * cos_ref[...][None] + sw * sin_ref[...][None]

    def qk_norm(u, g):
        ms = jnp.mean(u * u, axis=1, keepdims=True)
        return u * lax.rsqrt(ms + NORM_EPS) * g[None]

    nq = 2 * DIFF_HEADS
    qa = rot(t[0:512].reshape(nq, HEAD_DIM, tm))
    ka = rot(t[512:1024].reshape(nq, HEAD_DIM, tm))
    va = t[1024:1536]
    qb = rot(qk_norm(t[1536:2048].reshape(GQA_HEADS, HEAD_DIM, tm), qg_ref[...]))
    kb = rot(qk_norm(t[2048:2176].reshape(GQA_KV_HEADS, HEAD_DIM, tm), kg_ref[...]))
    vb = t[2176:2304]

    qs = (HEAD_DIM ** -0.5) * LOG2E
    qt_ref[0, 0:512] = (qa * qs).reshape(512, tm).astype(BF16)
    qt_ref[0, 512:1024] = (qb * qs).reshape(512, tm).astype(BF16)
    kt = jnp.concatenate([ka.reshape(512, tm), kb.reshape(128, tm)], axis=0)
    k_ref[0] = kt.T.astype(BF16)
    ones = jnp.ones((BF16_SUBLANES, tm), BF16)
    for u in range(DIFF_HEADS):
        vta_ref[0, u, 0:V_ROWS] = va[u * V_ROWS:(u + 1) * V_ROWS].astype(BF16)
        vta_ref[0, u, V_ROWS:V_ROWS_PADDED] = ones
    for g in range(GQA_KV_HEADS):
        vtb_ref[0, g, 0:HEAD_DIM] = vb[g * HEAD_DIM:(g + 1) * HEAD_DIM].astype(BF16)
        vtb_ref[0, g, HEAD_DIM:GV_ROWS_PADDED] = ones


def _attn_in(x, mod_l, mod_row, g, wt, cos_t, sin_t, qg, kg, *, rope, tm):
    b, s, d = x.shape
    tm = min(tm, s)
    body = functools.partial(_attn_in_body, rope=rope)
    return pl.pallas_call(
        body,
        grid=(b, s // tm),
        in_specs=[pl.BlockSpec((1, tm, d), lambda i, t: (i, t, 0)),
                  pl.BlockSpec((1, 6, d), lambda i, t: (mod_row(i), 0, 0)),
                  _resident((1, d)),
                  _resident((ATTN_IN_COLS, d)),
                  pl.BlockSpec((HEAD_DIM, tm), lambda i, t: (0, t)),
                  pl.BlockSpec((HEAD_DIM, tm), lambda i, t: (0, t)),
                  _resident((HEAD_DIM, 1)),
                  _resident((HEAD_DIM, 1))],
        out_specs=[pl.BlockSpec((1, 1024, tm), lambda i, t: (i, 0, t)),
                   pl.BlockSpec((1, tm, K_COLS), lambda i, t: (i, t, 0)),
                   pl.BlockSpec((1, DIFF_HEADS, V_ROWS_PADDED, tm), lambda i, t: (i, 0, 0, t)),
                   pl.BlockSpec((1, GQA_KV_HEADS, GV_ROWS_PADDED, tm), lambda i, t: (i, 0, 0, t))],
        out_shape=[jax.ShapeDtypeStruct((b, 1024, s), BF16),
                   jax.ShapeDtypeStruct((b, s, K_COLS), BF16),
                   jax.ShapeDtypeStruct((b, DIFF_HEADS, V_ROWS_PADDED, s), BF16),
                   jax.ShapeDtypeStruct((b, GQA_KV_HEADS, GV_ROWS_PADDED, s), BF16)],
        compiler_params=_cparams("parallel", "parallel"),
        name="attn_in_rope" if rope else "attn_in_ctx",
    )(x, mod_l, g, wt, cos_t, sin_t, qg, kg)


def _attn_unit_cols(tq):
    units = []
    for h in range(DIFF_HEADS):
        units.append((h * 128, h, V_ROWS_PADDED, h * 2 * tq, 2 * tq))
    base = DIFF_HEADS * 2 * tq
    for g in range(GQA_KV_HEADS):
        units.append((DIFF_HEADS * 128, g, GV_ROWS_PADDED, base + g * GQA_GROUP * tq,
                      GQA_GROUP * tq))
    return units


def _flash_body(*refs, tq, lambda_init, has_lat):
    if has_lat:
        (lq1, lk1, lq2, lk2, sg_ref, x_ref, mod_ref, wo_ref, qt_ref, kc_ref, vtac_ref, vtbc_ref,
         kl_ref, vtal_ref, vtbl_ref, o_ref, rhs, acc, mrow, pv_new, m_chunk) = refs
    else:
        (lq1, lk1, lq2, lk2, sg_ref, x_ref, mod_ref, wo_ref, qt_ref, kc_ref, vtac_ref, vtbc_ref,
         o_ref, rhs, acc, mrow, pv_new, m_chunk) = refs
    j = pl.program_id(2)
    nj = pl.num_programs(2)
    units = _attn_unit_cols(tq)
    diff_cols = DIFF_HEADS * 2 * tq

    def values(vta_ref, vtb_ref, u, vu):
        return vta_ref[0, vu] if u < DIFF_HEADS else vtb_ref[0, vu]

    def exact_step(k_ref, vta_ref, vtb_ref):
        for u, (kc0, vu, vr, c0, w) in enumerate(units):
            s = jnp.dot(k_ref[0, :, kc0:kc0 + 128], rhs[:, c0:c0 + w],
                        preferred_element_type=F32)
            mp = mrow[:, c0:c0 + w]
            mn = jnp.maximum(mp, jnp.max(s, axis=0, keepdims=True))
            alpha = jnp.exp2(mp - mn)
            p = jnp.exp2(s - mn).astype(BF16)
            pv = jnp.dot(values(vta_ref, vtb_ref, u, vu), p, preferred_element_type=F32)
            acc[0:vr, c0:c0 + w] = acc[0:vr, c0:c0 + w] * alpha + pv
            mrow[:, c0:c0 + w] = mn

    def streaming_step(k_ref, vta_ref, vtb_ref):
        nk = k_ref.shape[1]
        sub = min(nk, ATTN_KEY_SUBBLOCK)
        for u, (kc0, vu, vr, c0, w) in enumerate(units):
            m_used = mrow[:, c0:c0 + w]
            cm = None
            parts = []
            for r in range(nk // sub):
                s = jnp.dot(k_ref[0, r * sub:(r + 1) * sub, kc0:kc0 + 128], rhs[:, c0:c0 + w],
                            preferred_element_type=F32)
                parts.append(jnp.exp2(s - m_used).astype(BF16))
                sm = jnp.max(s.reshape(sub // 8, 8, w), axis=0)
                cm = sm if cm is None else jnp.maximum(cm, sm)
            p = jnp.concatenate(parts, axis=0)
            pv_new[0:vr, c0:c0 + w] = jnp.dot(values(vta_ref, vtb_ref, u, vu), p,
                                              preferred_element_type=F32)
            m_chunk[:, c0:c0 + w] = jnp.max(cm, axis=0, keepdims=True)

    @pl.when(j == 0)
    def _init():
        zeros = jnp.zeros((HEAD_DIM, tq), BF16)
        for h in range(DIFF_HEADS):
            c0 = h * 2 * tq
            rhs[0:64, c0:c0 + tq] = qt_ref[0, h * 128:h * 128 + 64, :]
            rhs[64:128, c0:c0 + tq] = zeros
            rhs[0:64, c0 + tq:c0 + 2 * tq] = zeros
            rhs[64:128, c0 + tq:c0 + 2 * tq] = qt_ref[0, h * 128 + 64:h * 128 + 128, :]
        base = DIFF_HEADS * 2 * tq
        for g in range(GQA_KV_HEADS):
            for r in range(GQA_GROUP):
                c0 = base + (g * GQA_GROUP + r) * tq
                hd = 512 + (g * GQA_GROUP + r) * HEAD_DIM
                rhs[g * 64:(g + 1) * 64, c0:c0 + tq] = qt_ref[0, hd:hd + HEAD_DIM, :]
                rhs[(1 - g) * 64:(2 - g) * 64, c0:c0 + tq] = zeros
        acc[...] = jnp.zeros(acc.shape, F32)
        mrow[...] = jnp.zeros(mrow.shape, F32)

    row_blocks = ((V_ROWS_PADDED, slice(0, diff_cols)),
                  (GV_ROWS_PADDED, slice(diff_cols, acc.shape[1])))

    def nothing_pending():
        for rows, cols in row_blocks:
            pv_new[0:rows, cols] = jnp.zeros((rows, cols.stop - cols.start), F32)
        m_chunk[...] = jnp.full(m_chunk.shape, -jnp.inf, F32)

    def commit():
        mp = mrow[...]
        mn = jnp.maximum(mp, m_chunk[...])
        mrow[...] = mn
        alpha = jnp.exp2(mp - mn)
        for rows, cols in row_blocks:
            acc[0:rows, cols] = (acc[0:rows, cols] + pv_new[0:rows, cols]) * alpha[:, cols]

    def stream_chunk(k_ref, vta_ref, vtb_ref, first):
        streaming_step(k_ref, vta_ref, vtb_ref)
        dev = m_chunk[...] - mrow[...]
        in_range = jnp.max(jnp.abs(dev) if first else dev) <= ATTN_STREAM_MAX_OCTAVES

        @pl.when(jnp.logical_not(in_range))
        def _redo():
            if first:
                mrow[...] = jnp.full(mrow.shape, -jnp.inf, F32)
            exact_step(k_ref, vta_ref, vtb_ref)
            nothing_pending()

    @pl.when(j == 0)
    def _context_keys():
        stream_chunk(kc_ref, vtac_ref, vtbc_ref, True)

    if has_lat:
        commit()
        stream_chunk(kl_ref, vtal_ref, vtbl_ref, False)

    @pl.when(j == nj - 1)
    def _finish():
        commit()
        lam = (jnp.exp(jnp.sum(lq1[...] * lk1[...], keepdims=True))
               - jnp.exp(jnp.sum(lq2[...] * lk2[...], keepdims=True)) + lambda_init)
        pieces = []
        for h in range(DIFF_HEADS):
            c0 = h * 2 * tq
            o1 = acc[0:V_ROWS, c0:c0 + tq] / acc[V_ROWS:V_ROWS + 1, c0:c0 + tq]
            o2 = acc[0:V_ROWS, c0 + tq:c0 + 2 * tq] / acc[V_ROWS:V_ROWS + 1, c0 + tq:c0 + 2 * tq]
            oh = o1 - lam * o2
            ms = jnp.mean(oh * oh, axis=0, keepdims=True)
            pieces.append(oh * lax.rsqrt(ms + NORM_EPS) * sg_ref[...] * (1.0 - lambda_init))
        base = DIFF_HEADS * 2 * tq
        for g in range(GQA_KV_HEADS):
            for r in range(GQA_GROUP):
                c0 = base + (g * GQA_GROUP + r) * tq
                pieces.append(acc[0:HEAD_DIM, c0:c0 + tq] / acc[HEAD_DIM:HEAD_DIM + 1, c0:c0 + tq])
        o = jnp.concatenate(pieces, axis=0).T.astype(BF16)
        y = jnp.dot(o, wo_ref[...], preferred_element_type=F32)
        o_ref[0] = x_ref[0] + mod_ref[0][2:3] * y


def _flash(x, mod_l, mod_row, w_out, lam_vecs, sg, qt, ctx_kv, lat_kv, *, lambda_init, tq, tk):
    b, _, sq = qt.shape
    d = x.shape[2]
    kc, vtac, vtbc = ctx_kv
    c = kc.shape[1]
    has_lat = lat_kv is not None
    tq = min(tq, sq)
    ncols = (DIFF_HEADS * 2 + GQA_HEADS) * tq
    in_specs = [_resident((1, HEAD_DIM))] * 4 + [
        _resident((V_ROWS, 1)),
        pl.BlockSpec((1, tq, d), lambda i, q, j: (i, q, 0)),
        pl.BlockSpec((1, 6, d), lambda i, q, j: (mod_row(i), 0, 0)),
        _resident((1024, d)),
        pl.BlockSpec((1, 1024, tq), lambda i, q, j: (i, 0, q)),
        pl.BlockSpec((1, c, K_COLS), lambda i, q, j: (i, 0, 0)),
        pl.BlockSpec((1, DIFF_HEADS, V_ROWS_PADDED, c), lambda i, q, j: (i, 0, 0, 0)),
        pl.BlockSpec((1, GQA_KV_HEADS, GV_ROWS_PADDED, c), lambda i, q, j: (i, 0, 0, 0)),
    ]
    args = list(lam_vecs) + [sg, x, mod_l, w_out, qt, kc, vtac, vtbc]
    nkv = 1
    if has_lat:
        kl, vtal, vtbl = lat_kv
        n = kl.shape[1]
        tk = min(tk, n)
        nkv = n // tk
        in_specs += [pl.BlockSpec((1, tk, K_COLS), lambda i, q, j: (i, j, 0)),
                     pl.BlockSpec((1, DIFF_HEADS, V_ROWS_PADDED, tk), lambda i, q, j: (i, 0, 0, j)),
                     pl.BlockSpec((1, GQA_KV_HEADS, GV_ROWS_PADDED, tk), lambda i, q, j: (i, 0, 0, j))]
        args += [kl, vtal, vtbl]
    scratch = [pltpu.VMEM((2 * HEAD_DIM, ncols), BF16),
               pltpu.VMEM((V_ROWS_PADDED, ncols), F32),
               pltpu.VMEM((1, ncols), F32),
               pltpu.VMEM((V_ROWS_PADDED, ncols), F32),
               pltpu.VMEM((1, ncols), F32)]
    body = functools.partial(_flash_body, tq=tq, lambda_init=lambda_init, has_lat=has_lat)
    return pl.pallas_call(
        body,
        grid=(b, sq // tq, nkv),
        in_specs=in_specs,
        out_specs=pl.BlockSpec((1, tq, d), lambda i, q, j: (i, q, 0)),
        out_shape=jax.ShapeDtypeStruct((b, sq, d), F32),
        scratch_shapes=scratch,
        compiler_params=_cparams("parallel", "parallel", "arbitrary"),
        name="attn_sweep_lat" if has_lat else "attn_sweep_ctx",
    )(*args)


def _ffn_body(x_ref, xp_ref, xn_ref, mod_ref, g_ref, wv_ref, wg_ref, cw_ref, cb_ref, wo_ref,
              fg_ref, o_ref, *, tm, final_norm):
    t = pl.program_id(1)
    nt = pl.num_programs(1)
    m = mod_ref[0]
    x = x_ref[0]
    xe = jnp.concatenate([xp_ref[0], x, xn_ref[0]], axis=0)
    he = _modulate(xe, g_ref[...], m[3:4], m[4:5]).astype(BF16)
    ge = jnp.dot(he, wg_ref[...], preferred_element_type=F32)
    rows = lax.broadcasted_iota(jnp.int32, (tm + 2 * CONV_HALO, 1), 0)
    lo = jnp.where(t > 0, 0, CONV_HALO)
    hi = jnp.where(t < nt - 1, tm + 2 * CONV_HALO, tm + CONV_HALO)
    ge = jnp.where((rows >= lo) & (rows < hi), ge, 0.0)
    val = jnp.dot(he[CONV_HALO:CONV_HALO + tm], wv_ref[...], preferred_element_type=F32)
    cw = cw_ref[...]
    h0 = CONV_HALO
    conv = (cb_ref[...] + cw[0:1] * ge[h0 - 1:h0 - 1 + tm] + cw[1:2] * ge[h0:h0 + tm]
            + cw[2:3] * ge[h0 + 1:h0 + 1 + tm])
    gelu = 0.5 * conv * (1.0 + lax.erf(conv * math.sqrt(0.5)))
    act = (gelu * val).astype(BF16)
    y = jnp.dot(act, wo_ref[...], preferred_element_type=F32)
    out = x + m[5:6] * y
    if final_norm:
        ms = jnp.mean(out * out, axis=-1, keepdims=True)
        out = out * lax.rsqrt(ms + NORM_EPS) * fg_ref[...]
    o_ref[0] = out


def _ffn(x, mod_l, mod_row, g, wv, wg, cw, cb, wo, fg, *, tm, final_norm):
    b, s, d = x.shape
    tm = min(tm, s)
    hb = tm // CONV_HALO
    last = s // CONV_HALO - 1
    body = functools.partial(_ffn_body, tm=tm, final_norm=final_norm)
    return pl.pallas_call(
        body,
        grid=(b, s // tm),
        in_specs=[pl.BlockSpec((1, tm, d), lambda i, t: (i, t, 0)),
                  pl.BlockSpec((1, CONV_HALO, d), lambda i, t: (i, jnp.maximum(t * hb - 1, 0), 0)),
                  pl.BlockSpec((1, CONV_HALO, d), lambda i, t: (i, jnp.minimum((t + 1) * hb, last), 0)),
                  pl.BlockSpec((1, 6, d), lambda i, t: (mod_row(i), 0, 0)),
                  _resident((1, d)),
                  _resident((d, D_FF)),
                  _resident((d, D_FF)),
                  _resident((3, D_FF)),
                  _resident((1, D_FF)),
                  _resident((D_FF, d)),
                  _resident((1, d))],
        out_specs=pl.BlockSpec((1, tm, d), lambda i, t: (i, t, 0)),
        out_shape=jax.ShapeDtypeStruct((b, s, d), F32),
        compiler_params=_cparams("parallel", "parallel"),
        name="conv_glu_ffn",
    )(x, x, x, mod_l, g, wv, wg, cw, cb, wo, fg)


def _ssm_proj_body(x_ref, mod_ref, g_ref, wt_ref, dtb_ref, zt_ref, xbct_ref, dtt_ref):
    m = mod_ref[0]
    h = _modulate(x_ref[0], g_ref[...], m[0:1], m[1:2]).astype(BF16)
    t = lax.dot_general(wt_ref[...], h, NT_DIMS, preferred_element_type=F32)
    zt_ref[0] = t[0:SSM_D_INNER].astype(BF16)
    xbct_ref[0] = t[SSM_D_INNER:SSM_D_INNER + SSM_XBC].astype(BF16)
    dtt_ref[0] = jax.nn.softplus(t[SSM_D_INNER + SSM_XBC:SSM_IN_COLS] + dtb_ref[...])


def _ssm_proj(x, mod_l, mod_row, g, wt, dtb, *, tm):
    b, s, d = x.shape
    tm = min(tm, s)
    return pl.pallas_call(
        _ssm_proj_body,
        grid=(b, s // tm),
        in_specs=[pl.BlockSpec((1, tm, d), lambda i, t: (i, t, 0)),
                  pl.BlockSpec((1, 6, d), lambda i, t: (mod_row(i), 0, 0)),
                  _resident((1, d)),
                  _resident((SSM_IN_COLS, d)),
                  _resident((2 * SSM_HEADS, 1))],
        out_specs=[pl.BlockSpec((1, SSM_D_INNER, tm), lambda i, t: (i, 0, t)),
                   pl.BlockSpec((1, SSM_XBC, tm), lambda i, t: (i, 0, t)),
                   pl.BlockSpec((1, 2 * SSM_HEADS, tm), lambda i, t: (i, 0, t))],
        out_shape=[jax.ShapeDtypeStruct((b, SSM_D_INNER, s), BF16),
                   jax.ShapeDtypeStruct((b, SSM_XBC, s), BF16),
                   jax.ShapeDtypeStruct((b, 2 * SSM_HEADS, s), F32)],
        compiler_params=_cparams("parallel", "parallel"),
        name="ssm_in_proj",
    )(x, mod_l, g, wt, dtb)


def _ssm_conv_body(u_ref, up_ref, un_ref, w_ref, b_ref, xst_ref, ct_ref, bm_ref, *, tc):
    t = pl.program_id(1)
    nt = pl.num_programs(1)
    u = u_ref[0].astype(F32)
    prev = jnp.where(t > 0, up_ref[0][:, V7X_LANES - 1:V7X_LANES].astype(F32), 0.0)
    nxt = jnp.where(t < nt - 1, un_ref[0][:, 0:1].astype(F32), 0.0)
    lane = lax.broadcasted_iota(jnp.int32, (1, tc), 1)
    packed = pltpu.bitcast(u_ref[0], jnp.uint32)
    rolled_l = pltpu.bitcast(pltpu.roll(packed, 1, 1), BF16).astype(F32)
    rolled_r = pltpu.bitcast(pltpu.roll(packed, tc - 1, 1), BF16).astype(F32)
    left = jnp.where(lane == 0, prev, rolled_l)
    right = jnp.where(lane == tc - 1, nxt, rolled_r)
    def lanes(a):
        return jnp.concatenate([a] * (tc // V7X_LANES), axis=1)

    v = _silu(lanes(b_ref[...]) + lanes(w_ref[0]) * left + lanes(w_ref[1]) * u
              + lanes(w_ref[2]) * right)
    bn = SSM_GROUPS * SSM_STATE
    xst_ref[0] = v[0:SSM_D_INNER].astype(BF16)
    bm_ref[0] = v[SSM_D_INNER:SSM_D_INNER + bn].T.astype(BF16)
    ct_ref[0] = v[SSM_D_INNER + bn:SSM_XBC].astype(BF16)


def _ssm_conv(xbct, w, bias, *, tc):
    b, ch, s = xbct.shape
    tc = min(tc, s)
    hb = tc // V7X_LANES
    last = s // V7X_LANES - 1
    bn = SSM_GROUPS * SSM_STATE
    body = functools.partial(_ssm_conv_body, tc=tc)
    return pl.pallas_call(
        body,
        grid=(b, s // tc),
        in_specs=[pl.BlockSpec((1, ch, tc), lambda i, t: (i, 0, t)),
                  pl.BlockSpec((1, ch, V7X_LANES), lambda i, t: (i, 0, jnp.maximum(t * hb - 1, 0))),
                  pl.BlockSpec((1, ch, V7X_LANES), lambda i, t: (i, 0, jnp.minimum((t + 1) * hb, last))),
                  _resident((3, ch, V7X_LANES)),
                  _resident((ch, V7X_LANES))],
        out_specs=[pl.BlockSpec((1, SSM_D_INNER, tc), lambda i, t: (i, 0, t)),
                   pl.BlockSpec((1, bn, tc), lambda i, t: (i, 0, t)),
                   pl.BlockSpec((1, tc, bn), lambda i, t: (i, t, 0))],
        out_shape=[jax.ShapeDtypeStruct((b, SSM_D_INNER, s), BF16),
                   jax.ShapeDtypeStruct((b, bn, s), BF16),
                   jax.ShapeDtypeStruct((b, s, bn), BF16)],
        compiler_params=_cparams("parallel", "parallel"),
        name="ssm_conv_silu",
    )(xbct, xbct, xbct, w, bias)


def _dot_f32_by_01(a, m01):
    hi = a.astype(BF16)
    r1 = a - hi.astype(F32)
    mid = r1.astype(BF16)
    lo = (r1 - mid.astype(F32)).astype(BF16)
    return (jnp.dot(hi, m01, preferred_element_type=F32)
            + jnp.dot(mid, m01, preferred_element_type=F32)
            + jnp.dot(lo, m01, preferred_element_type=F32))


def _scan_order_mask(d, chunk):
    jrow = lax.broadcasted_iota(jnp.int32, (chunk, chunk), 0)
    icol = lax.broadcasted_iota(jnp.int32, (chunk, chunk), 1)
    sign = jnp.where(d == 0, 1, -1)
    return (icol - jrow) * sign >= 0


DECAY_ROWS = 4 * SSM_HEADS


def _ssd_decay_body(a_ref, dt_ref, fac_ref, src_ref, *, chunk):
    d = pl.program_id(1)
    nh = SSM_HEADS
    mask01 = _scan_order_mask(d, chunk).astype(BF16)
    for s in range(dt_ref.shape[2] // chunk):
        tok = slice(s * chunk, (s + 1) * chunk)
        dt = dt_ref[0, :, tok]
        cum_t = _dot_f32_by_01(dt * a_ref[0], mask01)
        cum2_t = cum_t * LOG2E
        fac_ref[0, 0, 0:nh, tok] = cum2_t
        src_ref[0, 0, tok, :] = (cum2_t - jnp.log2(dt)).T
        tot = jnp.where(d == 0, cum_t[:, chunk - 1:chunk], cum_t[:, 0:1])
        fac_ref[0, 0, nh:2 * nh, tok] = jnp.exp(tot - cum_t) * dt
        fac_ref[0, 0, 2 * nh:3 * nh, tok] = jnp.exp(cum_t)
        fac_ref[0, 0, 3 * nh:4 * nh, tok] = jnp.broadcast_to(jnp.exp(tot), cum_t.shape)


def _ssd_decay(a, dtt, chunk):
    b, _, s = dtt.shape
    span = min(s, 8 * chunk)
    body = functools.partial(_ssd_decay_body, chunk=chunk)
    return pl.pallas_call(
        body,
        grid=(b, 2, s // span),
        in_specs=[pl.BlockSpec((1, SSM_HEADS, 1), lambda i, d, g: (d, 0, 0)),
                  pl.BlockSpec((1, SSM_HEADS, span), lambda i, d, g: (i, d, g))],
        out_specs=[pl.BlockSpec((1, 1, DECAY_ROWS, span), lambda i, d, g: (i, d, 0, g)),
                   pl.BlockSpec((1, 1, span, SSM_HEADS), lambda i, d, g: (i, d, g, 0))],
        out_shape=[jax.ShapeDtypeStruct((b, 2, DECAY_ROWS, s), F32),
                   jax.ShapeDtypeStruct((b, 2, s, SSM_HEADS), F32)],
        compiler_params=_cparams("parallel", "parallel", "parallel"),
        name="ssd_decay_factors",
    )(a, dtt)


def _scan_body(fac_ref, src_ref, xs_ref, ct_ref, b_ref, s0_ref, y_ref, sout_ref, state, *, chunk):
    d = pl.program_id(1)
    c = pl.program_id(2)
    nc = pl.num_programs(2)
    hp = SSM_HEADS_PER_GROUP
    hd = SSM_D_INNER // SSM_HEADS
    gw = hp * hd
    nh = SSM_HEADS
    nsub = xs_ref.shape[2] // chunk

    @pl.when(c == 0)
    def _load_state():
        state[...] = s0_ref[0, 0]

    def one_chunk(direction, tok):
        not_yet = jnp.where(_scan_order_mask(direction, chunk), 0.0, -jnp.inf)
        cum2_t = fac_ref[0, 0, 0:nh, tok]
        src_term = src_ref[0, 0, tok, :]
        to_end = fac_ref[0, 0, nh:2 * nh, tok]
        ecum = fac_ref[0, 0, 2 * nh:3 * nh, tok]
        etot = fac_ref[0, 0, 3 * nh:4 * nh, tok.start:tok.start + 1]
        for g in range(SSM_GROUPS):
            bg = b_ref[0, tok, g * SSM_STATE:(g + 1) * SSM_STATE]
            ctg = ct_ref[0, g * SSM_STATE:(g + 1) * SSM_STATE, tok]
            cbt = jnp.dot(bg, ctg, preferred_element_type=F32)
            xg = xs_ref[0, g * gw:(g + 1) * gw, tok]
            sg = state[g * gw:(g + 1) * gw, :]
            hs = slice(g * hp, (g + 1) * hp)
            y_state = (jnp.dot(sg.astype(BF16), ctg, preferred_element_type=F32)
                       .reshape(hp, hd, chunk) * ecum[hs][:, None, :])
            outs = []
            for r in range(hp):
                h = g * hp + r
                seg = cum2_t[h:h + 1, :] - src_term[:, h:h + 1]
                w = (cbt * jnp.exp2(seg + not_yet)).astype(BF16)
                outs.append(jnp.dot(xg[r * hd:(r + 1) * hd], w, preferred_element_type=F32)
                            + y_state[r])
            y_ref[0, 0, g * gw:(g + 1) * gw, tok] = jnp.concatenate(outs, axis=0).astype(BF16)
            xw = ((xg.astype(F32).reshape(hp, hd, chunk) * to_end[hs][:, None, :])
                  .reshape(gw, chunk).astype(BF16))
            upd = jnp.dot(xw, bg, preferred_element_type=F32)
            decayed = (sg.reshape(hp, hd, SSM_STATE) * etot[hs][:, None, :]).reshape(gw, SSM_STATE)
            state[g * gw:(g + 1) * gw, :] = decayed + upd

    for direction in (0, 1):
        @pl.when(d == direction)
        def _walk(direction=direction):
            order = range(nsub) if direction == 0 else reversed(range(nsub))
            for sub in order:
                one_chunk(direction, slice(sub * chunk, (sub + 1) * chunk))

    @pl.when(c == nc - 1)
    def _store_state():
        sout_ref[0, 0] = state[...]


def _ssd_scan(a, dtt, xst, ct, bm, s0):
    b, ch, s = xst.shape
    chunk = min(SSD_CHUNK, s)
    span = min(s, 2 * chunk)
    nc = s // span
    bn = SSM_GROUPS * SSM_STATE
    fac, src = _ssd_decay(a, dtt, chunk)

    def cidx(d, c):
        return c + d * (nc - 1 - 2 * c)

    body = functools.partial(_scan_body, chunk=chunk)
    return pl.pallas_call(
        body,
        grid=(b, 2, nc),
        in_specs=[pl.BlockSpec((1, 1, DECAY_ROWS, span), lambda i, d, c: (i, d, 0, cidx(d, c))),
                  pl.BlockSpec((1, 1, span, SSM_HEADS), lambda i, d, c: (i, d, cidx(d, c), 0)),
                  pl.BlockSpec((1, ch, span), lambda i, d, c: (i, 0, cidx(d, c))),
                  pl.BlockSpec((1, bn, span), lambda i, d, c: (i, 0, cidx(d, c))),
                  pl.BlockSpec((1, span, bn), lambda i, d, c: (i, cidx(d, c), 0)),
                  pl.BlockSpec((1, 1, ch, SSM_STATE), lambda i, d, c: (d, i, 0, 0))],
        out_specs=[pl.BlockSpec((1, 1, ch, span), lambda i, d, c: (d, i, 0, cidx(d, c))),
                   pl.BlockSpec((1, 1, ch, SSM_STATE), lambda i, d, c: (d, i, 0, 0))],
        out_shape=[jax.ShapeDtypeStruct((2, b, ch, s), BF16),
                   jax.ShapeDtypeStruct((2, b, ch, SSM_STATE), F32)],
        scratch_shapes=[pltpu.VMEM((ch, SSM_STATE), F32)],
        compiler_params=_cparams("parallel", "parallel", "arbitrary"),
        name="ssd_scan",
    )(fac, src, xst, ct, bm, s0)


def _ssm_out_body(x_ref, mod_ref, yf_ref, yb_ref, xs_ref, z_ref, dsk_ref, ng_ref, wt_ref, o_ref):
    y = (yf_ref[0, 0].astype(F32) + yb_ref[0, 0].astype(F32)
         + dsk_ref[...] * xs_ref[0].astype(F32))
    y = y * _silu(z_ref[0].astype(F32))
    ms = jnp.mean(y * y, axis=0, keepdims=True)
    yn = (y * lax.rsqrt(ms + NORM_EPS) * ng_ref[...]).astype(BF16)
    ot = jnp.dot(wt_ref[...], yn, preferred_element_type=F32)
    o_ref[0] = x_ref[0] + mod_ref[0][2:3] * ot.T


def _ssm_out(x, mod_l, mod_row, y, xst, zt, dsk, ng, wt, *, tm):
    b, s, d = x.shape
    tm = min(tm, s)
    ch = xst.shape[1]
    return pl.pallas_call(
        _ssm_out_body,
        grid=(b, s // tm),
        in_specs=[pl.BlockSpec((1, tm, d), lambda i, t: (i, t, 0)),
                  pl.BlockSpec((1, 6, d), lambda i, t: (mod_row(i), 0, 0)),
                  pl.BlockSpec((1, 1, ch, tm), lambda i, t: (0, i, 0, t)),
                  pl.BlockSpec((1, 1, ch, tm), lambda i, t: (1, i, 0, t)),
                  pl.BlockSpec((1, ch, tm), lambda i, t: (i, 0, t)),
                  pl.BlockSpec((1, ch, tm), lambda i, t: (i, 0, t)),
                  _resident((ch, 1)),
                  _resident((ch, 1)),
                  _resident((d, ch))],
        out_specs=pl.BlockSpec((1, tm, d), lambda i, t: (i, t, 0)),
        out_shape=jax.ShapeDtypeStruct((b, s, d), F32),
        compiler_params=_cparams("parallel", "parallel"),
        name="ssm_out_residual",
    )(x, mod_l, y, y, xst, zt, dsk, ng, wt)


def _rope_tables_t(n):
    t = jnp.arange(n)
    inv_freq = 1.0 / (ROPE_BASE ** (jnp.arange(ROPE_PAIRS, dtype=F32) / ROPE_PAIRS))
    ang_r = (t // GRID_W).astype(F32)[None, :] * inv_freq[:, None]
    ang_c = (t % GRID_W).astype(F32)[None, :] * inv_freq[:, None]
    cr, sr, cc, sc = jnp.cos(ang_r), jnp.sin(ang_r), jnp.cos(ang_c), jnp.sin(ang_c)
    return (jnp.concatenate([cr, cr, cc, cc], axis=0),
            jnp.concatenate([-sr, sr, -sc, sc], axis=0))


def _attn_layer(x, ctx, mod_l, lat_row, ctx_row, p, rope, lambda_init, with_ctx):
    wt = p["w_in"].T.astype(BF16)
    g = p["norm_g"].reshape(1, D_MODEL)
    qg = p["q_norm_g"].reshape(HEAD_DIM, 1)
    kg = p["k_norm_g"].reshape(HEAD_DIM, 1)
    cos_t, sin_t = rope
    c = ctx.shape[1]
    qt, *lat_kv = _attn_in(x, mod_l, lat_row, g, wt, cos_t, sin_t, qg, kg, rope=True, tm=512)
    qtc, *ctx_kv = _attn_in(ctx, mod_l, ctx_row, g, wt, cos_t[:, :c], sin_t[:, :c], qg, kg,
                            rope=False, tm=256)
    lam_vecs = [p[k].reshape(1, HEAD_DIM) for k in ("lq1", "lk1", "lq2", "lk2")]
    sg = p["subln_g"].reshape(V_ROWS, 1)
    w_out = p["w_out"].astype(BF16)
    x = _flash(x, mod_l, lat_row, w_out, lam_vecs, sg, qt, ctx_kv, lat_kv,
               lambda_init=lambda_init, tq=256, tk=2048)
    if with_ctx:
        ctx = _flash(ctx, mod_l, ctx_row, w_out, lam_vecs, sg, qtc, ctx_kv, None,
                     lambda_init=lambda_init, tq=256, tk=0)
    return x, ctx


def _ssm_layer(x, ctx, mod_l, lat_row, ctx_row, p, with_ctx):
    wt = p["w_in"].T.astype(BF16)
    g = p["norm_g"].reshape(1, D_MODEL)
    dtb = p["dt_bias"].reshape(2 * SSM_HEADS, 1)
    conv_w = jnp.broadcast_to(p["conv_w"][:, :, None], (3, SSM_XBC, V7X_LANES))
    conv_b = jnp.broadcast_to(p["conv_b"][:, None], (SSM_XBC, V7X_LANES))
    a = (-jnp.exp(p["a_log"].astype(F32))).reshape(2, SSM_HEADS, 1)
    b = x.shape[0]

    def pre(v, row, tm):
        zt, xbct, dtt = _ssm_proj(v, mod_l, row, g, wt, dtb, tm=tm)
        xst, ct, bm = _ssm_conv(xbct, conv_w, conv_b, tc=512)
        return zt, xst, ct, bm, dtt

    zt_c, xst_c, ct_c, bm_c, dtt_c = pre(ctx, ctx_row, 256)
    zt_l, xst_l, ct_l, bm_l, dtt_l = pre(x, lat_row, 512)
    zero = jnp.zeros((2, b, SSM_D_INNER, SSM_STATE), F32)
    y_c, s_ctx = _ssd_scan(a, dtt_c, xst_c, ct_c, bm_c, zero)
    y_l, _ = _ssd_scan(a, dtt_l, xst_l, ct_l, bm_l, s_ctx)
    dsk = jnp.repeat(p["d_skip"], SSM_D_INNER // SSM_HEADS).reshape(SSM_D_INNER, 1)
    ng = p["out_norm_g"].reshape(SSM_D_INNER, 1)
    w_out_t = p["w_out"].T.astype(BF16)
    x = _ssm_out(x, mod_l, lat_row, y_l, xst_l, zt_l, dsk, ng, w_out_t, tm=512)
    if with_ctx:
        ctx = _ssm_out(ctx, mod_l, ctx_row, y_c, xst_c, zt_c, dsk, ng, w_out_t, tm=256)
    return x, ctx


def kernel(x, c, ctx, c_ctx, mod_w, mod_b, norm_mix_g, norm_ffn_g, attn_w_in, attn_w_out,
           diff_lq1, diff_lk1, diff_lq2, diff_lk2, diff_subln_g, gqa_q_norm_g, gqa_k_norm_g,
           ssm_w_in, ssm_conv_w, ssm_conv_b, ssm_dt_bias, ssm_a_log, ssm_d, ssm_norm_g, ssm_w_out,
           ffn_w_in, ffn_conv_w, ffn_conv_b, ffn_w_out, final_norm_g):
    b, n, d = x.shape
    mod_rows = 16
    c_rows = jnp.zeros((mod_rows, d), F32).at[:b].set(c).at[b].set(c_ctx)
    mod = _mod_all(c_rows, mod_w, mod_b).reshape(DEPTH, mod_rows, 6, d)
    lat_row = lambda i: i
    ctx_row = lambda i: b
    rope = _rope_tables_t(n)

    for layer in range(DEPTH):
        with_ctx = layer < DEPTH - 1
        mod_l = mod[layer]
        i = layer // 2
        if layer % 2 == 0:
            p = dict(w_in=attn_w_in[i], w_out=attn_w_out[i], norm_g=norm_mix_g[layer],
                     lq1=diff_lq1[i], lk1=diff_lk1[i], lq2=diff_lq2[i], lk2=diff_lk2[i],
                     subln_g=diff_subln_g[i], q_norm_g=gqa_q_norm_g[i], k_norm_g=gqa_k_norm_g[i])
            lambda_init = 0.8 - 0.6 * math.exp(-0.3 * layer)
            x, ctx = _attn_layer(x, ctx, mod_l, lat_row, ctx_row, p, rope, lambda_init, with_ctx)
        else:
            p = dict(w_in=ssm_w_in[i], norm_g=norm_mix_g[layer], conv_w=ssm_conv_w[i],
                     conv_b=ssm_conv_b[i], dt_bias=ssm_dt_bias[i], a_log=ssm_a_log[i],
                     d_skip=ssm_d[i], out_norm_g=ssm_norm_g[i], w_out=ssm_w_out[i])
            x, ctx = _ssm_layer(x, ctx, mod_l, lat_row, ctx_row, p, with_ctx)
        g = norm_ffn_g[layer].reshape(1, d)
        wv = ffn_w_in[layer][:, :D_FF].astype(BF16)
        wg = ffn_w_in[layer][:, D_FF:].astype(BF16)
        cw = ffn_conv_w[layer]
        cb = ffn_conv_b[layer].reshape(1, D_FF)
        wo = ffn_w_out[layer].astype(BF16)
        fg = final_norm_g.reshape(1, d)
        x = _ffn(x, mod_l, lat_row, g, wv, wg, cw, cb, wo, fg, tm=256,
                 final_norm=layer == DEPTH - 1)
        if with_ctx:
            ctx = _ffn(ctx, mod_l, ctx_row, g, wv, wg, cw, cb, wo, fg, tm=256, final_norm=False)
    return x
```

```python
import functools
import math

import jax
import jax.numpy as jnp
from jax import lax
from jax.experimental import pallas as pl
from jax.experimental.pallas import tpu as pltpu

F32 = jnp.float32
BF16 = jnp.bfloat16

D_MODEL = 1024
DEPTH = 4
GRID_W = 64
HEAD_DIM = 64
ROPE_PAIRS = HEAD_DIM // 4
ROPE_BASE = 10000.0
NORM_EPS = 1e-6
DIFF_HEADS = 4
GQA_HEADS = 8
GQA_KV_HEADS = 2
GQA_GROUP = GQA_HEADS // GQA_KV_HEADS
ATTN_IN_COLS = 2304
SSM_D_INNER = 2048
SSM_HEADS = 32
SSM_GROUPS = 4
SSM_HEADS_PER_GROUP = SSM_HEADS // SSM_GROUPS
SSM_STATE = 128
SSM_XBC = 3072
SSM_IN_COLS = 5184
D_FF = 2816
LOG2E = math.log2(math.e)

V7X_VMEM_BYTES = 64 * 1024 * 1024
VMEM_LIMIT_BYTES = V7X_VMEM_BYTES - 8 * 1024 * 1024
V7X_LANES = 128
BF16_SUBLANES = 16

V_ROWS = 2 * HEAD_DIM
V_ROWS_PADDED = V_ROWS + BF16_SUBLANES
GV_ROWS_PADDED = HEAD_DIM + BF16_SUBLANES
K_COLS = (DIFF_HEADS + 1) * 2 * HEAD_DIM

ATTN_KEY_SUBBLOCK = 256
ATTN_STREAM_MAX_OCTAVES = 64.0

CONV_HALO = 16
SSD_CHUNK = 256
SSD_CHUNKS_PER_STEP = 2

NT_DIMS = (((1,), (1,)), ((), ()))


def _cparams(*sem):
    return pltpu.CompilerParams(dimension_semantics=sem, vmem_limit_bytes=VMEM_LIMIT_BYTES)


def _resident(shape):
    nd = len(shape)
    return pl.BlockSpec(shape, lambda *_: (0,) * nd, pipeline_mode=pl.Buffered(1))


def _silu(v):
    return v * jax.nn.sigmoid(v)


def _modulate(x, g, shift, scale):
    ms = jnp.mean(x * x, axis=-1, keepdims=True)
    return (x * lax.rsqrt(ms + NORM_EPS) * g) * (1.0 + scale) + shift


def _mod_body(c_ref, w_ref, b_ref, o_ref):
    s = _silu(c_ref[...])
    o_ref[0] = jnp.dot(s, w_ref[0], preferred_element_type=F32,
                       precision=lax.Precision.HIGHEST) + b_ref[0]


def _mod_all(c_rows, mod_w, mod_b):
    rows = c_rows.shape[0]
    depth, d, cols = mod_w.shape
    tn = 2048
    return pl.pallas_call(
        _mod_body,
        grid=(depth, cols // tn),
        in_specs=[pl.BlockSpec((rows, d), lambda l, n: (0, 0)),
                  pl.BlockSpec((1, d, tn), lambda l, n: (l, 0, n)),
                  pl.BlockSpec((1, 1, tn), lambda l, n: (l, 0, n))],
        out_specs=pl.BlockSpec((1, rows, tn), lambda l, n: (l, 0, n)),
        out_shape=jax.ShapeDtypeStruct((depth, rows, cols), F32),
        compiler_params=_cparams("parallel", "parallel"),
        name="mod_vectors",
    )(c_rows, mod_w, mod_b.reshape(depth, 1, cols))


def _attn_in_body(x_ref, mod_ref, g_ref, wt_ref, cos_ref, sin_ref, qg_ref, kg_ref,
                  qt_ref, k_ref, vta_ref, vtb_ref, *, rope):
    m = mod_ref[0]
    h = _modulate(x_ref[0], g_ref[...], m[0:1], m[1:2]).astype(BF16)
    t = lax.dot_general(wt_ref[...], h, NT_DIMS, preferred_element_type=F32)
    tm = t.shape[1]

    def rot(u):
        if not rope:
            return u
        sw = jnp.concatenate([u[:, 16:32], u[:, 0:16], u[:, 48:64], u[:, 32:48]], axis=1)
        return u * cos_ref[...][None] + sw * sin_ref[...][None]

    def qk_norm(u, g):
        ms = jnp.mean(u * u, axis=1, keepdims=True)
        return u * lax.rsqrt(ms + NORM_EPS) * g[None]

    nq = 2 * DIFF_HEADS
    qa = rot(t[0:512].reshape(nq, HEAD_DIM, tm))
    ka = rot(t[512:1024].reshape(nq, HEAD_DIM, tm))
    va = t[1024:1536]
    qb = rot(qk_norm(t[1536:2048].reshape(GQA_HEADS, HEAD_DIM, tm), qg_ref[...]))
    kb = rot(qk_norm(t[2048:2176].reshape(GQA_KV_HEADS, HEAD_DIM, tm), kg_ref[...]))
    vb = t[2176:2304]

    qs = (HEAD_DIM ** -0.5) * LOG2E
    qt_ref[0, 0:512] = (qa * qs).reshape(512, tm).astype(BF16)
    qt_ref[0, 512:1024] = (qb * qs).reshape(512, tm).astype(BF16)
    kt = jnp.concatenate([ka.reshape(512, tm), kb.reshape(128, tm)], axis=0)
    k_ref[0] = kt.T.astype(BF16)
    ones = jnp.ones((BF16_SUBLANES, tm), BF16)
    for u in range(DIFF_HEADS):
        vta_ref[0, u, 0:V_ROWS] = va[u * V_ROWS:(u + 1) * V_ROWS].astype(BF16)
        vta_ref[0, u, V_ROWS:V_ROWS_PADDED] = ones
    for g in range(GQA_KV_HEADS):
        vtb_ref[0, g, 0:HEAD_DIM] = vb[g * HEAD_DIM:(g + 1) * HEAD_DIM].astype(BF16)
        vtb_ref[0, g, HEAD_DIM:GV_ROWS_PADDED] = ones


def _attn_in(x, mod_l, mod_row, g, wt, cos_t, sin_t, qg, kg, *, rope, tm):
    b, s, d = x.shape
    tm = min(tm, s)
    body = functools.partial(_attn_in_body, rope=rope)
    return pl.pallas_call(
        body,
        grid=(b, s // tm),
        in_specs=[pl.BlockSpec((1, tm, d), lambda i, t: (i, t, 0)),
                  pl.BlockSpec((1, 6, d), lambda i, t: (mod_row(i), 0, 0)),
                  _resident((1, d)),
                  _resident((ATTN_IN_COLS, d)),
                  pl.BlockSpec((HEAD_DIM, tm), lambda i, t: (0, t)),
                  pl.BlockSpec((HEAD_DIM, tm), lambda i, t: (0, t)),
                  _resident((HEAD_DIM, 1)),
                  _resident((HEAD_DIM, 1))],
        out_specs=[pl.BlockSpec((1, 1024, tm), lambda i, t: (i, 0, t)),
                   pl.BlockSpec((1, tm, K_COLS), lambda i, t: (i, t, 0)),
                   pl.BlockSpec((1, DIFF_HEADS, V_ROWS_PADDED, tm), lambda i, t: (i, 0, 0, t)),
                   pl.BlockSpec((1, GQA_KV_HEADS, GV_ROWS_PADDED, tm), lambda i, t: (i, 0, 0, t))],
        out_shape=[jax.ShapeDtypeStruct((b, 1024, s), BF16),
                   jax.ShapeDtypeStruct((b, s, K_COLS), BF16),
                   jax.ShapeDtypeStruct((b, DIFF_HEADS, V_ROWS_PADDED, s), BF16),
                   jax.ShapeDtypeStruct((b, GQA_KV_HEADS, GV_ROWS_PADDED, s), BF16)],
        compiler_params=_cparams("parallel", "parallel"),
        name="attn_in_rope" if rope else "attn_in_ctx",
    )(x, mod_l, g, wt, cos_t, sin_t, qg, kg)


def _attn_unit_cols(tq):
    units = []
    for h in range(DIFF_HEADS):
        units.append((h * 128, h, V_ROWS_PADDED, h * 2 * tq, 2 * tq))
    base = DIFF_HEADS * 2 * tq
    for g in range(GQA_KV_HEADS):
        units.append((DIFF_HEADS * 128, g, GV_ROWS_PADDED, base + g * GQA_GROUP * tq,
                      GQA_GROUP * tq))
    return units


def _flash_body(*refs, tq, lambda_init, has_lat):
    if has_lat:
        (lq1, lk1, lq2, lk2, sg_ref, x_ref, mod_ref, wo_ref, qt_ref, kc_ref, vtac_ref, vtbc_ref,
         kl_ref, vtal_ref, vtbl_ref, o_ref, rhs, acc, mrow, pv_new, m_chunk) = refs
    else:
        (lq1, lk1, lq2, lk2, sg_ref, x_ref, mod_ref, wo_ref, qt_ref, kc_ref, vtac_ref, vtbc_ref,
         o_ref, rhs, acc, mrow, pv_new, m_chunk) = refs
    j = pl.program_id(2)
    nj = pl.num_programs(2)
    units = _attn_unit_cols(tq)
    diff_cols = DIFF_HEADS * 2 * tq

    def values(vta_ref, vtb_ref, u, vu):
        return vta_ref[0, vu] if u < DIFF_HEADS else vtb_ref[0, vu]

    def exact_step(k_ref, vta_ref, vtb_ref):
        for u, (kc0, vu, vr, c0, w) in enumerate(units):
            s = jnp.dot(k_ref[0, :, kc0:kc0 + 128], rhs[:, c0:c0 + w],
                        preferred_element_type=F32)
            mp = mrow[:, c0:c0 + w]
            mn = jnp.maximum(mp, jnp.max(s, axis=0, keepdims=True))
            alpha = jnp.exp2(mp - mn)
            p = jnp.exp2(s - mn).astype(BF16)
            pv = jnp.dot(values(vta_ref, vtb_ref, u, vu), p, preferred_element_type=F32)
            acc[0:vr, c0:c0 + w] = acc[0:vr, c0:c0 + w] * alpha + pv
            mrow[:, c0:c0 + w] = mn

    def streaming_step(k_ref, vta_ref, vtb_ref):
        nk = k_ref.shape[1]
        sub = min(nk, ATTN_KEY_SUBBLOCK)
        for u, (kc0, vu, vr, c0, w) in enumerate(units):
            m_used = mrow[:, c0:c0 + w]
            cm = None
            parts = []
            for r in range(nk // sub):
                s = jnp.dot(k_ref[0, r * sub:(r + 1) * sub, kc0:kc0 + 128], rhs[:, c0:c0 + w],
                            preferred_element_type=F32)
                parts.append(jnp.exp2(s - m_used).astype(BF16))
                sm = jnp.max(s.reshape(sub // 8, 8, w), axis=0)
                cm = sm if cm is None else jnp.maximum(cm, sm)
            p = jnp.concatenate(parts, axis=0)
            pv_new[0:vr, c0:c0 + w] = jnp.dot(values(vta_ref, vtb_ref, u, vu), p,
                                              preferred_element_type=F32)
            m_chunk[:, c0:c0 + w] = jnp.max(cm, axis=0, keepdims=True)

    @pl.when(j == 0)
    def _init():
        zeros = jnp.zeros((HEAD_DIM, tq), BF16)
        for h in range(DIFF_HEADS):
            c0 = h * 2 * tq
            rhs[0:64, c0:c0 + tq] = qt_ref[0, h * 128:h * 128 + 64, :]
            rhs[64:128, c0:c0 + tq] = zeros
            rhs[0:64, c0 + tq:c0 + 2 * tq] = zeros
            rhs[64:128, c0 + tq:c0 + 2 * tq] = qt_ref[0, h * 128 + 64:h * 128 + 128, :]
        base = DIFF_HEADS * 2 * tq
        for g in range(GQA_KV_HEADS):
            for r in range(GQA_GROUP):
                c0 = base + (g * GQA_GROUP + r) * tq
                hd = 512 + (g * GQA_GROUP + r) * HEAD_DIM
                rhs[g * 64:(g + 1) * 64, c0:c0 + tq] = qt_ref[0, hd:hd + HEAD_DIM, :]
                rhs[(1 - g) * 64:(2 - g) * 64, c0:c0 + tq] = zeros
        acc[...] = jnp.zeros(acc.shape, F32)
        mrow[...] = jnp.zeros(mrow.shape, F32)

    row_blocks = ((V_ROWS_PADDED, slice(0, diff_cols)),
                  (GV_ROWS_PADDED, slice(diff_cols, acc.shape[1])))

    def nothing_pending():
        for rows, cols in row_blocks:
            pv_new[0:rows, cols] = jnp.zeros((rows, cols.stop - cols.start), F32)
        m_chunk[...] = jnp.full(m_chunk.shape, -jnp.inf, F32)

    def commit():
        mp = mrow[...]
        mn = jnp.maximum(mp, m_chunk[...])
        mrow[...] = mn
        alpha = jnp.exp2(mp - mn)
        for rows, cols in row_blocks:
            acc[0:rows, cols] = (acc[0:rows, cols] + pv_new[0:rows, cols]) * alpha[:, cols]

    def stream_chunk(k_ref, vta_ref, vtb_ref, first):
        streaming_step(k_ref, vta_ref, vtb_ref)
        dev = m_chunk[...] - mrow[...]
        in_range = jnp.max(jnp.abs(dev) if first else dev) <= ATTN_STREAM_MAX_OCTAVES

        @pl.when(jnp.logical_not(in_range))
        def _redo():
            if first:
                mrow[...] = jnp.full(mrow.shape, -jnp.inf, F32)
            exact_step(k_ref, vta_ref, vtb_ref)
            nothing_pending()

    @pl.when(j == 0)
    def _context_keys():
        stream_chunk(kc_ref, vtac_ref, vtbc_ref, True)

    if has_lat:
        commit()
        stream_chunk(kl_ref, vtal_ref, vtbl_ref, False)

    @pl.when(j == nj - 1)
    def _finish():
        commit()
        lam = (jnp.exp(jnp.sum(lq1[...] * lk1[...], keepdims=True))
               - jnp.exp(jnp.sum(lq2[...] * lk2[...], keepdims=True)) + lambda_init)
        pieces = []
        for h in range(DIFF_HEADS):
            c0 = h * 2 * tq
            o1 = acc[0:V_ROWS, c0:c0 + tq] / acc[V_ROWS:V_ROWS + 1, c0:c0 + tq]
            o2 = acc[0:V_ROWS, c0 + tq:c0 + 2 * tq] / acc[V_ROWS:V_ROWS + 1, c0 + tq:c0 + 2 * tq]
            oh = o1 - lam * o2
            ms = jnp.mean(oh * oh, axis=0, keepdims=True)
            pieces.append(oh * lax.rsqrt(ms + NORM_EPS) * sg_ref[...] * (1.0 - lambda_init))
        base = DIFF_HEADS * 2 * tq
        for g in range(GQA_KV_HEADS):
            for r in range(GQA_GROUP):
                c0 = base + (g * GQA_GROUP + r) * tq
                pieces.append(acc[0:HEAD_DIM, c0:c0 + tq] / acc[HEAD_DIM:HEAD_DIM + 1, c0:c0 + tq])
        o = jnp.concatenate(pieces, axis=0).T.astype(BF16)
        y = jnp.dot(o, wo_ref[...], preferred_element_type=F32)
        o_ref[0] = x_ref[0] + mod_ref[0][2:3] * y


def _flash(x, mod_l, mod_row, w_out, lam_vecs, sg, qt, ctx_kv, lat_kv, *, lambda_init, tq, tk):
    b, _, sq = qt.shape
    d = x.shape[2]
    kc, vtac, vtbc = ctx_kv
    c = kc.shape[1]
    has_lat = lat_kv is not None
    tq = min(tq, sq)
    ncols = (DIFF_HEADS * 2 + GQA_HEADS) * tq
    in_specs = [_resident((1, HEAD_DIM))] * 4 + [
        _resident((V_ROWS, 1)),
        pl.BlockSpec((1, tq, d), lambda i, q, j: (i, q, 0)),
        pl.BlockSpec((1, 6, d), lambda i, q, j: (mod_row(i), 0, 0)),
        _resident((1024, d)),
        pl.BlockSpec((1, 1024, tq), lambda i, q, j: (i, 0, q)),
        pl.BlockSpec((1, c, K_COLS), lambda i, q, j: (i, 0, 0)),
        pl.BlockSpec((1, DIFF_HEADS, V_ROWS_PADDED, c), lambda i, q, j: (i, 0, 0, 0)),
        pl.BlockSpec((1, GQA_KV_HEADS, GV_ROWS_PADDED, c), lambda i, q, j: (i, 0, 0, 0)),
    ]
    args = list(lam_vecs) + [sg, x, mod_l, w_out, qt, kc, vtac, vtbc]
    nkv = 1
    if has_lat:
        kl, vtal, vtbl = lat_kv
        n = kl.shape[1]
        tk = min(tk, n)
        nkv = n // tk
        in_specs += [pl.BlockSpec((1, tk, K_COLS), lambda i, q, j: (i, j, 0)),
                     pl.BlockSpec((1, DIFF_HEADS, V_ROWS_PADDED, tk), lambda i, q, j: (i, 0, 0, j)),
                     pl.BlockSpec((1, GQA_KV_HEADS, GV_ROWS_PADDED, tk), lambda i, q, j: (i, 0, 0, j))]
        args += [kl, vtal, vtbl]
    scratch = [pltpu.VMEM((2 * HEAD_DIM, ncols), BF16),
               pltpu.VMEM((V_ROWS_PADDED, ncols), F32),
               pltpu.VMEM((1, ncols), F32),
               pltpu.VMEM((V_ROWS_PADDED, ncols), F32),
               pltpu.VMEM((1, ncols), F32)]
    body = functools.partial(_flash_body, tq=tq, lambda_init=lambda_init, has_lat=has_lat)
    return pl.pallas_call(
        body,
        grid=(b, sq // tq, nkv),
        in_specs=in_specs,
        out_specs=pl.BlockSpec((1, tq, d), lambda i, q, j: (i, q, 0)),
        out_shape=jax.ShapeDtypeStruct((b, sq, d), F32),
        scratch_shapes=scratch,
        compiler_params=_cparams("parallel", "parallel", "arbitrary"),
        name="attn_sweep_lat" if has_lat else "attn_sweep_ctx",
    )(*args)


def _ffn_body(x_ref, xp_ref, xn_ref, mod_ref, g_ref, wv_ref, wg_ref, cw_ref, cb_ref, wo_ref,
              fg_ref, o_ref, *, tm, final_norm):
    t = pl.program_id(1)
    nt = pl.num_programs(1)
    m = mod_ref[0]
    x = x_ref[0]
    xe = jnp.concatenate([xp_ref[0], x, xn_ref[0]], axis=0)
    he = _modulate(xe, g_ref[...], m[3:4], m[4:5]).astype(BF16)
    ge = jnp.dot(he, wg_ref[...], preferred_element_type=F32)
    rows = lax.broadcasted_iota(jnp.int32, (tm + 2 * CONV_HALO, 1), 0)
    lo = jnp.where(t > 0, 0, CONV_HALO)
    hi = jnp.where(t < nt - 1, tm + 2 * CONV_HALO, tm + CONV_HALO)
    ge = jnp.where((rows >= lo) & (rows < hi), ge, 0.0)
    val = jnp.dot(he[CONV_HALO:CONV_HALO + tm], wv_ref[...], preferred_element_type=F32)
    cw = cw_ref[...]
    h0 = CONV_HALO
    conv = (cb_ref[...] + cw[0:1] * ge[h0 - 1:h0 - 1 + tm] + cw[1:2] * ge[h0:h0 + tm]
            + cw[2:3] * ge[h0 + 1:h0 + 1 + tm])
    gelu = 0.5 * conv * (1.0 + lax.erf(conv * math.sqrt(0.5)))
    act = (gelu * val).astype(BF16)
    y = jnp.dot(act, wo_ref[...], preferred_element_type=F32)
    out = x + m[5:6] * y
    if final_norm:
        ms = jnp.mean(out * out, axis=-1, keepdims=True)
        out = out * lax.rsqrt(ms + NORM_EPS) * fg_ref[...]
    o_ref[0] = out


def _ffn(x, mod_l, mod_row, g, wv, wg, cw, cb, wo, fg, *, tm, final_norm):
    b, s, d = x.shape
    tm = min(tm, s)
    hb = tm // CONV_HALO
    last = s // CONV_HALO - 1
    body = functools.partial(_ffn_body, tm=tm, final_norm=final_norm)
    return pl.pallas_call(
        body,
        grid=(b, s // tm),
        in_specs=[pl.BlockSpec((1, tm, d), lambda i, t: (i, t, 0)),
                  pl.BlockSpec((1, CONV_HALO, d), lambda i, t: (i, jnp.maximum(t * hb - 1, 0), 0)),
                  pl.BlockSpec((1, CONV_HALO, d), lambda i, t: (i, jnp.minimum((t + 1) * hb, last), 0)),
                  pl.BlockSpec((1, 6, d), lambda i, t: (mod_row(i), 0, 0)),
                  _resident((1, d)),
                  _resident((d, D_FF)),
                  _resident((d, D_FF)),
                  _resident((3, D_FF)),
                  _resident((1, D_FF)),
                  _resident((D_FF, d)),
                  _resident((1, d))],
        out_specs=pl.BlockSpec((1, tm, d), lambda i, t: (i, t, 0)),
        out_shape=jax.ShapeDtypeStruct((b, s, d), F32),
        compiler_params=_cparams("parallel", "parallel"),
        name="conv_glu_ffn",
    )(x, x, x, mod_l, g, wv, wg, cw, cb, wo, fg)


def _ssm_proj_body(x_ref, mod_ref, g_ref, wt_ref, dtb_ref, zt_ref, xbct_ref, dtt_ref):
    m = mod_ref[0]
    h = _modulate(x_ref[0], g_ref[...], m[0:1], m[1:2]).astype(BF16)
    t = lax.dot_general(wt_ref[...], h, NT_DIMS, preferred_element_type=F32)
    zt_ref[0] = t[0:SSM_D_INNER].astype(BF16)
    xbct_ref[0] = t[SSM_D_INNER:SSM_D_INNER + SSM_XBC].astype(BF16)
    dtt_ref[0] = jax.nn.softplus(t[SSM_D_INNER + SSM_XBC:SSM_IN_COLS] + dtb_ref[...])


def _ssm_proj(x, mod_l, mod_row, g, wt, dtb, *, tm):
    b, s, d = x.shape
    tm = min(tm, s)
    return pl.pallas_call(
        _ssm_proj_body,
        grid=(b, s // tm),
        in_specs=[pl.BlockSpec((1, tm, d), lambda i, t: (i, t, 0)),
                  pl.BlockSpec((1, 6, d), lambda i, t: (mod_row(i), 0, 0)),
                  _resident((1, d)),
                  _resident((SSM_IN_COLS, d)),
                  _resident((2 * SSM_HEADS, 1))],
        out_specs=[pl.BlockSpec((1, SSM_D_INNER, tm), lambda i, t: (i, 0, t)),
                   pl.BlockSpec((1, SSM_XBC, tm), lambda i, t: (i, 0, t)),
                   pl.BlockSpec((1, 2 * SSM_HEADS, tm), lambda i, t: (i, 0, t))],
        out_shape=[jax.ShapeDtypeStruct((b, SSM_D_INNER, s), BF16),
                   jax.ShapeDtypeStruct((b, SSM_XBC, s), BF16),
                   jax.ShapeDtypeStruct((b, 2 * SSM_HEADS, s), F32)],
        compiler_params=_cparams("parallel", "parallel"),
        name="ssm_in_proj",
    )(x, mod_l, g, wt, dtb)


def _ssm_conv_body(u_ref, up_ref, un_ref, w_ref, b_ref, xst_ref, ct_ref, bm_ref, *, tc):
    t = pl.program_id(1)
    nt = pl.num_programs(1)
    u = u_ref[0].astype(F32)
    prev = jnp.where(t > 0, up_ref[0][:, V7X_LANES - 1:V7X_LANES].astype(F32), 0.0)
    nxt = jnp.where(t < nt - 1, un_ref[0][:, 0:1].astype(F32), 0.0)
    lane = lax.broadcasted_iota(jnp.int32, (1, tc), 1)
    packed = pltpu.bitcast(u_ref[0], jnp.uint32)
    rolled_l = pltpu.bitcast(pltpu.roll(packed, 1, 1), BF16).astype(F32)
    rolled_r = pltpu.bitcast(pltpu.roll(packed, tc - 1, 1), BF16).astype(F32)
    left = jnp.where(lane == 0, prev, rolled_l)
    right = jnp.where(lane == tc - 1, nxt, rolled_r)
    def lanes(a):
        return jnp.concatenate([a] * (tc // V7X_LANES), axis=1)

    v = _silu(lanes(b_ref[...]) + lanes(w_ref[0]) * left + lanes(w_ref[1]) * u
              + lanes(w_ref[2]) * right)
    bn = SSM_GROUPS * SSM_STATE
    xst_ref[0] = v[0:SSM_D_INNER].astype(BF16)
    bm_ref[0] = v[SSM_D_INNER:SSM_D_INNER + bn].T.astype(BF16)
    ct_ref[0] = v[SSM_D_INNER + bn:SSM_XBC].astype(BF16)


def _ssm_conv(xbct, w, bias, *, tc):
    b, ch, s = xbct.shape
    tc = min(tc, s)
    hb = tc // V7X_LANES
    last = s // V7X_LANES - 1
    bn = SSM_GROUPS * SSM_STATE
    body = functools.partial(_ssm_conv_body, tc=tc)
    return pl.pallas_call(
        body,
        grid=(b, s // tc),
        in_specs=[pl.BlockSpec((1, ch, tc), lambda i, t: (i, 0, t)),
                  pl.BlockSpec((1, ch, V7X_LANES), lambda i, t: (i, 0, jnp.maximum(t * hb - 1, 0))),
                  pl.BlockSpec((1, ch, V7X_LANES), lambda i, t: (i, 0, jnp.minimum((t + 1) * hb, last))),
                  _resident((3, ch, V7X_LANES)),
                  _resident((ch, V7X_LANES))],
        out_specs=[pl.BlockSpec((1, SSM_D_INNER, tc), lambda i, t: (i, 0, t)),
                   pl.BlockSpec((1, bn, tc), lambda i, t: (i, 0, t)),
                   pl.BlockSpec((1, tc, bn), lambda i, t: (i, t, 0))],
        out_shape=[jax.ShapeDtypeStruct((b, SSM_D_INNER, s), BF16),
                   jax.ShapeDtypeStruct((b, bn, s), BF16),
                   jax.ShapeDtypeStruct((b, s, bn), BF16)],
        compiler_params=_cparams("parallel", "parallel"),
        name="ssm_conv_silu",
    )(xbct, xbct, xbct, w, bias)


def _dot_f32_by_01(a, m01):
    hi = a.astype(BF16)
    r1 = a - hi.astype(F32)
    mid = r1.astype(BF16)
    lo = (r1 - mid.astype(F32)).astype(BF16)
    return (jnp.dot(hi, m01, preferred_element_type=F32)
            + jnp.dot(mid, m01, preferred_element_type=F32)
            + jnp.dot(lo, m01, preferred_element_type=F32))


def _scan_order_mask(d, chunk):
    jrow = lax.broadcasted_iota(jnp.int32, (chunk, chunk), 0)
    icol = lax.broadcasted_iota(jnp.int32, (chunk, chunk), 1)
    sign = jnp.where(d == 0, 1, -1)
    return (icol - jrow) * sign >= 0


DECAY_ROWS = 4 * SSM_HEADS


def _ssd_decay_body(a_ref, dt_ref, fac_ref, src_ref, *, chunk):
    d = pl.program_id(1)
    nh = SSM_HEADS
    mask01 = _scan_order_mask(d, chunk).astype(BF16)
    for s in range(dt_ref.shape[2] // chunk):
        tok = slice(s * chunk, (s + 1) * chunk)
        dt = dt_ref[0, :, tok]
        cum_t = _dot_f32_by_01(dt * a_ref[0], mask01)
        cum2_t = cum_t * LOG2E
        fac_ref[0, 0, 0:nh, tok] = cum2_t
        src_ref[0, 0, tok, :] = (cum2_t - jnp.log2(dt)).T
        tot = jnp.where(d == 0, cum_t[:, chunk - 1:chunk], cum_t[:, 0:1])
        fac_ref[0, 0, nh:2 * nh, tok] = jnp.exp(tot - cum_t) * dt
        fac_ref[0, 0, 2 * nh:3 * nh, tok] = jnp.exp(cum_t)
        fac_ref[0, 0, 3 * nh:4 * nh, tok] = jnp.broadcast_to(jnp.exp(tot), cum_t.shape)


def _ssd_decay(a, dtt, chunk):
    b, _, s = dtt.shape
    span = min(s, 8 * chunk)
    body = functools.partial(_ssd_decay_body, chunk=chunk)
    return pl.pallas_call(
        body,
        grid=(b, 2, s // span),
        in_specs=[pl.BlockSpec((1, SSM_HEADS, 1), lambda i, d, g: (d, 0, 0)),
                  pl.BlockSpec((1, SSM_HEADS, span), lambda i, d, g: (i, d, g))],
        out_specs=[pl.BlockSpec((1, 1, DECAY_ROWS, span), lambda i, d, g: (i, d, 0, g)),
                   pl.BlockSpec((1, 1, span, SSM_HEADS), lambda i, d, g: (i, d, g, 0))],
        out_shape=[jax.ShapeDtypeStruct((b, 2, DECAY_ROWS, s), F32),
                   jax.ShapeDtypeStruct((b, 2, s, SSM_HEADS), F32)],
        compiler_params=_cparams("parallel", "parallel", "parallel"),
        name="ssd_decay_factors",
    )(a, dtt)


def _scan_body(fac_ref, src_ref, xs_ref, ct_ref, b_ref, s0_ref, y_ref, sout_ref, state, *, chunk):
    d = pl.program_id(1)
    c = pl.program_id(2)
    nc = pl.num_programs(2)
    hp = SSM_HEADS_PER_GROUP
    hd = SSM_D_INNER // SSM_HEADS
    gw = hp * hd
    nh = SSM_HEADS
    nsub = xs_ref.shape[2] // chunk

    @pl.when(c == 0)
    def _load_state():
        state[...] = s0_ref[0, 0]

    blk = V7X_LANES
    nblk = chunk // blk

    def decay_weights(direction, cbt, cum2_row, src_col, not_yet_diag):
        rows = []
        for jb in range(nblk):
            cols = []
            for ib in range(nblk):
                ahead = ib - jb if direction == 0 else jb - ib
                if ahead < 0:
                    cols.append(jnp.zeros((blk, blk), BF16))
                    continue
                js, is_ = slice(jb * blk, (jb + 1) * blk), slice(ib * blk, (ib + 1) * blk)
                seg = cum2_row[:, is_] - src_col[js, :]
                if ahead == 0:
                    seg = seg + not_yet_diag
                cols.append((cbt[js, is_] * jnp.exp2(seg)).astype(BF16))
            rows.append(jnp.concatenate(cols, axis=1))
        return jnp.concatenate(rows, axis=0)

    def one_chunk(direction, tok):
        not_yet_diag = jnp.where(_scan_order_mask(direction, blk), 0.0, -jnp.inf)
        cum2_t = fac_ref[0, 0, 0:nh, tok]
        src_term = src_ref[0, 0, tok, :]
        to_end = fac_ref[0, 0, nh:2 * nh, tok]
        ecum = fac_ref[0, 0, 2 * nh:3 * nh, tok]
        etot = fac_ref[0, 0, 3 * nh:4 * nh, tok.start:tok.start + 1]
        for g in range(SSM_GROUPS):
            bg = b_ref[0, tok, g * SSM_STATE:(g + 1) * SSM_STATE]
            ctg = ct_ref[0, g * SSM_STATE:(g + 1) * SSM_STATE, tok]
            cbt = jnp.dot(bg, ctg, preferred_element_type=F32)
            xg = xs_ref[0, g * gw:(g + 1) * gw, tok]
            sg = state[g * gw:(g + 1) * gw, :]
            hs = slice(g * hp, (g + 1) * hp)
            y_state = (jnp.dot(sg.astype(BF16), ctg, preferred_element_type=F32)
                       .reshape(hp, hd, chunk) * ecum[hs][:, None, :])
            outs = []
            for r in range(hp):
                h = g * hp + r
                w = decay_weights(direction, cbt, cum2_t[h:h + 1, :], src_term[:, h:h + 1],
                                  not_yet_diag)
                outs.append(jnp.dot(xg[r * hd:(r + 1) * hd], w, preferred_element_type=F32)
                            + y_state[r])
            y_ref[0, 0, g * gw:(g + 1) * gw, tok] = jnp.concatenate(outs, axis=0).astype(BF16)
            xw = ((xg.astype(F32).reshape(hp, hd, chunk) * to_end[hs][:, None, :])
                  .reshape(gw, chunk).astype(BF16))
            upd = jnp.dot(xw, bg, preferred_element_type=F32)
            decayed = (sg.reshape(hp, hd, SSM_STATE) * etot[hs][:, None, :]).reshape(gw, SSM_STATE)
            state[g * gw:(g + 1) * gw, :] = decayed + upd

    for direction in (0, 1):
        @pl.when(d == direction)
        def _walk(direction=direction):
            order = range(nsub) if direction == 0 else reversed(range(nsub))
            for sub in order:
                one_chunk(direction, slice(sub * chunk, (sub + 1) * chunk))

    @pl.when(c == nc - 1)
    def _store_state():
        sout_ref[0, 0] = state[...]


def _ssd_scan(a, dtt, xst, ct, bm, s0):
    b, ch, s = xst.shape
    chunk = min(SSD_CHUNK, s)
    span = min(s, SSD_CHUNKS_PER_STEP * chunk)
    nc = s // span
    bn = SSM_GROUPS * SSM_STATE
    fac, src = _ssd_decay(a, dtt, chunk)

    def cidx(d, c):
        return c + d * (nc - 1 - 2 * c)

    body = functools.partial(_scan_body, chunk=chunk)
    return pl.pallas_call(
        body,
        grid=(b, 2, nc),
        in_specs=[pl.BlockSpec((1, 1, DECAY_ROWS, span), lambda i, d, c: (i, d, 0, cidx(d, c))),
                  pl.BlockSpec((1, 1, span, SSM_HEADS), lambda i, d, c: (i, d, cidx(d, c), 0)),
                  pl.BlockSpec((1, ch, span), lambda i, d, c: (i, 0, cidx(d, c))),
                  pl.BlockSpec((1, bn, span), lambda i, d, c: (i, 0, cidx(d, c))),
                  pl.BlockSpec((1, span, bn), lambda i, d, c: (i, cidx(d, c), 0)),
                  pl.BlockSpec((1, 1, ch, SSM_STATE), lambda i, d, c: (d, i, 0, 0))],
        out_specs=[pl.BlockSpec((1, 1, ch, span), lambda i, d, c: (d, i, 0, cidx(d, c))),
                   pl.BlockSpec((1, 1, ch, SSM_STATE), lambda i, d, c: (d, i, 0, 0))],
        out_shape=[jax.ShapeDtypeStruct((2, b, ch, s), BF16),
                   jax.ShapeDtypeStruct((2, b, ch, SSM_STATE), F32)],
        scratch_shapes=[pltpu.VMEM((ch, SSM_STATE), F32)],
        compiler_params=_cparams("parallel", "parallel", "arbitrary"),
        name="ssd_scan",
    )(fac, src, xst, ct, bm, s0)


def _ssm_out_body(x_ref, mod_ref, yf_ref, yb_ref, xs_ref, z_ref, dsk_ref, ng_ref, wt_ref, o_ref):
    y = (yf_ref[0, 0].astype(F32) + yb_ref[0, 0].astype(F32)
         + dsk_ref[...] * xs_ref[0].astype(F32))
    y = y * _silu(z_ref[0].astype(F32))
    ms = jnp.mean(y * y, axis=0, keepdims=True)
    yn = (y * lax.rsqrt(ms + NORM_EPS) * ng_ref[...]).astype(BF16)
    ot = jnp.dot(wt_ref[...], yn, preferred_element_type=F32)
    o_ref[0] = x_ref[0] + mod_ref[0][2:3] * ot.T


def _ssm_out(x, mod_l, mod_row, y, xst, zt, dsk, ng, wt, *, tm):
    b, s, d = x.shape
    tm = min(tm, s)
    ch = xst.shape[1]
    return pl.pallas_call(
        _ssm_out_body,
        grid=(b, s // tm),
        in_specs=[pl.BlockSpec((1, tm, d), lambda i, t: (i, t, 0)),
                  pl.BlockSpec((1, 6, d), lambda i, t: (mod_row(i), 0, 0)),
                  pl.BlockSpec((1, 1, ch, tm), lambda i, t: (0, i, 0, t)),
                  pl.BlockSpec((1, 1, ch, tm), lambda i, t: (1, i, 0, t)),
                  pl.BlockSpec((1, ch, tm), lambda i, t: (i, 0, t)),
                  pl.BlockSpec((1, ch, tm), lambda i, t: (i, 0, t)),
                  _resident((ch, 1)),
                  _resident((ch, 1)),
                  _resident((d, ch))],
        out_specs=pl.BlockSpec((1, tm, d), lambda i, t: (i, t, 0)),
        out_shape=jax.ShapeDtypeStruct((b, s, d), F32),
        compiler_params=_cparams("parallel", "parallel"),
        name="ssm_out_residual",
    )(x, mod_l, y, y, xst, zt, dsk, ng, wt)


def _rope_tables_t(n):
    t = jnp.arange(n)
    inv_freq = 1.0 / (ROPE_BASE ** (jnp.arange(ROPE_PAIRS, dtype=F32) / ROPE_PAIRS))
    ang_r = (t // GRID_W).astype(F32)[None, :] * inv_freq[:, None]
    ang_c = (t % GRID_W).astype(F32)[None, :] * inv_freq[:, None]
    cr, sr, cc, sc = jnp.cos(ang_r), jnp.sin(ang_r), jnp.cos(ang_c), jnp.sin(ang_c)
    return (jnp.concatenate([cr, cr, cc, cc], axis=0),
            jnp.concatenate([-sr, sr, -sc, sc], axis=0))


def _attn_layer(x, ctx, mod_l, lat_row, ctx_row, p, rope, lambda_init, with_ctx):
    wt = p["w_in"].T.astype(BF16)
    g = p["norm_g"].reshape(1, D_MODEL)
    qg = p["q_norm_g"].reshape(HEAD_DIM, 1)
    kg = p["k_norm_g"].reshape(HEAD_DIM, 1)
    cos_t, sin_t = rope
    c = ctx.shape[1]
    qt, *lat_kv = _attn_in(x, mod_l, lat_row, g, wt, cos_t, sin_t, qg, kg, rope=True, tm=512)
    qtc, *ctx_kv = _attn_in(ctx, mod_l, ctx_row, g, wt, cos_t[:, :c], sin_t[:, :c], qg, kg,
                            rope=False, tm=256)
    lam_vecs = [p[k].reshape(1, HEAD_DIM) for k in ("lq1", "lk1", "lq2", "lk2")]
    sg = p["subln_g"].reshape(V_ROWS, 1)
    w_out = p["w_out"].astype(BF16)
    x = _flash(x, mod_l, lat_row, w_out, lam_vecs, sg, qt, ctx_kv, lat_kv,
               lambda_init=lambda_init, tq=256, tk=2048)
    if with_ctx:
        ctx = _flash(ctx, mod_l, ctx_row, w_out, lam_vecs, sg, qtc, ctx_kv, None,
                     lambda_init=lambda_init, tq=256, tk=0)
    return x, ctx


def _ssm_layer(x, ctx, mod_l, lat_row, ctx_row, p, with_ctx):
    wt = p["w_in"].T.astype(BF16)
    g = p["norm_g"].reshape(1, D_MODEL)
    dtb = p["dt_bias"].reshape(2 * SSM_HEADS, 1)
    conv_w = jnp.broadcast_to(p["conv_w"][:, :, None], (3, SSM_XBC, V7X_LANES))
    conv_b = jnp.broadcast_to(p["conv_b"][:, None], (SSM_XBC, V7X_LANES))
    a = (-jnp.exp(p["a_log"].astype(F32))).reshape(2, SSM_HEADS, 1)
    b = x.shape[0]

    def pre(v, row, tm):
        zt, xbct, dtt = _ssm_proj(v, mod_l, row, g, wt, dtb, tm=tm)
        xst, ct, bm = _ssm_conv(xbct, conv_w, conv_b, tc=512)
        return zt, xst, ct, bm, dtt

    zt_c, xst_c, ct_c, bm_c, dtt_c = pre(ctx, ctx_row, 256)
    zt_l, xst_l, ct_l, bm_l, dtt_l = pre(x, lat_row, 512)
    zero = jnp.zeros((2, b, SSM_D_INNER, SSM_STATE), F32)
    y_c, s_ctx = _ssd_scan(a, dtt_c, xst_c, ct_c, bm_c, zero)
    y_l, _ = _ssd_scan(a, dtt_l, xst_l, ct_l, bm_l, s_ctx)
    dsk = jnp.repeat(p["d_skip"], SSM_D_INNER // SSM_HEADS).reshape(SSM_D_INNER, 1)
    ng = p["out_norm_g"].reshape(SSM_D_INNER, 1)
    w_out_t = p["w_out"].T.astype(BF16)
    x = _ssm_out(x, mod_l, lat_row, y_l, xst_l, zt_l, dsk, ng, w_out_t, tm=512)
    if with_ctx:
        ctx = _ssm_out(ctx, mod_l, ctx_row, y_c, xst_c, zt_c, dsk, ng, w_out_t, tm=256)
    return x, ctx


def kernel(x, c, ctx, c_ctx, mod_w, mod_b, norm_mix_g, norm_ffn_g, attn_w_in, attn_w_out,
           diff_lq1, diff_lk1, diff_lq2, diff_lk2, diff_subln_g, gqa_q_norm_g, gqa_k_norm_g,
           ssm_w_in, ssm_conv_w, ssm_conv_b, ssm_dt_bias, ssm_a_log, ssm_d, ssm_norm_g, ssm_w_out,
           ffn_w_in, ffn_conv_w, ffn_conv_b, ffn_w_out, final_norm_g):
    b, n, d = x.shape
    mod_rows = 16
    c_rows = jnp.zeros((mod_rows, d), F32).at[:b].set(c).at[b].set(c_ctx)
    mod = _mod_all(c_rows, mod_w, mod_b).reshape(DEPTH, mod_rows, 6, d)
    lat_row = lambda i: i
    ctx_row = lambda i: b
    rope = _rope_tables_t(n)

    for layer in range(DEPTH):
        with_ctx = layer < DEPTH - 1
        mod_l = mod[layer]
        i = layer // 2
        if layer % 2 == 0:
            p = dict(w_in=attn_w_in[i], w_out=attn_w_out[i], norm_g=norm_mix_g[layer],
                     lq1=diff_lq1[i], lk1=diff_lk1[i], lq2=diff_lq2[i], lk2=diff_lk2[i],
                     subln_g=diff_subln_g[i], q_norm_g=gqa_q_norm_g[i], k_norm_g=gqa_k_norm_g[i])
            lambda_init = 0.8 - 0.6 * math.exp(-0.3 * layer)
            x, ctx = _attn_layer(x, ctx, mod_l, lat_row, ctx_row, p, rope, lambda_init, with_ctx)
        else:
            p = dict(w_in=ssm_w_in[i], norm_g=norm_mix_g[layer], conv_w=ssm_conv_w[i],
                     conv_b=ssm_conv_b[i], dt_bias=ssm_dt_bias[i], a_log=ssm_a_log[i],
                     d_skip=ssm_d[i], out_norm_g=ssm_norm_g[i], w_out=ssm_w_out[i])
            x, ctx = _ssm_layer(x, ctx, mod_l, lat_row, ctx_row, p, with_ctx)
        g = norm_ffn_g[layer].reshape(1, d)
        wv = ffn_w_in[layer][:, :D_FF].astype(BF16)
        wg = ffn_w_in[layer][:, D_FF:].astype(BF16)
        cw = ffn_conv_w[layer]
        cb = ffn_conv_b[layer].reshape(1, D_FF)
        wo = ffn_w_out[layer].astype(BF16)
        fg = final_norm_g.reshape(1, d)
        x = _ffn(x, mod_l, lat_row, g, wv, wg, cw, cb, wo, fg, tm=256,
                 final_norm=layer == DEPTH - 1)
        if with_ctx:
            ctx = _ffn(ctx, mod_l, ctx_row, g, wv, wg, cw, cb, wo, fg, tm=256, final_norm=False)
    return x
```

```python
import functools
import math

import jax
import jax.numpy as jnp
from jax import lax
from jax.experimental import pallas as pl
from jax.experimental.pallas import tpu as pltpu

F32 = jnp.float32
BF16 = jnp.bfloat16

D_MODEL = 1024
DEPTH = 4
GRID_W = 64
HEAD_DIM = 64
ROPE_PAIRS = HEAD_DIM // 4
ROPE_BASE = 10000.0
NORM_EPS = 1e-6
DIFF_HEADS = 4
GQA_HEADS = 8
GQA_KV_HEADS = 2
GQA_GROUP = GQA_HEADS // GQA_KV_HEADS
ATTN_IN_COLS = 2304
SSM_D_INNER = 2048
SSM_HEADS = 32
SSM_GROUPS = 4
SSM_HEADS_PER_GROUP = SSM_HEADS // SSM_GROUPS
SSM_STATE = 128
SSM_XBC = 3072
SSM_IN_COLS = 5184
D_FF = 2816
LOG2E = math.log2(math.e)

V7X_VMEM_BYTES = 64 * 1024 * 1024
VMEM_LIMIT_BYTES = V7X_VMEM_BYTES - 8 * 1024 * 1024
V7X_LANES = 128
BF16_SUBLANES = 16

V_ROWS = 2 * HEAD_DIM
V_ROWS_PADDED = V_ROWS + BF16_SUBLANES
GV_ROWS_PADDED = HEAD_DIM + BF16_SUBLANES
K_COLS = (DIFF_HEADS + 1) * 2 * HEAD_DIM

ATTN_KEY_SUBBLOCK = 256
ATTN_STREAM_MAX_OCTAVES = 64.0

CONV_HALO = 16
SSD_CHUNK = 256
SSD_CHUNKS_PER_STEP = 2

TOKEN_TILE = 512
FFN_TOKEN_TILE = 256
ATTN_QUERY_TILE = 256
ATTN_KEY_CHUNK = 2048
MOD_COL_TILE = 2048

NT_DIMS = (((1,), (1,)), ((), ()))


def _cparams(*sem):
    return pltpu.CompilerParams(dimension_semantics=sem, vmem_limit_bytes=VMEM_LIMIT_BYTES)


def _resident(shape):
    nd = len(shape)
    return pl.BlockSpec(shape, lambda *_: (0,) * nd, pipeline_mode=pl.Buffered(1))


def _silu(v):
    return v * jax.nn.sigmoid(v)


def _modulate(x, g, shift, scale):
    ms = jnp.mean(x * x, axis=-1, keepdims=True)
    return (x * lax.rsqrt(ms + NORM_EPS) * g) * (1.0 + scale) + shift


def _mod_body(c_ref, w_ref, b_ref, o_ref):
    s = _silu(c_ref[...])
    o_ref[0] = jnp.dot(s, w_ref[0], preferred_element_type=F32,
                       precision=lax.Precision.HIGHEST) + b_ref[0]


def _mod_all(c_rows, mod_w, mod_b):
    rows = c_rows.shape[0]
    depth, d, cols = mod_w.shape
    tn = MOD_COL_TILE
    return pl.pallas_call(
        _mod_body,
        grid=(depth, cols // tn),
        in_specs=[pl.BlockSpec((rows, d), lambda l, n: (0, 0)),
                  pl.BlockSpec((1, d, tn), lambda l, n: (l, 0, n)),
                  pl.BlockSpec((1, 1, tn), lambda l, n: (l, 0, n))],
        out_specs=pl.BlockSpec((1, rows, tn), lambda l, n: (l, 0, n)),
        out_shape=jax.ShapeDtypeStruct((depth, rows, cols), F32),
        compiler_params=_cparams("parallel", "parallel"),
        name="mod_vectors",
    )(c_rows, mod_w, mod_b.reshape(depth, 1, cols))


def _attn_in_body(x_ref, mod_ref, g_ref, wt_ref, cos_ref, sin_ref, qg_ref, kg_ref,
                  qt_ref, k_ref, vta_ref, vtb_ref, *, rope):
    m = mod_ref[0]
    h = _modulate(x_ref[0], g_ref[...], m[0:1], m[1:2]).astype(BF16)
    t = lax.dot_general(wt_ref[...], h, NT_DIMS, preferred_element_type=F32)
    tm = t.shape[1]

    def rot(u):
        if not rope:
            return u
        sw = jnp.concatenate([u[:, 16:32], u[:, 0:16], u[:, 48:64], u[:, 32:48]], axis=1)
        return u * cos_ref[...][None] + sw * sin_ref[...][None]

    def qk_norm(u, g):
        ms = jnp.mean(u * u, axis=1, keepdims=True)
        return u * lax.rsqrt(ms + NORM_EPS) * g[None]

    nq = 2 * DIFF_HEADS
    qa = rot(t[0:512].reshape(nq, HEAD_DIM, tm))
    ka = rot(t[512:1024].reshape(nq, HEAD_DIM, tm))
    va = t[1024:1536]
    qb = rot(qk_norm(t[1536:2048].reshape(GQA_HEADS, HEAD_DIM, tm), qg_ref[...]))
    kb = rot(qk_norm(t[2048:2176].reshape(GQA_KV_HEADS, HEAD_DIM, tm), kg_ref[...]))
    vb = t[2176:2304]

    qs = (HEAD_DIM ** -0.5) * LOG2E
    qt_ref[0, 0:512] = (qa * qs).reshape(512, tm).astype(BF16)
    qt_ref[0, 512:1024] = (qb * qs).reshape(512, tm).astype(BF16)
    kt = jnp.concatenate([ka.reshape(512, tm), kb.reshape(128, tm)], axis=0)
    k_ref[0] = kt.T.astype(BF16)
    ones = jnp.ones((BF16_SUBLANES, tm), BF16)
    for u in range(DIFF_HEADS):
        vta_ref[0, u, 0:V_ROWS] = va[u * V_ROWS:(u + 1) * V_ROWS].astype(BF16)
        vta_ref[0, u, V_ROWS:V_ROWS_PADDED] = ones
    for g in range(GQA_KV_HEADS):
        vtb_ref[0, g, 0:HEAD_DIM] = vb[g * HEAD_DIM:(g + 1) * HEAD_DIM].astype(BF16)
        vtb_ref[0, g, HEAD_DIM:GV_ROWS_PADDED] = ones


def _attn_in(x, mod_l, mod_row, g, wt, cos_t, sin_t, qg, kg, *, rope):
    b, s, d = x.shape
    tm = min(TOKEN_TILE, s)
    body = functools.partial(_attn_in_body, rope=rope)
    return pl.pallas_call(
        body,
        grid=(b, s // tm),
        in_specs=[pl.BlockSpec((1, tm, d), lambda i, t: (i, t, 0)),
                  pl.BlockSpec((1, 6, d), lambda i, t: (mod_row(i), 0, 0)),
                  _resident((1, d)),
                  _resident((ATTN_IN_COLS, d)),
                  pl.BlockSpec((HEAD_DIM, tm), lambda i, t: (0, t)),
                  pl.BlockSpec((HEAD_DIM, tm), lambda i, t: (0, t)),
                  _resident((HEAD_DIM, 1)),
                  _resident((HEAD_DIM, 1))],
        out_specs=[pl.BlockSpec((1, 1024, tm), lambda i, t: (i, 0, t)),
                   pl.BlockSpec((1, tm, K_COLS), lambda i, t: (i, t, 0)),
                   pl.BlockSpec((1, DIFF_HEADS, V_ROWS_PADDED, tm), lambda i, t: (i, 0, 0, t)),
                   pl.BlockSpec((1, GQA_KV_HEADS, GV_ROWS_PADDED, tm), lambda i, t: (i, 0, 0, t))],
        out_shape=[jax.ShapeDtypeStruct((b, 1024, s), BF16),
                   jax.ShapeDtypeStruct((b, s, K_COLS), BF16),
                   jax.ShapeDtypeStruct((b, DIFF_HEADS, V_ROWS_PADDED, s), BF16),
                   jax.ShapeDtypeStruct((b, GQA_KV_HEADS, GV_ROWS_PADDED, s), BF16)],
        compiler_params=_cparams("parallel", "parallel"),
        name="attn_in_rope" if rope else "attn_in_ctx",
    )(x, mod_l, g, wt, cos_t, sin_t, qg, kg)


def _attn_unit_cols(tq):
    units = []
    for h in range(DIFF_HEADS):
        units.append((h * 128, h, V_ROWS_PADDED, h * 2 * tq, 2 * tq))
    base = DIFF_HEADS * 2 * tq
    for g in range(GQA_KV_HEADS):
        units.append((DIFF_HEADS * 128, g, GV_ROWS_PADDED, base + g * GQA_GROUP * tq,
                      GQA_GROUP * tq))
    return units


def _flash_body(*refs, tq, lambda_init, has_lat):
    if has_lat:
        (lq1, lk1, lq2, lk2, sg_ref, x_ref, mod_ref, wo_ref, qt_ref, kc_ref, vtac_ref, vtbc_ref,
         kl_ref, vtal_ref, vtbl_ref, o_ref, rhs, acc, mrow, pv_new, m_chunk) = refs
    else:
        (lq1, lk1, lq2, lk2, sg_ref, x_ref, mod_ref, wo_ref, qt_ref, kc_ref, vtac_ref, vtbc_ref,
         o_ref, rhs, acc, mrow, pv_new, m_chunk) = refs
    j = pl.program_id(2)
    nj = pl.num_programs(2)
    units = _attn_unit_cols(tq)
    diff_cols = DIFF_HEADS * 2 * tq

    def values(vta_ref, vtb_ref, u, vu):
        return vta_ref[0, vu] if u < DIFF_HEADS else vtb_ref[0, vu]

    def exact_step(k_ref, vta_ref, vtb_ref):
        for u, (kc0, vu, vr, c0, w) in enumerate(units):
            s = jnp.dot(k_ref[0, :, kc0:kc0 + 128], rhs[:, c0:c0 + w],
                        preferred_element_type=F32)
            mp = mrow[:, c0:c0 + w]
            mn = jnp.maximum(mp, jnp.max(s, axis=0, keepdims=True))
            alpha = jnp.exp2(mp - mn)
            p = jnp.exp2(s - mn).astype(BF16)
            pv = jnp.dot(values(vta_ref, vtb_ref, u, vu), p, preferred_element_type=F32)
            acc[0:vr, c0:c0 + w] = acc[0:vr, c0:c0 + w] * alpha + pv
            mrow[:, c0:c0 + w] = mn

    def streaming_step(k_ref, vta_ref, vtb_ref):
        nk = k_ref.shape[1]
        sub = min(nk, ATTN_KEY_SUBBLOCK)
        for u, (kc0, vu, vr, c0, w) in enumerate(units):
            m_used = mrow[:, c0:c0 + w]
            cm = None
            parts = []
            for r in range(nk // sub):
                s = jnp.dot(k_ref[0, r * sub:(r + 1) * sub, kc0:kc0 + 128], rhs[:, c0:c0 + w],
                            preferred_element_type=F32)
                parts.append(jnp.exp2(s - m_used).astype(BF16))
                sm = jnp.max(s.reshape(sub // 8, 8, w), axis=0)
                cm = sm if cm is None else jnp.maximum(cm, sm)
            p = jnp.concatenate(parts, axis=0)
            pv_new[0:vr, c0:c0 + w] = jnp.dot(values(vta_ref, vtb_ref, u, vu), p,
                                              preferred_element_type=F32)
            m_chunk[:, c0:c0 + w] = jnp.max(cm, axis=0, keepdims=True)

    @pl.when(j == 0)
    def _init():
        zeros = jnp.zeros((HEAD_DIM, tq), BF16)
        for h in range(DIFF_HEADS):
            c0 = h * 2 * tq
            rhs[0:64, c0:c0 + tq] = qt_ref[0, h * 128:h * 128 + 64, :]
            rhs[64:128, c0:c0 + tq] = zeros
            rhs[0:64, c0 + tq:c0 + 2 * tq] = zeros
            rhs[64:128, c0 + tq:c0 + 2 * tq] = qt_ref[0, h * 128 + 64:h * 128 + 128, :]
        base = DIFF_HEADS * 2 * tq
        for g in range(GQA_KV_HEADS):
            for r in range(GQA_GROUP):
                c0 = base + (g * GQA_GROUP + r) * tq
                hd = 512 + (g * GQA_GROUP + r) * HEAD_DIM
                rhs[g * 64:(g + 1) * 64, c0:c0 + tq] = qt_ref[0, hd:hd + HEAD_DIM, :]
                rhs[(1 - g) * 64:(2 - g) * 64, c0:c0 + tq] = zeros
        acc[...] = jnp.zeros(acc.shape, F32)
        mrow[...] = jnp.zeros(mrow.shape, F32)

    row_blocks = ((V_ROWS_PADDED, slice(0, diff_cols)),
                  (GV_ROWS_PADDED, slice(diff_cols, acc.shape[1])))

    def nothing_pending():
        for rows, cols in row_blocks:
            pv_new[0:rows, cols] = jnp.zeros((rows, cols.stop - cols.start), F32)
        m_chunk[...] = jnp.full(m_chunk.shape, -jnp.inf, F32)

    def commit():
        mp = mrow[...]
        mn = jnp.maximum(mp, m_chunk[...])
        mrow[...] = mn
        alpha = jnp.exp2(mp - mn)
        for rows, cols in row_blocks:
            acc[0:rows, cols] = (acc[0:rows, cols] + pv_new[0:rows, cols]) * alpha[:, cols]

    def stream_chunk(k_ref, vta_ref, vtb_ref, first):
        streaming_step(k_ref, vta_ref, vtb_ref)
        dev = m_chunk[...] - mrow[...]
        in_range = jnp.max(jnp.abs(dev) if first else dev) <= ATTN_STREAM_MAX_OCTAVES

        @pl.when(jnp.logical_not(in_range))
        def _redo():
            if first:
                mrow[...] = jnp.full(mrow.shape, -jnp.inf, F32)
            exact_step(k_ref, vta_ref, vtb_ref)
            nothing_pending()

    @pl.when(j == 0)
    def _context_keys():
        stream_chunk(kc_ref, vtac_ref, vtbc_ref, True)

    if has_lat:
        commit()
        stream_chunk(kl_ref, vtal_ref, vtbl_ref, False)

    @pl.when(j == nj - 1)
    def _finish():
        commit()
        lam = (jnp.exp(jnp.sum(lq1[...] * lk1[...], keepdims=True))
               - jnp.exp(jnp.sum(lq2[...] * lk2[...], keepdims=True)) + lambda_init)
        pieces = []
        for h in range(DIFF_HEADS):
            c0 = h * 2 * tq
            o1 = acc[0:V_ROWS, c0:c0 + tq] / acc[V_ROWS:V_ROWS + 1, c0:c0 + tq]
            o2 = acc[0:V_ROWS, c0 + tq:c0 + 2 * tq] / acc[V_ROWS:V_ROWS + 1, c0 + tq:c0 + 2 * tq]
            oh = o1 - lam * o2
            ms = jnp.mean(oh * oh, axis=0, keepdims=True)
            pieces.append(oh * lax.rsqrt(ms + NORM_EPS) * sg_ref[...] * (1.0 - lambda_init))
        base = DIFF_HEADS * 2 * tq
        for g in range(GQA_KV_HEADS):
            for r in range(GQA_GROUP):
                c0 = base + (g * GQA_GROUP + r) * tq
                pieces.append(acc[0:HEAD_DIM, c0:c0 + tq] / acc[HEAD_DIM:HEAD_DIM + 1, c0:c0 + tq])
        o = jnp.concatenate(pieces, axis=0).T.astype(BF16)
        y = jnp.dot(o, wo_ref[...], preferred_element_type=F32)
        o_ref[0] = x_ref[0] + mod_ref[0][2:3] * y


def _flash(x, mod_l, mod_row, w_out, lam_vecs, sg, qt, ctx_kv, lat_kv, *, lambda_init):
    b, _, sq = qt.shape
    d = x.shape[2]
    kc, vtac, vtbc = ctx_kv
    c = kc.shape[1]
    has_lat = lat_kv is not None
    tq = min(ATTN_QUERY_TILE, sq)
    ncols = (DIFF_HEADS * 2 + GQA_HEADS) * tq
    in_specs = [_resident((1, HEAD_DIM))] * 4 + [
        _resident((V_ROWS, 1)),
        pl.BlockSpec((1, tq, d), lambda i, q, j: (i, q, 0)),
        pl.BlockSpec((1, 6, d), lambda i, q, j: (mod_row(i), 0, 0)),
        _resident((1024, d)),
        pl.BlockSpec((1, 1024, tq), lambda i, q, j: (i, 0, q)),
        pl.BlockSpec((1, c, K_COLS), lambda i, q, j: (i, 0, 0)),
        pl.BlockSpec((1, DIFF_HEADS, V_ROWS_PADDED, c), lambda i, q, j: (i, 0, 0, 0)),
        pl.BlockSpec((1, GQA_KV_HEADS, GV_ROWS_PADDED, c), lambda i, q, j: (i, 0, 0, 0)),
    ]
    args = list(lam_vecs) + [sg, x, mod_l, w_out, qt, kc, vtac, vtbc]
    nkv = 1
    if has_lat:
        kl, vtal, vtbl = lat_kv
        n = kl.shape[1]
        tk = min(ATTN_KEY_CHUNK, n)
        nkv = n // tk
        in_specs += [pl.BlockSpec((1, tk, K_COLS), lambda i, q, j: (i, j, 0)),
                     pl.BlockSpec((1, DIFF_HEADS, V_ROWS_PADDED, tk), lambda i, q, j: (i, 0, 0, j)),
                     pl.BlockSpec((1, GQA_KV_HEADS, GV_ROWS_PADDED, tk), lambda i, q, j: (i, 0, 0, j))]
        args += [kl, vtal, vtbl]
    scratch = [pltpu.VMEM((2 * HEAD_DIM, ncols), BF16),
               pltpu.VMEM((V_ROWS_PADDED, ncols), F32),
               pltpu.VMEM((1, ncols), F32),
               pltpu.VMEM((V_ROWS_PADDED, ncols), F32),
               pltpu.VMEM((1, ncols), F32)]
    body = functools.partial(_flash_body, tq=tq, lambda_init=lambda_init, has_lat=has_lat)
    return pl.pallas_call(
        body,
        grid=(b, sq // tq, nkv),
        in_specs=in_specs,
        out_specs=pl.BlockSpec((1, tq, d), lambda i, q, j: (i, q, 0)),
        out_shape=jax.ShapeDtypeStruct((b, sq, d), F32),
        scratch_shapes=scratch,
        compiler_params=_cparams("parallel", "parallel", "arbitrary"),
        name="attn_sweep_lat" if has_lat else "attn_sweep_ctx",
    )(*args)


def _ffn_body(x_ref, xp_ref, xn_ref, mod_ref, g_ref, wv_ref, wg_ref, cw_ref, cb_ref, wo_ref,
              fg_ref, o_ref, *, tm, final_norm):
    t = pl.program_id(1)
    nt = pl.num_programs(1)
    m = mod_ref[0]
    x = x_ref[0]
    xe = jnp.concatenate([xp_ref[0], x, xn_ref[0]], axis=0)
    he = _modulate(xe, g_ref[...], m[3:4], m[4:5]).astype(BF16)
    ge = jnp.dot(he, wg_ref[...], preferred_element_type=F32)
    rows = lax.broadcasted_iota(jnp.int32, (tm + 2 * CONV_HALO, 1), 0)
    lo = jnp.where(t > 0, 0, CONV_HALO)
    hi = jnp.where(t < nt - 1, tm + 2 * CONV_HALO, tm + CONV_HALO)
    ge = jnp.where((rows >= lo) & (rows < hi), ge, 0.0)
    val = jnp.dot(he[CONV_HALO:CONV_HALO + tm], wv_ref[...], preferred_element_type=F32)
    cw = cw_ref[...]
    h0 = CONV_HALO
    conv = (cb_ref[...] + cw[0:1] * ge[h0 - 1:h0 - 1 + tm] + cw[1:2] * ge[h0:h0 + tm]
            + cw[2:3] * ge[h0 + 1:h0 + 1 + tm])
    gelu = 0.5 * conv * (1.0 + lax.erf(conv * math.sqrt(0.5)))
    act = (gelu * val).astype(BF16)
    y = jnp.dot(act, wo_ref[...], preferred_element_type=F32)
    out = x + m[5:6] * y
    if final_norm:
        ms = jnp.mean(out * out, axis=-1, keepdims=True)
        out = out * lax.rsqrt(ms + NORM_EPS) * fg_ref[...]
    o_ref[0] = out


def _ffn(x, mod_l, mod_row, g, wv, wg, cw, cb, wo, fg, *, final_norm):
    b, s, d = x.shape
    tm = min(FFN_TOKEN_TILE, s)
    hb = tm // CONV_HALO
    last = s // CONV_HALO - 1
    body = functools.partial(_ffn_body, tm=tm, final_norm=final_norm)
    return pl.pallas_call(
        body,
        grid=(b, s // tm),
        in_specs=[pl.BlockSpec((1, tm, d), lambda i, t: (i, t, 0)),
                  pl.BlockSpec((1, CONV_HALO, d), lambda i, t: (i, jnp.maximum(t * hb - 1, 0), 0)),
                  pl.BlockSpec((1, CONV_HALO, d), lambda i, t: (i, jnp.minimum((t + 1) * hb, last), 0)),
                  pl.BlockSpec((1, 6, d), lambda i, t: (mod_row(i), 0, 0)),
                  _resident((1, d)),
                  _resident((d, D_FF)),
                  _resident((d, D_FF)),
                  _resident((3, D_FF)),
                  _resident((1, D_FF)),
                  _resident((D_FF, d)),
                  _resident((1, d))],
        out_specs=pl.BlockSpec((1, tm, d), lambda i, t: (i, t, 0)),
        out_shape=jax.ShapeDtypeStruct((b, s, d), F32),
        compiler_params=_cparams("parallel", "parallel"),
        name="conv_glu_ffn",
    )(x, x, x, mod_l, g, wv, wg, cw, cb, wo, fg)


def _ssm_proj_body(x_ref, mod_ref, g_ref, wt_ref, dtb_ref, zt_ref, xbct_ref, dtt_ref):
    m = mod_ref[0]
    h = _modulate(x_ref[0], g_ref[...], m[0:1], m[1:2]).astype(BF16)
    t = lax.dot_general(wt_ref[...], h, NT_DIMS, preferred_element_type=F32)
    zt_ref[0] = t[0:SSM_D_INNER].astype(BF16)
    xbct_ref[0] = t[SSM_D_INNER:SSM_D_INNER + SSM_XBC].astype(BF16)
    dtt_ref[0] = jax.nn.softplus(t[SSM_D_INNER + SSM_XBC:SSM_IN_COLS] + dtb_ref[...])


def _ssm_proj(x, mod_l, mod_row, g, wt, dtb):
    b, s, d = x.shape
    tm = min(TOKEN_TILE, s)
    return pl.pallas_call(
        _ssm_proj_body,
        grid=(b, s // tm),
        in_specs=[pl.BlockSpec((1, tm, d), lambda i, t: (i, t, 0)),
                  pl.BlockSpec((1, 6, d), lambda i, t: (mod_row(i), 0, 0)),
                  _resident((1, d)),
                  _resident((SSM_IN_COLS, d)),
                  _resident((2 * SSM_HEADS, 1))],
        out_specs=[pl.BlockSpec((1, SSM_D_INNER, tm), lambda i, t: (i, 0, t)),
                   pl.BlockSpec((1, SSM_XBC, tm), lambda i, t: (i, 0, t)),
                   pl.BlockSpec((1, 2 * SSM_HEADS, tm), lambda i, t: (i, 0, t))],
        out_shape=[jax.ShapeDtypeStruct((b, SSM_D_INNER, s), BF16),
                   jax.ShapeDtypeStruct((b, SSM_XBC, s), BF16),
                   jax.ShapeDtypeStruct((b, 2 * SSM_HEADS, s), F32)],
        compiler_params=_cparams("parallel", "parallel"),
        name="ssm_in_proj",
    )(x, mod_l, g, wt, dtb)


def _ssm_conv_body(u_ref, up_ref, un_ref, w_ref, b_ref, xst_ref, ct_ref, bm_ref, *, tc):
    t = pl.program_id(1)
    nt = pl.num_programs(1)
    prev = jnp.where(t > 0, up_ref[0][:, V7X_LANES - 1:V7X_LANES].astype(F32), 0.0)
    nxt = jnp.where(t < nt - 1, un_ref[0][:, 0:1].astype(F32), 0.0)
    lane = lax.broadcasted_iota(jnp.int32, (1, V7X_LANES), 1)
    packed = pltpu.bitcast(u_ref[0], jnp.uint32)
    rolled_l = pltpu.bitcast(pltpu.roll(packed, 1, 1), BF16)
    rolled_r = pltpu.bitcast(pltpu.roll(packed, tc - 1, 1), BF16)
    bn = SSM_GROUPS * SSM_STATE
    edges = sorted({0, min(V7X_LANES, tc), max(tc - V7X_LANES, 0), tc})
    for c0, c1 in zip(edges[:-1], edges[1:]):
        reps = (c1 - c0) // V7X_LANES

        def lanes(a):
            return jnp.concatenate([a] * reps, axis=1) if reps > 1 else a

        u = u_ref[0, :, c0:c1].astype(F32)
        left = rolled_l[:, c0:c1].astype(F32)
        right = rolled_r[:, c0:c1].astype(F32)
        if c0 == 0:
            left = jnp.where(lane == 0, prev, left)
        if c1 == tc:
            right = jnp.where(lane == V7X_LANES - 1, nxt, right)
        v = _silu(lanes(b_ref[...]) + lanes(w_ref[0]) * left + lanes(w_ref[1]) * u
                  + lanes(w_ref[2]) * right)
        xst_ref[0, :, c0:c1] = v[0:SSM_D_INNER].astype(BF16)
        bm_ref[0, c0:c1, :] = v[SSM_D_INNER:SSM_D_INNER + bn].T.astype(BF16)
        ct_ref[0, :, c0:c1] = v[SSM_D_INNER + bn:SSM_XBC].astype(BF16)


def _ssm_conv(xbct, w, bias):
    b, ch, s = xbct.shape
    tc = min(TOKEN_TILE, s)
    hb = tc // V7X_LANES
    last = s // V7X_LANES - 1
    bn = SSM_GROUPS * SSM_STATE
    body = functools.partial(_ssm_conv_body, tc=tc)
    return pl.pallas_call(
        body,
        grid=(b, s // tc),
        in_specs=[pl.BlockSpec((1, ch, tc), lambda i, t: (i, 0, t)),
                  pl.BlockSpec((1, ch, V7X_LANES), lambda i, t: (i, 0, jnp.maximum(t * hb - 1, 0))),
                  pl.BlockSpec((1, ch, V7X_LANES), lambda i, t: (i, 0, jnp.minimum((t + 1) * hb, last))),
                  _resident((3, ch, V7X_LANES)),
                  _resident((ch, V7X_LANES))],
        out_specs=[pl.BlockSpec((1, SSM_D_INNER, tc), lambda i, t: (i, 0, t)),
                   pl.BlockSpec((1, bn, tc), lambda i, t: (i, 0, t)),
                   pl.BlockSpec((1, tc, bn), lambda i, t: (i, t, 0))],
        out_shape=[jax.ShapeDtypeStruct((b, SSM_D_INNER, s), BF16),
                   jax.ShapeDtypeStruct((b, bn, s), BF16),
                   jax.ShapeDtypeStruct((b, s, bn), BF16)],
        compiler_params=_cparams("parallel", "parallel"),
        name="ssm_conv_silu",
    )(xbct, xbct, xbct, w, bias)


def _dot_f32_by_01(a, m01):
    hi = a.astype(BF16)
    r1 = a - hi.astype(F32)
    mid = r1.astype(BF16)
    lo = (r1 - mid.astype(F32)).astype(BF16)
    return (jnp.dot(hi, m01, preferred_element_type=F32)
            + jnp.dot(mid, m01, preferred_element_type=F32)
            + jnp.dot(lo, m01, preferred_element_type=F32))


def _scan_order_mask(d, chunk):
    jrow = lax.broadcasted_iota(jnp.int32, (chunk, chunk), 0)
    icol = lax.broadcasted_iota(jnp.int32, (chunk, chunk), 1)
    sign = jnp.where(d == 0, 1, -1)
    return (icol - jrow) * sign >= 0


DECAY_ROWS = 4 * SSM_HEADS


def _ssd_decay_body(a_ref, dt_ref, fac_ref, src_ref, *, chunk):
    d = pl.program_id(1)
    nh = SSM_HEADS
    mask01 = _scan_order_mask(d, chunk).astype(BF16)
    for s in range(dt_ref.shape[2] // chunk):
        tok = slice(s * chunk, (s + 1) * chunk)
        dt = dt_ref[0, :, tok]
        cum_t = _dot_f32_by_01(dt * a_ref[0], mask01)
        cum2_t = cum_t * LOG2E
        fac_ref[0, 0, 0:nh, tok] = cum2_t
        src_ref[0, 0, tok, :] = (cum2_t - jnp.log2(dt)).T
        tot = jnp.where(d == 0, cum_t[:, chunk - 1:chunk], cum_t[:, 0:1])
        fac_ref[0, 0, nh:2 * nh, tok] = jnp.exp(tot - cum_t) * dt
        fac_ref[0, 0, 2 * nh:3 * nh, tok] = jnp.exp(cum_t)
        fac_ref[0, 0, 3 * nh:4 * nh, tok] = jnp.broadcast_to(jnp.exp(tot), cum_t.shape)


def _ssd_decay(a, dtt, chunk):
    b, _, s = dtt.shape
    span = min(s, 8 * chunk)
    body = functools.partial(_ssd_decay_body, chunk=chunk)
    return pl.pallas_call(
        body,
        grid=(b, 2, s // span),
        in_specs=[pl.BlockSpec((1, SSM_HEADS, 1), lambda i, d, g: (d, 0, 0)),
                  pl.BlockSpec((1, SSM_HEADS, span), lambda i, d, g: (i, d, g))],
        out_specs=[pl.BlockSpec((1, 1, DECAY_ROWS, span), lambda i, d, g: (i, d, 0, g)),
                   pl.BlockSpec((1, 1, span, SSM_HEADS), lambda i, d, g: (i, d, g, 0))],
        out_shape=[jax.ShapeDtypeStruct((b, 2, DECAY_ROWS, s), F32),
                   jax.ShapeDtypeStruct((b, 2, s, SSM_HEADS), F32)],
        compiler_params=_cparams("parallel", "parallel", "parallel"),
        name="ssd_decay_factors",
    )(a, dtt)


def _scan_body(fac_ref, src_ref, xs_ref, ct_ref, b_ref, s0_ref, y_ref, sout_ref, state, *, chunk):
    d = pl.program_id(1)
    c = pl.program_id(2)
    nc = pl.num_programs(2)
    hp = SSM_HEADS_PER_GROUP
    hd = SSM_D_INNER // SSM_HEADS
    gw = hp * hd
    nh = SSM_HEADS
    nsub = xs_ref.shape[2] // chunk

    @pl.when(c == 0)
    def _load_state():
        state[...] = s0_ref[0, 0]

    blk = V7X_LANES
    nblk = chunk // blk

    def decay_weights(direction, cbt, cum2_row, src_col, not_yet_diag):
        rows = []
        for jb in range(nblk):
            cols = []
            for ib in range(nblk):
                ahead = ib - jb if direction == 0 else jb - ib
                if ahead < 0:
                    cols.append(jnp.zeros((blk, blk), BF16))
                    continue
                js, is_ = slice(jb * blk, (jb + 1) * blk), slice(ib * blk, (ib + 1) * blk)
                seg = cum2_row[:, is_] - src_col[js, :]
                if ahead == 0:
                    seg = seg + not_yet_diag
                cols.append((cbt[js, is_] * jnp.exp2(seg)).astype(BF16))
            rows.append(jnp.concatenate(cols, axis=1))
        return jnp.concatenate(rows, axis=0)

    def one_chunk(direction, tok):
        not_yet_diag = jnp.where(_scan_order_mask(direction, blk), 0.0, -jnp.inf)
        cum2_t = fac_ref[0, 0, 0:nh, tok]
        src_term = src_ref[0, 0, tok, :]
        to_end = fac_ref[0, 0, nh:2 * nh, tok]
        ecum = fac_ref[0, 0, 2 * nh:3 * nh, tok]
        etot = fac_ref[0, 0, 3 * nh:4 * nh, tok.start:tok.start + 1]
        for g in range(SSM_GROUPS):
            bg = b_ref[0, tok, g * SSM_STATE:(g + 1) * SSM_STATE]
            ctg = ct_ref[0, g * SSM_STATE:(g + 1) * SSM_STATE, tok]
            cbt = jnp.dot(bg, ctg, preferred_element_type=F32)
            xg = xs_ref[0, g * gw:(g + 1) * gw, tok]
            sg = state[g * gw:(g + 1) * gw, :]
            hs = slice(g * hp, (g + 1) * hp)
            y_state = (jnp.dot(sg.astype(BF16), ctg, preferred_element_type=F32)
                       .reshape(hp, hd, chunk) * ecum[hs][:, None, :])
            outs = []
            for r in range(hp):
                h = g * hp + r
                w = decay_weights(direction, cbt, cum2_t[h:h + 1, :], src_term[:, h:h + 1],
                                  not_yet_diag)
                outs.append(jnp.dot(xg[r * hd:(r + 1) * hd], w, preferred_element_type=F32)
                            + y_state[r])
            y_ref[0, 0, g * gw:(g + 1) * gw, tok] = jnp.concatenate(outs, axis=0).astype(BF16)
            xw = ((xg.astype(F32).reshape(hp, hd, chunk) * to_end[hs][:, None, :])
                  .reshape(gw, chunk).astype(BF16))
            upd = jnp.dot(xw, bg, preferred_element_type=F32)
            decayed = (sg.reshape(hp, hd, SSM_STATE) * etot[hs][:, None, :]).reshape(gw, SSM_STATE)
            state[g * gw:(g + 1) * gw, :] = decayed + upd

    for direction in (0, 1):
        @pl.when(d == direction)
        def _walk(direction=direction):
            order = range(nsub) if direction == 0 else reversed(range(nsub))
            for sub in order:
                one_chunk(direction, slice(sub * chunk, (sub + 1) * chunk))

    @pl.when(c == nc - 1)
    def _store_state():
        sout_ref[0, 0] = state[...]


def _ssd_scan(a, dtt, xst, ct, bm, s0):
    b, ch, s = xst.shape
    chunk = min(SSD_CHUNK, s)
    span = min(s, SSD_CHUNKS_PER_STEP * chunk)
    nc = s // span
    bn = SSM_GROUPS * SSM_STATE
    fac, src = _ssd_decay(a, dtt, chunk)

    def cidx(d, c):
        return c + d * (nc - 1 - 2 * c)

    body = functools.partial(_scan_body, chunk=chunk)
    return pl.pallas_call(
        body,
        grid=(b, 2, nc),
        in_specs=[pl.BlockSpec((1, 1, DECAY_ROWS, span), lambda i, d, c: (i, d, 0, cidx(d, c))),
                  pl.BlockSpec((1, 1, span, SSM_HEADS), lambda i, d, c: (i, d, cidx(d, c), 0)),
                  pl.BlockSpec((1, ch, span), lambda i, d, c: (i, 0, cidx(d, c))),
                  pl.BlockSpec((1, bn, span), lambda i, d, c: (i, 0, cidx(d, c))),
                  pl.BlockSpec((1, span, bn), lambda i, d, c: (i, cidx(d, c), 0)),
                  pl.BlockSpec((1, 1, ch, SSM_STATE), lambda i, d, c: (d, i, 0, 0))],
        out_specs=[pl.BlockSpec((1, 1, ch, span), lambda i, d, c: (d, i, 0, cidx(d, c))),
                   pl.BlockSpec((1, 1, ch, SSM_STATE), lambda i, d, c: (d, i, 0, 0))],
        out_shape=[jax.ShapeDtypeStruct((2, b, ch, s), BF16),
                   jax.ShapeDtypeStruct((2, b, ch, SSM_STATE), F32)],
        scratch_shapes=[pltpu.VMEM((ch, SSM_STATE), F32)],
        compiler_params=_cparams("parallel", "parallel", "arbitrary"),
        name="ssd_scan",
    )(fac, src, xst, ct, bm, s0)


def _ssm_out_body(x_ref, mod_ref, yf_ref, yb_ref, xs_ref, z_ref, dsk_ref, ng_ref, wt_ref, o_ref):
    y = (yf_ref[0, 0].astype(F32) + yb_ref[0, 0].astype(F32)
         + dsk_ref[...] * xs_ref[0].astype(F32))
    y = y * _silu(z_ref[0].astype(F32))
    ms = jnp.mean(y * y, axis=0, keepdims=True)
    yn = (y * lax.rsqrt(ms + NORM_EPS) * ng_ref[...]).astype(BF16)
    ot = jnp.dot(wt_ref[...], yn, preferred_element_type=F32)
    o_ref[0] = x_ref[0] + mod_ref[0][2:3] * ot.T


def _ssm_out(x, mod_l, mod_row, y, xst, zt, dsk, ng, wt):
    b, s, d = x.shape
    tm = min(TOKEN_TILE, s)
    ch = xst.shape[1]
    return pl.pallas_call(
        _ssm_out_body,
        grid=(b, s // tm),
        in_specs=[pl.BlockSpec((1, tm, d), lambda i, t: (i, t, 0)),
                  pl.BlockSpec((1, 6, d), lambda i, t: (mod_row(i), 0, 0)),
                  pl.BlockSpec((1, 1, ch, tm), lambda i, t: (0, i, 0, t)),
                  pl.BlockSpec((1, 1, ch, tm), lambda i, t: (1, i, 0, t)),
                  pl.BlockSpec((1, ch, tm), lambda i, t: (i, 0, t)),
                  pl.BlockSpec((1, ch, tm), lambda i, t: (i, 0, t)),
                  _resident((ch, 1)),
                  _resident((ch, 1)),
                  _resident((d, ch))],
        out_specs=pl.BlockSpec((1, tm, d), lambda i, t: (i, t, 0)),
        out_shape=jax.ShapeDtypeStruct((b, s, d), F32),
        compiler_params=_cparams("parallel", "parallel"),
        name="ssm_out_residual",
    )(x, mod_l, y, y, xst, zt, dsk, ng, wt)


def _rope_tables_t(n):
    t = jnp.arange(n)
    inv_freq = 1.0 / (ROPE_BASE ** (jnp.arange(ROPE_PAIRS, dtype=F32) / ROPE_PAIRS))
    ang_r = (t // GRID_W).astype(F32)[None, :] * inv_freq[:, None]
    ang_c = (t % GRID_W).astype(F32)[None, :] * inv_freq[:, None]
    cr, sr, cc, sc = jnp.cos(ang_r), jnp.sin(ang_r), jnp.cos(ang_c), jnp.sin(ang_c)
    return (jnp.concatenate([cr, cr, cc, cc], axis=0),
            jnp.concatenate([-sr, sr, -sc, sc], axis=0))


def _attn_layer(x, ctx, mod_l, lat_row, ctx_row, p, rope, lambda_init, with_ctx):
    wt = p["w_in"].T.astype(BF16)
    g = p["norm_g"].reshape(1, D_MODEL)
    qg = p["q_norm_g"].reshape(HEAD_DIM, 1)
    kg = p["k_norm_g"].reshape(HEAD_DIM, 1)
    cos_t, sin_t = rope
    c = ctx.shape[1]
    qt, *lat_kv = _attn_in(x, mod_l, lat_row, g, wt, cos_t, sin_t, qg, kg, rope=True)
    qtc, *ctx_kv = _attn_in(ctx, mod_l, ctx_row, g, wt, cos_t[:, :c], sin_t[:, :c], qg, kg,
                            rope=False)
    lam_vecs = [p[k].reshape(1, HEAD_DIM) for k in ("lq1", "lk1", "lq2", "lk2")]
    sg = p["subln_g"].reshape(V_ROWS, 1)
    w_out = p["w_out"].astype(BF16)
    x = _flash(x, mod_l, lat_row, w_out, lam_vecs, sg, qt, ctx_kv, lat_kv,
               lambda_init=lambda_init)
    if with_ctx:
        ctx = _flash(ctx, mod_l, ctx_row, w_out, lam_vecs, sg, qtc, ctx_kv, None,
                     lambda_init=lambda_init)
    return x, ctx


def _ssm_layer(x, ctx, mod_l, lat_row, ctx_row, p, with_ctx):
    wt = p["w_in"].T.astype(BF16)
    g = p["norm_g"].reshape(1, D_MODEL)
    dtb = p["dt_bias"].reshape(2 * SSM_HEADS, 1)
    conv_w = jnp.broadcast_to(p["conv_w"][:, :, None], (3, SSM_XBC, V7X_LANES))
    conv_b = jnp.broadcast_to(p["conv_b"][:, None], (SSM_XBC, V7X_LANES))
    a = (-jnp.exp(p["a_log"].astype(F32))).reshape(2, SSM_HEADS, 1)
    b = x.shape[0]

    def pre(v, row):
        zt, xbct, dtt = _ssm_proj(v, mod_l, row, g, wt, dtb)
        xst, ct, bm = _ssm_conv(xbct, conv_w, conv_b)
        return zt, xst, ct, bm, dtt

    zt_c, xst_c, ct_c, bm_c, dtt_c = pre(ctx, ctx_row)
    zt_l, xst_l, ct_l, bm_l, dtt_l = pre(x, lat_row)
    zero = jnp.zeros((2, b, SSM_D_INNER, SSM_STATE), F32)
    y_c, s_ctx = _ssd_scan(a, dtt_c, xst_c, ct_c, bm_c, zero)
    y_l, _ = _ssd_scan(a, dtt_l, xst_l, ct_l, bm_l, s_ctx)
    dsk = jnp.repeat(p["d_skip"], SSM_D_INNER // SSM_HEADS).reshape(SSM_D_INNER, 1)
    ng = p["out_norm_g"].reshape(SSM_D_INNER, 1)
    w_out_t = p["w_out"].T.astype(BF16)
    x = _ssm_out(x, mod_l, lat_row, y_l, xst_l, zt_l, dsk, ng, w_out_t)
    if with_ctx:
        ctx = _ssm_out(ctx, mod_l, ctx_row, y_c, xst_c, zt_c, dsk, ng, w_out_t)
    return x, ctx


def kernel(x, c, ctx, c_ctx, mod_w, mod_b, norm_mix_g, norm_ffn_g, attn_w_in, attn_w_out,
           diff_lq1, diff_lk1, diff_lq2, diff_lk2, diff_subln_g, gqa_q_norm_g, gqa_k_norm_g,
           ssm_w_in, ssm_conv_w, ssm_conv_b, ssm_dt_bias, ssm_a_log, ssm_d, ssm_norm_g, ssm_w_out,
           ffn_w_in, ffn_conv_w, ffn_conv_b, ffn_w_out, final_norm_g):
    b, n, d = x.shape
    mod_rows = 16
    c_rows = jnp.zeros((mod_rows, d), F32).at[:b].set(c).at[b].set(c_ctx)
    mod = _mod_all(c_rows, mod_w, mod_b).reshape(DEPTH, mod_rows, 6, d)
    lat_row = lambda i: i
    ctx_row = lambda i: b
    rope = _rope_tables_t(n)

    for layer in range(DEPTH):
        with_ctx = layer < DEPTH - 1
        mod_l = mod[layer]
        i = layer // 2
        if layer % 2 == 0:
            p = dict(w_in=attn_w_in[i], w_out=attn_w_out[i], norm_g=norm_mix_g[layer],
                     lq1=diff_lq1[i], lk1=diff_lk1[i], lq2=diff_lq2[i], lk2=diff_lk2[i],
                     subln_g=diff_subln_g[i], q_norm_g=gqa_q_norm_g[i], k_norm_g=gqa_k_norm_g[i])
            lambda_init = 0.8 - 0.6 * math.exp(-0.3 * layer)
            x, ctx = _attn_layer(x, ctx, mod_l, lat_row, ctx_row, p, rope, lambda_init, with_ctx)
        else:
            p = dict(w_in=ssm_w_in[i], norm_g=norm_mix_g[layer], conv_w=ssm_conv_w[i],
                     conv_b=ssm_conv_b[i], dt_bias=ssm_dt_bias[i], a_log=ssm_a_log[i],
                     d_skip=ssm_d[i], out_norm_g=ssm_norm_g[i], w_out=ssm_w_out[i])
            x, ctx = _ssm_layer(x, ctx, mod_l, lat_row, ctx_row, p, with_ctx)
        g = norm_ffn_g[layer].reshape(1, d)
        wv = ffn_w_in[layer][:, :D_FF].astype(BF16)
        wg = ffn_w_in[layer][:, D_FF:].astype(BF16)
        cw = ffn_conv_w[layer]
        cb = ffn_conv_b[layer].reshape(1, D_FF)
        wo = ffn_w_out[layer].astype(BF16)
        fg = final_norm_g.reshape(1, d)
        x = _ffn(x, mod_l, lat_row, g, wv, wg, cw, cb, wo, fg, final_norm=layer == DEPTH - 1)
        if with_ctx:
            ctx = _ffn(ctx, mod_l, ctx_row, g, wv, wg, cw, cb, wo, fg, final_norm=False)
    return x
```

```python
import functools
import math

import jax
import jax.numpy as jnp
from jax import lax
from jax.experimental import pallas as pl
from jax.experimental.pallas import tpu as pltpu

F32 = jnp.float32
BF16 = jnp.bfloat16

D_MODEL = 1024
DEPTH = 4
GRID_W = 64
HEAD_DIM = 64
ROPE_PAIRS = HEAD_DIM // 4
ROPE_BASE = 10000.0
NORM_EPS = 1e-6
DIFF_HEADS = 4
GQA_HEADS = 8
GQA_KV_HEADS = 2
GQA_GROUP = GQA_HEADS // GQA_KV_HEADS
ATTN_IN_COLS = 2304
SSM_D_INNER = 2048
SSM_HEADS = 32
SSM_GROUPS = 4
SSM_HEADS_PER_GROUP = SSM_HEADS // SSM_GROUPS
SSM_STATE = 128
SSM_XBC = 3072
SSM_IN_COLS = 5184
D_FF = 2816
LOG2E = math.log2(math.e)

V7X_VMEM_BYTES = 64 * 1024 * 1024
VMEM_LIMIT_BYTES = V7X_VMEM_BYTES - 8 * 1024 * 1024
V7X_LANES = 128
BF16_SUBLANES = 16

V_ROWS = 2 * HEAD_DIM
V_ROWS_PADDED = V_ROWS + BF16_SUBLANES
GV_ROWS_PADDED = HEAD_DIM + BF16_SUBLANES
K_COLS = (DIFF_HEADS + 1) * 2 * HEAD_DIM

ATTN_KEY_SUBBLOCK = 256
ATTN_STREAM_MAX_OCTAVES = 64.0

CONV_HALO = 16
SSD_CHUNK = 256
SSD_CHUNKS_PER_STEP = 2

TOKEN_TILE = 512
FFN_TOKEN_TILE = 256
ATTN_QUERY_TILE = 256
ATTN_KEY_CHUNK = 2048
MOD_COL_TILE = 2048

NT_DIMS = (((1,), (1,)), ((), ()))


def _cparams(*sem):
    return pltpu.CompilerParams(dimension_semantics=sem, vmem_limit_bytes=VMEM_LIMIT_BYTES)


def _resident(shape):
    nd = len(shape)
    return pl.BlockSpec(shape, lambda *_: (0,) * nd, pipeline_mode=pl.Buffered(1))


def _silu(v):
    return v * jax.nn.sigmoid(v)


def _modulate(x, g, shift, scale):
    ms = jnp.mean(x * x, axis=-1, keepdims=True)
    return (x * lax.rsqrt(ms + NORM_EPS) * g) * (1.0 + scale) + shift


def _mod_body(c_ref, w_ref, b_ref, o_ref):
    s = _silu(c_ref[...])
    o_ref[0] = jnp.dot(s, w_ref[0], preferred_element_type=F32,
                       precision=lax.Precision.HIGHEST) + b_ref[0]


def _mod_all(c_rows, mod_w, mod_b):
    rows = c_rows.shape[0]
    depth, d, cols = mod_w.shape
    tn = MOD_COL_TILE
    return pl.pallas_call(
        _mod_body,
        grid=(depth, cols // tn),
        in_specs=[pl.BlockSpec((rows, d), lambda l, n: (0, 0)),
                  pl.BlockSpec((1, d, tn), lambda l, n: (l, 0, n)),
                  pl.BlockSpec((1, 1, tn), lambda l, n: (l, 0, n))],
        out_specs=pl.BlockSpec((1, rows, tn), lambda l, n: (l, 0, n)),
        out_shape=jax.ShapeDtypeStruct((depth, rows, cols), F32),
        compiler_params=_cparams("parallel", "parallel"),
        name="mod_vectors",
    )(c_rows, mod_w, mod_b.reshape(depth, 1, cols))


def _attn_in_body(x_ref, mod_ref, g_ref, wt_ref, cos_ref, sin_ref, qg_ref, kg_ref,
                  qt_ref, k_ref, vta_ref, vtb_ref, *, rope):
    m = mod_ref[0]
    h = _modulate(x_ref[0], g_ref[...], m[0:1], m[1:2]).astype(BF16)
    t = lax.dot_general(wt_ref[...], h, NT_DIMS, preferred_element_type=F32)
    tm = t.shape[1]

    def rot(u):
        if not rope:
            return u
        sw = jnp.concatenate([u[:, 16:32], u[:, 0:16], u[:, 48:64], u[:, 32:48]], axis=1)
        return u * cos_ref[...][None] + sw * sin_ref[...][None]

    def qk_norm(u, g):
        ms = jnp.mean(u * u, axis=1, keepdims=True)
        return u * lax.rsqrt(ms + NORM_EPS) * g[None]

    nq = 2 * DIFF_HEADS
    qa = rot(t[0:512].reshape(nq, HEAD_DIM, tm))
    ka = rot(t[512:1024].reshape(nq, HEAD_DIM, tm))
    va = t[1024:1536]
    qb = rot(qk_norm(t[1536:2048].reshape(GQA_HEADS, HEAD_DIM, tm), qg_ref[...]))
    kb = rot(qk_norm(t[2048:2176].reshape(GQA_KV_HEADS, HEAD_DIM, tm), kg_ref[...]))
    vb = t[2176:2304]

    qs = (HEAD_DIM ** -0.5) * LOG2E
    qt_ref[0, 0:512] = (qa * qs).reshape(512, tm).astype(BF16)
    qt_ref[0, 512:1024] = (qb * qs).reshape(512, tm).astype(BF16)
    kt = jnp.concatenate([ka.reshape(512, tm), kb.reshape(128, tm)], axis=0)
    k_ref[0] = kt.T.astype(BF16)
    ones = jnp.ones((BF16_SUBLANES, tm), BF16)
    for u in range(DIFF_HEADS):
        vta_ref[0, u, 0:V_ROWS] = va[u * V_ROWS:(u + 1) * V_ROWS].astype(BF16)
        vta_ref[0, u, V_ROWS:V_ROWS_PADDED] = ones
    for g in range(GQA_KV_HEADS):
        vtb_ref[0, g, 0:HEAD_DIM] = vb[g * HEAD_DIM:(g + 1) * HEAD_DIM].astype(BF16)
        vtb_ref[0, g, HEAD_DIM:GV_ROWS_PADDED] = ones


def _attn_in(x, mod_l, mod_row, g, wt, cos_t, sin_t, qg, kg, *, rope):
    b, s, d = x.shape
    tm = min(TOKEN_TILE, s)
    body = functools.partial(_attn_in_body, rope=rope)
    return pl.pallas_call(
        body,
        grid=(b, s // tm),
        in_specs=[pl.BlockSpec((1, tm, d), lambda i, t: (i, t, 0)),
                  pl.BlockSpec((1, 6, d), lambda i, t: (mod_row(i), 0, 0)),
                  _resident((1, d)),
                  _resident((ATTN_IN_COLS, d)),
                  pl.BlockSpec((HEAD_DIM, tm), lambda i, t: (0, t)),
                  pl.BlockSpec((HEAD_DIM, tm), lambda i, t: (0, t)),
                  _resident((HEAD_DIM, 1)),
                  _resident((HEAD_DIM, 1))],
        out_specs=[pl.BlockSpec((1, 1024, tm), lambda i, t: (i, 0, t)),
                   pl.BlockSpec((1, tm, K_COLS), lambda i, t: (i, t, 0)),
                   pl.BlockSpec((1, DIFF_HEADS, V_ROWS_PADDED, tm), lambda i, t: (i, 0, 0, t)),
                   pl.BlockSpec((1, GQA_KV_HEADS, GV_ROWS_PADDED, tm), lambda i, t: (i, 0, 0, t))],
        out_shape=[jax.ShapeDtypeStruct((b, 1024, s), BF16),
                   jax.ShapeDtypeStruct((b, s, K_COLS), BF16),
                   jax.ShapeDtypeStruct((b, DIFF_HEADS, V_ROWS_PADDED, s), BF16),
                   jax.ShapeDtypeStruct((b, GQA_KV_HEADS, GV_ROWS_PADDED, s), BF16)],
        compiler_params=_cparams("parallel", "parallel"),
        name="attn_in_rope" if rope else "attn_in_ctx",
    )(x, mod_l, g, wt, cos_t, sin_t, qg, kg)


def _attn_unit_cols(tq):
    units = []
    for h in range(DIFF_HEADS):
        units.append((h * 128, h, V_ROWS_PADDED, h * 2 * tq, 2 * tq))
    base = DIFF_HEADS * 2 * tq
    for g in range(GQA_KV_HEADS):
        units.append((DIFF_HEADS * 128, g, GV_ROWS_PADDED, base + g * GQA_GROUP * tq,
                      GQA_GROUP * tq))
    return units


def _flash_body(*refs, tq, lambda_init, has_lat):
    if has_lat:
        (lq1, lk1, lq2, lk2, sg_ref, x_ref, mod_ref, wo_ref, qt_ref, kc_ref, vtac_ref, vtbc_ref,
         kl_ref, vtal_ref, vtbl_ref, o_ref, rhs, acc, mrow, pv_new, m_chunk) = refs
    else:
        (lq1, lk1, lq2, lk2, sg_ref, x_ref, mod_ref, wo_ref, qt_ref, kc_ref, vtac_ref, vtbc_ref,
         o_ref, rhs, acc, mrow, pv_new, m_chunk) = refs
    j = pl.program_id(2)
    nj = pl.num_programs(2)
    units = _attn_unit_cols(tq)
    diff_cols = DIFF_HEADS * 2 * tq

    def values(vta_ref, vtb_ref, u, vu):
        return vta_ref[0, vu] if u < DIFF_HEADS else vtb_ref[0, vu]

    def exact_step(k_ref, vta_ref, vtb_ref):
        for u, (kc0, vu, vr, c0, w) in enumerate(units):
            s = jnp.dot(k_ref[0, :, kc0:kc0 + 128], rhs[:, c0:c0 + w],
                        preferred_element_type=F32)
            mp = mrow[:, c0:c0 + w]
            mn = jnp.maximum(mp, jnp.max(s, axis=0, keepdims=True))
            alpha = jnp.exp2(mp - mn)
            p = jnp.exp2(s - mn).astype(BF16)
            pv = jnp.dot(values(vta_ref, vtb_ref, u, vu), p, preferred_element_type=F32)
            acc[0:vr, c0:c0 + w] = acc[0:vr, c0:c0 + w] * alpha + pv
            mrow[:, c0:c0 + w] = mn

    def streaming_step(k_ref, vta_ref, vtb_ref):
        nk = k_ref.shape[1]
        sub = min(nk, ATTN_KEY_SUBBLOCK)
        for u, (kc0, vu, vr, c0, w) in enumerate(units):
            m_used = mrow[:, c0:c0 + w]
            cm = None
            parts = []
            for r in range(nk // sub):
                s = jnp.dot(k_ref[0, r * sub:(r + 1) * sub, kc0:kc0 + 128], rhs[:, c0:c0 + w],
                            preferred_element_type=F32)
                parts.append(jnp.exp2(s - m_used).astype(BF16))
                sm = jnp.max(s.reshape(sub // 8, 8, w), axis=0)
                cm = sm if cm is None else jnp.maximum(cm, sm)
            p = jnp.concatenate(parts, axis=0)
            pv_new[0:vr, c0:c0 + w] = jnp.dot(values(vta_ref, vtb_ref, u, vu), p,
                                              preferred_element_type=F32)
            m_chunk[:, c0:c0 + w] = jnp.max(cm, axis=0, keepdims=True)

    @pl.when(j == 0)
    def _init():
        zeros = jnp.zeros((HEAD_DIM, tq), BF16)
        for h in range(DIFF_HEADS):
            c0 = h * 2 * tq
            rhs[0:64, c0:c0 + tq] = qt_ref[0, h * 128:h * 128 + 64, :]
            rhs[64:128, c0:c0 + tq] = zeros
            rhs[0:64, c0 + tq:c0 + 2 * tq] = zeros
            rhs[64:128, c0 + tq:c0 + 2 * tq] = qt_ref[0, h * 128 + 64:h * 128 + 128, :]
        base = DIFF_HEADS * 2 * tq
        for g in range(GQA_KV_HEADS):
            for r in range(GQA_GROUP):
                c0 = base + (g * GQA_GROUP + r) * tq
                hd = 512 + (g * GQA_GROUP + r) * HEAD_DIM
                rhs[g * 64:(g + 1) * 64, c0:c0 + tq] = qt_ref[0, hd:hd + HEAD_DIM, :]
                rhs[(1 - g) * 64:(2 - g) * 64, c0:c0 + tq] = zeros
        acc[...] = jnp.zeros(acc.shape, F32)
        mrow[...] = jnp.zeros(mrow.shape, F32)

    row_blocks = ((V_ROWS_PADDED, slice(0, diff_cols)),
                  (GV_ROWS_PADDED, slice(diff_cols, acc.shape[1])))

    def nothing_pending():
        for rows, cols in row_blocks:
            pv_new[0:rows, cols] = jnp.zeros((rows, cols.stop - cols.start), F32)
        m_chunk[...] = jnp.full(m_chunk.shape, -jnp.inf, F32)

    def commit():
        mp = mrow[...]
        mn = jnp.maximum(mp, m_chunk[...])
        mrow[...] = mn
        alpha = jnp.exp2(mp - mn)
        for rows, cols in row_blocks:
            acc[0:rows, cols] = (acc[0:rows, cols] + pv_new[0:rows, cols]) * alpha[:, cols]

    def stream_chunk(k_ref, vta_ref, vtb_ref, first):
        streaming_step(k_ref, vta_ref, vtb_ref)
        dev = m_chunk[...] - mrow[...]
        in_range = jnp.max(jnp.abs(dev) if first else dev) <= ATTN_STREAM_MAX_OCTAVES

        @pl.when(jnp.logical_not(in_range))
        def _redo():
            if first:
                mrow[...] = jnp.full(mrow.shape, -jnp.inf, F32)
            exact_step(k_ref, vta_ref, vtb_ref)
            nothing_pending()

    @pl.when(j == 0)
    def _context_keys():
        stream_chunk(kc_ref, vtac_ref, vtbc_ref, True)

    if has_lat:
        commit()
        stream_chunk(kl_ref, vtal_ref, vtbl_ref, False)

    @pl.when(j == nj - 1)
    def _finish():
        commit()
        lam = (jnp.exp(jnp.sum(lq1[...] * lk1[...], keepdims=True))
               - jnp.exp(jnp.sum(lq2[...] * lk2[...], keepdims=True)) + lambda_init)
        pieces = []
        for h in range(DIFF_HEADS):
            c0 = h * 2 * tq
            o1 = acc[0:V_ROWS, c0:c0 + tq] / acc[V_ROWS:V_ROWS + 1, c0:c0 + tq]
            o2 = acc[0:V_ROWS, c0 + tq:c0 + 2 * tq] / acc[V_ROWS:V_ROWS + 1, c0 + tq:c0 + 2 * tq]
            oh = o1 - lam * o2
            ms = jnp.mean(oh * oh, axis=0, keepdims=True)
            pieces.append(oh * lax.rsqrt(ms + NORM_EPS) * sg_ref[...] * (1.0 - lambda_init))
        base = DIFF_HEADS * 2 * tq
        for g in range(GQA_KV_HEADS):
            for r in range(GQA_GROUP):
                c0 = base + (g * GQA_GROUP + r) * tq
                pieces.append(acc[0:HEAD_DIM, c0:c0 + tq] / acc[HEAD_DIM:HEAD_DIM + 1, c0:c0 + tq])
        o = jnp.concatenate(pieces, axis=0).T.astype(BF16)
        y = jnp.dot(o, wo_ref[...], preferred_element_type=F32)
        o_ref[0] = x_ref[0] + mod_ref[0][2:3] * y


def _flash(x, mod_l, mod_row, w_out, lam_vecs, sg, qt, ctx_kv, lat_kv, *, lambda_init):
    b, _, sq = qt.shape
    d = x.shape[2]
    kc, vtac, vtbc = ctx_kv
    c = kc.shape[1]
    has_lat = lat_kv is not None
    tq = min(ATTN_QUERY_TILE, sq)
    ncols = (DIFF_HEADS * 2 + GQA_HEADS) * tq
    in_specs = [_resident((1, HEAD_DIM))] * 4 + [
        _resident((V_ROWS, 1)),
        pl.BlockSpec((1, tq, d), lambda i, q, j: (i, q, 0)),
        pl.BlockSpec((1, 6, d), lambda i, q, j: (mod_row(i), 0, 0)),
        _resident((1024, d)),
        pl.BlockSpec((1, 1024, tq), lambda i, q, j: (i, 0, q)),
        pl.BlockSpec((1, c, K_COLS), lambda i, q, j: (i, 0, 0)),
        pl.BlockSpec((1, DIFF_HEADS, V_ROWS_PADDED, c), lambda i, q, j: (i, 0, 0, 0)),
        pl.BlockSpec((1, GQA_KV_HEADS, GV_ROWS_PADDED, c), lambda i, q, j: (i, 0, 0, 0)),
    ]
    args = list(lam_vecs) + [sg, x, mod_l, w_out, qt, kc, vtac, vtbc]
    nkv = 1
    if has_lat:
        kl, vtal, vtbl = lat_kv
        n = kl.shape[1]
        tk = min(ATTN_KEY_CHUNK, n)
        nkv = n // tk
        in_specs += [pl.BlockSpec((1, tk, K_COLS), lambda i, q, j: (i, j, 0)),
                     pl.BlockSpec((1, DIFF_HEADS, V_ROWS_PADDED, tk), lambda i, q, j: (i, 0, 0, j)),
                     pl.BlockSpec((1, GQA_KV_HEADS, GV_ROWS_PADDED, tk), lambda i, q, j: (i, 0, 0, j))]
        args += [kl, vtal, vtbl]
    scratch = [pltpu.VMEM((2 * HEAD_DIM, ncols), BF16),
               pltpu.VMEM((V_ROWS_PADDED, ncols), F32),
               pltpu.VMEM((1, ncols), F32),
               pltpu.VMEM((V_ROWS_PADDED, ncols), F32),
               pltpu.VMEM((1, ncols), F32)]
    body = functools.partial(_flash_body, tq=tq, lambda_init=lambda_init, has_lat=has_lat)
    return pl.pallas_call(
        body,
        grid=(b, sq // tq, nkv),
        in_specs=in_specs,
        out_specs=pl.BlockSpec((1, tq, d), lambda i, q, j: (i, q, 0)),
        out_shape=jax.ShapeDtypeStruct((b, sq, d), F32),
        scratch_shapes=scratch,
        compiler_params=_cparams("parallel", "parallel", "arbitrary"),
        name="attn_sweep_lat" if has_lat else "attn_sweep_ctx",
    )(*args)


def _ffn_body(x_ref, xp_ref, xn_ref, mod_ref, g_ref, wv_ref, wg_ref, cw_ref, cb_ref, wo_ref,
              fg_ref, o_ref, *, tm, final_norm):
    t = pl.program_id(1)
    nt = pl.num_programs(1)
    m = mod_ref[0]
    x = x_ref[0]
    xe = jnp.concatenate([xp_ref[0], x, xn_ref[0]], axis=0)
    he = _modulate(xe, g_ref[...], m[3:4], m[4:5]).astype(BF16)
    ge = jnp.dot(he, wg_ref[...], preferred_element_type=F32)
    rows = lax.broadcasted_iota(jnp.int32, (tm + 2 * CONV_HALO, 1), 0)
    lo = jnp.where(t > 0, 0, CONV_HALO)
    hi = jnp.where(t < nt - 1, tm + 2 * CONV_HALO, tm + CONV_HALO)
    ge = jnp.where((rows >= lo) & (rows < hi), ge, 0.0)
    val = jnp.dot(he[CONV_HALO:CONV_HALO + tm], wv_ref[...], preferred_element_type=F32)
    cw = cw_ref[...]
    h0 = CONV_HALO
    conv = (cb_ref[...] + cw[0:1] * ge[h0 - 1:h0 - 1 + tm] + cw[1:2] * ge[h0:h0 + tm]
            + cw[2:3] * ge[h0 + 1:h0 + 1 + tm])
    gelu = 0.5 * conv * (1.0 + lax.erf(conv * math.sqrt(0.5)))
    act = (gelu * val).astype(BF16)
    y = jnp.dot(act, wo_ref[...], preferred_element_type=F32)
    out = x + m[5:6] * y
    if final_norm:
        ms = jnp.mean(out * out, axis=-1, keepdims=True)
        out = out * lax.rsqrt(ms + NORM_EPS) * fg_ref[...]
    o_ref[0] = out


def _ffn(x, mod_l, mod_row, g, wv, wg, cw, cb, wo, fg, *, final_norm):
    b, s, d = x.shape
    tm = min(FFN_TOKEN_TILE, s)
    hb = tm // CONV_HALO
    last = s // CONV_HALO - 1
    body = functools.partial(_ffn_body, tm=tm, final_norm=final_norm)
    return pl.pallas_call(
        body,
        grid=(b, s // tm),
        in_specs=[pl.BlockSpec((1, tm, d), lambda i, t: (i, t, 0)),
                  pl.BlockSpec((1, CONV_HALO, d), lambda i, t: (i, jnp.maximum(t * hb - 1, 0), 0)),
                  pl.BlockSpec((1, CONV_HALO, d), lambda i, t: (i, jnp.minimum((t + 1) * hb, last), 0)),
                  pl.BlockSpec((1, 6, d), lambda i, t: (mod_row(i), 0, 0)),
                  _resident((1, d)),
                  _resident((d, D_FF)),
                  _resident((d, D_FF)),
                  _resident((3, D_FF)),
                  _resident((1, D_FF)),
                  _resident((D_FF, d)),
                  _resident((1, d))],
        out_specs=pl.BlockSpec((1, tm, d), lambda i, t: (i, t, 0)),
        out_shape=jax.ShapeDtypeStruct((b, s, d), F32),
        compiler_params=_cparams("parallel", "parallel"),
        name="conv_glu_ffn",
    )(x, x, x, mod_l, g, wv, wg, cw, cb, wo, fg)


def _ssm_proj_body(x_ref, mod_ref, g_ref, wt_ref, dtb_ref, zt_ref, xbct_ref, dtt_ref):
    m = mod_ref[0]
    h = _modulate(x_ref[0], g_ref[...], m[0:1], m[1:2]).astype(BF16)
    t = lax.dot_general(wt_ref[...], h, NT_DIMS, preferred_element_type=F32)
    zt_ref[0] = t[0:SSM_D_INNER].astype(BF16)
    xbct_ref[0] = t[SSM_D_INNER:SSM_D_INNER + SSM_XBC].astype(BF16)
    dtt_ref[0] = jax.nn.softplus(t[SSM_D_INNER + SSM_XBC:SSM_IN_COLS] + dtb_ref[...])


def _ssm_proj(x, mod_l, mod_row, g, wt, dtb):
    b, s, d = x.shape
    tm = min(TOKEN_TILE, s)
    return pl.pallas_call(
        _ssm_proj_body,
        grid=(b, s // tm),
        in_specs=[pl.BlockSpec((1, tm, d), lambda i, t: (i, t, 0)),
                  pl.BlockSpec((1, 6, d), lambda i, t: (mod_row(i), 0, 0)),
                  _resident((1, d)),
                  _resident((SSM_IN_COLS, d)),
                  _resident((2 * SSM_HEADS, 1))],
        out_specs=[pl.BlockSpec((1, SSM_D_INNER, tm), lambda i, t: (i, 0, t)),
                   pl.BlockSpec((1, SSM_XBC, tm), lambda i, t: (i, 0, t)),
                   pl.BlockSpec((1, 2 * SSM_HEADS, tm), lambda i, t: (i, 0, t))],
        out_shape=[jax.ShapeDtypeStruct((b, SSM_D_INNER, s), BF16),
                   jax.ShapeDtypeStruct((b, SSM_XBC, s), BF16),
                   jax.ShapeDtypeStruct((b, 2 * SSM_HEADS, s), F32)],
        compiler_params=_cparams("parallel", "parallel"),
        name="ssm_in_proj",
    )(x, mod_l, g, wt, dtb)


def _ssm_conv_body(u_ref, up_ref, un_ref, w_ref, b_ref, xst_ref, ct_ref, bm_ref, *, tc):
    t = pl.program_id(1)
    nt = pl.num_programs(1)
    prev = jnp.where(t > 0, up_ref[0][:, V7X_LANES - 1:V7X_LANES].astype(F32), 0.0)
    nxt = jnp.where(t < nt - 1, un_ref[0][:, 0:1].astype(F32), 0.0)
    lane = lax.broadcasted_iota(jnp.int32, (1, V7X_LANES), 1)
    packed = pltpu.bitcast(u_ref[0], jnp.uint32)
    rolled_l = pltpu.bitcast(pltpu.roll(packed, 1, 1), BF16)
    rolled_r = pltpu.bitcast(pltpu.roll(packed, tc - 1, 1), BF16)
    bn = SSM_GROUPS * SSM_STATE
    edges = sorted({0, min(V7X_LANES, tc), max(tc - V7X_LANES, 0), tc})
    for c0, c1 in zip(edges[:-1], edges[1:]):
        reps = (c1 - c0) // V7X_LANES

        def lanes(a):
            return jnp.concatenate([a] * reps, axis=1) if reps > 1 else a

        u = u_ref[0, :, c0:c1].astype(F32)
        left = rolled_l[:, c0:c1].astype(F32)
        right = rolled_r[:, c0:c1].astype(F32)
        if c0 == 0:
            left = jnp.where(lane == 0, prev, left)
        if c1 == tc:
            right = jnp.where(lane == V7X_LANES - 1, nxt, right)
        v = _silu(lanes(b_ref[...]) + lanes(w_ref[0]) * left + lanes(w_ref[1]) * u
                  + lanes(w_ref[2]) * right)
        xst_ref[0, :, c0:c1] = v[0:SSM_D_INNER].astype(BF16)
        bm_ref[0, c0:c1, :] = v[SSM_D_INNER:SSM_D_INNER + bn].T.astype(BF16)
        ct_ref[0, :, c0:c1] = v[SSM_D_INNER + bn:SSM_XBC].astype(BF16)


def _ssm_conv(xbct, w, bias):
    b, ch, s = xbct.shape
    tc = min(TOKEN_TILE, s)
    hb = tc // V7X_LANES
    last = s // V7X_LANES - 1
    bn = SSM_GROUPS * SSM_STATE
    body = functools.partial(_ssm_conv_body, tc=tc)
    return pl.pallas_call(
        body,
        grid=(b, s // tc),
        in_specs=[pl.BlockSpec((1, ch, tc), lambda i, t: (i, 0, t)),
                  pl.BlockSpec((1, ch, V7X_LANES), lambda i, t: (i, 0, jnp.maximum(t * hb - 1, 0))),
                  pl.BlockSpec((1, ch, V7X_LANES), lambda i, t: (i, 0, jnp.minimum((t + 1) * hb, last))),
                  _resident((3, ch, V7X_LANES)),
                  _resident((ch, V7X_LANES))],
        out_specs=[pl.BlockSpec((1, SSM_D_INNER, tc), lambda i, t: (i, 0, t)),
                   pl.BlockSpec((1, bn, tc), lambda i, t: (i, 0, t)),
                   pl.BlockSpec((1, tc, bn), lambda i, t: (i, t, 0))],
        out_shape=[jax.ShapeDtypeStruct((b, SSM_D_INNER, s), BF16),
                   jax.ShapeDtypeStruct((b, bn, s), BF16),
                   jax.ShapeDtypeStruct((b, s, bn), BF16)],
        compiler_params=_cparams("parallel", "parallel"),
        name="ssm_conv_silu",
    )(xbct, xbct, xbct, w, bias)


def _dot_f32_by_01(a, m01):
    hi = a.astype(BF16)
    r1 = a - hi.astype(F32)
    mid = r1.astype(BF16)
    lo = (r1 - mid.astype(F32)).astype(BF16)
    return (jnp.dot(hi, m01, preferred_element_type=F32)
            + jnp.dot(mid, m01, preferred_element_type=F32)
            + jnp.dot(lo, m01, preferred_element_type=F32))


def _scan_order_mask(d, chunk):
    jrow = lax.broadcasted_iota(jnp.int32, (chunk, chunk), 0)
    icol = lax.broadcasted_iota(jnp.int32, (chunk, chunk), 1)
    sign = jnp.where(d == 0, 1, -1)
    return (icol - jrow) * sign >= 0


DECAY_ROWS = 4 * SSM_HEADS


def _ssd_decay_body(a_ref, dt_ref, fac_ref, src_ref, *, chunk):
    d = pl.program_id(1)
    nh = SSM_HEADS
    mask01 = _scan_order_mask(d, chunk).astype(BF16)
    for s in range(dt_ref.shape[2] // chunk):
        tok = slice(s * chunk, (s + 1) * chunk)
        dt = dt_ref[0, :, tok]
        cum_t = _dot_f32_by_01(dt * a_ref[0], mask01)
        cum2_t = cum_t * LOG2E
        fac_ref[0, 0, 0:nh, tok] = cum2_t
        src_ref[0, 0, tok, :] = (cum2_t - jnp.log2(dt)).T
        tot = jnp.where(d == 0, cum_t[:, chunk - 1:chunk], cum_t[:, 0:1])
        fac_ref[0, 0, nh:2 * nh, tok] = jnp.exp(tot - cum_t) * dt
        fac_ref[0, 0, 2 * nh:3 * nh, tok] = jnp.exp(cum_t)
        fac_ref[0, 0, 3 * nh:4 * nh, tok] = jnp.broadcast_to(jnp.exp(tot), cum_t.shape)


def _ssd_decay(a, dtt, chunk):
    b, _, s = dtt.shape
    span = min(s, 8 * chunk)
    body = functools.partial(_ssd_decay_body, chunk=chunk)
    return pl.pallas_call(
        body,
        grid=(b, 2, s // span),
        in_specs=[pl.BlockSpec((1, SSM_HEADS, 1), lambda i, d, g: (d, 0, 0)),
                  pl.BlockSpec((1, SSM_HEADS, span), lambda i, d, g: (i, d, g))],
        out_specs=[pl.BlockSpec((1, 1, DECAY_ROWS, span), lambda i, d, g: (i, d, 0, g)),
                   pl.BlockSpec((1, 1, span, SSM_HEADS), lambda i, d, g: (i, d, g, 0))],
        out_shape=[jax.ShapeDtypeStruct((b, 2, DECAY_ROWS, s), F32),
                   jax.ShapeDtypeStruct((b, 2, s, SSM_HEADS), F32)],
        compiler_params=_cparams("parallel", "parallel", "parallel"),
        name="ssd_decay_factors",
    )(a, dtt)


def _scan_body(fac_ref, src_ref, xs_ref, ct_ref, b_ref, s0_ref, y_ref, sout_ref, state, *, chunk):
    d = pl.program_id(1)
    c = pl.program_id(2)
    nc = pl.num_programs(2)
    hp = SSM_HEADS_PER_GROUP
    hd = SSM_D_INNER // SSM_HEADS
    gw = hp * hd
    nh = SSM_HEADS
    nsub = xs_ref.shape[2] // chunk

    @pl.when(c == 0)
    def _load_state():
        state[...] = s0_ref[0, 0]

    blk = V7X_LANES
    nblk = chunk // blk

    def decay_weights(direction, cbt, cum2_row, src_col, not_yet_diag):
        rows = []
        for jb in range(nblk):
            cols = []
            for ib in range(nblk):
                ahead = ib - jb if direction == 0 else jb - ib
                if ahead < 0:
                    cols.append(jnp.zeros((blk, blk), BF16))
                    continue
                js, is_ = slice(jb * blk, (jb + 1) * blk), slice(ib * blk, (ib + 1) * blk)
                seg = cum2_row[:, is_] - src_col[js, :]
                if ahead == 0:
                    seg = seg + not_yet_diag
                cols.append((cbt[js, is_] * jnp.exp2(seg)).astype(BF16))
            rows.append(jnp.concatenate(cols, axis=1))
        return jnp.concatenate(rows, axis=0)

    def one_chunk(direction, tok):
        not_yet_diag = jnp.where(_scan_order_mask(direction, blk), 0.0, -jnp.inf)
        cum2_t = fac_ref[0, 0, 0:nh, tok]
        src_term = src_ref[0, 0, tok, :]
        to_end = fac_ref[0, 0, nh:2 * nh, tok]
        ecum = fac_ref[0, 0, 2 * nh:3 * nh, tok]
        etot = fac_ref[0, 0, 3 * nh:4 * nh, tok.start:tok.start + 1]
        for g in range(SSM_GROUPS):
            bg = b_ref[0, tok, g * SSM_STATE:(g + 1) * SSM_STATE]
            ctg = ct_ref[0, g * SSM_STATE:(g + 1) * SSM_STATE, tok]
            cbt = jnp.dot(bg, ctg, preferred_element_type=F32)
            xg = xs_ref[0, g * gw:(g + 1) * gw, tok]
            sg = state[g * gw:(g + 1) * gw, :]
            hs = slice(g * hp, (g + 1) * hp)
            y_state = (jnp.dot(sg.astype(BF16), ctg, preferred_element_type=F32)
                       .reshape(hp, hd, chunk) * ecum[hs][:, None, :])
            outs = []
            for r in range(hp):
                h = g * hp + r
                w = decay_weights(direction, cbt, cum2_t[h:h + 1, :], src_term[:, h:h + 1],
                                  not_yet_diag)
                outs.append(jnp.dot(xg[r * hd:(r + 1) * hd], w, preferred_element_type=F32)
                            + y_state[r])
            y_ref[0, 0, g * gw:(g + 1) * gw, tok] = jnp.concatenate(outs, axis=0).astype(BF16)
            xw = (xg.reshape(hp, hd, chunk) * to_end[hs].astype(BF16)[:, None, :]).reshape(gw, chunk)
            upd = jnp.dot(xw, bg, preferred_element_type=F32)
            decayed = (sg.reshape(hp, hd, SSM_STATE) * etot[hs][:, None, :]).reshape(gw, SSM_STATE)
            state[g * gw:(g + 1) * gw, :] = decayed + upd

    for direction in (0, 1):
        @pl.when(d == direction)
        def _walk(direction=direction):
            order = range(nsub) if direction == 0 else reversed(range(nsub))
            for sub in order:
                one_chunk(direction, slice(sub * chunk, (sub + 1) * chunk))

    @pl.when(c == nc - 1)
    def _store_state():
        sout_ref[0, 0] = state[...]


def _ssd_scan(a, dtt, xst, ct, bm, s0):
    b, ch, s = xst.shape
    chunk = min(SSD_CHUNK, s)
    span = min(s, SSD_CHUNKS_PER_STEP * chunk)
    nc = s // span
    bn = SSM_GROUPS * SSM_STATE
    fac, src = _ssd_decay(a, dtt, chunk)

    def cidx(d, c):
        return c + d * (nc - 1 - 2 * c)

    body = functools.partial(_scan_body, chunk=chunk)
    return pl.pallas_call(
        body,
        grid=(b, 2, nc),
        in_specs=[pl.BlockSpec((1, 1, DECAY_ROWS, span), lambda i, d, c: (i, d, 0, cidx(d, c))),
                  pl.BlockSpec((1, 1, span, SSM_HEADS), lambda i, d, c: (i, d, cidx(d, c), 0)),
                  pl.BlockSpec((1, ch, span), lambda i, d, c: (i, 0, cidx(d, c))),
                  pl.BlockSpec((1, bn, span), lambda i, d, c: (i, 0, cidx(d, c))),
                  pl.BlockSpec((1, span, bn), lambda i, d, c: (i, cidx(d, c), 0)),
                  pl.BlockSpec((1, 1, ch, SSM_STATE), lambda i, d, c: (d, i, 0, 0))],
        out_specs=[pl.BlockSpec((1, 1, ch, span), lambda i, d, c: (d, i, 0, cidx(d, c))),
                   pl.BlockSpec((1, 1, ch, SSM_STATE), lambda i, d, c: (d, i, 0, 0))],
        out_shape=[jax.ShapeDtypeStruct((2, b, ch, s), BF16),
                   jax.ShapeDtypeStruct((2, b, ch, SSM_STATE), F32)],
        scratch_shapes=[pltpu.VMEM((ch, SSM_STATE), F32)],
        compiler_params=_cparams("parallel", "parallel", "arbitrary"),
        name="ssd_scan",
    )(fac, src, xst, ct, bm, s0)


def _ssm_out_body(x_ref, mod_ref, yf_ref, yb_ref, xs_ref, z_ref, dsk_ref, ng_ref, wt_ref, o_ref):
    y = (yf_ref[0, 0].astype(F32) + yb_ref[0, 0].astype(F32)
         + dsk_ref[...] * xs_ref[0].astype(F32))
    y = y * _silu(z_ref[0].astype(F32))
    ms = jnp.mean(y * y, axis=0, keepdims=True)
    yn = (y * lax.rsqrt(ms + NORM_EPS) * ng_ref[...]).astype(BF16)
    ot = jnp.dot(wt_ref[...], yn, preferred_element_type=F32)
    o_ref[0] = x_ref[0] + mod_ref[0][2:3] * ot.T


def _ssm_out(x, mod_l, mod_row, y, xst, zt, dsk, ng, wt):
    b, s, d = x.shape
    tm = min(TOKEN_TILE, s)
    ch = xst.shape[1]
    return pl.pallas_call(
        _ssm_out_body,
        grid=(b, s // tm),
        in_specs=[pl.BlockSpec((1, tm, d), lambda i, t: (i, t, 0)),
                  pl.BlockSpec((1, 6, d), lambda i, t: (mod_row(i), 0, 0)),
                  pl.BlockSpec((1, 1, ch, tm), lambda i, t: (0, i, 0, t)),
                  pl.BlockSpec((1, 1, ch, tm), lambda i, t: (1, i, 0, t)),
                  pl.BlockSpec((1, ch, tm), lambda i, t: (i, 0, t)),
                  pl.BlockSpec((1, ch, tm), lambda i, t: (i, 0, t)),
                  _resident((ch, 1)),
                  _resident((ch, 1)),
                  _resident((d, ch))],
        out_specs=pl.BlockSpec((1, tm, d), lambda i, t: (i, t, 0)),
        out_shape=jax.ShapeDtypeStruct((b, s, d), F32),
        compiler_params=_cparams("parallel", "parallel"),
        name="ssm_out_residual",
    )(x, mod_l, y, y, xst, zt, dsk, ng, wt)


def _rope_tables_t(n):
    t = jnp.arange(n)
    inv_freq = 1.0 / (ROPE_BASE ** (jnp.arange(ROPE_PAIRS, dtype=F32) / ROPE_PAIRS))
    ang_r = (t // GRID_W).astype(F32)[None, :] * inv_freq[:, None]
    ang_c = (t % GRID_W).astype(F32)[None, :] * inv_freq[:, None]
    cr, sr, cc, sc = jnp.cos(ang_r), jnp.sin(ang_r), jnp.cos(ang_c), jnp.sin(ang_c)
    return (jnp.concatenate([cr, cr, cc, cc], axis=0),
            jnp.concatenate([-sr, sr, -sc, sc], axis=0))


def _attn_layer(x, ctx, mod_l, lat_row, ctx_row, p, rope, lambda_init, with_ctx):
    wt = p["w_in"].T.astype(BF16)
    g = p["norm_g"].reshape(1, D_MODEL)
    qg = p["q_norm_g"].reshape(HEAD_DIM, 1)
    kg = p["k_norm_g"].reshape(HEAD_DIM, 1)
    cos_t, sin_t = rope
    c = ctx.shape[1]
    qt, *lat_kv = _attn_in(x, mod_l, lat_row, g, wt, cos_t, sin_t, qg, kg, rope=True)
    qtc, *ctx_kv = _attn_in(ctx, mod_l, ctx_row, g, wt, cos_t[:, :c], sin_t[:, :c], qg, kg,
                            rope=False)
    lam_vecs = [p[k].reshape(1, HEAD_DIM) for k in ("lq1", "lk1", "lq2", "lk2")]
    sg = p["subln_g"].reshape(V_ROWS, 1)
    w_out = p["w_out"].astype(BF16)
    x = _flash(x, mod_l, lat_row, w_out, lam_vecs, sg, qt, ctx_kv, lat_kv,
               lambda_init=lambda_init)
    if with_ctx:
        ctx = _flash(ctx, mod_l, ctx_row, w_out, lam_vecs, sg, qtc, ctx_kv, None,
                     lambda_init=lambda_init)
    return x, ctx


def _ssm_layer(x, ctx, mod_l, lat_row, ctx_row, p, with_ctx):
    wt = p["w_in"].T.astype(BF16)
    g = p["norm_g"].reshape(1, D_MODEL)
    dtb = p["dt_bias"].reshape(2 * SSM_HEADS, 1)
    conv_w = jnp.broadcast_to(p["conv_w"][:, :, None], (3, SSM_XBC, V7X_LANES))
    conv_b = jnp.broadcast_to(p["conv_b"][:, None], (SSM_XBC, V7X_LANES))
    a = (-jnp.exp(p["a_log"].astype(F32))).reshape(2, SSM_HEADS, 1)
    b = x.shape[0]

    def pre(v, row):
        zt, xbct, dtt = _ssm_proj(v, mod_l, row, g, wt, dtb)
        xst, ct, bm = _ssm_conv(xbct, conv_w, conv_b)
        return zt, xst, ct, bm, dtt

    zt_c, xst_c, ct_c, bm_c, dtt_c = pre(ctx, ctx_row)
    zt_l, xst_l, ct_l, bm_l, dtt_l = pre(x, lat_row)
    zero = jnp.zeros((2, b, SSM_D_INNER, SSM_STATE), F32)
    y_c, s_ctx = _ssd_scan(a, dtt_c, xst_c, ct_c, bm_c, zero)
    y_l, _ = _ssd_scan(a, dtt_l, xst_l, ct_l, bm_l, s_ctx)
    dsk = jnp.repeat(p["d_skip"], SSM_D_INNER // SSM_HEADS).reshape(SSM_D_INNER, 1)
    ng = p["out_norm_g"].reshape(SSM_D_INNER, 1)
    w_out_t = p["w_out"].T.astype(BF16)
    x = _ssm_out(x, mod_l, lat_row, y_l, xst_l, zt_l, dsk, ng, w_out_t)
    if with_ctx:
        ctx = _ssm_out(ctx, mod_l, ctx_row, y_c, xst_c, zt_c, dsk, ng, w_out_t)
    return x, ctx


def kernel(x, c, ctx, c_ctx, mod_w, mod_b, norm_mix_g, norm_ffn_g, attn_w_in, attn_w_out,
           diff_lq1, diff_lk1, diff_lq2, diff_lk2, diff_subln_g, gqa_q_norm_g, gqa_k_norm_g,
           ssm_w_in, ssm_conv_w, ssm_conv_b, ssm_dt_bias, ssm_a_log, ssm_d, ssm_norm_g, ssm_w_out,
           ffn_w_in, ffn_conv_w, ffn_conv_b, ffn_w_out, final_norm_g):
    b, n, d = x.shape
    mod_rows = 16
    c_rows = jnp.zeros((mod_rows, d), F32).at[:b].set(c).at[b].set(c_ctx)
    mod = _mod_all(c_rows, mod_w, mod_b).reshape(DEPTH, mod_rows, 6, d)
    lat_row = lambda i: i
    ctx_row = lambda i: b
    rope = _rope_tables_t(n)

    for layer in range(DEPTH):
        with_ctx = layer < DEPTH - 1
        mod_l = mod[layer]
        i = layer // 2
        if layer % 2 == 0:
            p = dict(w_in=attn_w_in[i], w_out=attn_w_out[i], norm_g=norm_mix_g[layer],
                     lq1=diff_lq1[i], lk1=diff_lk1[i], lq2=diff_lq2[i], lk2=diff_lk2[i],
                     subln_g=diff_subln_g[i], q_norm_g=gqa_q_norm_g[i], k_norm_g=gqa_k_norm_g[i])
            lambda_init = 0.8 - 0.6 * math.exp(-0.3 * layer)
            x, ctx = _attn_layer(x, ctx, mod_l, lat_row, ctx_row, p, rope, lambda_init, with_ctx)
        else:
            p = dict(w_in=ssm_w_in[i], norm_g=norm_mix_g[layer], conv_w=ssm_conv_w[i],
                     conv_b=ssm_conv_b[i], dt_bias=ssm_dt_bias[i], a_log=ssm_a_log[i],
                     d_skip=ssm_d[i], out_norm_g=ssm_norm_g[i], w_out=ssm_w_out[i])
            x, ctx = _ssm_layer(x, ctx, mod_l, lat_row, ctx_row, p, with_ctx)
        g = norm_ffn_g[layer].reshape(1, d)
        wv = ffn_w_in[layer][:, :D_FF].astype(BF16)
        wg = ffn_w_in[layer][:, D_FF:].astype(BF16)
        cw = ffn_conv_w[layer]
        cb = ffn_conv_b[layer].reshape(1, D_FF)
        wo = ffn_w_out[layer].astype(BF16)
        fg = final_norm_g.reshape(1, d)
        x = _ffn(x, mod_l, lat_row, g, wv, wg, cw, cb, wo, fg, final_norm=layer == DEPTH - 1)
        if with_ctx:
            ctx = _ffn(ctx, mod_l, ctx_row, g, wv, wg, cw, cb, wo, fg, final_norm=False)
    return x
```

```python
import functools
import math

import jax
import jax.numpy as jnp
from jax import lax
from jax.experimental import pallas as pl
from jax.experimental.pallas import tpu as pltpu

F32 = jnp.float32
BF16 = jnp.bfloat16

D_MODEL = 1024
DEPTH = 4
GRID_W = 64
HEAD_DIM = 64
ROPE_PAIRS = HEAD_DIM // 4
ROPE_BASE = 10000.0
NORM_EPS = 1e-6
DIFF_HEADS = 4
GQA_HEADS = 8
GQA_KV_HEADS = 2
GQA_GROUP = GQA_HEADS // GQA_KV_HEADS
ATTN_IN_COLS = 2304
SSM_D_INNER = 2048
SSM_HEADS = 32
SSM_GROUPS = 4
SSM_HEADS_PER_GROUP = SSM_HEADS // SSM_GROUPS
SSM_STATE = 128
SSM_XBC = 3072
SSM_IN_COLS = 5184
D_FF = 2816
LOG2E = math.log2(math.e)

V7X_VMEM_BYTES = 64 * 1024 * 1024
VMEM_LIMIT_BYTES = V7X_VMEM_BYTES - 8 * 1024 * 1024
V7X_LANES = 128
BF16_SUBLANES = 16

V_ROWS = 2 * HEAD_DIM
V_ROWS_PADDED = V_ROWS + BF16_SUBLANES
GV_ROWS_PADDED = HEAD_DIM + BF16_SUBLANES
K_COLS = (DIFF_HEADS + 1) * 2 * HEAD_DIM

ATTN_KEY_SUBBLOCK = 256
ATTN_STREAM_MAX_OCTAVES = 64.0

CONV_HALO = 16
SSD_CHUNK = 256
SSD_CHUNKS_PER_STEP = 4

TOKEN_TILE = 512
FFN_TOKEN_TILE = 256
ATTN_QUERY_TILE = 256
ATTN_KEY_CHUNK = 2048
MOD_COL_TILE = 2048

NT_DIMS = (((1,), (1,)), ((), ()))


def _cparams(*sem):
    return pltpu.CompilerParams(dimension_semantics=sem, vmem_limit_bytes=VMEM_LIMIT_BYTES)


def _resident(shape):
    nd = len(shape)
    return pl.BlockSpec(shape, lambda *_: (0,) * nd, pipeline_mode=pl.Buffered(1))


def _silu(v):
    return v * jax.nn.sigmoid(v)


def _modulate(x, g, shift, scale):
    ms = jnp.mean(x * x, axis=-1, keepdims=True)
    return (x * lax.rsqrt(ms + NORM_EPS) * g) * (1.0 + scale) + shift


def _mod_body(c_ref, w_ref, b_ref, o_ref):
    s = _silu(c_ref[...])
    o_ref[0] = jnp.dot(s, w_ref[0], preferred_element_type=F32,
                       precision=lax.Precision.HIGHEST) + b_ref[0]


def _mod_all(c_rows, mod_w, mod_b):
    rows = c_rows.shape[0]
    depth, d, cols = mod_w.shape
    tn = MOD_COL_TILE
    return pl.pallas_call(
        _mod_body,
        grid=(depth, cols // tn),
        in_specs=[pl.BlockSpec((rows, d), lambda l, n: (0, 0)),
                  pl.BlockSpec((1, d, tn), lambda l, n: (l, 0, n)),
                  pl.BlockSpec((1, 1, tn), lambda l, n: (l, 0, n))],
        out_specs=pl.BlockSpec((1, rows, tn), lambda l, n: (l, 0, n)),
        out_shape=jax.ShapeDtypeStruct((depth, rows, cols), F32),
        compiler_params=_cparams("parallel", "parallel"),
        name="mod_vectors",
    )(c_rows, mod_w, mod_b.reshape(depth, 1, cols))


def _attn_in_body(x_ref, mod_ref, g_ref, wt_ref, cos_ref, sin_ref, qg_ref, kg_ref,
                  qt_ref, k_ref, vta_ref, vtb_ref, *, rope):
    m = mod_ref[0]
    h = _modulate(x_ref[0], g_ref[...], m[0:1], m[1:2]).astype(BF16)
    t = lax.dot_general(wt_ref[...], h, NT_DIMS, preferred_element_type=F32)
    tm = t.shape[1]

    def rot(u):
        if not rope:
            return u
        sw = jnp.concatenate([u[:, 16:32], u[:, 0:16], u[:, 48:64], u[:, 32:48]], axis=1)
        return u * cos_ref[...][None] + sw * sin_ref[...][None]

    def qk_norm(u, g):
        ms = jnp.mean(u * u, axis=1, keepdims=True)
        return u * lax.rsqrt(ms + NORM_EPS) * g[None]

    nq = 2 * DIFF_HEADS
    qa = rot(t[0:512].reshape(nq, HEAD_DIM, tm))
    ka = rot(t[512:1024].reshape(nq, HEAD_DIM, tm))
    va = t[1024:1536]
    qb = rot(qk_norm(t[1536:2048].reshape(GQA_HEADS, HEAD_DIM, tm), qg_ref[...]))
    kb = rot(qk_norm(t[2048:2176].reshape(GQA_KV_HEADS, HEAD_DIM, tm), kg_ref[...]))
    vb = t[2176:2304]

    qs = (HEAD_DIM ** -0.5) * LOG2E
    qt_ref[0, 0:512] = (qa * qs).reshape(512, tm).astype(BF16)
    qt_ref[0, 512:1024] = (qb * qs).reshape(512, tm).astype(BF16)
    kt = jnp.concatenate([ka.reshape(512, tm), kb.reshape(128, tm)], axis=0)
    k_ref[0] = kt.T.astype(BF16)
    ones = jnp.ones((BF16_SUBLANES, tm), BF16)
    for u in range(DIFF_HEADS):
        vta_ref[0, u, 0:V_ROWS] = va[u * V_ROWS:(u + 1) * V_ROWS].astype(BF16)
        vta_ref[0, u, V_ROWS:V_ROWS_PADDED] = ones
    for g in range(GQA_KV_HEADS):
        vtb_ref[0, g, 0:HEAD_DIM] = vb[g * HEAD_DIM:(g + 1) * HEAD_DIM].astype(BF16)
        vtb_ref[0, g, HEAD_DIM:GV_ROWS_PADDED] = ones


def _attn_in(x, mod_l, mod_row, g, wt, cos_t, sin_t, qg, kg, *, rope):
    b, s, d = x.shape
    tm = min(TOKEN_TILE, s)
    body = functools.partial(_attn_in_body, rope=rope)
    return pl.pallas_call(
        body,
        grid=(b, s // tm),
        in_specs=[pl.BlockSpec((1, tm, d), lambda i, t: (i, t, 0)),
                  pl.BlockSpec((1, 6, d), lambda i, t: (mod_row(i), 0, 0)),
                  _resident((1, d)),
                  _resident((ATTN_IN_COLS, d)),
                  pl.BlockSpec((HEAD_DIM, tm), lambda i, t: (0, t)),
                  pl.BlockSpec((HEAD_DIM, tm), lambda i, t: (0, t)),
                  _resident((HEAD_DIM, 1)),
                  _resident((HEAD_DIM, 1))],
        out_specs=[pl.BlockSpec((1, 1024, tm), lambda i, t: (i, 0, t)),
                   pl.BlockSpec((1, tm, K_COLS), lambda i, t: (i, t, 0)),
                   pl.BlockSpec((1, DIFF_HEADS, V_ROWS_PADDED, tm), lambda i, t: (i, 0, 0, t)),
                   pl.BlockSpec((1, GQA_KV_HEADS, GV_ROWS_PADDED, tm), lambda i, t: (i, 0, 0, t))],
        out_shape=[jax.ShapeDtypeStruct((b, 1024, s), BF16),
                   jax.ShapeDtypeStruct((b, s, K_COLS), BF16),
                   jax.ShapeDtypeStruct((b, DIFF_HEADS, V_ROWS_PADDED, s), BF16),
                   jax.ShapeDtypeStruct((b, GQA_KV_HEADS, GV_ROWS_PADDED, s), BF16)],
        compiler_params=_cparams("parallel", "parallel"),
        name="attn_in_rope" if rope else "attn_in_ctx",
    )(x, mod_l, g, wt, cos_t, sin_t, qg, kg)


def _attn_unit_cols(tq):
    units = []
    for h in range(DIFF_HEADS):
        units.append((h * 128, h, V_ROWS_PADDED, h * 2 * tq, 2 * tq))
    base = DIFF_HEADS * 2 * tq
    for g in range(GQA_KV_HEADS):
        units.append((DIFF_HEADS * 128, g, GV_ROWS_PADDED, base + g * GQA_GROUP * tq,
                      GQA_GROUP * tq))
    return units


def _flash_body(*refs, tq, lambda_init, has_lat):
    if has_lat:
        (lq1, lk1, lq2, lk2, sg_ref, x_ref, mod_ref, wo_ref, qt_ref, kc_ref, vtac_ref, vtbc_ref,
         kl_ref, vtal_ref, vtbl_ref, o_ref, rhs, acc, mrow, pv_new, m_chunk) = refs
    else:
        (lq1, lk1, lq2, lk2, sg_ref, x_ref, mod_ref, wo_ref, qt_ref, kc_ref, vtac_ref, vtbc_ref,
         o_ref, rhs, acc, mrow, pv_new, m_chunk) = refs
    j = pl.program_id(2)
    nj = pl.num_programs(2)
    units = _attn_unit_cols(tq)
    diff_cols = DIFF_HEADS * 2 * tq

    def values(vta_ref, vtb_ref, u, vu):
        return vta_ref[0, vu] if u < DIFF_HEADS else vtb_ref[0, vu]

    def exact_step(k_ref, vta_ref, vtb_ref):
        for u, (kc0, vu, vr, c0, w) in enumerate(units):
            s = jnp.dot(k_ref[0, :, kc0:kc0 + 128], rhs[:, c0:c0 + w],
                        preferred_element_type=F32)
            mp = mrow[:, c0:c0 + w]
            mn = jnp.maximum(mp, jnp.max(s, axis=0, keepdims=True))
            alpha = jnp.exp2(mp - mn)
            p = jnp.exp2(s - mn).astype(BF16)
            pv = jnp.dot(values(vta_ref, vtb_ref, u, vu), p, preferred_element_type=F32)
            acc[0:vr, c0:c0 + w] = acc[0:vr, c0:c0 + w] * alpha + pv
            mrow[:, c0:c0 + w] = mn

    def streaming_step(k_ref, vta_ref, vtb_ref):
        nk = k_ref.shape[1]
        sub = min(nk, ATTN_KEY_SUBBLOCK)
        for u, (kc0, vu, vr, c0, w) in enumerate(units):
            m_used = mrow[:, c0:c0 + w]
            cm = None
            parts = []
            for r in range(nk // sub):
                s = jnp.dot(k_ref[0, r * sub:(r + 1) * sub, kc0:kc0 + 128], rhs[:, c0:c0 + w],
                            preferred_element_type=F32)
                parts.append(jnp.exp2(s - m_used).astype(BF16))
                sm = jnp.max(s.reshape(sub // 8, 8, w), axis=0)
                cm = sm if cm is None else jnp.maximum(cm, sm)
            p = jnp.concatenate(parts, axis=0)
            pv_new[0:vr, c0:c0 + w] = jnp.dot(values(vta_ref, vtb_ref, u, vu), p,
                                              preferred_element_type=F32)
            m_chunk[:, c0:c0 + w] = jnp.max(cm, axis=0, keepdims=True)

    @pl.when(j == 0)
    def _init():
        zeros = jnp.zeros((HEAD_DIM, tq), BF16)
        for h in range(DIFF_HEADS):
            c0 = h * 2 * tq
            rhs[0:64, c0:c0 + tq] = qt_ref[0, h * 128:h * 128 + 64, :]
            rhs[64:128, c0:c0 + tq] = zeros
            rhs[0:64, c0 + tq:c0 + 2 * tq] = zeros
            rhs[64:128, c0 + tq:c0 + 2 * tq] = qt_ref[0, h * 128 + 64:h * 128 + 128, :]
        base = DIFF_HEADS * 2 * tq
        for g in range(GQA_KV_HEADS):
            for r in range(GQA_GROUP):
                c0 = base + (g * GQA_GROUP + r) * tq
                hd = 512 + (g * GQA_GROUP + r) * HEAD_DIM
                rhs[g * 64:(g + 1) * 64, c0:c0 + tq] = qt_ref[0, hd:hd + HEAD_DIM, :]
                rhs[(1 - g) * 64:(2 - g) * 64, c0:c0 + tq] = zeros
        acc[...] = jnp.zeros(acc.shape, F32)
        mrow[...] = jnp.zeros(mrow.shape, F32)

    row_blocks = ((V_ROWS_PADDED, slice(0, diff_cols)),
                  (GV_ROWS_PADDED, slice(diff_cols, acc.shape[1])))

    def nothing_pending():
        for rows, cols in row_blocks:
            pv_new[0:rows, cols] = jnp.zeros((rows, cols.stop - cols.start), F32)
        m_chunk[...] = jnp.full(m_chunk.shape, -jnp.inf, F32)

    def commit():
        mp = mrow[...]
        mn = jnp.maximum(mp, m_chunk[...])
        mrow[...] = mn
        alpha = jnp.exp2(mp - mn)
        for rows, cols in row_blocks:
            acc[0:rows, cols] = (acc[0:rows, cols] + pv_new[0:rows, cols]) * alpha[:, cols]

    def stream_chunk(k_ref, vta_ref, vtb_ref, first):
        streaming_step(k_ref, vta_ref, vtb_ref)
        dev = m_chunk[...] - mrow[...]
        in_range = jnp.max(jnp.abs(dev) if first else dev) <= ATTN_STREAM_MAX_OCTAVES

        @pl.when(jnp.logical_not(in_range))
        def _redo():
            if first:
                mrow[...] = jnp.full(mrow.shape, -jnp.inf, F32)
            exact_step(k_ref, vta_ref, vtb_ref)
            nothing_pending()

    @pl.when(j == 0)
    def _context_keys():
        stream_chunk(kc_ref, vtac_ref, vtbc_ref, True)

    if has_lat:
        commit()
        stream_chunk(kl_ref, vtal_ref, vtbl_ref, False)

    @pl.when(j == nj - 1)
    def _finish():
        commit()
        lam = (jnp.exp(jnp.sum(lq1[...] * lk1[...], keepdims=True))
               - jnp.exp(jnp.sum(lq2[...] * lk2[...], keepdims=True)) + lambda_init)
        pieces = []
        for h in range(DIFF_HEADS):
            c0 = h * 2 * tq
            o1 = acc[0:V_ROWS, c0:c0 + tq] / acc[V_ROWS:V_ROWS + 1, c0:c0 + tq]
            o2 = acc[0:V_ROWS, c0 + tq:c0 + 2 * tq] / acc[V_ROWS:V_ROWS + 1, c0 + tq:c0 + 2 * tq]
            oh = o1 - lam * o2
            ms = jnp.mean(oh * oh, axis=0, keepdims=True)
            pieces.append(oh * lax.rsqrt(ms + NORM_EPS) * sg_ref[...] * (1.0 - lambda_init))
        base = DIFF_HEADS * 2 * tq
        for g in range(GQA_KV_HEADS):
            for r in range(GQA_GROUP):
                c0 = base + (g * GQA_GROUP + r) * tq
                pieces.append(acc[0:HEAD_DIM, c0:c0 + tq] / acc[HEAD_DIM:HEAD_DIM + 1, c0:c0 + tq])
        o = jnp.concatenate(pieces, axis=0).T.astype(BF16)
        y = jnp.dot(o, wo_ref[...], preferred_element_type=F32)
        o_ref[0] = x_ref[0] + mod_ref[0][2:3] * y


def _flash(x, mod_l, mod_row, w_out, lam_vecs, sg, qt, ctx_kv, lat_kv, *, lambda_init):
    b, _, sq = qt.shape
    d = x.shape[2]
    kc, vtac, vtbc = ctx_kv
    c = kc.shape[1]
    has_lat = lat_kv is not None
    tq = min(ATTN_QUERY_TILE, sq)
    ncols = (DIFF_HEADS * 2 + GQA_HEADS) * tq
    in_specs = [_resident((1, HEAD_DIM))] * 4 + [
        _resident((V_ROWS, 1)),
        pl.BlockSpec((1, tq, d), lambda i, q, j: (i, q, 0)),
        pl.BlockSpec((1, 6, d), lambda i, q, j: (mod_row(i), 0, 0)),
        _resident((1024, d)),
        pl.BlockSpec((1, 1024, tq), lambda i, q, j: (i, 0, q)),
        pl.BlockSpec((1, c, K_COLS), lambda i, q, j: (i, 0, 0)),
        pl.BlockSpec((1, DIFF_HEADS, V_ROWS_PADDED, c), lambda i, q, j: (i, 0, 0, 0)),
        pl.BlockSpec((1, GQA_KV_HEADS, GV_ROWS_PADDED, c), lambda i, q, j: (i, 0, 0, 0)),
    ]
    args = list(lam_vecs) + [sg, x, mod_l, w_out, qt, kc, vtac, vtbc]
    nkv = 1
    if has_lat:
        kl, vtal, vtbl = lat_kv
        n = kl.shape[1]
        tk = min(ATTN_KEY_CHUNK, n)
        nkv = n // tk
        in_specs += [pl.BlockSpec((1, tk, K_COLS), lambda i, q, j: (i, j, 0)),
                     pl.BlockSpec((1, DIFF_HEADS, V_ROWS_PADDED, tk), lambda i, q, j: (i, 0, 0, j)),
                     pl.BlockSpec((1, GQA_KV_HEADS, GV_ROWS_PADDED, tk), lambda i, q, j: (i, 0, 0, j))]
        args += [kl, vtal, vtbl]
    scratch = [pltpu.VMEM((2 * HEAD_DIM, ncols), BF16),
               pltpu.VMEM((V_ROWS_PADDED, ncols), F32),
               pltpu.VMEM((1, ncols), F32),
               pltpu.VMEM((V_ROWS_PADDED, ncols), F32),
               pltpu.VMEM((1, ncols), F32)]
    body = functools.partial(_flash_body, tq=tq, lambda_init=lambda_init, has_lat=has_lat)
    return pl.pallas_call(
        body,
        grid=(b, sq // tq, nkv),
        in_specs=in_specs,
        out_specs=pl.BlockSpec((1, tq, d), lambda i, q, j: (i, q, 0)),
        out_shape=jax.ShapeDtypeStruct((b, sq, d), F32),
        scratch_shapes=scratch,
        compiler_params=_cparams("parallel", "parallel", "arbitrary"),
        name="attn_sweep_lat" if has_lat else "attn_sweep_ctx",
    )(*args)


def _ffn_body(x_ref, xp_ref, xn_ref, mod_ref, g_ref, wv_ref, wg_ref, cw_ref, cb_ref, wo_ref,
              fg_ref, o_ref, *, tm, final_norm):
    t = pl.program_id(1)
    nt = pl.num_programs(1)
    m = mod_ref[0]
    x = x_ref[0]
    xe = jnp.concatenate([xp_ref[0], x, xn_ref[0]], axis=0)
    he = _modulate(xe, g_ref[...], m[3:4], m[4:5]).astype(BF16)
    ge = jnp.dot(he, wg_ref[...], preferred_element_type=F32)
    rows = lax.broadcasted_iota(jnp.int32, (tm + 2 * CONV_HALO, 1), 0)
    lo = jnp.where(t > 0, 0, CONV_HALO)
    hi = jnp.where(t < nt - 1, tm + 2 * CONV_HALO, tm + CONV_HALO)
    ge = jnp.where((rows >= lo) & (rows < hi), ge, 0.0)
    val = jnp.dot(he[CONV_HALO:CONV_HALO + tm], wv_ref[...], preferred_element_type=F32)
    cw = cw_ref[...]
    h0 = CONV_HALO
    conv = (cb_ref[...] + cw[0:1] * ge[h0 - 1:h0 - 1 + tm] + cw[1:2] * ge[h0:h0 + tm]
            + cw[2:3] * ge[h0 + 1:h0 + 1 + tm])
    gelu = 0.5 * conv * (1.0 + lax.erf(conv * math.sqrt(0.5)))
    act = (gelu * val).astype(BF16)
    y = jnp.dot(act, wo_ref[...], preferred_element_type=F32)
    out = x + m[5:6] * y
    if final_norm:
        ms = jnp.mean(out * out, axis=-1, keepdims=True)
        out = out * lax.rsqrt(ms + NORM_EPS) * fg_ref[...]
    o_ref[0] = out


def _ffn(x, mod_l, mod_row, g, wv, wg, cw, cb, wo, fg, *, final_norm):
    b, s, d = x.shape
    tm = min(FFN_TOKEN_TILE, s)
    hb = tm // CONV_HALO
    last = s // CONV_HALO - 1
    body = functools.partial(_ffn_body, tm=tm, final_norm=final_norm)
    return pl.pallas_call(
        body,
        grid=(b, s // tm),
        in_specs=[pl.BlockSpec((1, tm, d), lambda i, t: (i, t, 0)),
                  pl.BlockSpec((1, CONV_HALO, d), lambda i, t: (i, jnp.maximum(t * hb - 1, 0), 0)),
                  pl.BlockSpec((1, CONV_HALO, d), lambda i, t: (i, jnp.minimum((t + 1) * hb, last), 0)),
                  pl.BlockSpec((1, 6, d), lambda i, t: (mod_row(i), 0, 0)),
                  _resident((1, d)),
                  _resident((d, D_FF)),
                  _resident((d, D_FF)),
                  _resident((3, D_FF)),
                  _resident((1, D_FF)),
                  _resident((D_FF, d)),
                  _resident((1, d))],
        out_specs=pl.BlockSpec((1, tm, d), lambda i, t: (i, t, 0)),
        out_shape=jax.ShapeDtypeStruct((b, s, d), F32),
        compiler_params=_cparams("parallel", "parallel"),
        name="conv_glu_ffn",
    )(x, x, x, mod_l, g, wv, wg, cw, cb, wo, fg)


def _ssm_proj_body(x_ref, mod_ref, g_ref, wt_ref, dtb_ref, zt_ref, xbct_ref, dtt_ref):
    m = mod_ref[0]
    h = _modulate(x_ref[0], g_ref[...], m[0:1], m[1:2]).astype(BF16)
    t = lax.dot_general(wt_ref[...], h, NT_DIMS, preferred_element_type=F32)
    zt_ref[0] = t[0:SSM_D_INNER].astype(BF16)
    xbct_ref[0] = t[SSM_D_INNER:SSM_D_INNER + SSM_XBC].astype(BF16)
    dtt_ref[0] = jax.nn.softplus(t[SSM_D_INNER + SSM_XBC:SSM_IN_COLS] + dtb_ref[...])


def _ssm_proj(x, mod_l, mod_row, g, wt, dtb):
    b, s, d = x.shape
    tm = min(TOKEN_TILE, s)
    return pl.pallas_call(
        _ssm_proj_body,
        grid=(b, s // tm),
        in_specs=[pl.BlockSpec((1, tm, d), lambda i, t: (i, t, 0)),
                  pl.BlockSpec((1, 6, d), lambda i, t: (mod_row(i), 0, 0)),
                  _resident((1, d)),
                  _resident((SSM_IN_COLS, d)),
                  _resident((2 * SSM_HEADS, 1))],
        out_specs=[pl.BlockSpec((1, SSM_D_INNER, tm), lambda i, t: (i, 0, t)),
                   pl.BlockSpec((1, SSM_XBC, tm), lambda i, t: (i, 0, t)),
                   pl.BlockSpec((1, 2 * SSM_HEADS, tm), lambda i, t: (i, 0, t))],
        out_shape=[jax.ShapeDtypeStruct((b, SSM_D_INNER, s), BF16),
                   jax.ShapeDtypeStruct((b, SSM_XBC, s), BF16),
                   jax.ShapeDtypeStruct((b, 2 * SSM_HEADS, s), F32)],
        compiler_params=_cparams("parallel", "parallel"),
        name="ssm_in_proj",
    )(x, mod_l, g, wt, dtb)


def _ssm_conv_body(u_ref, up_ref, un_ref, w_ref, b_ref, xst_ref, ct_ref, bm_ref, *, tc):
    t = pl.program_id(1)
    nt = pl.num_programs(1)
    prev = jnp.where(t > 0, up_ref[0][:, V7X_LANES - 1:V7X_LANES].astype(F32), 0.0)
    nxt = jnp.where(t < nt - 1, un_ref[0][:, 0:1].astype(F32), 0.0)
    lane = lax.broadcasted_iota(jnp.int32, (1, V7X_LANES), 1)
    packed = pltpu.bitcast(u_ref[0], jnp.uint32)
    rolled_l = pltpu.bitcast(pltpu.roll(packed, 1, 1), BF16)
    rolled_r = pltpu.bitcast(pltpu.roll(packed, tc - 1, 1), BF16)
    bn = SSM_GROUPS * SSM_STATE
    edges = sorted({0, min(V7X_LANES, tc), max(tc - V7X_LANES, 0), tc})
    for c0, c1 in zip(edges[:-1], edges[1:]):
        reps = (c1 - c0) // V7X_LANES

        def lanes(a):
            return jnp.concatenate([a] * reps, axis=1) if reps > 1 else a

        u = u_ref[0, :, c0:c1].astype(F32)
        left = rolled_l[:, c0:c1].astype(F32)
        right = rolled_r[:, c0:c1].astype(F32)
        if c0 == 0:
            left = jnp.where(lane == 0, prev, left)
        if c1 == tc:
            right = jnp.where(lane == V7X_LANES - 1, nxt, right)
        v = _silu(lanes(b_ref[...]) + lanes(w_ref[0]) * left + lanes(w_ref[1]) * u
                  + lanes(w_ref[2]) * right)
        xst_ref[0, :, c0:c1] = v[0:SSM_D_INNER].astype(BF16)
        bm_ref[0, c0:c1, :] = v[SSM_D_INNER:SSM_D_INNER + bn].T.astype(BF16)
        ct_ref[0, :, c0:c1] = v[SSM_D_INNER + bn:SSM_XBC].astype(BF16)


def _ssm_conv(xbct, w, bias):
    b, ch, s = xbct.shape
    tc = min(TOKEN_TILE, s)
    hb = tc // V7X_LANES
    last = s // V7X_LANES - 1
    bn = SSM_GROUPS * SSM_STATE
    body = functools.partial(_ssm_conv_body, tc=tc)
    return pl.pallas_call(
        body,
        grid=(b, s // tc),
        in_specs=[pl.BlockSpec((1, ch, tc), lambda i, t: (i, 0, t)),
                  pl.BlockSpec((1, ch, V7X_LANES), lambda i, t: (i, 0, jnp.maximum(t * hb - 1, 0))),
                  pl.BlockSpec((1, ch, V7X_LANES), lambda i, t: (i, 0, jnp.minimum((t + 1) * hb, last))),
                  _resident((3, ch, V7X_LANES)),
                  _resident((ch, V7X_LANES))],
        out_specs=[pl.BlockSpec((1, SSM_D_INNER, tc), lambda i, t: (i, 0, t)),
                   pl.BlockSpec((1, bn, tc), lambda i, t: (i, 0, t)),
                   pl.BlockSpec((1, tc, bn), lambda i, t: (i, t, 0))],
        out_shape=[jax.ShapeDtypeStruct((b, SSM_D_INNER, s), BF16),
                   jax.ShapeDtypeStruct((b, bn, s), BF16),
                   jax.ShapeDtypeStruct((b, s, bn), BF16)],
        compiler_params=_cparams("parallel", "parallel"),
        name="ssm_conv_silu",
    )(xbct, xbct, xbct, w, bias)


def _dot_f32_by_01(a, m01):
    hi = a.astype(BF16)
    r1 = a - hi.astype(F32)
    mid = r1.astype(BF16)
    lo = (r1 - mid.astype(F32)).astype(BF16)
    return (jnp.dot(hi, m01, preferred_element_type=F32)
            + jnp.dot(mid, m01, preferred_element_type=F32)
            + jnp.dot(lo, m01, preferred_element_type=F32))


def _scan_order_mask(d, chunk):
    jrow = lax.broadcasted_iota(jnp.int32, (chunk, chunk), 0)
    icol = lax.broadcasted_iota(jnp.int32, (chunk, chunk), 1)
    sign = jnp.where(d == 0, 1, -1)
    return (icol - jrow) * sign >= 0


DECAY_ROWS = 4 * SSM_HEADS


def _ssd_decay_body(a_ref, dt_ref, fac_ref, src_ref, *, chunk):
    d = pl.program_id(1)
    nh = SSM_HEADS
    mask01 = _scan_order_mask(d, chunk).astype(BF16)
    for s in range(dt_ref.shape[2] // chunk):
        tok = slice(s * chunk, (s + 1) * chunk)
        dt = dt_ref[0, :, tok]
        cum_t = _dot_f32_by_01(dt * a_ref[0], mask01)
        cum2_t = cum_t * LOG2E
        fac_ref[0, 0, 0:nh, tok] = cum2_t
        src_ref[0, 0, tok, :] = (cum2_t - jnp.log2(dt)).T
        tot = jnp.where(d == 0, cum_t[:, chunk - 1:chunk], cum_t[:, 0:1])
        fac_ref[0, 0, nh:2 * nh, tok] = jnp.exp(tot - cum_t) * dt
        fac_ref[0, 0, 2 * nh:3 * nh, tok] = jnp.exp(cum_t)
        fac_ref[0, 0, 3 * nh:4 * nh, tok] = jnp.broadcast_to(jnp.exp(tot), cum_t.shape)


def _ssd_decay(a, dtt, chunk):
    b, _, s = dtt.shape
    span = min(s, 8 * chunk)
    body = functools.partial(_ssd_decay_body, chunk=chunk)
    return pl.pallas_call(
        body,
        grid=(b, 2, s // span),
        in_specs=[pl.BlockSpec((1, SSM_HEADS, 1), lambda i, d, g: (d, 0, 0)),
                  pl.BlockSpec((1, SSM_HEADS, span), lambda i, d, g: (i, d, g))],
        out_specs=[pl.BlockSpec((1, 1, DECAY_ROWS, span), lambda i, d, g: (i, d, 0, g)),
                   pl.BlockSpec((1, 1, span, SSM_HEADS), lambda i, d, g: (i, d, g, 0))],
        out_shape=[jax.ShapeDtypeStruct((b, 2, DECAY_ROWS, s), F32),
                   jax.ShapeDtypeStruct((b, 2, s, SSM_HEADS), F32)],
        compiler_params=_cparams("parallel", "parallel", "parallel"),
        name="ssd_decay_factors",
    )(a, dtt)


def _scan_body(fac_ref, src_ref, xs_ref, ct_ref, b_ref, s0_ref, y_ref, sout_ref, state, *, chunk):
    d = pl.program_id(1)
    c = pl.program_id(2)
    nc = pl.num_programs(2)
    hp = SSM_HEADS_PER_GROUP
    hd = SSM_D_INNER // SSM_HEADS
    gw = hp * hd
    nh = SSM_HEADS
    nsub = xs_ref.shape[2] // chunk

    @pl.when(c == 0)
    def _load_state():
        state[...] = s0_ref[0, 0]

    blk = V7X_LANES
    nblk = chunk // blk

    def decay_weights(direction, cbt, cum2_row, src_col, not_yet_diag):
        rows = []
        for jb in range(nblk):
            cols = []
            for ib in range(nblk):
                ahead = ib - jb if direction == 0 else jb - ib
                if ahead < 0:
                    cols.append(jnp.zeros((blk, blk), BF16))
                    continue
                js, is_ = slice(jb * blk, (jb + 1) * blk), slice(ib * blk, (ib + 1) * blk)
                seg = cum2_row[:, is_] - src_col[js, :]
                if ahead == 0:
                    seg = seg + not_yet_diag
                cols.append((cbt[js, is_] * jnp.exp2(seg)).astype(BF16))
            rows.append(jnp.concatenate(cols, axis=1))
        return jnp.concatenate(rows, axis=0)

    def one_chunk(direction, tok):
        not_yet_diag = jnp.where(_scan_order_mask(direction, blk), 0.0, -jnp.inf)
        cum2_t = fac_ref[0, 0, 0:nh, tok]
        src_term = src_ref[0, 0, tok, :]
        to_end = fac_ref[0, 0, nh:2 * nh, tok]
        ecum = fac_ref[0, 0, 2 * nh:3 * nh, tok]
        etot = fac_ref[0, 0, 3 * nh:4 * nh, tok.start:tok.start + 1]
        for g in range(SSM_GROUPS):
            bg = b_ref[0, tok, g * SSM_STATE:(g + 1) * SSM_STATE]
            ctg = ct_ref[0, g * SSM_STATE:(g + 1) * SSM_STATE, tok]
            cbt = jnp.dot(bg, ctg, preferred_element_type=F32)
            xg = xs_ref[0, g * gw:(g + 1) * gw, tok]
            sg = state[g * gw:(g + 1) * gw, :]
            hs = slice(g * hp, (g + 1) * hp)
            y_state = (jnp.dot(sg.astype(BF16), ctg, preferred_element_type=F32)
                       .reshape(hp, hd, chunk) * ecum[hs][:, None, :])
            outs = []
            for r in range(hp):
                h = g * hp + r
                w = decay_weights(direction, cbt, cum2_t[h:h + 1, :], src_term[:, h:h + 1],
                                  not_yet_diag)
                outs.append(jnp.dot(xg[r * hd:(r + 1) * hd], w, preferred_element_type=F32)
                            + y_state[r])
            y_ref[0, 0, g * gw:(g + 1) * gw, tok] = jnp.concatenate(outs, axis=0).astype(BF16)
            xw = (xg.reshape(hp, hd, chunk) * to_end[hs].astype(BF16)[:, None, :]).reshape(gw, chunk)
            upd = jnp.dot(xw, bg, preferred_element_type=F32)
            decayed = (sg.reshape(hp, hd, SSM_STATE) * etot[hs][:, None, :]).reshape(gw, SSM_STATE)
            state[g * gw:(g + 1) * gw, :] = decayed + upd

    for direction in (0, 1):
        @pl.when(d == direction)
        def _walk(direction=direction):
            order = range(nsub) if direction == 0 else reversed(range(nsub))
            for sub in order:
                one_chunk(direction, slice(sub * chunk, (sub + 1) * chunk))

    @pl.when(c == nc - 1)
    def _store_state():
        sout_ref[0, 0] = state[...]


def _ssd_scan(a, dtt, xst, ct, bm, s0):
    b, ch, s = xst.shape
    chunk = min(SSD_CHUNK, s)
    span = min(s, SSD_CHUNKS_PER_STEP * chunk)
    nc = s // span
    bn = SSM_GROUPS * SSM_STATE
    fac, src = _ssd_decay(a, dtt, chunk)

    def cidx(d, c):
        return c + d * (nc - 1 - 2 * c)

    body = functools.partial(_scan_body, chunk=chunk)
    return pl.pallas_call(
        body,
        grid=(b, 2, nc),
        in_specs=[pl.BlockSpec((1, 1, DECAY_ROWS, span), lambda i, d, c: (i, d, 0, cidx(d, c))),
                  pl.BlockSpec((1, 1, span, SSM_HEADS), lambda i, d, c: (i, d, cidx(d, c), 0)),
                  pl.BlockSpec((1, ch, span), lambda i, d, c: (i, 0, cidx(d, c))),
                  pl.BlockSpec((1, bn, span), lambda i, d, c: (i, 0, cidx(d, c))),
                  pl.BlockSpec((1, span, bn), lambda i, d, c: (i, cidx(d, c), 0)),
                  pl.BlockSpec((1, 1, ch, SSM_STATE), lambda i, d, c: (d, i, 0, 0))],
        out_specs=[pl.BlockSpec((1, 1, ch, span), lambda i, d, c: (d, i, 0, cidx(d, c))),
                   pl.BlockSpec((1, 1, ch, SSM_STATE), lambda i, d, c: (d, i, 0, 0))],
        out_shape=[jax.ShapeDtypeStruct((2, b, ch, s), BF16),
                   jax.ShapeDtypeStruct((2, b, ch, SSM_STATE), F32)],
        scratch_shapes=[pltpu.VMEM((ch, SSM_STATE), F32)],
        compiler_params=_cparams("parallel", "parallel", "arbitrary"),
        name="ssd_scan",
    )(fac, src, xst, ct, bm, s0)


def _ssm_out_body(x_ref, mod_ref, yf_ref, yb_ref, xs_ref, z_ref, dsk_ref, ng_ref, wt_ref, o_ref):
    y = (yf_ref[0, 0].astype(F32) + yb_ref[0, 0].astype(F32)
         + dsk_ref[...] * xs_ref[0].astype(F32))
    y = y * _silu(z_ref[0].astype(F32))
    ms = jnp.mean(y * y, axis=0, keepdims=True)
    yn = (y * lax.rsqrt(ms + NORM_EPS) * ng_ref[...]).astype(BF16)
    ot = jnp.dot(wt_ref[...], yn, preferred_element_type=F32)
    o_ref[0] = x_ref[0] + mod_ref[0][2:3] * ot.T


def _ssm_out(x, mod_l, mod_row, y, xst, zt, dsk, ng, wt):
    b, s, d = x.shape
    tm = min(TOKEN_TILE, s)
    ch = xst.shape[1]
    return pl.pallas_call(
        _ssm_out_body,
        grid=(b, s // tm),
        in_specs=[pl.BlockSpec((1, tm, d), lambda i, t: (i, t, 0)),
                  pl.BlockSpec((1, 6, d), lambda i, t: (mod_row(i), 0, 0)),
                  pl.BlockSpec((1, 1, ch, tm), lambda i, t: (0, i, 0, t)),
                  pl.BlockSpec((1, 1, ch, tm), lambda i, t: (1, i, 0, t)),
                  pl.BlockSpec((1, ch, tm), lambda i, t: (i, 0, t)),
                  pl.BlockSpec((1, ch, tm), lambda i, t: (i, 0, t)),
                  _resident((ch, 1)),
                  _resident((ch, 1)),
                  _resident((d, ch))],
        out_specs=pl.BlockSpec((1, tm, d), lambda i, t: (i, t, 0)),
        out_shape=jax.ShapeDtypeStruct((b, s, d), F32),
        compiler_params=_cparams("parallel", "parallel"),
        name="ssm_out_residual",
    )(x, mod_l, y, y, xst, zt, dsk, ng, wt)


def _rope_tables_t(n):
    t = jnp.arange(n)
    inv_freq = 1.0 / (ROPE_BASE ** (jnp.arange(ROPE_PAIRS, dtype=F32) / ROPE_PAIRS))
    ang_r = (t // GRID_W).astype(F32)[None, :] * inv_freq[:, None]
    ang_c = (t % GRID_W).astype(F32)[None, :] * inv_freq[:, None]
    cr, sr, cc, sc = jnp.cos(ang_r), jnp.sin(ang_r), jnp.cos(ang_c), jnp.sin(ang_c)
    return (jnp.concatenate([cr, cr, cc, cc], axis=0),
            jnp.concatenate([-sr, sr, -sc, sc], axis=0))


def _attn_layer(x, ctx, mod_l, lat_row, ctx_row, p, rope, lambda_init, with_ctx):
    wt = p["w_in"].T.astype(BF16)
    g = p["norm_g"].reshape(1, D_MODEL)
    qg = p["q_norm_g"].reshape(HEAD_DIM, 1)
    kg = p["k_norm_g"].reshape(HEAD_DIM, 1)
    cos_t, sin_t = rope
    c = ctx.shape[1]
    qt, *lat_kv = _attn_in(x, mod_l, lat_row, g, wt, cos_t, sin_t, qg, kg, rope=True)
    qtc, *ctx_kv = _attn_in(ctx, mod_l, ctx_row, g, wt, cos_t[:, :c], sin_t[:, :c], qg, kg,
                            rope=False)
    lam_vecs = [p[k].reshape(1, HEAD_DIM) for k in ("lq1", "lk1", "lq2", "lk2")]
    sg = p["subln_g"].reshape(V_ROWS, 1)
    w_out = p["w_out"].astype(BF16)
    x = _flash(x, mod_l, lat_row, w_out, lam_vecs, sg, qt, ctx_kv, lat_kv,
               lambda_init=lambda_init)
    if with_ctx:
        ctx = _flash(ctx, mod_l, ctx_row, w_out, lam_vecs, sg, qtc, ctx_kv, None,
                     lambda_init=lambda_init)
    return x, ctx


def _ssm_layer(x, ctx, mod_l, lat_row, ctx_row, p, with_ctx):
    wt = p["w_in"].T.astype(BF16)
    g = p["norm_g"].reshape(1, D_MODEL)
    dtb = p["dt_bias"].reshape(2 * SSM_HEADS, 1)
    conv_w = jnp.broadcast_to(p["conv_w"][:, :, None], (3, SSM_XBC, V7X_LANES))
    conv_b = jnp.broadcast_to(p["conv_b"][:, None], (SSM_XBC, V7X_LANES))
    a = (-jnp.exp(p["a_log"].astype(F32))).reshape(2, SSM_HEADS, 1)
    b = x.shape[0]

    def pre(v, row):
        zt, xbct, dtt = _ssm_proj(v, mod_l, row, g, wt, dtb)
        xst, ct, bm = _ssm_conv(xbct, conv_w, conv_b)
        return zt, xst, ct, bm, dtt

    zt_c, xst_c, ct_c, bm_c, dtt_c = pre(ctx, ctx_row)
    zt_l, xst_l, ct_l, bm_l, dtt_l = pre(x, lat_row)
    zero = jnp.zeros((2, b, SSM_D_INNER, SSM_STATE), F32)
    y_c, s_ctx = _ssd_scan(a, dtt_c, xst_c, ct_c, bm_c, zero)
    y_l, _ = _ssd_scan(a, dtt_l, xst_l, ct_l, bm_l, s_ctx)
    dsk = jnp.repeat(p["d_skip"], SSM_D_INNER // SSM_HEADS).reshape(SSM_D_INNER, 1)
    ng = p["out_norm_g"].reshape(SSM_D_INNER, 1)
    w_out_t = p["w_out"].T.astype(BF16)
    x = _ssm_out(x, mod_l, lat_row, y_l, xst_l, zt_l, dsk, ng, w_out_t)
    if with_ctx:
        ctx = _ssm_out(ctx, mod_l, ctx_row, y_c, xst_c, zt_c, dsk, ng, w_out_t)
    return x, ctx


def kernel(x, c, ctx, c_ctx, mod_w, mod_b, norm_mix_g, norm_ffn_g, attn_w_in, attn_w_out,
           diff_lq1, diff_lk1, diff_lq2, diff_lk2, diff_subln_g, gqa_q_norm_g, gqa_k_norm_g,
           ssm_w_in, ssm_conv_w, ssm_conv_b, ssm_dt_bias, ssm_a_log, ssm_d, ssm_norm_g, ssm_w_out,
           ffn_w_in, ffn_conv_w, ffn_conv_b, ffn_w_out, final_norm_g):
    b, n, d = x.shape
    mod_rows = 16
    c_rows = jnp.zeros((mod_rows, d), F32).at[:b].set(c).at[b].set(c_ctx)
    mod = _mod_all(c_rows, mod_w, mod_b).reshape(DEPTH, mod_rows, 6, d)
    lat_row = lambda i: i
    ctx_row = lambda i: b
    rope = _rope_tables_t(n)

    for layer in range(DEPTH):
        with_ctx = layer < DEPTH - 1
        mod_l = mod[layer]
        i = layer // 2
        if layer % 2 == 0:
            p = dict(w_in=attn_w_in[i], w_out=attn_w_out[i], norm_g=norm_mix_g[layer],
                     lq1=diff_lq1[i], lk1=diff_lk1[i], lq2=diff_lq2[i], lk2=diff_lk2[i],
                     subln_g=diff_subln_g[i], q_norm_g=gqa_q_norm_g[i], k_norm_g=gqa_k_norm_g[i])
            lambda_init = 0.8 - 0.6 * math.exp(-0.3 * layer)
            x, ctx = _attn_layer(x, ctx, mod_l, lat_row, ctx_row, p, rope, lambda_init, with_ctx)
        else:
            p = dict(w_in=ssm_w_in[i], norm_g=norm_mix_g[layer], conv_w=ssm_conv_w[i],
                     conv_b=ssm_conv_b[i], dt_bias=ssm_dt_bias[i], a_log=ssm_a_log[i],
                     d_skip=ssm_d[i], out_norm_g=ssm_norm_g[i], w_out=ssm_w_out[i])
            x, ctx = _ssm_layer(x, ctx, mod_l, lat_row, ctx_row, p, with_ctx)
        g = norm_ffn_g[layer].reshape(1, d)
        wv = ffn_w_in[layer][:, :D_FF].astype(BF16)
        wg = ffn_w_in[layer][:, D_FF:].astype(BF16)
        cw = ffn_conv_w[layer]
        cb = ffn_conv_b[layer].reshape(1, D_FF)
        wo = ffn_w_out[layer].astype(BF16)
        fg = final_norm_g.reshape(1, d)
        x = _ffn(x, mod_l, lat_row, g, wv, wg, cw, cb, wo, fg, final_norm=layer == DEPTH - 1)
        if with_ctx:
            ctx = _ffn(ctx, mod_l, ctx_row, g, wv, wg, cw, cb, wo, fg, final_norm=False)
    return x
```

```python
import functools
import math

import jax
import jax.numpy as jnp
from jax import lax
from jax.experimental import pallas as pl
from jax.experimental.pallas import tpu as pltpu

F32 = jnp.float32
BF16 = jnp.bfloat16

D_MODEL = 1024
DEPTH = 4
GRID_W = 64
HEAD_DIM = 64
ROPE_PAIRS = HEAD_DIM // 4
ROPE_BASE = 10000.0
NORM_EPS = 1e-6
DIFF_HEADS = 4
GQA_HEADS = 8
GQA_KV_HEADS = 2
GQA_GROUP = GQA_HEADS // GQA_KV_HEADS
ATTN_IN_COLS = 2304
SSM_D_INNER = 2048
SSM_HEADS = 32
SSM_GROUPS = 4
SSM_HEADS_PER_GROUP = SSM_HEADS // SSM_GROUPS
SSM_STATE = 128
SSM_XBC = 3072
SSM_IN_COLS = 5184
D_FF = 2816
LOG2E = math.log2(math.e)

V7X_VMEM_BYTES = 64 * 1024 * 1024
VMEM_LIMIT_BYTES = V7X_VMEM_BYTES - 8 * 1024 * 1024
V7X_LANES = 128
BF16_SUBLANES = 16

V_ROWS = 2 * HEAD_DIM
V_ROWS_PADDED = V_ROWS + BF16_SUBLANES
GV_ROWS_PADDED = HEAD_DIM + BF16_SUBLANES
K_COLS = (DIFF_HEADS + 1) * 2 * HEAD_DIM

ATTN_KEY_SUBBLOCK = 256
ATTN_STREAM_MAX_OCTAVES = 64.0

CONV_HALO = 16
CONV_CHANNEL_BLOCK = 256
SSD_CHUNK = 256
SSD_CHUNKS_PER_STEP = 4

TOKEN_TILE = 512
FFN_TOKEN_TILE = 256
ATTN_QUERY_TILE = 256
ATTN_KEY_CHUNK = 2048
MOD_COL_TILE = 2048

NT_DIMS = (((1,), (1,)), ((), ()))


def _cparams(*sem):
    return pltpu.CompilerParams(dimension_semantics=sem, vmem_limit_bytes=VMEM_LIMIT_BYTES)


def _resident(shape):
    nd = len(shape)
    return pl.BlockSpec(shape, lambda *_: (0,) * nd, pipeline_mode=pl.Buffered(1))


def _silu(v):
    return v * jax.nn.sigmoid(v)


def _modulate(x, g, shift, scale):
    ms = jnp.mean(x * x, axis=-1, keepdims=True)
    return (x * lax.rsqrt(ms + NORM_EPS) * g) * (1.0 + scale) + shift


def _mod_body(c_ref, w_ref, b_ref, o_ref):
    s = _silu(c_ref[...])
    o_ref[0] = jnp.dot(s, w_ref[0], preferred_element_type=F32,
                       precision=lax.Precision.HIGHEST) + b_ref[0]


def _mod_all(c_rows, mod_w, mod_b):
    rows = c_rows.shape[0]
    depth, d, cols = mod_w.shape
    tn = MOD_COL_TILE
    return pl.pallas_call(
        _mod_body,
        grid=(depth, cols // tn),
        in_specs=[pl.BlockSpec((rows, d), lambda l, n: (0, 0)),
                  pl.BlockSpec((1, d, tn), lambda l, n: (l, 0, n)),
                  pl.BlockSpec((1, 1, tn), lambda l, n: (l, 0, n))],
        out_specs=pl.BlockSpec((1, rows, tn), lambda l, n: (l, 0, n)),
        out_shape=jax.ShapeDtypeStruct((depth, rows, cols), F32),
        compiler_params=_cparams("parallel", "parallel"),
        name="mod_vectors",
    )(c_rows, mod_w, mod_b.reshape(depth, 1, cols))


def _attn_in_body(x_ref, mod_ref, g_ref, wt_ref, cos_ref, sin_ref, qg_ref, kg_ref,
                  qt_ref, k_ref, vta_ref, vtb_ref, *, rope):
    m = mod_ref[0]
    h = _modulate(x_ref[0], g_ref[...], m[0:1], m[1:2]).astype(BF16)
    t = lax.dot_general(wt_ref[...], h, NT_DIMS, preferred_element_type=F32)
    tm = t.shape[1]

    def rot(u):
        if not rope:
            return u
        sw = jnp.concatenate([u[:, 16:32], u[:, 0:16], u[:, 48:64], u[:, 32:48]], axis=1)
        return u * cos_ref[...][None] + sw * sin_ref[...][None]

    def qk_norm(u, g):
        ms = jnp.mean(u * u, axis=1, keepdims=True)
        return u * lax.rsqrt(ms + NORM_EPS) * g[None]

    nq = 2 * DIFF_HEADS
    qa = rot(t[0:512].reshape(nq, HEAD_DIM, tm))
    ka = rot(t[512:1024].reshape(nq, HEAD_DIM, tm))
    va = t[1024:1536]
    qb = rot(qk_norm(t[1536:2048].reshape(GQA_HEADS, HEAD_DIM, tm), qg_ref[...]))
    kb = rot(qk_norm(t[2048:2176].reshape(GQA_KV_HEADS, HEAD_DIM, tm), kg_ref[...]))
    vb = t[2176:2304]

    qs = (HEAD_DIM ** -0.5) * LOG2E
    qt_ref[0, 0:512] = (qa * qs).reshape(512, tm).astype(BF16)
    qt_ref[0, 512:1024] = (qb * qs).reshape(512, tm).astype(BF16)
    kt = jnp.concatenate([ka.reshape(512, tm), kb.reshape(128, tm)], axis=0)
    k_ref[0] = kt.T.astype(BF16)
    ones = jnp.ones((BF16_SUBLANES, tm), BF16)
    for u in range(DIFF_HEADS):
        vta_ref[0, u, 0:V_ROWS] = va[u * V_ROWS:(u + 1) * V_ROWS].astype(BF16)
        vta_ref[0, u, V_ROWS:V_ROWS_PADDED] = ones
    for g in range(GQA_KV_HEADS):
        vtb_ref[0, g, 0:HEAD_DIM] = vb[g * HEAD_DIM:(g + 1) * HEAD_DIM].astype(BF16)
        vtb_ref[0, g, HEAD_DIM:GV_ROWS_PADDED] = ones


def _attn_in(x, mod_l, mod_row, g, wt, cos_t, sin_t, qg, kg, *, rope):
    b, s, d = x.shape
    tm = min(TOKEN_TILE, s)
    body = functools.partial(_attn_in_body, rope=rope)
    return pl.pallas_call(
        body,
        grid=(b, s // tm),
        in_specs=[pl.BlockSpec((1, tm, d), lambda i, t: (i, t, 0)),
                  pl.BlockSpec((1, 6, d), lambda i, t: (mod_row(i), 0, 0)),
                  _resident((1, d)),
                  _resident((ATTN_IN_COLS, d)),
                  pl.BlockSpec((HEAD_DIM, tm), lambda i, t: (0, t)),
                  pl.BlockSpec((HEAD_DIM, tm), lambda i, t: (0, t)),
                  _resident((HEAD_DIM, 1)),
                  _resident((HEAD_DIM, 1))],
        out_specs=[pl.BlockSpec((1, 1024, tm), lambda i, t: (i, 0, t)),
                   pl.BlockSpec((1, tm, K_COLS), lambda i, t: (i, t, 0)),
                   pl.BlockSpec((1, DIFF_HEADS, V_ROWS_PADDED, tm), lambda i, t: (i, 0, 0, t)),
                   pl.BlockSpec((1, GQA_KV_HEADS, GV_ROWS_PADDED, tm), lambda i, t: (i, 0, 0, t))],
        out_shape=[jax.ShapeDtypeStruct((b, 1024, s), BF16),
                   jax.ShapeDtypeStruct((b, s, K_COLS), BF16),
                   jax.ShapeDtypeStruct((b, DIFF_HEADS, V_ROWS_PADDED, s), BF16),
                   jax.ShapeDtypeStruct((b, GQA_KV_HEADS, GV_ROWS_PADDED, s), BF16)],
        compiler_params=_cparams("parallel", "parallel"),
        name="attn_in_rope" if rope else "attn_in_ctx",
    )(x, mod_l, g, wt, cos_t, sin_t, qg, kg)


def _attn_unit_cols(tq):
    units = []
    for h in range(DIFF_HEADS):
        units.append((h * 128, h, V_ROWS_PADDED, h * 2 * tq, 2 * tq))
    base = DIFF_HEADS * 2 * tq
    for g in range(GQA_KV_HEADS):
        units.append((DIFF_HEADS * 128, g, GV_ROWS_PADDED, base + g * GQA_GROUP * tq,
                      GQA_GROUP * tq))
    return units


def _flash_body(*refs, tq, lambda_init, has_lat):
    if has_lat:
        (lq1, lk1, lq2, lk2, sg_ref, x_ref, mod_ref, wo_ref, qt_ref, kc_ref, vtac_ref, vtbc_ref,
         kl_ref, vtal_ref, vtbl_ref, o_ref, rhs, acc, mrow, pv_new, m_chunk) = refs
    else:
        (lq1, lk1, lq2, lk2, sg_ref, x_ref, mod_ref, wo_ref, qt_ref, kc_ref, vtac_ref, vtbc_ref,
         o_ref, rhs, acc, mrow, pv_new, m_chunk) = refs
    j = pl.program_id(2)
    nj = pl.num_programs(2)
    units = _attn_unit_cols(tq)
    diff_cols = DIFF_HEADS * 2 * tq

    def values(vta_ref, vtb_ref, u, vu):
        return vta_ref[0, vu] if u < DIFF_HEADS else vtb_ref[0, vu]

    def exact_step(k_ref, vta_ref, vtb_ref):
        for u, (kc0, vu, vr, c0, w) in enumerate(units):
            s = jnp.dot(k_ref[0, :, kc0:kc0 + 128], rhs[:, c0:c0 + w],
                        preferred_element_type=F32)
            mp = mrow[:, c0:c0 + w]
            mn = jnp.maximum(mp, jnp.max(s, axis=0, keepdims=True))
            alpha = jnp.exp2(mp - mn)
            p = jnp.exp2(s - mn).astype(BF16)
            pv = jnp.dot(values(vta_ref, vtb_ref, u, vu), p, preferred_element_type=F32)
            acc[0:vr, c0:c0 + w] = acc[0:vr, c0:c0 + w] * alpha + pv
            mrow[:, c0:c0 + w] = mn

    def streaming_step(k_ref, vta_ref, vtb_ref):
        nk = k_ref.shape[1]
        sub = min(nk, ATTN_KEY_SUBBLOCK)
        for u, (kc0, vu, vr, c0, w) in enumerate(units):
            m_used = mrow[:, c0:c0 + w]
            cm = None
            parts = []
            for r in range(nk // sub):
                s = jnp.dot(k_ref[0, r * sub:(r + 1) * sub, kc0:kc0 + 128], rhs[:, c0:c0 + w],
                            preferred_element_type=F32)
                parts.append(jnp.exp2(s - m_used).astype(BF16))
                sm = jnp.max(s.reshape(sub // 8, 8, w), axis=0)
                cm = sm if cm is None else jnp.maximum(cm, sm)
            p = jnp.concatenate(parts, axis=0)
            pv_new[0:vr, c0:c0 + w] = jnp.dot(values(vta_ref, vtb_ref, u, vu), p,
                                              preferred_element_type=F32)
            m_chunk[:, c0:c0 + w] = jnp.max(cm, axis=0, keepdims=True)

    @pl.when(j == 0)
    def _init():
        zeros = jnp.zeros((HEAD_DIM, tq), BF16)
        for h in range(DIFF_HEADS):
            c0 = h * 2 * tq
            rhs[0:64, c0:c0 + tq] = qt_ref[0, h * 128:h * 128 + 64, :]
            rhs[64:128, c0:c0 + tq] = zeros
            rhs[0:64, c0 + tq:c0 + 2 * tq] = zeros
            rhs[64:128, c0 + tq:c0 + 2 * tq] = qt_ref[0, h * 128 + 64:h * 128 + 128, :]
        base = DIFF_HEADS * 2 * tq
        for g in range(GQA_KV_HEADS):
            for r in range(GQA_GROUP):
                c0 = base + (g * GQA_GROUP + r) * tq
                hd = 512 + (g * GQA_GROUP + r) * HEAD_DIM
                rhs[g * 64:(g + 1) * 64, c0:c0 + tq] = qt_ref[0, hd:hd + HEAD_DIM, :]
                rhs[(1 - g) * 64:(2 - g) * 64, c0:c0 + tq] = zeros
        acc[...] = jnp.zeros(acc.shape, F32)
        mrow[...] = jnp.zeros(mrow.shape, F32)

    row_blocks = ((V_ROWS_PADDED, slice(0, diff_cols)),
                  (GV_ROWS_PADDED, slice(diff_cols, acc.shape[1])))

    def nothing_pending():
        for rows, cols in row_blocks:
            pv_new[0:rows, cols] = jnp.zeros((rows, cols.stop - cols.start), F32)
        m_chunk[...] = jnp.full(m_chunk.shape, -jnp.inf, F32)

    def commit():
        mp = mrow[...]
        mn = jnp.maximum(mp, m_chunk[...])
        mrow[...] = mn
        alpha = jnp.exp2(mp - mn)
        for rows, cols in row_blocks:
            acc[0:rows, cols] = (acc[0:rows, cols] + pv_new[0:rows, cols]) * alpha[:, cols]

    def stream_chunk(k_ref, vta_ref, vtb_ref, first):
        streaming_step(k_ref, vta_ref, vtb_ref)
        dev = m_chunk[...] - mrow[...]
        in_range = jnp.max(jnp.abs(dev) if first else dev) <= ATTN_STREAM_MAX_OCTAVES

        @pl.when(jnp.logical_not(in_range))
        def _redo():
            if first:
                mrow[...] = jnp.full(mrow.shape, -jnp.inf, F32)
            exact_step(k_ref, vta_ref, vtb_ref)
            nothing_pending()

    @pl.when(j == 0)
    def _context_keys():
        stream_chunk(kc_ref, vtac_ref, vtbc_ref, True)

    if has_lat:
        commit()
        stream_chunk(kl_ref, vtal_ref, vtbl_ref, False)

    @pl.when(j == nj - 1)
    def _finish():
        commit()
        lam = (jnp.exp(jnp.sum(lq1[...] * lk1[...], keepdims=True))
               - jnp.exp(jnp.sum(lq2[...] * lk2[...], keepdims=True)) + lambda_init)
        pieces = []
        for h in range(DIFF_HEADS):
            c0 = h * 2 * tq
            o1 = acc[0:V_ROWS, c0:c0 + tq] / acc[V_ROWS:V_ROWS + 1, c0:c0 + tq]
            o2 = acc[0:V_ROWS, c0 + tq:c0 + 2 * tq] / acc[V_ROWS:V_ROWS + 1, c0 + tq:c0 + 2 * tq]
            oh = o1 - lam * o2
            ms = jnp.mean(oh * oh, axis=0, keepdims=True)
            pieces.append(oh * lax.rsqrt(ms + NORM_EPS) * sg_ref[...] * (1.0 - lambda_init))
        base = DIFF_HEADS * 2 * tq
        for g in range(GQA_KV_HEADS):
            for r in range(GQA_GROUP):
                c0 = base + (g * GQA_GROUP + r) * tq
                pieces.append(acc[0:HEAD_DIM, c0:c0 + tq] / acc[HEAD_DIM:HEAD_DIM + 1, c0:c0 + tq])
        o = jnp.concatenate(pieces, axis=0).T.astype(BF16)
        y = jnp.dot(o, wo_ref[...], preferred_element_type=F32)
        o_ref[0] = x_ref[0] + mod_ref[0][2:3] * y


def _flash(x, mod_l, mod_row, w_out, lam_vecs, sg, qt, ctx_kv, lat_kv, *, lambda_init):
    b, _, sq = qt.shape
    d = x.shape[2]
    kc, vtac, vtbc = ctx_kv
    c = kc.shape[1]
    has_lat = lat_kv is not None
    tq = min(ATTN_QUERY_TILE, sq)
    ncols = (DIFF_HEADS * 2 + GQA_HEADS) * tq
    in_specs = [_resident((1, HEAD_DIM))] * 4 + [
        _resident((V_ROWS, 1)),
        pl.BlockSpec((1, tq, d), lambda i, q, j: (i, q, 0)),
        pl.BlockSpec((1, 6, d), lambda i, q, j: (mod_row(i), 0, 0)),
        _resident((1024, d)),
        pl.BlockSpec((1, 1024, tq), lambda i, q, j: (i, 0, q)),
        pl.BlockSpec((1, c, K_COLS), lambda i, q, j: (i, 0, 0)),
        pl.BlockSpec((1, DIFF_HEADS, V_ROWS_PADDED, c), lambda i, q, j: (i, 0, 0, 0)),
        pl.BlockSpec((1, GQA_KV_HEADS, GV_ROWS_PADDED, c), lambda i, q, j: (i, 0, 0, 0)),
    ]
    args = list(lam_vecs) + [sg, x, mod_l, w_out, qt, kc, vtac, vtbc]
    nkv = 1
    if has_lat:
        kl, vtal, vtbl = lat_kv
        n = kl.shape[1]
        tk = min(ATTN_KEY_CHUNK, n)
        nkv = n // tk
        in_specs += [pl.BlockSpec((1, tk, K_COLS), lambda i, q, j: (i, j, 0)),
                     pl.BlockSpec((1, DIFF_HEADS, V_ROWS_PADDED, tk), lambda i, q, j: (i, 0, 0, j)),
                     pl.BlockSpec((1, GQA_KV_HEADS, GV_ROWS_PADDED, tk), lambda i, q, j: (i, 0, 0, j))]
        args += [kl, vtal, vtbl]
    scratch = [pltpu.VMEM((2 * HEAD_DIM, ncols), BF16),
               pltpu.VMEM((V_ROWS_PADDED, ncols), F32),
               pltpu.VMEM((1, ncols), F32),
               pltpu.VMEM((V_ROWS_PADDED, ncols), F32),
               pltpu.VMEM((1, ncols), F32)]
    body = functools.partial(_flash_body, tq=tq, lambda_init=lambda_init, has_lat=has_lat)
    return pl.pallas_call(
        body,
        grid=(b, sq // tq, nkv),
        in_specs=in_specs,
        out_specs=pl.BlockSpec((1, tq, d), lambda i, q, j: (i, q, 0)),
        out_shape=jax.ShapeDtypeStruct((b, sq, d), F32),
        scratch_shapes=scratch,
        compiler_params=_cparams("parallel", "parallel", "arbitrary"),
        name="attn_sweep_lat" if has_lat else "attn_sweep_ctx",
    )(*args)


def _ffn_body(x_ref, xp_ref, xn_ref, mod_ref, g_ref, wv_ref, wg_ref, cw_ref, cb_ref, wo_ref,
              fg_ref, o_ref, *, tm, final_norm):
    t = pl.program_id(1)
    nt = pl.num_programs(1)
    m = mod_ref[0]
    x = x_ref[0]
    xe = jnp.concatenate([xp_ref[0], x, xn_ref[0]], axis=0)
    he = _modulate(xe, g_ref[...], m[3:4], m[4:5]).astype(BF16)
    ge = jnp.dot(he, wg_ref[...], preferred_element_type=F32)
    rows = lax.broadcasted_iota(jnp.int32, (tm + 2 * CONV_HALO, 1), 0)
    lo = jnp.where(t > 0, 0, CONV_HALO)
    hi = jnp.where(t < nt - 1, tm + 2 * CONV_HALO, tm + CONV_HALO)
    ge = jnp.where((rows >= lo) & (rows < hi), ge, 0.0)
    val = jnp.dot(he[CONV_HALO:CONV_HALO + tm], wv_ref[...], preferred_element_type=F32)
    cw = cw_ref[...]
    h0 = CONV_HALO
    conv = (cb_ref[...] + cw[0:1] * ge[h0 - 1:h0 - 1 + tm] + cw[1:2] * ge[h0:h0 + tm]
            + cw[2:3] * ge[h0 + 1:h0 + 1 + tm])
    gelu = 0.5 * conv * (1.0 + lax.erf(conv * math.sqrt(0.5)))
    act = (gelu * val).astype(BF16)
    y = jnp.dot(act, wo_ref[...], preferred_element_type=F32)
    out = x + m[5:6] * y
    if final_norm:
        ms = jnp.mean(out * out, axis=-1, keepdims=True)
        out = out * lax.rsqrt(ms + NORM_EPS) * fg_ref[...]
    o_ref[0] = out


def _ffn(x, mod_l, mod_row, g, wv, wg, cw, cb, wo, fg, *, final_norm):
    b, s, d = x.shape
    tm = min(FFN_TOKEN_TILE, s)
    hb = tm // CONV_HALO
    last = s // CONV_HALO - 1
    body = functools.partial(_ffn_body, tm=tm, final_norm=final_norm)
    return pl.pallas_call(
        body,
        grid=(b, s // tm),
        in_specs=[pl.BlockSpec((1, tm, d), lambda i, t: (i, t, 0)),
                  pl.BlockSpec((1, CONV_HALO, d), lambda i, t: (i, jnp.maximum(t * hb - 1, 0), 0)),
                  pl.BlockSpec((1, CONV_HALO, d), lambda i, t: (i, jnp.minimum((t + 1) * hb, last), 0)),
                  pl.BlockSpec((1, 6, d), lambda i, t: (mod_row(i), 0, 0)),
                  _resident((1, d)),
                  _resident((d, D_FF)),
                  _resident((d, D_FF)),
                  _resident((3, D_FF)),
                  _resident((1, D_FF)),
                  _resident((D_FF, d)),
                  _resident((1, d))],
        out_specs=pl.BlockSpec((1, tm, d), lambda i, t: (i, t, 0)),
        out_shape=jax.ShapeDtypeStruct((b, s, d), F32),
        compiler_params=_cparams("parallel", "parallel"),
        name="conv_glu_ffn",
    )(x, x, x, mod_l, g, wv, wg, cw, cb, wo, fg)


def _ssm_proj_body(x_ref, mod_ref, g_ref, wt_ref, dtb_ref, zt_ref, xbct_ref, dtt_ref):
    m = mod_ref[0]
    h = _modulate(x_ref[0], g_ref[...], m[0:1], m[1:2]).astype(BF16)
    t = lax.dot_general(wt_ref[...], h, NT_DIMS, preferred_element_type=F32)
    zt_ref[0] = t[0:SSM_D_INNER].astype(BF16)
    xbct_ref[0] = t[SSM_D_INNER:SSM_D_INNER + SSM_XBC].astype(BF16)
    dtt_ref[0] = jax.nn.softplus(t[SSM_D_INNER + SSM_XBC:SSM_IN_COLS] + dtb_ref[...])


def _ssm_proj(x, mod_l, mod_row, g, wt, dtb):
    b, s, d = x.shape
    tm = min(TOKEN_TILE, s)
    return pl.pallas_call(
        _ssm_proj_body,
        grid=(b, s // tm),
        in_specs=[pl.BlockSpec((1, tm, d), lambda i, t: (i, t, 0)),
                  pl.BlockSpec((1, 6, d), lambda i, t: (mod_row(i), 0, 0)),
                  _resident((1, d)),
                  _resident((SSM_IN_COLS, d)),
                  _resident((2 * SSM_HEADS, 1))],
        out_specs=[pl.BlockSpec((1, SSM_D_INNER, tm), lambda i, t: (i, 0, t)),
                   pl.BlockSpec((1, SSM_XBC, tm), lambda i, t: (i, 0, t)),
                   pl.BlockSpec((1, 2 * SSM_HEADS, tm), lambda i, t: (i, 0, t))],
        out_shape=[jax.ShapeDtypeStruct((b, SSM_D_INNER, s), BF16),
                   jax.ShapeDtypeStruct((b, SSM_XBC, s), BF16),
                   jax.ShapeDtypeStruct((b, 2 * SSM_HEADS, s), F32)],
        compiler_params=_cparams("parallel", "parallel"),
        name="ssm_in_proj",
    )(x, mod_l, g, wt, dtb)


def _ssm_conv_body(u_ref, up_ref, un_ref, w_ref, b_ref, xst_ref, ct_ref, bm_ref, *, tc):
    t = pl.program_id(1)
    nt = pl.num_programs(1)
    prev = jnp.where(t > 0, up_ref[0][:, V7X_LANES - 1:V7X_LANES].astype(F32), 0.0)
    nxt = jnp.where(t < nt - 1, un_ref[0][:, 0:1].astype(F32), 0.0)
    lane = lax.broadcasted_iota(jnp.int32, (1, tc), 1)
    bn = SSM_GROUPS * SSM_STATE
    reps = tc // V7X_LANES

    def lanes(a):
        return jnp.concatenate([a] * reps, axis=1) if reps > 1 else a

    for r0 in range(0, SSM_XBC, CONV_CHANNEL_BLOCK):
        rows = slice(r0, r0 + CONV_CHANNEL_BLOCK)
        ub = u_ref[0, rows, :]
        packed = pltpu.bitcast(ub, jnp.uint32)
        left = pltpu.bitcast(pltpu.roll(packed, 1, 1), BF16).astype(F32)
        right = pltpu.bitcast(pltpu.roll(packed, tc - 1, 1), BF16).astype(F32)
        left = jnp.where(lane == 0, prev[rows], left)
        right = jnp.where(lane == tc - 1, nxt[rows], right)
        v = _silu(lanes(b_ref[rows, :]) + lanes(w_ref[0, rows, :]) * left
                  + lanes(w_ref[1, rows, :]) * ub.astype(F32) + lanes(w_ref[2, rows, :]) * right)
        if r0 < SSM_D_INNER:
            xst_ref[0, rows, :] = v.astype(BF16)
        elif r0 < SSM_D_INNER + bn:
            bm_ref[0, :, r0 - SSM_D_INNER:r0 - SSM_D_INNER + CONV_CHANNEL_BLOCK] = v.T.astype(BF16)
        else:
            q0 = r0 - SSM_D_INNER - bn
            ct_ref[0, q0:q0 + CONV_CHANNEL_BLOCK, :] = v.astype(BF16)


def _ssm_conv(xbct, w, bias):
    b, ch, s = xbct.shape
    tc = min(TOKEN_TILE, s)
    hb = tc // V7X_LANES
    last = s // V7X_LANES - 1
    bn = SSM_GROUPS * SSM_STATE
    body = functools.partial(_ssm_conv_body, tc=tc)
    return pl.pallas_call(
        body,
        grid=(b, s // tc),
        in_specs=[pl.BlockSpec((1, ch, tc), lambda i, t: (i, 0, t)),
                  pl.BlockSpec((1, ch, V7X_LANES), lambda i, t: (i, 0, jnp.maximum(t * hb - 1, 0))),
                  pl.BlockSpec((1, ch, V7X_LANES), lambda i, t: (i, 0, jnp.minimum((t + 1) * hb, last))),
                  _resident((3, ch, V7X_LANES)),
                  _resident((ch, V7X_LANES))],
        out_specs=[pl.BlockSpec((1, SSM_D_INNER, tc), lambda i, t: (i, 0, t)),
                   pl.BlockSpec((1, bn, tc), lambda i, t: (i, 0, t)),
                   pl.BlockSpec((1, tc, bn), lambda i, t: (i, t, 0))],
        out_shape=[jax.ShapeDtypeStruct((b, SSM_D_INNER, s), BF16),
                   jax.ShapeDtypeStruct((b, bn, s), BF16),
                   jax.ShapeDtypeStruct((b, s, bn), BF16)],
        compiler_params=_cparams("parallel", "parallel"),
        name="ssm_conv_silu",
    )(xbct, xbct, xbct, w, bias)


def _dot_f32_by_01(a, m01):
    hi = a.astype(BF16)
    r1 = a - hi.astype(F32)
    mid = r1.astype(BF16)
    lo = (r1 - mid.astype(F32)).astype(BF16)
    return (jnp.dot(hi, m01, preferred_element_type=F32)
            + jnp.dot(mid, m01, preferred_element_type=F32)
            + jnp.dot(lo, m01, preferred_element_type=F32))


def _scan_order_mask(d, chunk):
    jrow = lax.broadcasted_iota(jnp.int32, (chunk, chunk), 0)
    icol = lax.broadcasted_iota(jnp.int32, (chunk, chunk), 1)
    sign = jnp.where(d == 0, 1, -1)
    return (icol - jrow) * sign >= 0


DECAY_ROWS = 4 * SSM_HEADS


def _ssd_decay_body(a_ref, dt_ref, fac_ref, src_ref, *, chunk):
    d = pl.program_id(1)
    nh = SSM_HEADS
    mask01 = _scan_order_mask(d, chunk).astype(BF16)
    for s in range(dt_ref.shape[2] // chunk):
        tok = slice(s * chunk, (s + 1) * chunk)
        dt = dt_ref[0, :, tok]
        cum_t = _dot_f32_by_01(dt * a_ref[0], mask01)
        cum2_t = cum_t * LOG2E
        fac_ref[0, 0, 0:nh, tok] = cum2_t
        src_ref[0, 0, tok, :] = (cum2_t - jnp.log2(dt)).T
        tot = jnp.where(d == 0, cum_t[:, chunk - 1:chunk], cum_t[:, 0:1])
        fac_ref[0, 0, nh:2 * nh, tok] = jnp.exp(tot - cum_t) * dt
        fac_ref[0, 0, 2 * nh:3 * nh, tok] = jnp.exp(cum_t)
        fac_ref[0, 0, 3 * nh:4 * nh, tok] = jnp.broadcast_to(jnp.exp(tot), cum_t.shape)


def _ssd_decay(a, dtt, chunk):
    b, _, s = dtt.shape
    span = min(s, 8 * chunk)
    body = functools.partial(_ssd_decay_body, chunk=chunk)
    return pl.pallas_call(
        body,
        grid=(b, 2, s // span),
        in_specs=[pl.BlockSpec((1, SSM_HEADS, 1), lambda i, d, g: (d, 0, 0)),
                  pl.BlockSpec((1, SSM_HEADS, span), lambda i, d, g: (i, d, g))],
        out_specs=[pl.BlockSpec((1, 1, DECAY_ROWS, span), lambda i, d, g: (i, d, 0, g)),
                   pl.BlockSpec((1, 1, span, SSM_HEADS), lambda i, d, g: (i, d, g, 0))],
        out_shape=[jax.ShapeDtypeStruct((b, 2, DECAY_ROWS, s), F32),
                   jax.ShapeDtypeStruct((b, 2, s, SSM_HEADS), F32)],
        compiler_params=_cparams("parallel", "parallel", "parallel"),
        name="ssd_decay_factors",
    )(a, dtt)


def _scan_body(fac_ref, src_ref, xs_ref, ct_ref, b_ref, s0_ref, y_ref, sout_ref, state, *, chunk):
    d = pl.program_id(1)
    c = pl.program_id(2)
    nc = pl.num_programs(2)
    hp = SSM_HEADS_PER_GROUP
    hd = SSM_D_INNER // SSM_HEADS
    gw = hp * hd
    nh = SSM_HEADS
    nsub = xs_ref.shape[2] // chunk

    @pl.when(c == 0)
    def _load_state():
        state[...] = s0_ref[0, 0]

    blk = V7X_LANES
    nblk = chunk // blk

    def decay_weights(direction, cbt, cum2_row, src_col, not_yet_diag):
        rows = []
        for jb in range(nblk):
            cols = []
            for ib in range(nblk):
                ahead = ib - jb if direction == 0 else jb - ib
                if ahead < 0:
                    cols.append(jnp.zeros((blk, blk), BF16))
                    continue
                js, is_ = slice(jb * blk, (jb + 1) * blk), slice(ib * blk, (ib + 1) * blk)
                seg = cum2_row[:, is_] - src_col[js, :]
                if ahead == 0:
                    seg = seg + not_yet_diag
                cols.append((cbt[js, is_] * jnp.exp2(seg)).astype(BF16))
            rows.append(jnp.concatenate(cols, axis=1))
        return jnp.concatenate(rows, axis=0)

    def one_chunk(direction, tok):
        not_yet_diag = jnp.where(_scan_order_mask(direction, blk), 0.0, -jnp.inf)
        cum2_t = fac_ref[0, 0, 0:nh, tok]
        src_term = src_ref[0, 0, tok, :]
        to_end = fac_ref[0, 0, nh:2 * nh, tok]
        ecum = fac_ref[0, 0, 2 * nh:3 * nh, tok]
        etot = fac_ref[0, 0, 3 * nh:4 * nh, tok.start:tok.start + 1]
        for g in range(SSM_GROUPS):
            bg = b_ref[0, tok, g * SSM_STATE:(g + 1) * SSM_STATE]
            ctg = ct_ref[0, g * SSM_STATE:(g + 1) * SSM_STATE, tok]
            cbt = jnp.dot(bg, ctg, preferred_element_type=F32)
            xg = xs_ref[0, g * gw:(g + 1) * gw, tok]
            sg = state[g * gw:(g + 1) * gw, :]
            hs = slice(g * hp, (g + 1) * hp)
            y_state = (jnp.dot(sg.astype(BF16), ctg, preferred_element_type=F32)
                       .reshape(hp, hd, chunk) * ecum[hs][:, None, :])
            outs = []
            for r in range(hp):
                h = g * hp + r
                w = decay_weights(direction, cbt, cum2_t[h:h + 1, :], src_term[:, h:h + 1],
                                  not_yet_diag)
                outs.append(jnp.dot(xg[r * hd:(r + 1) * hd], w, preferred_element_type=F32)
                            + y_state[r])
            y_ref[0, 0, g * gw:(g + 1) * gw, tok] = jnp.concatenate(outs, axis=0).astype(BF16)
            xw = (xg.reshape(hp, hd, chunk) * to_end[hs].astype(BF16)[:, None, :]).reshape(gw, chunk)
            upd = jnp.dot(xw, bg, preferred_element_type=F32)
            decayed = (sg.reshape(hp, hd, SSM_STATE) * etot[hs][:, None, :]).reshape(gw, SSM_STATE)
            state[g * gw:(g + 1) * gw, :] = decayed + upd

    for direction in (0, 1):
        @pl.when(d == direction)
        def _walk(direction=direction):
            order = range(nsub) if direction == 0 else reversed(range(nsub))
            for sub in order:
                one_chunk(direction, slice(sub * chunk, (sub + 1) * chunk))

    @pl.when(c == nc - 1)
    def _store_state():
        sout_ref[0, 0] = state[...]


def _ssd_scan(a, dtt, xst, ct, bm, s0):
    b, ch, s = xst.shape
    chunk = min(SSD_CHUNK, s)
    span = min(s, SSD_CHUNKS_PER_STEP * chunk)
    nc = s // span
    bn = SSM_GROUPS * SSM_STATE
    fac, src = _ssd_decay(a, dtt, chunk)

    def cidx(d, c):
        return c + d * (nc - 1 - 2 * c)

    body = functools.partial(_scan_body, chunk=chunk)
    return pl.pallas_call(
        body,
        grid=(b, 2, nc),
        in_specs=[pl.BlockSpec((1, 1, DECAY_ROWS, span), lambda i, d, c: (i, d, 0, cidx(d, c))),
                  pl.BlockSpec((1, 1, span, SSM_HEADS), lambda i, d, c: (i, d, cidx(d, c), 0)),
                  pl.BlockSpec((1, ch, span), lambda i, d, c: (i, 0, cidx(d, c))),
                  pl.BlockSpec((1, bn, span), lambda i, d, c: (i, 0, cidx(d, c))),
                  pl.BlockSpec((1, span, bn), lambda i, d, c: (i, cidx(d, c), 0)),
                  pl.BlockSpec((1, 1, ch, SSM_STATE), lambda i, d, c: (d, i, 0, 0))],
        out_specs=[pl.BlockSpec((1, 1, ch, span), lambda i, d, c: (d, i, 0, cidx(d, c))),
                   pl.BlockSpec((1, 1, ch, SSM_STATE), lambda i, d, c: (d, i, 0, 0))],
        out_shape=[jax.ShapeDtypeStruct((2, b, ch, s), BF16),
                   jax.ShapeDtypeStruct((2, b, ch, SSM_STATE), F32)],
        scratch_shapes=[pltpu.VMEM((ch, SSM_STATE), F32)],
        compiler_params=_cparams("parallel", "parallel", "arbitrary"),
        name="ssd_scan",
    )(fac, src, xst, ct, bm, s0)


def _ssm_out_body(x_ref, mod_ref, yf_ref, yb_ref, xs_ref, z_ref, dsk_ref, ng_ref, wt_ref, o_ref):
    y = (yf_ref[0, 0].astype(F32) + yb_ref[0, 0].astype(F32)
         + dsk_ref[...] * xs_ref[0].astype(F32))
    y = y * _silu(z_ref[0].astype(F32))
    ms = jnp.mean(y * y, axis=0, keepdims=True)
    yn = (y * lax.rsqrt(ms + NORM_EPS) * ng_ref[...]).astype(BF16)
    ot = jnp.dot(wt_ref[...], yn, preferred_element_type=F32)
    o_ref[0] = x_ref[0] + mod_ref[0][2:3] * ot.T


def _ssm_out(x, mod_l, mod_row, y, xst, zt, dsk, ng, wt):
    b, s, d = x.shape
    tm = min(TOKEN_TILE, s)
    ch = xst.shape[1]
    return pl.pallas_call(
        _ssm_out_body,
        grid=(b, s // tm),
        in_specs=[pl.BlockSpec((1, tm, d), lambda i, t: (i, t, 0)),
                  pl.BlockSpec((1, 6, d), lambda i, t: (mod_row(i), 0, 0)),
                  pl.BlockSpec((1, 1, ch, tm), lambda i, t: (0, i, 0, t)),
                  pl.BlockSpec((1, 1, ch, tm), lambda i, t: (1, i, 0, t)),
                  pl.BlockSpec((1, ch, tm), lambda i, t: (i, 0, t)),
                  pl.BlockSpec((1, ch, tm), lambda i, t: (i, 0, t)),
                  _resident((ch, 1)),
                  _resident((ch, 1)),
                  _resident((d, ch))],
        out_specs=pl.BlockSpec((1, tm, d), lambda i, t: (i, t, 0)),
        out_shape=jax.ShapeDtypeStruct((b, s, d), F32),
        compiler_params=_cparams("parallel", "parallel"),
        name="ssm_out_residual",
    )(x, mod_l, y, y, xst, zt, dsk, ng, wt)


def _rope_tables_t(n):
    t = jnp.arange(n)
    inv_freq = 1.0 / (ROPE_BASE ** (jnp.arange(ROPE_PAIRS, dtype=F32) / ROPE_PAIRS))
    ang_r = (t // GRID_W).astype(F32)[None, :] * inv_freq[:, None]
    ang_c = (t % GRID_W).astype(F32)[None, :] * inv_freq[:, None]
    cr, sr, cc, sc = jnp.cos(ang_r), jnp.sin(ang_r), jnp.cos(ang_c), jnp.sin(ang_c)
    return (jnp.concatenate([cr, cr, cc, cc], axis=0),
            jnp.concatenate([-sr, sr, -sc, sc], axis=0))


def _attn_layer(x, ctx, mod_l, lat_row, ctx_row, p, rope, lambda_init, with_ctx):
    wt = p["w_in"].T.astype(BF16)
    g = p["norm_g"].reshape(1, D_MODEL)
    qg = p["q_norm_g"].reshape(HEAD_DIM, 1)
    kg = p["k_norm_g"].reshape(HEAD_DIM, 1)
    cos_t, sin_t = rope
    c = ctx.shape[1]
    qt, *lat_kv = _attn_in(x, mod_l, lat_row, g, wt, cos_t, sin_t, qg, kg, rope=True)
    qtc, *ctx_kv = _attn_in(ctx, mod_l, ctx_row, g, wt, cos_t[:, :c], sin_t[:, :c], qg, kg,
                            rope=False)
    lam_vecs = [p[k].reshape(1, HEAD_DIM) for k in ("lq1", "lk1", "lq2", "lk2")]
    sg = p["subln_g"].reshape(V_ROWS, 1)
    w_out = p["w_out"].astype(BF16)
    x = _flash(x, mod_l, lat_row, w_out, lam_vecs, sg, qt, ctx_kv, lat_kv,
               lambda_init=lambda_init)
    if with_ctx:
        ctx = _flash(ctx, mod_l, ctx_row, w_out, lam_vecs, sg, qtc, ctx_kv, None,
                     lambda_init=lambda_init)
    return x, ctx


def _ssm_layer(x, ctx, mod_l, lat_row, ctx_row, p, with_ctx):
    wt = p["w_in"].T.astype(BF16)
    g = p["norm_g"].reshape(1, D_MODEL)
    dtb = p["dt_bias"].reshape(2 * SSM_HEADS, 1)
    conv_w = jnp.broadcast_to(p["conv_w"][:, :, None], (3, SSM_XBC, V7X_LANES))
    conv_b = jnp.broadcast_to(p["conv_b"][:, None], (SSM_XBC, V7X_LANES))
    a = (-jnp.exp(p["a_log"].astype(F32))).reshape(2, SSM_HEADS, 1)
    b = x.shape[0]

    def pre(v, row):
        zt, xbct, dtt = _ssm_proj(v, mod_l, row, g, wt, dtb)
        xst, ct, bm = _ssm_conv(xbct, conv_w, conv_b)
        return zt, xst, ct, bm, dtt

    zt_c, xst_c, ct_c, bm_c, dtt_c = pre(ctx, ctx_row)
    zt_l, xst_l, ct_l, bm_l, dtt_l = pre(x, lat_row)
    zero = jnp.zeros((2, b, SSM_D_INNER, SSM_STATE), F32)
    y_c, s_ctx = _ssd_scan(a, dtt_c, xst_c, ct_c, bm_c, zero)
    y_l, _ = _ssd_scan(a, dtt_l, xst_l, ct_l, bm_l, s_ctx)
    dsk = jnp.repeat(p["d_skip"], SSM_D_INNER // SSM_HEADS).reshape(SSM_D_INNER, 1)
    ng = p["out_norm_g"].reshape(SSM_D_INNER, 1)
    w_out_t = p["w_out"].T.astype(BF16)
    x = _ssm_out(x, mod_l, lat_row, y_l, xst_l, zt_l, dsk, ng, w_out_t)
    if with_ctx:
        ctx = _ssm_out(ctx, mod_l, ctx_row, y_c, xst_c, zt_c, dsk, ng, w_out_t)
    return x, ctx


def kernel(x, c, ctx, c_ctx, mod_w, mod_b, norm_mix_g, norm_ffn_g, attn_w_in, attn_w_out,
           diff_lq1, diff_lk1, diff_lq2, diff_lk2, diff_subln_g, gqa_q_norm_g, gqa_k_norm_g,
           ssm_w_in, ssm_conv_w, ssm_conv_b, ssm_dt_bias, ssm_a_log, ssm_d, ssm_norm_g, ssm_w_out,
           ffn_w_in, ffn_conv_w, ffn_conv_b, ffn_w_out, final_norm_g):
    b, n, d = x.shape
    mod_rows = 16
    c_rows = jnp.zeros((mod_rows, d), F32).at[:b].set(c).at[b].set(c_ctx)
    mod = _mod_all(c_rows, mod_w, mod_b).reshape(DEPTH, mod_rows, 6, d)
    lat_row = lambda i: i
    ctx_row = lambda i: b
    rope = _rope_tables_t(n)

    for layer in range(DEPTH):
        with_ctx = layer < DEPTH - 1
        mod_l = mod[layer]
        i = layer // 2
        if layer % 2 == 0:
            p = dict(w_in=attn_w_in[i], w_out=attn_w_out[i], norm_g=norm_mix_g[layer],
                     lq1=diff_lq1[i], lk1=diff_lk1[i], lq2=diff_lq2[i], lk2=diff_lk2[i],
                     subln_g=diff_subln_g[i], q_norm_g=gqa_q_norm_g[i], k_norm_g=gqa_k_norm_g[i])
            lambda_init = 0.8 - 0.6 * math.exp(-0.3 * layer)
            x, ctx = _attn_layer(x, ctx, mod_l, lat_row, ctx_row, p, rope, lambda_init, with_ctx)
        else:
            p = dict(w_in=ssm_w_in[i], norm_g=norm_mix_g[layer], conv_w=ssm_conv_w[i],
                     conv_b=ssm_conv_b[i], dt_bias=ssm_dt_bias[i], a_log=ssm_a_log[i],
                     d_skip=ssm_d[i], out_norm_g=ssm_norm_g[i], w_out=ssm_w_out[i])
            x, ctx = _ssm_layer(x, ctx, mod_l, lat_row, ctx_row, p, with_ctx)
        g = norm_ffn_g[layer].reshape(1, d)
        wv = ffn_w_in[layer][:, :D_FF].astype(BF16)
        wg = ffn_w_in[layer][:, D_FF:].astype(BF16)
        cw = ffn_conv_w[layer]
        cb = ffn_conv_b[layer].reshape(1, D_FF)
        wo = ffn_w_out[layer].astype(BF16)
        fg = final_norm_g.reshape(1, d)
        x = _ffn(x, mod_l, lat_row, g, wv, wg, cw, cb, wo, fg, final_norm=layer == DEPTH - 1)
        if with_ctx:
            ctx = _ffn(ctx, mod_l, ctx_row, g, wv, wg, cw, cb, wo, fg, final_norm=False)
    return x
```
